```python
import jax, jax.numpy as jnp
from jax import lax
import numpy as np

D_MODEL = 1024
BATCH = 1
SEQ = 16384
DEPTH = 4

EPS = 1e-6
BLOCK = 128
CONV_CH = 512
CONV_K = 3
POOL_WINDOWS = (2, 4, 8, 16)
POOL_GROUP_CH = 128
POOL_CH = POOL_GROUP_CH * len(POOL_WINDOWS)
EVEN_IN = 3 * CONV_CH + POOL_CH
EVEN_MIX = CONV_CH + POOL_CH
SWA_HEADS = 8
SWA_KV_HEADS = 2
SWA_GROUP = SWA_HEADS // SWA_KV_HEADS
HEAD_DIM = 64
WINDOW = 128
SWA_Q_W = SWA_HEADS * HEAD_DIM
SWA_KV_W = SWA_KV_HEADS * HEAD_DIM
MLA_HEADS = 8
MLA_NOPE = 64
MLA_ROPE = 32
MLA_QK = MLA_NOPE + MLA_ROPE
MLA_V = 64
Q_LORA = 384
KV_LORA = 256
ROPE_THETA = 10000.0
ODD_SPLITS = (SWA_Q_W, SWA_KV_W, SWA_KV_W, Q_LORA, KV_LORA, MLA_ROPE)
ODD_IN = sum(ODD_SPLITS)
ODD_MIX = SWA_HEADS * HEAD_DIM + MLA_HEADS * MLA_V
N_GROUPS = 4
EXPERTS_PER_GROUP = 8
N_EXPERTS = N_GROUPS * EXPERTS_PER_GROUP
EXPERT_FF = 256
TOP_K = 2
N_EVEN = (DEPTH + 1) // 2
N_ODD = DEPTH // 2

kernel_name = "hybrid_conv_pool_swa_mla_hmoe_adaln"


def rms_norm(x, g):
    xf = x.astype(jnp.float32)
    y = xf * lax.rsqrt(jnp.mean(xf * xf, axis=-1, keepdims=True) + EPS)
    return (y * g.astype(jnp.float32)).astype(x.dtype)


def apply_rope(x, positions):
    half = x.shape[-1] // 2
    inv = jnp.power(ROPE_THETA, -jnp.arange(half, dtype=jnp.float32) / half)
    ang = positions.astype(jnp.float32)[..., None] * inv
    cos = jnp.cos(ang)[:, :, None, :]
    sin = jnp.sin(ang)[:, :, None, :]
    x1 = x[..., :half].astype(jnp.float32)
    x2 = x[..., half:].astype(jnp.float32)
    return jnp.concatenate([x1 * cos - x2 * sin, x1 * sin + x2 * cos], axis=-1).astype(x.dtype)


def multiscale_pool(u, pool_w, pool_scale):
    S = u.shape[1]
    cs = jnp.cumsum(u.astype(jnp.float32), axis=1)
    t = jnp.arange(1, S + 1, dtype=jnp.float32)[None, :, None]
    outs = []
    for gi, w in enumerate(POOL_WINDOWS):
        sl = slice(gi * POOL_GROUP_CH, (gi + 1) * POOL_GROUP_CH)
        csg = cs[..., sl]
        lag = jnp.pad(csg, ((0, 0), (w, 0), (0, 0)))[:, :S]
        mean = (csg - lag) / jnp.minimum(t, float(w))
        d = (mean - u[..., sl].astype(jnp.float32)).astype(u.dtype)
        outs.append(d @ pool_w[gi])
    return jnp.concatenate(outs, axis=-1) * pool_scale


def conv_pool_mixer(h, w_in, conv_w, pool_w, pool_scale, w_out):
    S = h.shape[1]
    z = h @ w_in
    b_gate, c_gate, xa, u = jnp.split(z, [CONV_CH, 2 * CONV_CH, 3 * CONV_CH], axis=-1)
    v = c_gate * xa
    vp = jnp.pad(v, ((0, 0), (CONV_K - 1, 0), (0, 0)))
    conv = sum(vp[:, CONV_K - 1 - k: CONV_K - 1 - k + S] * conv_w[k] for k in range(CONV_K))
    y_a = b_gate * conv
    y_b = multiscale_pool(u, pool_w, pool_scale)
    return jnp.concatenate([y_a, y_b], axis=-1) @ w_out


def swa_sink_attention(q, k, v, q_g, k_g, sinks):
    B, S = q.shape[:2]
    nb = S // BLOCK
    q = rms_norm(q, q_g)
    k = rms_norm(k, k_g)
    qb = q.reshape(B, nb, BLOCK, SWA_KV_HEADS, SWA_GROUP, HEAD_DIM)

    def band(t):
        tb = t.reshape(B, nb, BLOCK, SWA_KV_HEADS, HEAD_DIM)
        prev = jnp.pad(tb, ((0, 0), (1, 0), (0, 0), (0, 0), (0, 0)))[:, :-1]
        return jnp.concatenate([prev, tb], axis=2)

    kb, vb = band(k), band(v)
    s = jnp.einsum('bnqkgd,bnskd->bnkgqs', qb, kb).astype(jnp.float32) * (HEAD_DIM ** -0.5)
    qi = jnp.arange(BLOCK)[:, None]
    si = jnp.arange(2 * BLOCK)[None, :]
    rel = qi + BLOCK - si
    in_win = (rel >= 0) & (rel < WINDOW)
    valid = (jnp.arange(nb)[:, None, None] * BLOCK - BLOCK + si[None]) >= 0
    mask = (in_win[None] & valid)[None, :, None, None]
    s = jnp.where(mask, s, -jnp.inf)
    sink = sinks.astype(jnp.float32).reshape(1, 1, SWA_KV_HEADS, SWA_GROUP, 1, 1)
    m = jnp.maximum(jnp.max(s, axis=-1, keepdims=True), sink)
    e = jnp.exp(s - m)
    p = e / (jnp.sum(e, axis=-1, keepdims=True) + jnp.exp(sink - m))
    o = jnp.einsum('bnkgqs,bnskd->bnqkgd', p.astype(v.dtype), vb)
    return o.reshape(B, S, SWA_HEADS * HEAD_DIM)


def mla_attention(c_q, c_kv, k_rope, positions, q_norm_g, kv_norm_g, w_uq, w_ukv, q_g, k_g):
    B, S = c_q.shape[:2]
    nb = S // BLOCK
    q = (rms_norm(c_q, q_norm_g) @ w_uq).reshape(B, S, MLA_HEADS, MLA_QK)
    kv = (rms_norm(c_kv, kv_norm_g) @ w_ukv).reshape(B, S, MLA_HEADS, MLA_NOPE + MLA_V)
    k_nope, v = kv[..., :MLA_NOPE], kv[..., MLA_NOPE:]
    k = jnp.concatenate(
        [k_nope, jnp.broadcast_to(k_rope[:, :, None, :], (B, S, MLA_HEADS, MLA_ROPE))], axis=-1)
    q = rms_norm(q, q_g)
    k = rms_norm(k, k_g)
    q = jnp.concatenate([q[..., :MLA_NOPE], apply_rope(q[..., MLA_NOPE:], positions)], axis=-1)
    k = jnp.concatenate([k[..., :MLA_NOPE], apply_rope(k[..., MLA_NOPE:], positions)], axis=-1)
    qb = q.reshape(B, nb, BLOCK, MLA_HEADS, MLA_QK).transpose(1, 0, 2, 3, 4)
    kpos = jnp.arange(S)

    def one_block(args):
        qblk, n = args
        s = jnp.einsum('bqhd,bkhd->bhqk', qblk, k).astype(jnp.float32) * (MLA_QK ** -0.5)
        qpos = n * BLOCK + jnp.arange(BLOCK)
        s = jnp.where(kpos[None, :] <= qpos[:, None], s, -jnp.inf)
        p = jax.nn.softmax(s, axis=-1)
        return jnp.einsum('bhqk,bkhd->bqhd', p.astype(v.dtype), v)

    o = lax.map(one_block, (qb, jnp.arange(nb)))
    return o.transpose(1, 0, 2, 3, 4).reshape(B, S, MLA_HEADS * MLA_V)


def attention_mixer(h, positions, w_in, swa_q_g, swa_k_g, swa_sinks, mla_q_norm_g,
                    mla_kv_norm_g, mla_w_uq, mla_w_ukv, mla_q_g, mla_k_g, w_out):
    B, S = h.shape[:2]
    z = h @ w_in
    q_s, k_s, v_s, c_q, c_kv, k_r = jnp.split(z, list(np.cumsum(ODD_SPLITS)[:-1]), axis=-1)
    y_c = swa_sink_attention(q_s.reshape(B, S, SWA_HEADS, HEAD_DIM),
                             k_s.reshape(B, S, SWA_KV_HEADS, HEAD_DIM),
                             v_s.reshape(B, S, SWA_KV_HEADS, HEAD_DIM),
                             swa_q_g, swa_k_g, swa_sinks)
    y_d = mla_attention(c_q, c_kv, k_r, positions, mla_q_norm_g, mla_kv_norm_g,
                        mla_w_uq, mla_w_ukv, mla_q_g, mla_k_g)
    return jnp.concatenate([y_c, y_d], axis=-1) @ w_out


def hier_moe(h, w_group, b_group, w_expert, b_expert, w_gate, w_up, w_down):
    B, S, _ = h.shape
    g_logits = (h @ w_group).astype(jnp.float32) + b_group
    p_group = jax.nn.softmax(g_logits, axis=-1)
    g_w, g_idx = lax.top_k(p_group, 1)
    e_logits = ((h @ w_expert).astype(jnp.float32) + b_expert).reshape(
        B, S, N_GROUPS, EXPERTS_PER_GROUP)
    sel = jnp.take_along_axis(e_logits, g_idx[..., None], axis=2)[:, :, 0]
    p_exp = jax.nn.softmax(sel, axis=-1)
    top_v, top_i = lax.top_k(p_exp, TOP_K)
    weights = g_w * top_v / jnp.sum(top_v, axis=-1, keepdims=True)
    global_idx = g_idx * EXPERTS_PER_GROUP + top_i
    gates = jnp.sum(jax.nn.one_hot(global_idx, N_EXPERTS, dtype=jnp.float32)
                    * weights[..., None], axis=-2).astype(h.dtype)

    def expert(acc, xs):
        wg, wu, wd, g = xs
        y = (jax.nn.silu(h @ wg) * (h @ wu)) @ wd
        return acc + g[..., None] * y, None

    out, _ = lax.scan(expert, jnp.zeros_like(h), (w_gate, w_up, w_down, jnp.moveaxis(gates, -1, 0)))
    return out


def setup_inputs(seed: int = 0) -> dict:
    key = jax.random.key(seed)
    ks = iter(jax.random.split(key, 40))
    f32 = jnp.float32

    def nrm(shape, scale):
        return jax.random.normal(next(ks), shape, f32) * scale

    def gain(shape):
        return 1.0 + 0.1 * jax.random.normal(next(ks), shape, f32)

    D = D_MODEL
    return {
        "x": nrm((BATCH, SEQ, D), 1.0),
        "c": nrm((BATCH, D), 1.0),
        "positions": jnp.broadcast_to(jnp.arange(SEQ, dtype=jnp.int32), (BATCH, SEQ)),
        "ada_w": nrm((DEPTH, D, 6 * D), 0.5 * D ** -0.5),
        "ada_b": nrm((DEPTH, 6 * D), 0.02),
        "norm1_g": gain((DEPTH, D)),
        "norm2_g": gain((DEPTH, D)),
        "cp_w_in": nrm((N_EVEN, D, EVEN_IN), D ** -0.5),
        "conv_w": nrm((N_EVEN, CONV_K, CONV_CH), CONV_K ** -0.5),
        "pool_w": nrm((N_EVEN, len(POOL_WINDOWS), POOL_GROUP_CH, POOL_GROUP_CH), POOL_GROUP_CH ** -0.5),
        "pool_scale": gain((N_EVEN, POOL_CH)),
        "cp_w_out": nrm((N_EVEN, EVEN_MIX, D), EVEN_MIX ** -0.5),
        "at_w_in": nrm((N_ODD, D, ODD_IN), D ** -0.5),
        "swa_q_g": gain((N_ODD, HEAD_DIM)),
        "swa_k_g": gain((N_ODD, HEAD_DIM)),
        "swa_sinks": nrm((N_ODD, SWA_HEADS), 0.5),
        "mla_q_norm_g": gain((N_ODD, Q_LORA)),
        "mla_kv_norm_g": gain((N_ODD, KV_LORA)),
        "mla_w_uq": nrm((N_ODD, Q_LORA, MLA_HEADS * MLA_QK), Q_LORA ** -0.5),
        "mla_w_ukv": nrm((N_ODD, KV_LORA, MLA_HEADS * (MLA_NOPE + MLA_V)), KV_LORA ** -0.5),
        "mla_q_g": gain((N_ODD, MLA_QK)),
        "mla_k_g": gain((N_ODD, MLA_QK)),
        "at_w_out": nrm((N_ODD, ODD_MIX, D), ODD_MIX ** -0.5),
        "moe_w_group": nrm((DEPTH, D, N_GROUPS), D ** -0.5),
        "moe_b_group": nrm((DEPTH, N_GROUPS), 0.01),
        "moe_w_expert": nrm((DEPTH, D, N_EXPERTS), D ** -0.5),
        "moe_b_expert": nrm((DEPTH, N_EXPERTS), 0.01),
        "moe_w_gate": nrm((DEPTH, N_EXPERTS, D, EXPERT_FF), D ** -0.5),
        "moe_w_up": nrm((DEPTH, N_EXPERTS, D, EXPERT_FF), D ** -0.5),
        "moe_w_down": nrm((DEPTH, N_EXPERTS, EXPERT_FF, D), EXPERT_FF ** -0.5),
    }


def reference(x, c, positions, ada_w, ada_b, norm1_g, norm2_g, cp_w_in, conv_w, pool_w,
              pool_scale, cp_w_out, at_w_in, swa_q_g, swa_k_g, swa_sinks, mla_q_norm_g,
              mla_kv_norm_g, mla_w_uq, mla_w_ukv, mla_q_g, mla_k_g, at_w_out, moe_w_group,
              moe_b_group, moe_w_expert, moe_b_expert, moe_w_gate, moe_w_up, moe_w_down):
    c_act = jax.nn.silu(c)
    for l in range(DEPTH):
        mod = (c_act @ ada_w[l] + ada_b[l])[:, None, :]
        shift1, scale1, gate1, shift2, scale2, gate2 = jnp.split(mod, 6, axis=-1)
        h = rms_norm(x, norm1_g[l]) * (1 + scale1) + shift1
        i = l // 2
        if l % 2 == 0:
            y = conv_pool_mixer(h, cp_w_in[i], conv_w[i], pool_w[i], pool_scale[i], cp_w_out[i])
        else:
            y = attention_mixer(h, positions, at_w_in[i], swa_q_g[i], swa_k_g[i], swa_sinks[i],
                                mla_q_norm_g[i], mla_kv_norm_g[i], mla_w_uq[i], mla_w_ukv[i],
                                mla_q_g[i], mla_k_g[i], at_w_out[i])
        x = x + gate1 * y
        h = rms_norm(x, norm2_g[l]) * (1 + scale2) + shift2
        x = x + gate2 * hier_moe(h, moe_w_group[l], moe_b_group[l], moe_w_expert[l],
                                 moe_b_expert[l], moe_w_gate[l], moe_w_up[l], moe_w_down[l])
    return x
```

```python
import functools

import numpy as np
import jax
import jax.numpy as jnp
from jax import lax
from jax.experimental import pallas as pl
from jax.experimental.pallas import tpu as pltpu

F32 = jnp.float32
BF16 = jnp.bfloat16

D = 1024
SEQ = 16384
DEPTH = 4
EPS = 1e-6
LANES = 128
CONV_CH = 512
POOL_WINDOWS = (2, 4, 8, 16)
POOL_G = 128
HALO = 16
SWA_HEADS = 8
SWA_KV = 2
HEAD_DIM = 64
WINDOW = 128
MLA_HEADS = 8
MLA_NOPE = 64
MLA_ROPE = 32
MLA_QK = MLA_NOPE + MLA_ROPE
MLA_V = 64
Q_LORA = 384
KV_LORA = 256
ROPE_THETA = 10000.0
N_GROUPS = 4
EPG = 8
N_EXPERTS = N_GROUPS * EPG
FF = 256
NEG = -1e30

T_TOK = 512
T_ATT = 512
T_MOE = 1024
VMEM_LIMIT = 48 * 1024 * 1024

O_QS, O_KS, O_VS, O_CQ, O_CKV, O_KR, ODD_W = 0, 1024, 1280, 1536, 1920, 2176, 2304


def _cparams(n_axes=1):
    return pltpu.CompilerParams(dimension_semantics=("arbitrary",) * n_axes,
                                vmem_limit_bytes=VMEM_LIMIT)


def _rms(x):
    return x * lax.rsqrt(jnp.mean(x * x, axis=-1, keepdims=True) + EPS)


def _dot(a, b):
    return jnp.dot(a, b, preferred_element_type=F32)


def _dot_nt(a, b):
    return lax.dot_general(a, b, (((1,), (1,)), ((), ())), preferred_element_type=F32)


def _ada_kernel(c_ref, w_ref, b_ref, o_ref):
    c = c_ref[...]
    ca = c * jax.nn.sigmoid(c)
    o_ref[0] = jnp.sum(w_ref[0] * ca, axis=0, keepdims=True) + b_ref[0]


def _ada_mod(c, ada_w, ada_b):
    c_col = c.reshape(D, 1)
    b = ada_b.reshape(DEPTH * 6, 1, D)
    out = pl.pallas_call(
        _ada_kernel,
        out_shape=jax.ShapeDtypeStruct((DEPTH * 6, 1, D), F32),
        grid=(DEPTH, 6),
        in_specs=[pl.BlockSpec((D, 1), lambda l, j: (0, 0)),
                  pl.BlockSpec((1, D, D), lambda l, j: (l, 0, j)),
                  pl.BlockSpec((1, 1, D), lambda l, j: (l * 6 + j, 0, 0))],
        out_specs=pl.BlockSpec((1, 1, D), lambda l, j: (l * 6 + j, 0, 0)),
        compiler_params=_cparams(2),
        name="ada_mod",
    )(c_col, ada_w, b)
    mod = out.reshape(DEPTH, 6, D)
    return jnp.pad(mod, ((0, 0), (0, 2), (0, 0)))


def _rope_kernel(pos_ref, inv_ref, c_ref, s1_ref, s2_ref):
    pos = pos_ref[...].astype(F32)
    ang = pos * inv_ref[...]
    lane = lax.broadcasted_iota(jnp.int32, ang.shape, 1)
    cs = jnp.cos(ang)
    sn = jnp.sin(ang)
    c_ref[...] = jnp.where(lane < 64, 1.0, jnp.where(lane < 96, cs, 0.0))
    s1_ref[...] = jnp.where((lane >= 64) & (lane < 80), -sn, 0.0)
    s2_ref[...] = jnp.where((lane >= 80) & (lane < 96), sn, 0.0)


def _rope_tables(positions):
    half = MLA_ROPE // 2
    inv = jnp.power(ROPE_THETA, -jnp.arange(half, dtype=F32) / half)
    inv_lane = jnp.concatenate([jnp.zeros((64,), F32), inv, inv, jnp.zeros((32,), F32)]).reshape(1, LANES)
    pos = positions.reshape(SEQ, 1)
    shp = jax.ShapeDtypeStruct((SEQ, LANES), F32)
    spec = pl.BlockSpec((T_TOK, LANES), lambda i: (i, 0))
    return pl.pallas_call(
        _rope_kernel,
        out_shape=(shp, shp, shp),
        grid=(SEQ // T_TOK,),
        in_specs=[pl.BlockSpec((T_TOK, 1), lambda i: (i, 0)),
                  pl.BlockSpec((1, LANES), lambda i: (0, 0))],
        out_specs=(spec, spec, spec),
        compiler_params=_cparams(1),
        name="rope_tables",
    )(pos, inv_lane)


def _route(lg):
    lane = lax.broadcasted_iota(jnp.int32, lg.shape, 1)
    lane_f = lane.astype(F32)
    is_g = (lane >= N_EXPERTS) & (lane < N_EXPERTS + N_GROUPS)
    gl = jnp.where(is_g, lg, NEG)
    gmax = jnp.max(gl, axis=-1, keepdims=True)
    gidx = jnp.min(jnp.where(is_g & (gl == gmax), lane_f - N_EXPERTS, 1e3), axis=-1, keepdims=True)
    gsum = jnp.sum(jnp.where(is_g, jnp.exp(gl - gmax), 0.0), axis=-1, keepdims=True)
    gw = 1.0 / gsum
    grp_of_lane = (lane >> 3).astype(F32)
    in_grp = (lane < N_EXPERTS) & (grp_of_lane == gidx)
    el = jnp.where(in_grp, lg, NEG)
    m1 = jnp.max(el, axis=-1, keepdims=True)
    i1 = jnp.min(jnp.where(in_grp & (el == m1), lane_f, 1e3), axis=-1, keepdims=True)
    rest = in_grp & (lane_f != i1)
    el2 = jnp.where(rest, lg, NEG)
    m2 = jnp.max(el2, axis=-1, keepdims=True)
    i2 = jnp.min(jnp.where(rest & (el2 == m2), lane_f, 1e3), axis=-1, keepdims=True)
    r = jnp.exp(m2 - m1)
    w1 = gw / (1.0 + r)
    w2 = w1 * r
    return jnp.where(lane_f == i1, w1, jnp.where(lane_f == i2, w2, 0.0))


def _tail(x, y, mod, n2, wr_hi, wr_lo, br, xo_ref, h2_ref, gates_ref):
    gate1 = mod[2:3]
    shift2, scale2 = mod[3:4], mod[4:5]
    xn = x + gate1 * y
    xo_ref[...] = xn
    h2 = _rms(xn) * n2 * (1.0 + scale2) + shift2
    hi = h2.astype(BF16)
    h2_ref[...] = hi
    lo = (h2 - hi.astype(F32)).astype(BF16)
    lg = _dot(hi, wr_hi) + _dot(lo, wr_hi) + _dot(hi, wr_lo) + br
    gates_ref[...] = _route(lg)


def _even_kernel(x_ref, xh_ref, mod_ref, n1_ref, n2_ref, win_ref, cw_ref, pw_ref, ps_ref,
                 wout_ref, wrh_ref, wrl_ref, br_ref, xo_ref, h2_ref, gates_ref):
    i = pl.program_id(0)
    x = x_ref[...]
    mod = mod_ref[...]
    shift1, scale1 = mod[0:1], mod[1:2]
    xa = jnp.concatenate([xh_ref[...], x], axis=0)
    h = _rms(xa) * n1_ref[...] * (1.0 + scale1) + shift1
    z = _dot(h.astype(BF16), win_ref[...])
    rows = T_TOK + HALO
    row = lax.broadcasted_iota(jnp.int32, (rows, 1), 0)
    tpos = i * T_TOK + row - HALO
    live = (tpos >= 0).astype(F32)
    bg = z[:, 0:CONV_CH]
    v = z[:, CONV_CH:2 * CONV_CH] * z[:, 2 * CONV_CH:3 * CONV_CH] * live
    cw = cw_ref[...]
    conv = v * cw[0:1] + pltpu.roll(v, 1, 0) * cw[1:2] + pltpu.roll(v, 2, 0) * cw[2:3]
    parts = [(bg * conv)[HALO:]]
    ps = ps_ref[...]
    tcount = (tpos + 1).astype(F32)
    for gi, w in enumerate(POOL_WINDOWS):
        ug = z[:, 3 * CONV_CH + gi * POOL_G: 3 * CONV_CH + (gi + 1) * POOL_G] * live
        s = ug
        k = 1
        while k < w:
            s = s + pltpu.roll(s, k, 0)
            k *= 2
        inv = 1.0 / jnp.minimum(tcount, float(w))
        d = (s * inv - ug)[HALO:].astype(BF16)
        parts.append(_dot(d, pw_ref[gi]) * ps[:, gi * POOL_G:(gi + 1) * POOL_G])
    cat = jnp.concatenate(parts, axis=-1).astype(BF16)
    y = _dot(cat, wout_ref[...])
    _tail(x, y, mod, n2_ref[...], wrh_ref[...], wrl_ref[...], br_ref[...], xo_ref, h2_ref, gates_ref)


def _tail_out(n):
    shapes = (jax.ShapeDtypeStruct((SEQ, D), F32), jax.ShapeDtypeStruct((SEQ, D), BF16),
              jax.ShapeDtypeStruct((SEQ, LANES), F32))
    specs = (pl.BlockSpec((n, D), lambda i: (i, 0)), pl.BlockSpec((n, D), lambda i: (i, 0)),
             pl.BlockSpec((n, LANES), lambda i: (i, 0)))
    return shapes, specs


def _full(shape):
    nd = len(shape)
    return pl.BlockSpec(shape, lambda i: (0,) * nd)


def _even_layer(x, mod, n1, n2, w_in, conv_w, pool_w, pool_scale, w_out, wr_hi, wr_lo, br):
    shapes, specs = _tail_out(T_TOK)
    hb = T_TOK // HALO
    return pl.pallas_call(
        _even_kernel,
        out_shape=shapes,
        grid=(SEQ // T_TOK,),
        in_specs=[pl.BlockSpec((T_TOK, D), lambda i: (i, 0)),
                  pl.BlockSpec((HALO, D), lambda i: (jnp.maximum(i * hb - 1, 0), 0)),
                  _full((8, D)), _full((1, D)), _full((1, D)),
                  _full((D, 4 * CONV_CH)), _full((3, CONV_CH)), _full((4, POOL_G, POOL_G)),
                  _full((1, 4 * POOL_G)), _full((D, D)),
                  _full((D, LANES)), _full((D, LANES)), _full((1, LANES))],
        out_specs=specs,
        compiler_params=_cparams(1),
        name="even_mixer",
    )(x, x, mod, n1, n2, w_in, conv_w, pool_w, pool_scale, w_out, wr_hi, wr_lo, br)


def _proj_kernel(x_ref, mod_ref, n1_ref, win_ref, gsq_ref, gsk_ref, gqn_ref, gkvn_ref,
                 wuq_ref, wuk_ref, wuv_ref, gmq_ref, gmk_ref, c_ref, s1_ref, s2_ref,
                 qs_ref, ks_ref, vs_ref, qm_ref, km_ref, vm_ref):
    x = x_ref[...]
    mod = mod_ref[...]
    shift1, scale1 = mod[0:1], mod[1:2]
    h = _rms(x) * n1_ref[...] * (1.0 + scale1) + shift1
    z = _dot(h.astype(BF16), win_ref[...])

    def head_norm(t, g, dim):
        ms = jnp.sum(t * t, axis=-1, keepdims=True) * (1.0 / dim)
        return t * lax.rsqrt(ms + EPS) * g

    gsq, gsk = gsq_ref[...], gsk_ref[...]
    for hd in range(SWA_HEADS):
        qh = head_norm(z[:, O_QS + hd * LANES: O_QS + (hd + 1) * LANES], gsq, HEAD_DIM)
        qs_ref[:, hd * LANES:(hd + 1) * LANES] = (qh * (HEAD_DIM ** -0.5)).astype(BF16)
    for kv in range(SWA_KV):
        kh = head_norm(z[:, O_KS + kv * LANES: O_KS + (kv + 1) * LANES], gsk, HEAD_DIM)
        ks_ref[:, kv * LANES:(kv + 1) * LANES] = kh.astype(BF16)
    vs_ref[...] = z[:, O_VS:O_CQ].astype(BF16)

    cq = (_rms(z[:, O_CQ:O_CKV]) * gqn_ref[...]).astype(BF16)
    ckv = (_rms(z[:, O_CKV:O_KR]) * gkvn_ref[...]).astype(BF16)
    qm = _dot(cq, wuq_ref[...])
    kn = _dot(ckv, wuk_ref[...])
    vm_ref[...] = _dot(ckv, wuv_ref[...]).astype(BF16)
    kr = z[:, O_KR:ODD_W]
    cs, s1, s2 = c_ref[...], s1_ref[...], s2_ref[...]

    def rope(t):
        return t * cs + pltpu.roll(t, LANES - 16, 1) * s1 + pltpu.roll(t, 16, 1) * s2

    gmq, gmk = gmq_ref[...], gmk_ref[...]
    for hd in range(MLA_HEADS):
        sl = slice(hd * LANES, (hd + 1) * LANES)
        qh = rope(head_norm(qm[:, sl], gmq, MLA_QK))
        qm_ref[:, sl] = (qh * (MLA_QK ** -0.5)).astype(BF16)
        kh = rope(head_norm(kn[:, sl] + kr, gmk, MLA_QK))
        km_ref[:, sl] = kh.astype(BF16)


def _proj_layer(x, mod, n1, w_in, gsq, gsk, gqn, gkvn, wuq, wuk, wuv, gmq, gmk, tc, ts1, ts2):
    def tok(wd):
        return pl.BlockSpec((T_TOK, wd), lambda i: (i, 0))
    widths = (1024, 256, 256, 1024, 1024, 512)
    return pl.pallas_call(
        _proj_kernel,
        out_shape=tuple(jax.ShapeDtypeStruct((SEQ, wd), BF16) for wd in widths),
        grid=(SEQ // T_TOK,),
        in_specs=[tok(D), _full((8, D)), _full((1, D)), _full((D, ODD_W)),
                  _full((1, LANES)), _full((1, LANES)), _full((1, Q_LORA)), _full((1, KV_LORA)),
                  _full((Q_LORA, 1024)), _full((KV_LORA, 1024)), _full((KV_LORA, 512)),
                  _full((1, LANES)), _full((1, LANES)), tok(LANES), tok(LANES), tok(LANES)],
        out_specs=tuple(tok(wd) for wd in widths),
        compiler_params=_cparams(1),
        name="odd_proj",
    )(x, mod, n1, w_in, gsq, gsk, gqn, gkvn, wuq, wuk, wuv, gmq, gmk, tc, ts1, ts2)


def _swa_kernel(sink_ref, q_ref, k_ref, kh_ref, v_ref, vh_ref, o_ref):
    i = pl.program_id(0)
    kcat = jnp.concatenate([kh_ref[...], k_ref[...]], axis=0)
    vcat = jnp.concatenate([vh_ref[...], v_ref[...]], axis=0)
    r = lax.broadcasted_iota(jnp.int32, (WINDOW, 2 * WINDOW), 0)
    c = lax.broadcasted_iota(jnp.int32, (WINDOW, 2 * WINDOW), 1)
    rel = WINDOW + r - c
    lane = lax.broadcasted_iota(jnp.int32, (2 * WINDOW, LANES), 1)
    for sb in range(T_ATT // WINDOW):
        kb = kcat[sb * WINDOW: sb * WINDOW + 2 * WINDOW]
        vb = vcat[sb * WINDOW: sb * WINDOW + 2 * WINDOW]
        kpos = i * T_ATT + (sb - 1) * WINDOW + c
        ok = (rel >= 0) & (rel < WINDOW) & (kpos >= 0)
        for j in range(SWA_HEADS // 2):
            kv = (2 * j) // (SWA_HEADS // SWA_KV)
            kk = kb[:, kv * LANES:(kv + 1) * LANES]
            vv = vb[:, kv * LANES:(kv + 1) * LANES]
            vsel = (jnp.where(lane < 64, vv, jnp.zeros_like(vv)),
                    jnp.where(lane >= 64, vv, jnp.zeros_like(vv)))
            out = None
            for hh in range(2):
                hd = 2 * j + hh
                q = q_ref[sb * WINDOW:(sb + 1) * WINDOW, hd * LANES:(hd + 1) * LANES]
                s = jnp.where(ok, _dot_nt(q, kk), NEG)
                sink = sink_ref[hd]
                m = jnp.maximum(jnp.max(s, axis=-1, keepdims=True), sink)
                e = jnp.exp(s - m)
                den = jnp.sum(e, axis=-1, keepdims=True) + jnp.exp(sink - m)
                p = (e / den).astype(BF16)
                dd = _dot(p, vsel[hh])
                out = dd if out is None else out + dd
            o_ref[sb * WINDOW:(sb + 1) * WINDOW, j * LANES:(j + 1) * LANES] = out.astype(BF16)


def _swa_layer(sinks, qs, ks, vs):
    hb = T_ATT // WINDOW
    return pl.pallas_call(
        _swa_kernel,
        out_shape=jax.ShapeDtypeStruct((SEQ, 512), BF16),
        grid=(SEQ // T_ATT,),
        in_specs=[pl.BlockSpec(memory_space=pltpu.SMEM),
                  pl.BlockSpec((T_ATT, 1024), lambda i: (i, 0)),
                  pl.BlockSpec((T_ATT, 256), lambda i: (i, 0)),
                  pl.BlockSpec((WINDOW, 256), lambda i: (jnp.maximum(i * hb - 1, 0), 0)),
                  pl.BlockSpec((T_ATT, 256), lambda i: (i, 0)),
                  pl.BlockSpec((WINDOW, 256), lambda i: (jnp.maximum(i * hb - 1, 0), 0))],
        out_specs=pl.BlockSpec((T_ATT, 512), lambda i: (i, 0)),
        compiler_params=_cparams(1),
        name="swa_attn",
    )(sinks, qs, ks, ks, vs, vs)


def _mla_kernel(qi_ref, ki_ref, q_ref, k_ref, v_ref, o_ref, m_ref, l_ref, acc_ref):
    step = pl.program_id(0)
    qi = qi_ref[step]
    ki = ki_ref[step]

    @pl.when(ki == 0)
    def _():
        m_ref[...] = jnp.full(m_ref.shape, NEG, F32)
        l_ref[...] = jnp.zeros(l_ref.shape, F32)
        acc_ref[...] = jnp.zeros(acc_ref.shape, F32)

    lane = lax.broadcasted_iota(jnp.int32, (T_ATT, LANES), 1)
    lo = lane < 64

    def update(masked):
        if masked:
            r = lax.broadcasted_iota(jnp.int32, (T_ATT, T_ATT), 0)
            c = lax.broadcasted_iota(jnp.int32, (T_ATT, T_ATT), 1)
            causal = r >= c
        for j in range(MLA_HEADS // 2):
            vv = v_ref[:, j * LANES:(j + 1) * LANES]
            vsel = (jnp.where(lo, vv, jnp.zeros_like(vv)), jnp.where(lo, jnp.zeros_like(vv), vv))
            alphas = []
            pv = None
            for hh in range(2):
                hd = 2 * j + hh
                sl = slice(hd * LANES, (hd + 1) * LANES)
                s = _dot_nt(q_ref[:, sl], k_ref[:, sl])
                if masked:
                    s = jnp.where(causal, s, NEG)
                m_prev = m_ref[hd]
                m_new = jnp.maximum(m_prev, jnp.max(s, axis=-1, keepdims=True))
                alpha = jnp.exp(m_prev - m_new)
                p = jnp.exp(s - m_new)
                l_ref[hd] = alpha * l_ref[hd] + jnp.sum(p, axis=-1, keepdims=True)
                m_ref[hd] = m_new
                alphas.append(alpha)
                dd = _dot(p.astype(BF16), vsel[hh])
                pv = dd if pv is None else pv + dd
            acc_ref[j] = acc_ref[j] * jnp.where(lo, alphas[0], alphas[1]) + pv

    @pl.when(ki < qi)
    def _():
        update(False)

    @pl.when(ki == qi)
    def _():
        update(True)
        for j in range(MLA_HEADS // 2):
            inv = jnp.where(lo, 1.0 / l_ref[2 * j], 1.0 / l_ref[2 * j + 1])
            o_ref[:, j * LANES:(j + 1) * LANES] = (acc_ref[j] * inv).astype(BF16)


def _mla_layer(qm, km, vm):
    nb = SEQ // T_ATT
    qi = np.concatenate([np.full(n + 1, n, np.int32) for n in range(nb)])
    ki = np.concatenate([np.arange(n + 1, dtype=np.int32) for n in range(nb)])
    grid_spec = pltpu.PrefetchScalarGridSpec(
        num_scalar_prefetch=2,
        grid=(int(qi.shape[0]),),
        in_specs=[pl.BlockSpec((T_ATT, 1024), lambda s, qi, ki: (qi[s], 0)),
                  pl.BlockSpec((T_ATT, 1024), lambda s, qi, ki: (ki[s], 0)),
                  pl.BlockSpec((T_ATT, 512), lambda s, qi, ki: (ki[s], 0))],
        out_specs=pl.BlockSpec((T_ATT, 512), lambda s, qi, ki: (qi[s], 0)),
        scratch_shapes=[pltpu.VMEM((MLA_HEADS, T_ATT, 1), F32),
                        pltpu.VMEM((MLA_HEADS, T_ATT, 1), F32),
                        pltpu.VMEM((MLA_HEADS // 2, T_ATT, LANES), F32)],
    )
    return pl.pallas_call(
        _mla_kernel,
        out_shape=jax.ShapeDtypeStruct((SEQ, 512), BF16),
        grid_spec=grid_spec,
        compiler_params=_cparams(1),
        name="mla_attn",
    )(jnp.asarray(qi), jnp.asarray(ki), qm, km, vm)


def _post_kernel(x_ref, os_ref, om_ref, mod_ref, n2_ref, wout_ref, wrh_ref, wrl_ref, br_ref,
                 xo_ref, h2_ref, gates_ref):
    y = _dot(os_ref[...], wout_ref[0:512, :]) + _dot(om_ref[...], wout_ref[512:1024, :])
    _tail(x_ref[...], y, mod_ref[...], n2_ref[...], wrh_ref[...], wrl_ref[...], br_ref[...],
          xo_ref, h2_ref, gates_ref)


def _post_layer(x, o_s, o_m, mod, n2, w_out, wr_hi, wr_lo, br):
    shapes, specs = _tail_out(T_TOK)
    return pl.pallas_call(
        _post_kernel,
        out_shape=shapes,
        grid=(SEQ // T_TOK,),
        in_specs=[pl.BlockSpec((T_TOK, D), lambda i: (i, 0)),
                  pl.BlockSpec((T_TOK, 512), lambda i: (i, 0)),
                  pl.BlockSpec((T_TOK, 512), lambda i: (i, 0)),
                  _full((8, D)), _full((1, D)), _full((D, D)),
                  _full((D, LANES)), _full((D, LANES)), _full((1, LANES))],
        out_specs=specs,
        compiler_params=_cparams(1),
        name="odd_post",
    )(x, o_s, o_m, mod, n2, w_out, wr_hi, wr_lo, br)


def _moe_kernel(x_ref, h_ref, g_ref, mod_ref, wg_ref, wu_ref, wd_ref, o_ref, acc_ref):
    e = pl.program_id(1)

    @pl.when(e == 0)
    def _():
        acc_ref[...] = jnp.zeros(acc_ref.shape, F32)

    h = h_ref[...]
    a = _dot(h, wg_ref[0])
    u = _dot(h, wu_ref[0])
    act = (a * jax.nn.sigmoid(a) * u).astype(BF16)
    y = _dot(act, wd_ref[0])
    gates = g_ref[...]
    lane = lax.broadcasted_iota(jnp.int32, gates.shape, 1)
    gcol = jnp.sum(jnp.where(lane == e, gates, 0.0), axis=-1, keepdims=True)
    acc_ref[...] += gcol * y

    @pl.when(e == N_EXPERTS - 1)
    def _():
        o_ref[...] = x_ref[...] + mod_ref[5:6, :] * acc_ref[...]


def _moe_layer(x, h2, gates, mod, wg, wu, wd):
    return pl.pallas_call(
        _moe_kernel,
        out_shape=jax.ShapeDtypeStruct((SEQ, D), F32),
        grid=(SEQ // T_MOE, N_EXPERTS),
        in_specs=[pl.BlockSpec((T_MOE, D), lambda i, e: (i, 0)),
                  pl.BlockSpec((T_MOE, D), lambda i, e: (i, 0)),
                  pl.BlockSpec((T_MOE, LANES), lambda i, e: (i, 0)),
                  pl.BlockSpec((8, D), lambda i, e: (0, 0)),
                  pl.BlockSpec((1, D, FF), lambda i, e: (e, 0, 0)),
                  pl.BlockSpec((1, D, FF), lambda i, e: (e, 0, 0)),
                  pl.BlockSpec((1, FF, D), lambda i, e: (e, 0, 0))],
        out_specs=pl.BlockSpec((T_MOE, D), lambda i, e: (i, 0)),
        scratch_shapes=[pltpu.VMEM((T_MOE, D), F32)],
        compiler_params=_cparams(2),
        name="moe_experts",
    )(x, h2, gates, mod, wg, wu, wd)


def _pad_heads(w, heads, dim):
    k = w.shape[0]
    w = w.reshape(k, heads, dim)
    return jnp.pad(w, ((0, 0), (0, 0), (0, LANES - dim))).reshape(k, heads * LANES)


def _pad_gain(g):
    return jnp.pad(g, (0, LANES - g.shape[0])).reshape(1, LANES)


def _odd_weights(w_in):
    q_s = _pad_heads(w_in[:, 0:512], SWA_HEADS, HEAD_DIM)
    k_s = _pad_heads(w_in[:, 512:640], SWA_KV, HEAD_DIM)
    v = w_in[:, 640:768]
    v_s = jnp.concatenate([v[:, 0:64], v[:, 0:64], v[:, 64:128], v[:, 64:128]], axis=1)
    c_q = w_in[:, 768:1152]
    c_kv = w_in[:, 1152:1408]
    k_r = jnp.pad(w_in[:, 1408:1440], ((0, 0), (MLA_NOPE, LANES - MLA_QK)))
    return jnp.concatenate([q_s, k_s, v_s, c_q, c_kv, k_r], axis=1).astype(BF16)


def _router_weights(w_group, b_group, w_expert, b_expert):
    w = jnp.pad(jnp.concatenate([w_expert, w_group], axis=1), ((0, 0), (0, LANES - N_EXPERTS - N_GROUPS)))
    hi = w.astype(BF16)
    lo = (w - hi.astype(F32)).astype(BF16)
    b = jnp.pad(jnp.concatenate([b_expert, b_group]), (0, LANES - N_EXPERTS - N_GROUPS)).reshape(1, LANES)
    return hi, lo, b


def kernel(x, c, positions, ada_w, ada_b, norm1_g, norm2_g, cp_w_in, conv_w, pool_w, pool_scale,
           cp_w_out, at_w_in, swa_q_g, swa_k_g, swa_sinks, mla_q_norm_g, mla_kv_norm_g, mla_w_uq,
           mla_w_ukv, mla_q_g, mla_k_g, at_w_out, moe_w_group, moe_b_group, moe_w_expert,
           moe_b_expert, moe_w_gate, moe_w_up, moe_w_down):
    xs = x.reshape(SEQ, D)
    mods = _ada_mod(c, ada_w, ada_b)
    tc, ts1, ts2 = _rope_tables(positions)
    for l in range(DEPTH):
        i = l // 2
        mod = mods[l]
        n1 = norm1_g[l].reshape(1, D)
        n2 = norm2_g[l].reshape(1, D)
        wr_hi, wr_lo, br = _router_weights(moe_w_group[l], moe_b_group[l], moe_w_expert[l], moe_b_expert[l])
        if l % 2 == 0:
            xs, h2, gates = _even_layer(
                xs, mod, n1, n2, cp_w_in[i].astype(BF16), conv_w[i], pool_w[i].astype(BF16),
                pool_scale[i].reshape(1, 4 * POOL_G), cp_w_out[i].astype(BF16), wr_hi, wr_lo, br)
        else:
            ukv = mla_w_ukv[i].reshape(KV_LORA, MLA_HEADS, MLA_NOPE + MLA_V)
            wuk = _pad_heads(ukv[:, :, :MLA_NOPE].reshape(KV_LORA, MLA_HEADS * MLA_NOPE), MLA_HEADS, MLA_NOPE)
            wuv = ukv[:, :, MLA_NOPE:].reshape(KV_LORA, MLA_HEADS * MLA_V)
            wuq = _pad_heads(mla_w_uq[i], MLA_HEADS, MLA_QK)
            qs, ks, vs, qm, km, vm = _proj_layer(
                xs, mod, n1, _odd_weights(at_w_in[i]), _pad_gain(swa_q_g[i]), _pad_gain(swa_k_g[i]),
                mla_q_norm_g[i].reshape(1, Q_LORA), mla_kv_norm_g[i].reshape(1, KV_LORA),
                wuq.astype(BF16), wuk.astype(BF16), wuv.astype(BF16),
                _pad_gain(mla_q_g[i]), _pad_gain(mla_k_g[i]), tc, ts1, ts2)
            o_s = _swa_layer(swa_sinks[i], qs, ks, vs)
            o_m = _mla_layer(qm, km, vm)
            xs, h2, gates = _post_layer(xs, o_s, o_m, mod, n2, at_w_out[i].astype(BF16), wr_hi, wr_lo, br)
        xs = _moe_layer(xs, h2, gates, mod, moe_w_gate[l].astype(BF16), moe_w_up[l].astype(BF16),
                        moe_w_down[l].astype(BF16))
    return xs.reshape(1, SEQ, D)
```

```python
import functools

import numpy as np
import jax
import jax.numpy as jnp
from jax import lax
from jax.experimental import pallas as pl
from jax.experimental.pallas import tpu as pltpu

F32 = jnp.float32
BF16 = jnp.bfloat16

D = 1024
SEQ = 16384
DEPTH = 4
EPS = 1e-6
LANES = 128
CONV_CH = 512
POOL_WINDOWS = (2, 4, 8, 16)
POOL_G = 128
HALO = 16
SWA_HEADS = 8
SWA_KV = 2
HEAD_DIM = 64
WINDOW = 128
MLA_HEADS = 8
MLA_NOPE = 64
MLA_ROPE = 32
MLA_QK = MLA_NOPE + MLA_ROPE
MLA_V = 64
Q_LORA = 384
KV_LORA = 256
ROPE_THETA = 10000.0
N_GROUPS = 4
EPG = 8
N_EXPERTS = N_GROUPS * EPG
FF = 256
NEG = -1e30
LOG2E = 1.4426950408889634

T_TOK = 512
T_ATT = 512
T_MOE = 1024
VMEM_LIMIT = 48 * 1024 * 1024

O_QS, O_KS, O_VS, O_CQ, O_CKV, O_KR, ODD_W = 0, 1024, 1280, 1536, 1920, 2176, 2304


def _cparams(n_axes=1):
    return pltpu.CompilerParams(dimension_semantics=("arbitrary",) * n_axes,
                                vmem_limit_bytes=VMEM_LIMIT)


def _rms(x):
    return x * lax.rsqrt(jnp.mean(x * x, axis=-1, keepdims=True) + EPS)


def _dot(a, b):
    return jnp.dot(a, b, preferred_element_type=F32)


def _dot_nt(a, b):
    return lax.dot_general(a, b, (((1,), (1,)), ((), ())), preferred_element_type=F32)


def _ada_kernel(c_ref, w_ref, b_ref, o_ref):
    c = c_ref[...]
    ca = c * jax.nn.sigmoid(c)
    o_ref[0] = jnp.sum(w_ref[0] * ca, axis=0, keepdims=True) + b_ref[0]


def _ada_mod(c, ada_w, ada_b):
    c_col = c.reshape(D, 1)
    b = ada_b.reshape(DEPTH * 6, 1, D)
    out = pl.pallas_call(
        _ada_kernel,
        out_shape=jax.ShapeDtypeStruct((DEPTH * 6, 1, D), F32),
        grid=(DEPTH, 6),
        in_specs=[pl.BlockSpec((D, 1), lambda l, j: (0, 0)),
                  pl.BlockSpec((1, D, D), lambda l, j: (l, 0, j)),
                  pl.BlockSpec((1, 1, D), lambda l, j: (l * 6 + j, 0, 0))],
        out_specs=pl.BlockSpec((1, 1, D), lambda l, j: (l * 6 + j, 0, 0)),
        compiler_params=_cparams(2),
        name="ada_mod",
    )(c_col, ada_w, b)
    mod = out.reshape(DEPTH, 6, D)
    return jnp.pad(mod, ((0, 0), (0, 2), (0, 0)))


def _rope_kernel(pos_ref, inv_ref, c_ref, s1_ref, s2_ref):
    pos = pos_ref[...].astype(F32)
    ang = pos * inv_ref[...]
    lane = lax.broadcasted_iota(jnp.int32, ang.shape, 1)
    cs = jnp.cos(ang)
    sn = jnp.sin(ang)
    c_ref[...] = jnp.where(lane < 64, 1.0, jnp.where(lane < 96, cs, 0.0))
    s1_ref[...] = jnp.where((lane >= 64) & (lane < 80), -sn, 0.0)
    s2_ref[...] = jnp.where((lane >= 80) & (lane < 96), sn, 0.0)


def _rope_tables(positions):
    half = MLA_ROPE // 2
    inv = jnp.power(ROPE_THETA, -jnp.arange(half, dtype=F32) / half)
    inv_lane = jnp.concatenate([jnp.zeros((64,), F32), inv, inv, jnp.zeros((32,), F32)]).reshape(1, LANES)
    pos = positions.reshape(SEQ, 1)
    shp = jax.ShapeDtypeStruct((SEQ, LANES), F32)
    spec = pl.BlockSpec((T_TOK, LANES), lambda i: (i, 0))
    return pl.pallas_call(
        _rope_kernel,
        out_shape=(shp, shp, shp),
        grid=(SEQ // T_TOK,),
        in_specs=[pl.BlockSpec((T_TOK, 1), lambda i: (i, 0)),
                  pl.BlockSpec((1, LANES), lambda i: (0, 0))],
        out_specs=(spec, spec, spec),
        compiler_params=_cparams(1),
        name="rope_tables",
    )(pos, inv_lane)


def _route(lg):
    lane = lax.broadcasted_iota(jnp.int32, lg.shape, 1)
    lane_f = lane.astype(F32)
    is_g = (lane >= N_EXPERTS) & (lane < N_EXPERTS + N_GROUPS)
    gl = jnp.where(is_g, lg, NEG)
    gmax = jnp.max(gl, axis=-1, keepdims=True)
    gidx = jnp.min(jnp.where(is_g & (gl == gmax), lane_f - N_EXPERTS, 1e3), axis=-1, keepdims=True)
    gsum = jnp.sum(jnp.where(is_g, jnp.exp(gl - gmax), 0.0), axis=-1, keepdims=True)
    gw = 1.0 / gsum
    grp_of_lane = (lane >> 3).astype(F32)
    in_grp = (lane < N_EXPERTS) & (grp_of_lane == gidx)
    el = jnp.where(in_grp, lg, NEG)
    m1 = jnp.max(el, axis=-1, keepdims=True)
    i1 = jnp.min(jnp.where(in_grp & (el == m1), lane_f, 1e3), axis=-1, keepdims=True)
    rest = in_grp & (lane_f != i1)
    el2 = jnp.where(rest, lg, NEG)
    m2 = jnp.max(el2, axis=-1, keepdims=True)
    i2 = jnp.min(jnp.where(rest & (el2 == m2), lane_f, 1e3), axis=-1, keepdims=True)
    r = jnp.exp(m2 - m1)
    w1 = gw / (1.0 + r)
    w2 = w1 * r
    return jnp.where(lane_f == i1, w1, jnp.where(lane_f == i2, w2, 0.0))


def _tail(x, y, mod, n2, wr_hi, wr_lo, br, xo_ref, h2_ref, gates_ref):
    gate1 = mod[2:3]
    shift2, scale2 = mod[3:4], mod[4:5]
    xn = x + gate1 * y
    xo_ref[...] = xn
    h2 = _rms(xn) * n2 * (1.0 + scale2) + shift2
    hi = h2.astype(BF16)
    h2_ref[...] = hi
    lo = (h2 - hi.astype(F32)).astype(BF16)
    lg = _dot(hi, wr_hi) + _dot(lo, wr_hi) + _dot(hi, wr_lo) + br
    gates_ref[...] = _route(lg)


def _even_kernel(x_ref, xh_ref, mod_ref, n1_ref, n2_ref, win_ref, cw_ref, pw_ref, ps_ref,
                 wout_ref, wrh_ref, wrl_ref, br_ref, xo_ref, h2_ref, gates_ref):
    i = pl.program_id(0)
    x = x_ref[...]
    mod = mod_ref[...]
    shift1, scale1 = mod[0:1], mod[1:2]
    xa = jnp.concatenate([xh_ref[...], x], axis=0)
    h = _rms(xa) * n1_ref[...] * (1.0 + scale1) + shift1
    z = _dot(h.astype(BF16), win_ref[...])
    rows = T_TOK + HALO
    row = lax.broadcasted_iota(jnp.int32, (rows, 1), 0)
    tpos = i * T_TOK + row - HALO
    live = (tpos >= 0).astype(F32)
    bg = z[:, 0:CONV_CH]
    v = z[:, CONV_CH:2 * CONV_CH] * z[:, 2 * CONV_CH:3 * CONV_CH] * live
    cw = cw_ref[...]
    conv = v * cw[0:1] + pltpu.roll(v, 1, 0) * cw[1:2] + pltpu.roll(v, 2, 0) * cw[2:3]
    parts = [(bg * conv)[HALO:]]
    ps = ps_ref[...]
    tcount = (tpos + 1).astype(F32)
    for gi, w in enumerate(POOL_WINDOWS):
        ug = z[:, 3 * CONV_CH + gi * POOL_G: 3 * CONV_CH + (gi + 1) * POOL_G] * live
        s = ug
        k = 1
        while k < w:
            s = s + pltpu.roll(s, k, 0)
            k *= 2
        inv = 1.0 / jnp.minimum(tcount, float(w))
        d = (s * inv - ug)[HALO:].astype(BF16)
        parts.append(_dot(d, pw_ref[gi]) * ps[:, gi * POOL_G:(gi + 1) * POOL_G])
    cat = jnp.concatenate(parts, axis=-1).astype(BF16)
    y = _dot(cat, wout_ref[...])
    _tail(x, y, mod, n2_ref[...], wrh_ref[...], wrl_ref[...], br_ref[...], xo_ref, h2_ref, gates_ref)


def _tail_out(n):
    shapes = (jax.ShapeDtypeStruct((SEQ, D), F32), jax.ShapeDtypeStruct((SEQ, D), BF16),
              jax.ShapeDtypeStruct((SEQ, LANES), F32))
    specs = (pl.BlockSpec((n, D), lambda i: (i, 0)), pl.BlockSpec((n, D), lambda i: (i, 0)),
             pl.BlockSpec((n, LANES), lambda i: (i, 0)))
    return shapes, specs


def _full(shape):
    nd = len(shape)
    return pl.BlockSpec(shape, lambda i: (0,) * nd)


def _even_layer(x, mod, n1, n2, w_in, conv_w, pool_w, pool_scale, w_out, wr_hi, wr_lo, br):
    shapes, specs = _tail_out(T_TOK)
    hb = T_TOK // HALO
    return pl.pallas_call(
        _even_kernel,
        out_shape=shapes,
        grid=(SEQ // T_TOK,),
        in_specs=[pl.BlockSpec((T_TOK, D), lambda i: (i, 0)),
                  pl.BlockSpec((HALO, D), lambda i: (jnp.maximum(i * hb - 1, 0), 0)),
                  _full((8, D)), _full((1, D)), _full((1, D)),
                  _full((D, 4 * CONV_CH)), _full((3, CONV_CH)), _full((4, POOL_G, POOL_G)),
                  _full((1, 4 * POOL_G)), _full((D, D)),
                  _full((D, LANES)), _full((D, LANES)), _full((1, LANES))],
        out_specs=specs,
        compiler_params=_cparams(1),
        name="even_mixer",
    )(x, x, mod, n1, n2, w_in, conv_w, pool_w, pool_scale, w_out, wr_hi, wr_lo, br)


def _proj_kernel(x_ref, mod_ref, n1_ref, win_ref, gsq_ref, gsk_ref, gqn_ref, gkvn_ref,
                 wuq_ref, wuk_ref, wuv_ref, gmq_ref, gmk_ref, c_ref, s1_ref, s2_ref,
                 qs_ref, ks_ref, vs_ref, qm_ref, km_ref, vm_ref):
    x = x_ref[...]
    mod = mod_ref[...]
    shift1, scale1 = mod[0:1], mod[1:2]
    h = _rms(x) * n1_ref[...] * (1.0 + scale1) + shift1
    z = _dot(h.astype(BF16), win_ref[...])

    def head_norm(t, g, dim):
        ms = jnp.sum(t * t, axis=-1, keepdims=True) * (1.0 / dim)
        return t * lax.rsqrt(ms + EPS) * g

    gsq, gsk = gsq_ref[...], gsk_ref[...]
    for hd in range(SWA_HEADS):
        qh = head_norm(z[:, O_QS + hd * LANES: O_QS + (hd + 1) * LANES], gsq, HEAD_DIM)
        qs_ref[:, hd * LANES:(hd + 1) * LANES] = (qh * (HEAD_DIM ** -0.5)).astype(BF16)
    for kv in range(SWA_KV):
        kh = head_norm(z[:, O_KS + kv * LANES: O_KS + (kv + 1) * LANES], gsk, HEAD_DIM)
        ks_ref[:, kv * LANES:(kv + 1) * LANES] = kh.astype(BF16)
    vs_ref[...] = z[:, O_VS:O_CQ].astype(BF16)

    cq = (_rms(z[:, O_CQ:O_CKV]) * gqn_ref[...]).astype(BF16)
    ckv = (_rms(z[:, O_CKV:O_KR]) * gkvn_ref[...]).astype(BF16)
    qm = _dot(cq, wuq_ref[...])
    kn = _dot(ckv, wuk_ref[...])
    vm = _dot(ckv, wuv_ref[...])
    lane = lax.broadcasted_iota(jnp.int32, (T_TOK, LANES), 1)
    for j in range(MLA_HEADS // 2):
        vv = vm[:, j * LANES:(j + 1) * LANES]
        even = jnp.where(lane < 64, vv, jnp.where(lane == 64, 1.0, 0.0))
        odd = jnp.where(lane < 64, jnp.where(lane == 0, 1.0, 0.0), vv)
        vm_ref[:, (2 * j) * LANES:(2 * j + 1) * LANES] = even.astype(BF16)
        vm_ref[:, (2 * j + 1) * LANES:(2 * j + 2) * LANES] = odd.astype(BF16)
    kr = z[:, O_KR:ODD_W]
    cs, s1, s2 = c_ref[...], s1_ref[...], s2_ref[...]

    def rope(t):
        return t * cs + pltpu.roll(t, LANES - 16, 1) * s1 + pltpu.roll(t, 16, 1) * s2

    gmq, gmk = gmq_ref[...], gmk_ref[...]
    for hd in range(MLA_HEADS):
        sl = slice(hd * LANES, (hd + 1) * LANES)
        qh = rope(head_norm(qm[:, sl], gmq, MLA_QK))
        qm_ref[:, sl] = (qh * (MLA_QK ** -0.5 * LOG2E)).astype(BF16)
        kh = rope(head_norm(kn[:, sl] + kr, gmk, MLA_QK))
        km_ref[:, sl] = kh.astype(BF16)


def _proj_layer(x, mod, n1, w_in, gsq, gsk, gqn, gkvn, wuq, wuk, wuv, gmq, gmk, tc, ts1, ts2):
    def tok(wd):
        return pl.BlockSpec((T_TOK, wd), lambda i: (i, 0))
    widths = (1024, 256, 256, 1024, 1024, 1024)
    return pl.pallas_call(
        _proj_kernel,
        out_shape=tuple(jax.ShapeDtypeStruct((SEQ, wd), BF16) for wd in widths),
        grid=(SEQ // T_TOK,),
        in_specs=[tok(D), _full((8, D)), _full((1, D)), _full((D, ODD_W)),
                  _full((1, LANES)), _full((1, LANES)), _full((1, Q_LORA)), _full((1, KV_LORA)),
                  _full((Q_LORA, 1024)), _full((KV_LORA, 1024)), _full((KV_LORA, 512)),
                  _full((1, LANES)), _full((1, LANES)), tok(LANES), tok(LANES), tok(LANES)],
        out_specs=tuple(tok(wd) for wd in widths),
        compiler_params=_cparams(1),
        name="odd_proj",
    )(x, mod, n1, w_in, gsq, gsk, gqn, gkvn, wuq, wuk, wuv, gmq, gmk, tc, ts1, ts2)


def _swa_kernel(sink_ref, q_ref, k_ref, kh_ref, v_ref, vh_ref, o_ref):
    i = pl.program_id(0)
    kcat = jnp.concatenate([kh_ref[...], k_ref[...]], axis=0)
    vcat = jnp.concatenate([vh_ref[...], v_ref[...]], axis=0)
    r = lax.broadcasted_iota(jnp.int32, (WINDOW, 2 * WINDOW), 0)
    c = lax.broadcasted_iota(jnp.int32, (WINDOW, 2 * WINDOW), 1)
    rel = WINDOW + r - c
    lane = lax.broadcasted_iota(jnp.int32, (2 * WINDOW, LANES), 1)
    for sb in range(T_ATT // WINDOW):
        kb = kcat[sb * WINDOW: sb * WINDOW + 2 * WINDOW]
        vb = vcat[sb * WINDOW: sb * WINDOW + 2 * WINDOW]
        kpos = i * T_ATT + (sb - 1) * WINDOW + c
        ok = (rel >= 0) & (rel < WINDOW) & (kpos >= 0)
        for j in range(SWA_HEADS // 2):
            kv = (2 * j) // (SWA_HEADS // SWA_KV)
            kk = kb[:, kv * LANES:(kv + 1) * LANES]
            vv = vb[:, kv * LANES:(kv + 1) * LANES]
            vsel = (jnp.where(lane < 64, vv, jnp.zeros_like(vv)),
                    jnp.where(lane >= 64, vv, jnp.zeros_like(vv)))
            out = None
            for hh in range(2):
                hd = 2 * j + hh
                q = q_ref[sb * WINDOW:(sb + 1) * WINDOW, hd * LANES:(hd + 1) * LANES]
                s = jnp.where(ok, _dot_nt(q, kk), NEG)
                sink = sink_ref[hd]
                m = jnp.maximum(jnp.max(s, axis=-1, keepdims=True), sink)
                e = jnp.exp(s - m)
                den = jnp.sum(e, axis=-1, keepdims=True) + jnp.exp(sink - m)
                p = (e / den).astype(BF16)
                dd = _dot(p, vsel[hh])
                out = dd if out is None else out + dd
            o_ref[sb * WINDOW:(sb + 1) * WINDOW, j * LANES:(j + 1) * LANES] = out.astype(BF16)


def _swa_layer(sinks, qs, ks, vs):
    hb = T_ATT // WINDOW
    return pl.pallas_call(
        _swa_kernel,
        out_shape=jax.ShapeDtypeStruct((SEQ, 512), BF16),
        grid=(SEQ // T_ATT,),
        in_specs=[pl.BlockSpec(memory_space=pltpu.SMEM),
                  pl.BlockSpec((T_ATT, 1024), lambda i: (i, 0)),
                  pl.BlockSpec((T_ATT, 256), lambda i: (i, 0)),
                  pl.BlockSpec((WINDOW, 256), lambda i: (jnp.maximum(i * hb - 1, 0), 0)),
                  pl.BlockSpec((T_ATT, 256), lambda i: (i, 0)),
                  pl.BlockSpec((WINDOW, 256), lambda i: (jnp.maximum(i * hb - 1, 0), 0))],
        out_specs=pl.BlockSpec((T_ATT, 512), lambda i: (i, 0)),
        compiler_params=_cparams(1),
        name="swa_attn",
    )(sinks, qs, ks, ks, vs, vs)


def _mla_kernel(qi_ref, ki_ref, q_ref, k_ref, v_ref, o_ref, m_ref, acc_ref):
    step = pl.program_id(0)
    qi = qi_ref[step]
    ki = ki_ref[step]

    @pl.when(ki == 0)
    def _():
        m_ref[...] = jnp.full(m_ref.shape, NEG, F32)
        acc_ref[...] = jnp.zeros(acc_ref.shape, F32)

    lane = lax.broadcasted_iota(jnp.int32, (T_ATT, LANES), 1)
    lo = lane < 64
    nc = T_ATT // LANES

    def scores(hd):
        sl = slice(hd * LANES, (hd + 1) * LANES)
        return _dot_nt(q_ref[:, sl], k_ref[:, sl])

    def update(masked):
        if masked:
            r = lax.broadcasted_iota(jnp.int32, (T_ATT, LANES), 0)
        s_next = scores(0)
        for hd in range(MLA_HEADS):
            s = s_next
            if hd + 1 < MLA_HEADS:
                s_next = scores(hd + 1)
            cols = [s[:, c * LANES:(c + 1) * LANES] for c in range(nc)]
            if masked:
                cols = [jnp.where(r >= lane + c * LANES, cols[c], NEG) for c in range(nc)]
            cmax = cols[0]
            for c in range(1, nc):
                cmax = jnp.maximum(cmax, cols[c])
            m_prev = m_ref[hd]
            m_new = jnp.maximum(m_prev, jnp.max(cmax, axis=-1, keepdims=True))
            m_ref[hd] = m_new
            alpha = jnp.exp2(m_prev - m_new)
            p = jnp.concatenate([jnp.exp2(cols[c] - m_new).astype(BF16) for c in range(nc)], axis=-1)
            acc_ref[hd] = acc_ref[hd] * alpha + _dot(p, v_ref[:, hd * LANES:(hd + 1) * LANES])

    @pl.when(ki < qi)
    def _():
        update(False)

    @pl.when(ki == qi)
    def _():
        update(True)
        for j in range(MLA_HEADS // 2):
            ae = acc_ref[2 * j]
            ao = acc_ref[2 * j + 1]
            out = jnp.where(lo, ae * (1.0 / ae[:, 64:65]), ao * (1.0 / ao[:, 0:1]))
            o_ref[:, j * LANES:(j + 1) * LANES] = out.astype(BF16)


def _mla_layer(qm, km, vm):
    nb = SEQ // T_ATT
    qi = np.concatenate([np.full(n + 1, n, np.int32) for n in range(nb)])
    ki = np.concatenate([np.arange(n + 1, dtype=np.int32) for n in range(nb)])
    grid_spec = pltpu.PrefetchScalarGridSpec(
        num_scalar_prefetch=2,
        grid=(int(qi.shape[0]),),
        in_specs=[pl.BlockSpec((T_ATT, 1024), lambda s, qi, ki: (qi[s], 0)),
                  pl.BlockSpec((T_ATT, 1024), lambda s, qi, ki: (ki[s], 0)),
                  pl.BlockSpec((T_ATT, 1024), lambda s, qi, ki: (ki[s], 0))],
        out_specs=pl.BlockSpec((T_ATT, 512), lambda s, qi, ki: (qi[s], 0)),
        scratch_shapes=[pltpu.VMEM((MLA_HEADS, T_ATT, LANES), F32),
                        pltpu.VMEM((MLA_HEADS, T_ATT, LANES), F32)],
    )
    return pl.pallas_call(
        _mla_kernel,
        out_shape=jax.ShapeDtypeStruct((SEQ, 512), BF16),
        grid_spec=grid_spec,
        compiler_params=_cparams(1),
        name="mla_attn",
    )(jnp.asarray(qi), jnp.asarray(ki), qm, km, vm)


def _post_kernel(x_ref, os_ref, om_ref, mod_ref, n2_ref, wout_ref, wrh_ref, wrl_ref, br_ref,
                 xo_ref, h2_ref, gates_ref):
    y = _dot(os_ref[...], wout_ref[0:512, :]) + _dot(om_ref[...], wout_ref[512:1024, :])
    _tail(x_ref[...], y, mod_ref[...], n2_ref[...], wrh_ref[...], wrl_ref[...], br_ref[...],
          xo_ref, h2_ref, gates_ref)


def _post_layer(x, o_s, o_m, mod, n2, w_out, wr_hi, wr_lo, br):
    shapes, specs = _tail_out(T_TOK)
    return pl.pallas_call(
        _post_kernel,
        out_shape=shapes,
        grid=(SEQ // T_TOK,),
        in_specs=[pl.BlockSpec((T_TOK, D), lambda i: (i, 0)),
                  pl.BlockSpec((T_TOK, 512), lambda i: (i, 0)),
                  pl.BlockSpec((T_TOK, 512), lambda i: (i, 0)),
                  _full((8, D)), _full((1, D)), _full((D, D)),
                  _full((D, LANES)), _full((D, LANES)), _full((1, LANES))],
        out_specs=specs,
        compiler_params=_cparams(1),
        name="odd_post",
    )(x, o_s, o_m, mod, n2, w_out, wr_hi, wr_lo, br)


def _moe_kernel(x_ref, h_ref, g_ref, mod_ref, wg_ref, wu_ref, wd_ref, o_ref, acc_ref):
    e = pl.program_id(1)

    @pl.when(e == 0)
    def _():
        acc_ref[...] = jnp.zeros(acc_ref.shape, F32)

    h = h_ref[...]
    a = _dot(h, wg_ref[0])
    u = _dot(h, wu_ref[0])
    act = (a * jax.nn.sigmoid(a) * u).astype(BF16)
    y = _dot(act, wd_ref[0])
    gates = g_ref[...]
    lane = lax.broadcasted_iota(jnp.int32, gates.shape, 1)
    gcol = jnp.sum(jnp.where(lane == e, gates, 0.0), axis=-1, keepdims=True)
    acc_ref[...] += gcol * y

    @pl.when(e == N_EXPERTS - 1)
    def _():
        o_ref[...] = x_ref[...] + mod_ref[5:6, :] * acc_ref[...]


def _moe_layer(x, h2, gates, mod, wg, wu, wd):
    return pl.pallas_call(
        _moe_kernel,
        out_shape=jax.ShapeDtypeStruct((SEQ, D), F32),
        grid=(SEQ // T_MOE, N_EXPERTS),
        in_specs=[pl.BlockSpec((T_MOE, D), lambda i, e: (i, 0)),
                  pl.BlockSpec((T_MOE, D), lambda i, e: (i, 0)),
                  pl.BlockSpec((T_MOE, LANES), lambda i, e: (i, 0)),
                  pl.BlockSpec((8, D), lambda i, e: (0, 0)),
                  pl.BlockSpec((1, D, FF), lambda i, e: (e, 0, 0)),
                  pl.BlockSpec((1, D, FF), lambda i, e: (e, 0, 0)),
                  pl.BlockSpec((1, FF, D), lambda i, e: (e, 0, 0))],
        out_specs=pl.BlockSpec((T_MOE, D), lambda i, e: (i, 0)),
        scratch_shapes=[pltpu.VMEM((T_MOE, D), F32)],
        compiler_params=_cparams(2),
        name="moe_experts",
    )(x, h2, gates, mod, wg, wu, wd)


def _pad_heads(w, heads, dim):
    k = w.shape[0]
    w = w.reshape(k, heads, dim)
    return jnp.pad(w, ((0, 0), (0, 0), (0, LANES - dim))).reshape(k, heads * LANES)


def _pad_gain(g):
    return jnp.pad(g, (0, LANES - g.shape[0])).reshape(1, LANES)


def _odd_weights(w_in):
    q_s = _pad_heads(w_in[:, 0:512], SWA_HEADS, HEAD_DIM)
    k_s = _pad_heads(w_in[:, 512:640], SWA_KV, HEAD_DIM)
    v = w_in[:, 640:768]
    v_s = jnp.concatenate([v[:, 0:64], v[:, 0:64], v[:, 64:128], v[:, 64:128]], axis=1)
    c_q = w_in[:, 768:1152]
    c_kv = w_in[:, 1152:1408]
    k_r = jnp.pad(w_in[:, 1408:1440], ((0, 0), (MLA_NOPE, LANES - MLA_QK)))
    return jnp.concatenate([q_s, k_s, v_s, c_q, c_kv, k_r], axis=1).astype(BF16)


def _router_weights(w_group, b_group, w_expert, b_expert):
    w = jnp.pad(jnp.concatenate([w_expert, w_group], axis=1), ((0, 0), (0, LANES - N_EXPERTS - N_GROUPS)))
    hi = w.astype(BF16)
    lo = (w - hi.astype(F32)).astype(BF16)
    b = jnp.pad(jnp.concatenate([b_expert, b_group]), (0, LANES - N_EXPERTS - N_GROUPS)).reshape(1, LANES)
    return hi, lo, b


def kernel(x, c, positions, ada_w, ada_b, norm1_g, norm2_g, cp_w_in, conv_w, pool_w, pool_scale,
           cp_w_out, at_w_in, swa_q_g, swa_k_g, swa_sinks, mla_q_norm_g, mla_kv_norm_g, mla_w_uq,
           mla_w_ukv, mla_q_g, mla_k_g, at_w_out, moe_w_group, moe_b_group, moe_w_expert,
           moe_b_expert, moe_w_gate, moe_w_up, moe_w_down):
    xs = x.reshape(SEQ, D)
    mods = _ada_mod(c, ada_w, ada_b)
    tc, ts1, ts2 = _rope_tables(positions)
    for l in range(DEPTH):
        i = l // 2
        mod = mods[l]
        n1 = norm1_g[l].reshape(1, D)
        n2 = norm2_g[l].reshape(1, D)
        wr_hi, wr_lo, br = _router_weights(moe_w_group[l], moe_b_group[l], moe_w_expert[l], moe_b_expert[l])
        if l % 2 == 0:
            xs, h2, gates = _even_layer(
                xs, mod, n1, n2, cp_w_in[i].astype(BF16), conv_w[i], pool_w[i].astype(BF16),
                pool_scale[i].reshape(1, 4 * POOL_G), cp_w_out[i].astype(BF16), wr_hi, wr_lo, br)
        else:
            ukv = mla_w_ukv[i].reshape(KV_LORA, MLA_HEADS, MLA_NOPE + MLA_V)
            wuk = _pad_heads(ukv[:, :, :MLA_NOPE].reshape(KV_LORA, MLA_HEADS * MLA_NOPE), MLA_HEADS, MLA_NOPE)
            wuv = ukv[:, :, MLA_NOPE:].reshape(KV_LORA, MLA_HEADS * MLA_V)
            wuq = _pad_heads(mla_w_uq[i], MLA_HEADS, MLA_QK)
            qs, ks, vs, qm, km, vm = _proj_layer(
                xs, mod, n1, _odd_weights(at_w_in[i]), _pad_gain(swa_q_g[i]), _pad_gain(swa_k_g[i]),
                mla_q_norm_g[i].reshape(1, Q_LORA), mla_kv_norm_g[i].reshape(1, KV_LORA),
                wuq.astype(BF16), wuk.astype(BF16), wuv.astype(BF16),
                _pad_gain(mla_q_g[i]), _pad_gain(mla_k_g[i]), tc, ts1, ts2)
            o_s = _swa_layer(swa_sinks[i], qs, ks, vs)
            o_m = _mla_layer(qm, km, vm)
            xs, h2, gates = _post_layer(xs, o_s, o_m, mod, n2, at_w_out[i].astype(BF16), wr_hi, wr_lo, br)
        xs = _moe_layer(xs, h2, gates, mod, moe_w_gate[l].astype(BF16), moe_w_up[l].astype(BF16),
                        moe_w_down[l].astype(BF16))
    return xs.reshape(1, SEQ, D)
```

```python
import functools

import numpy as np
import jax
import jax.numpy as jnp
from jax import lax
from jax.experimental import pallas as pl
from jax.experimental.pallas import tpu as pltpu

F32 = jnp.float32
BF16 = jnp.bfloat16

D = 1024
SEQ = 16384
DEPTH = 4
EPS = 1e-6
LANES = 128
CONV_CH = 512
POOL_WINDOWS = (2, 4, 8, 16)
POOL_G = 128
HALO = 16
SWA_HEADS = 8
SWA_KV = 2
HEAD_DIM = 64
WINDOW = 128
MLA_HEADS = 8
MLA_NOPE = 64
MLA_ROPE = 32
MLA_QK = MLA_NOPE + MLA_ROPE
MLA_V = 64
Q_LORA = 384
KV_LORA = 256
ROPE_THETA = 10000.0
N_GROUPS = 4
EPG = 8
N_EXPERTS = N_GROUPS * EPG
FF = 256
NEG = -1e30
LOG2E = 1.4426950408889634

T_TOK = 512
T_ATT = 512
T_MOE = 1024
T_DISP = 512
ROW_W = D + LANES
N_SORT = SEQ + N_GROUPS * T_MOE
N_TILES = N_SORT // T_MOE
VMEM_LIMIT = 48 * 1024 * 1024

O_QS, O_KS, O_VS, O_CQ, O_CKV, O_KR, ODD_W = 0, 1024, 1280, 1536, 1920, 2176, 2304


def _cparams(n_axes=1):
    return pltpu.CompilerParams(dimension_semantics=("arbitrary",) * n_axes,
                                vmem_limit_bytes=VMEM_LIMIT)


def _rms(x):
    return x * lax.rsqrt(jnp.mean(x * x, axis=-1, keepdims=True) + EPS)


def _dot(a, b):
    return jnp.dot(a, b, preferred_element_type=F32)


def _dot_nt(a, b):
    return lax.dot_general(a, b, (((1,), (1,)), ((), ())), preferred_element_type=F32)


def _ada_kernel(c_ref, w_ref, b_ref, o_ref):
    c = c_ref[...]
    ca = c * jax.nn.sigmoid(c)
    o_ref[0] = jnp.sum(w_ref[0] * ca, axis=0, keepdims=True) + b_ref[0]


def _ada_mod(c, ada_w, ada_b):
    c_col = c.reshape(D, 1)
    b = ada_b.reshape(DEPTH * 6, 1, D)
    out = pl.pallas_call(
        _ada_kernel,
        out_shape=jax.ShapeDtypeStruct((DEPTH * 6, 1, D), F32),
        grid=(DEPTH, 6),
        in_specs=[pl.BlockSpec((D, 1), lambda l, j: (0, 0)),
                  pl.BlockSpec((1, D, D), lambda l, j: (l, 0, j)),
                  pl.BlockSpec((1, 1, D), lambda l, j: (l * 6 + j, 0, 0))],
        out_specs=pl.BlockSpec((1, 1, D), lambda l, j: (l * 6 + j, 0, 0)),
        compiler_params=_cparams(2),
        name="ada_mod",
    )(c_col, ada_w, b)
    mod = out.reshape(DEPTH, 6, D)
    return jnp.pad(mod, ((0, 0), (0, 2), (0, 0)))


def _rope_kernel(pos_ref, inv_ref, c_ref, s1_ref, s2_ref):
    pos = pos_ref[...].astype(F32)
    ang = pos * inv_ref[...]
    lane = lax.broadcasted_iota(jnp.int32, ang.shape, 1)
    cs = jnp.cos(ang)
    sn = jnp.sin(ang)
    c_ref[...] = jnp.where(lane < 64, 1.0, jnp.where(lane < 96, cs, 0.0))
    s1_ref[...] = jnp.where((lane >= 64) & (lane < 80), -sn, 0.0)
    s2_ref[...] = jnp.where((lane >= 80) & (lane < 96), sn, 0.0)


def _rope_tables(positions):
    half = MLA_ROPE // 2
    inv = jnp.power(ROPE_THETA, -jnp.arange(half, dtype=F32) / half)
    inv_lane = jnp.concatenate([jnp.zeros((64,), F32), inv, inv, jnp.zeros((32,), F32)]).reshape(1, LANES)
    pos = positions.reshape(SEQ, 1)
    shp = jax.ShapeDtypeStruct((SEQ, LANES), F32)
    spec = pl.BlockSpec((T_TOK, LANES), lambda i: (i, 0))
    return pl.pallas_call(
        _rope_kernel,
        out_shape=(shp, shp, shp),
        grid=(SEQ // T_TOK,),
        in_specs=[pl.BlockSpec((T_TOK, 1), lambda i: (i, 0)),
                  pl.BlockSpec((1, LANES), lambda i: (0, 0))],
        out_specs=(spec, spec, spec),
        compiler_params=_cparams(1),
        name="rope_tables",
    )(pos, inv_lane)


def _route(lg):
    lane = lax.broadcasted_iota(jnp.int32, lg.shape, 1)
    lane_f = lane.astype(F32)
    is_g = (lane >= N_EXPERTS) & (lane < N_EXPERTS + N_GROUPS)
    gl = jnp.where(is_g, lg, NEG)
    gmax = jnp.max(gl, axis=-1, keepdims=True)
    gidx = jnp.min(jnp.where(is_g & (gl == gmax), lane_f - N_EXPERTS, 1e3), axis=-1, keepdims=True)
    gsum = jnp.sum(jnp.where(is_g, jnp.exp(gl - gmax), 0.0), axis=-1, keepdims=True)
    gw = 1.0 / gsum
    grp_of_lane = (lane >> 3).astype(F32)
    in_grp = (lane < N_EXPERTS) & (grp_of_lane == gidx)
    el = jnp.where(in_grp, lg, NEG)
    m1 = jnp.max(el, axis=-1, keepdims=True)
    i1 = jnp.min(jnp.where(in_grp & (el == m1), lane_f, 1e3), axis=-1, keepdims=True)
    rest = in_grp & (lane_f != i1)
    el2 = jnp.where(rest, lg, NEG)
    m2 = jnp.max(el2, axis=-1, keepdims=True)
    i2 = jnp.min(jnp.where(rest & (el2 == m2), lane_f, 1e3), axis=-1, keepdims=True)
    r = jnp.exp(m2 - m1)
    w1 = gw / (1.0 + r)
    w2 = w1 * r
    return jnp.where(lane_f == i1, w1, jnp.where(lane_f == i2, w2, 0.0)), gidx


def _tail(x, y, mod, n2, wr_hi, wr_lo, br, xo_ref, h2_ref, meta_ref, cnt_ref, carry_ref):
    i = pl.program_id(0)

    @pl.when(i == 0)
    def _():
        carry_ref[...] = jnp.zeros(carry_ref.shape, F32)

    gate1 = mod[2:3]
    shift2, scale2 = mod[3:4], mod[4:5]
    xn = x + gate1 * y
    xo_ref[...] = xn
    h2 = _rms(xn) * n2 * (1.0 + scale2) + shift2
    hi = h2.astype(BF16)
    lo = (h2 - hi.astype(F32)).astype(BF16)
    lg = _dot(hi, wr_hi) + _dot(lo, wr_hi) + _dot(hi, wr_lo) + br
    gates, gidx = _route(lg)
    rows = lg.shape[0]
    lane = lax.broadcasted_iota(jnp.int32, (rows, LANES), 1)
    lane_f = lane.astype(F32)
    g8 = gates
    for g in range(1, N_GROUPS):
        g8 = g8 + pltpu.roll(gates, LANES - EPG * g, 1)
    onehot = (lane_f == gidx).astype(F32)
    r = lax.broadcasted_iota(jnp.int32, (rows, rows), 0)
    c = lax.broadcasted_iota(jnp.int32, (rows, rows), 1)
    before = jnp.where(c < r, 1.0, 0.0).astype(BF16)
    cum = _dot(before, onehot.astype(BF16)) + carry_ref[0:1, :]
    rank = jnp.sum(jnp.where(lane_f == gidx, cum, 0.0), axis=-1, keepdims=True)
    meta = jnp.where(lane < EPG, g8, jnp.where(lane == EPG, gidx, jnp.where(lane == EPG + 1, rank, 0.0)))
    h2_ref[:, 0:D] = h2
    h2_ref[:, D:D + LANES] = meta
    meta_ref[...] = meta
    total = carry_ref[0:1, :] + jnp.sum(onehot, axis=0, keepdims=True)
    carry_ref[...] = jnp.broadcast_to(total, carry_ref.shape)
    cnt_ref[...] = jnp.broadcast_to(total, cnt_ref.shape)


def _even_kernel(x_ref, xh_ref, mod_ref, n1_ref, n2_ref, win_ref, cw_ref, pw_ref, ps_ref,
                 wout_ref, wrh_ref, wrl_ref, br_ref, xo_ref, h2_ref, meta_ref, cnt_ref, carry_ref):
    i = pl.program_id(0)
    x = x_ref[...]
    mod = mod_ref[...]
    shift1, scale1 = mod[0:1], mod[1:2]
    xa = jnp.concatenate([xh_ref[...], x], axis=0)
    h = _rms(xa) * n1_ref[...] * (1.0 + scale1) + shift1
    z = _dot(h.astype(BF16), win_ref[...])
    rows = T_TOK + HALO
    row = lax.broadcasted_iota(jnp.int32, (rows, 1), 0)
    tpos = i * T_TOK + row - HALO
    live = (tpos >= 0).astype(F32)
    bg = z[:, 0:CONV_CH]
    v = z[:, CONV_CH:2 * CONV_CH] * z[:, 2 * CONV_CH:3 * CONV_CH] * live
    cw = cw_ref[...]
    conv = v * cw[0:1] + pltpu.roll(v, 1, 0) * cw[1:2] + pltpu.roll(v, 2, 0) * cw[2:3]
    parts = [(bg * conv)[HALO:]]
    ps = ps_ref[...]
    tcount = (tpos + 1).astype(F32)
    for gi, w in enumerate(POOL_WINDOWS):
        ug = z[:, 3 * CONV_CH + gi * POOL_G: 3 * CONV_CH + (gi + 1) * POOL_G] * live
        s = ug
        k = 1
        while k < w:
            s = s + pltpu.roll(s, k, 0)
            k *= 2
        inv = 1.0 / jnp.minimum(tcount, float(w))
        d = (s * inv - ug)[HALO:].astype(BF16)
        parts.append(_dot(d, pw_ref[gi]) * ps[:, gi * POOL_G:(gi + 1) * POOL_G])
    cat = jnp.concatenate(parts, axis=-1).astype(BF16)
    y = _dot(cat, wout_ref[...])
    _tail(x, y, mod, n2_ref[...], wrh_ref[...], wrl_ref[...], br_ref[...], xo_ref, h2_ref, meta_ref,
          cnt_ref, carry_ref)


def _tail_out(n):
    shapes = (jax.ShapeDtypeStruct((SEQ, D), F32), jax.ShapeDtypeStruct((SEQ, ROW_W), F32),
              jax.ShapeDtypeStruct((SEQ, LANES), F32), jax.ShapeDtypeStruct((8, LANES), F32))
    specs = (pl.BlockSpec((n, D), lambda i: (i, 0)), pl.BlockSpec((n, ROW_W), lambda i: (i, 0)),
             pl.BlockSpec((n, LANES), lambda i: (i, 0)), pl.BlockSpec((8, LANES), lambda i: (0, 0)))
    scratch = [pltpu.VMEM((8, LANES), F32)]
    return shapes, specs, scratch


def _full(shape):
    nd = len(shape)
    return pl.BlockSpec(shape, lambda i: (0,) * nd)


def _even_layer(x, mod, n1, n2, w_in, conv_w, pool_w, pool_scale, w_out, wr_hi, wr_lo, br):
    shapes, specs, scratch = _tail_out(T_TOK)
    hb = T_TOK // HALO
    return pl.pallas_call(
        _even_kernel,
        out_shape=shapes,
        grid=(SEQ // T_TOK,),
        in_specs=[pl.BlockSpec((T_TOK, D), lambda i: (i, 0)),
                  pl.BlockSpec((HALO, D), lambda i: (jnp.maximum(i * hb - 1, 0), 0)),
                  _full((8, D)), _full((1, D)), _full((1, D)),
                  _full((D, 4 * CONV_CH)), _full((3, CONV_CH)), _full((4, POOL_G, POOL_G)),
                  _full((1, 4 * POOL_G)), _full((D, D)),
                  _full((D, LANES)), _full((D, LANES)), _full((1, LANES))],
        out_specs=specs,
        scratch_shapes=scratch,
        compiler_params=_cparams(1),
        name="even_mixer",
    )(x, x, mod, n1, n2, w_in, conv_w, pool_w, pool_scale, w_out, wr_hi, wr_lo, br)


def _proj_kernel(x_ref, mod_ref, n1_ref, win_ref, gsq_ref, gsk_ref, gqn_ref, gkvn_ref,
                 wuq_ref, wuk_ref, wuv_ref, gmq_ref, gmk_ref, c_ref, s1_ref, s2_ref,
                 qs_ref, ks_ref, vs_ref, qm_ref, km_ref, vm_ref):
    x = x_ref[...]
    mod = mod_ref[...]
    shift1, scale1 = mod[0:1], mod[1:2]
    h = _rms(x) * n1_ref[...] * (1.0 + scale1) + shift1
    z = _dot(h.astype(BF16), win_ref[...])

    def head_norm(t, g, dim):
        ms = jnp.sum(t * t, axis=-1, keepdims=True) * (1.0 / dim)
        return t * lax.rsqrt(ms + EPS) * g

    gsq, gsk = gsq_ref[...], gsk_ref[...]
    for hd in range(SWA_HEADS):
        qh = head_norm(z[:, O_QS + hd * LANES: O_QS + (hd + 1) * LANES], gsq, HEAD_DIM)
        qs_ref[:, hd * LANES:(hd + 1) * LANES] = (qh * (HEAD_DIM ** -0.5)).astype(BF16)
    for kv in range(SWA_KV):
        kh = head_norm(z[:, O_KS + kv * LANES: O_KS + (kv + 1) * LANES], gsk, HEAD_DIM)
        ks_ref[:, kv * LANES:(kv + 1) * LANES] = kh.astype(BF16)
    vs_ref[...] = z[:, O_VS:O_CQ].astype(BF16)

    cq = (_rms(z[:, O_CQ:O_CKV]) * gqn_ref[...]).astype(BF16)
    ckv = (_rms(z[:, O_CKV:O_KR]) * gkvn_ref[...]).astype(BF16)
    qm = _dot(cq, wuq_ref[...])
    kn = _dot(ckv, wuk_ref[...])
    vm = _dot(ckv, wuv_ref[...])
    lane = lax.broadcasted_iota(jnp.int32, (T_TOK, LANES), 1)
    for j in range(MLA_HEADS // 2):
        vv = vm[:, j * LANES:(j + 1) * LANES]
        even = jnp.where(lane < 64, vv, jnp.where(lane == 64, 1.0, 0.0))
        odd = jnp.where(lane < 64, jnp.where(lane == 0, 1.0, 0.0), vv)
        vm_ref[:, (2 * j) * LANES:(2 * j + 1) * LANES] = even.astype(BF16)
        vm_ref[:, (2 * j + 1) * LANES:(2 * j + 2) * LANES] = odd.astype(BF16)
    kr = z[:, O_KR:ODD_W]
    cs, s1, s2 = c_ref[...], s1_ref[...], s2_ref[...]

    def rope(t):
        return t * cs + pltpu.roll(t, LANES - 16, 1) * s1 + pltpu.roll(t, 16, 1) * s2

    gmq, gmk = gmq_ref[...], gmk_ref[...]
    for hd in range(MLA_HEADS):
        sl = slice(hd * LANES, (hd + 1) * LANES)
        qh = rope(head_norm(qm[:, sl], gmq, MLA_QK))
        qm_ref[:, sl] = (qh * (MLA_QK ** -0.5 * LOG2E)).astype(BF16)
        kh = rope(head_norm(kn[:, sl] + kr, gmk, MLA_QK))
        km_ref[:, sl] = kh.astype(BF16)


def _proj_layer(x, mod, n1, w_in, gsq, gsk, gqn, gkvn, wuq, wuk, wuv, gmq, gmk, tc, ts1, ts2):
    def tok(wd):
        return pl.BlockSpec((T_TOK, wd), lambda i: (i, 0))
    widths = (1024, 256, 256, 1024, 1024, 1024)
    return pl.pallas_call(
        _proj_kernel,
        out_shape=tuple(jax.ShapeDtypeStruct((SEQ, wd), BF16) for wd in widths),
        grid=(SEQ // T_TOK,),
        in_specs=[tok(D), _full((8, D)), _full((1, D)), _full((D, ODD_W)),
                  _full((1, LANES)), _full((1, LANES)), _full((1, Q_LORA)), _full((1, KV_LORA)),
                  _full((Q_LORA, 1024)), _full((KV_LORA, 1024)), _full((KV_LORA, 512)),
                  _full((1, LANES)), _full((1, LANES)), tok(LANES), tok(LANES), tok(LANES)],
        out_specs=tuple(tok(wd) for wd in widths),
        compiler_params=_cparams(1),
        name="odd_proj",
    )(x, mod, n1, w_in, gsq, gsk, gqn, gkvn, wuq, wuk, wuv, gmq, gmk, tc, ts1, ts2)


def _swa_kernel(sink_ref, q_ref, k_ref, kh_ref, v_ref, vh_ref, o_ref):
    i = pl.program_id(0)
    kcat = jnp.concatenate([kh_ref[...], k_ref[...]], axis=0)
    vcat = jnp.concatenate([vh_ref[...], v_ref[...]], axis=0)
    r = lax.broadcasted_iota(jnp.int32, (WINDOW, 2 * WINDOW), 0)
    c = lax.broadcasted_iota(jnp.int32, (WINDOW, 2 * WINDOW), 1)
    rel = WINDOW + r - c
    lane = lax.broadcasted_iota(jnp.int32, (2 * WINDOW, LANES), 1)
    for sb in range(T_ATT // WINDOW):
        kb = kcat[sb * WINDOW: sb * WINDOW + 2 * WINDOW]
        vb = vcat[sb * WINDOW: sb * WINDOW + 2 * WINDOW]
        kpos = i * T_ATT + (sb - 1) * WINDOW + c
        ok = (rel >= 0) & (rel < WINDOW) & (kpos >= 0)
        for j in range(SWA_HEADS // 2):
            kv = (2 * j) // (SWA_HEADS // SWA_KV)
            kk = kb[:, kv * LANES:(kv + 1) * LANES]
            vv = vb[:, kv * LANES:(kv + 1) * LANES]
            vsel = (jnp.where(lane < 64, vv, jnp.zeros_like(vv)),
                    jnp.where(lane >= 64, vv, jnp.zeros_like(vv)))
            out = None
            for hh in range(2):
                hd = 2 * j + hh
                q = q_ref[sb * WINDOW:(sb + 1) * WINDOW, hd * LANES:(hd + 1) * LANES]
                s = jnp.where(ok, _dot_nt(q, kk), NEG)
                sink = sink_ref[hd]
                m = jnp.maximum(jnp.max(s, axis=-1, keepdims=True), sink)
                e = jnp.exp(s - m)
                den = jnp.sum(e, axis=-1, keepdims=True) + jnp.exp(sink - m)
                p = (e / den).astype(BF16)
                dd = _dot(p, vsel[hh])
                out = dd if out is None else out + dd
            o_ref[sb * WINDOW:(sb + 1) * WINDOW, j * LANES:(j + 1) * LANES] = out.astype(BF16)


def _swa_layer(sinks, qs, ks, vs):
    hb = T_ATT // WINDOW
    return pl.pallas_call(
        _swa_kernel,
        out_shape=jax.ShapeDtypeStruct((SEQ, 512), BF16),
        grid=(SEQ // T_ATT,),
        in_specs=[pl.BlockSpec(memory_space=pltpu.SMEM),
                  pl.BlockSpec((T_ATT, 1024), lambda i: (i, 0)),
                  pl.BlockSpec((T_ATT, 256), lambda i: (i, 0)),
                  pl.BlockSpec((WINDOW, 256), lambda i: (jnp.maximum(i * hb - 1, 0), 0)),
                  pl.BlockSpec((T_ATT, 256), lambda i: (i, 0)),
                  pl.BlockSpec((WINDOW, 256), lambda i: (jnp.maximum(i * hb - 1, 0), 0))],
        out_specs=pl.BlockSpec((T_ATT, 512), lambda i: (i, 0)),
        compiler_params=_cparams(1),
        name="swa_attn",
    )(sinks, qs, ks, ks, vs, vs)


def _mla_kernel(qi_ref, ki_ref, q_ref, k_ref, v_ref, o_ref, m_ref, acc_ref):
    step = pl.program_id(0)
    qi = qi_ref[step]
    ki = ki_ref[step]

    @pl.when(ki == 0)
    def _():
        m_ref[...] = jnp.full(m_ref.shape, NEG, F32)
        acc_ref[...] = jnp.zeros(acc_ref.shape, F32)

    lane = lax.broadcasted_iota(jnp.int32, (T_ATT, LANES), 1)
    lo = lane < 64
    nc = T_ATT // LANES

    def scores(hd):
        sl = slice(hd * LANES, (hd + 1) * LANES)
        return _dot_nt(q_ref[:, sl], k_ref[:, sl])

    def update(masked):
        if masked:
            r = lax.broadcasted_iota(jnp.int32, (T_ATT, LANES), 0)
        s_next = scores(0)
        for hd in range(MLA_HEADS):
            s = s_next
            if hd + 1 < MLA_HEADS:
                s_next = scores(hd + 1)
            cols = [s[:, c * LANES:(c + 1) * LANES] for c in range(nc)]
            if masked:
                cols = [jnp.where(r >= lane + c * LANES, cols[c], NEG) for c in range(nc)]
            cmax = cols[0]
            for c in range(1, nc):
                cmax = jnp.maximum(cmax, cols[c])
            m_prev = m_ref[hd]
            m_new = jnp.maximum(m_prev, jnp.max(cmax, axis=-1, keepdims=True))
            m_ref[hd] = m_new
            alpha = jnp.exp2(m_prev - m_new)
            p = jnp.concatenate([jnp.exp2(cols[c] - m_new).astype(BF16) for c in range(nc)], axis=-1)
            acc_ref[hd] = acc_ref[hd] * alpha + _dot(p, v_ref[:, hd * LANES:(hd + 1) * LANES])

    @pl.when(ki < qi)
    def _():
        update(False)

    @pl.when(ki == qi)
    def _():
        update(True)
        for j in range(MLA_HEADS // 2):
            ae = acc_ref[2 * j]
            ao = acc_ref[2 * j + 1]
            out = jnp.where(lo, ae * (1.0 / ae[:, 64:65]), ao * (1.0 / ao[:, 0:1]))
            o_ref[:, j * LANES:(j + 1) * LANES] = out.astype(BF16)


def _mla_layer(qm, km, vm):
    nb = SEQ // T_ATT
    qi = np.concatenate([np.full(n + 1, n, np.int32) for n in range(nb)])
    ki = np.concatenate([np.arange(n + 1, dtype=np.int32) for n in range(nb)])
    grid_spec = pltpu.PrefetchScalarGridSpec(
        num_scalar_prefetch=2,
        grid=(int(qi.shape[0]),),
        in_specs=[pl.BlockSpec((T_ATT, 1024), lambda s, qi, ki: (qi[s], 0)),
                  pl.BlockSpec((T_ATT, 1024), lambda s, qi, ki: (ki[s], 0)),
                  pl.BlockSpec((T_ATT, 1024), lambda s, qi, ki: (ki[s], 0))],
        out_specs=pl.BlockSpec((T_ATT, 512), lambda s, qi, ki: (qi[s], 0)),
        scratch_shapes=[pltpu.VMEM((MLA_HEADS, T_ATT, LANES), F32),
                        pltpu.VMEM((MLA_HEADS, T_ATT, LANES), F32)],
    )
    return pl.pallas_call(
        _mla_kernel,
        out_shape=jax.ShapeDtypeStruct((SEQ, 512), BF16),
        grid_spec=grid_spec,
        compiler_params=_cparams(1),
        name="mla_attn",
    )(jnp.asarray(qi), jnp.asarray(ki), qm, km, vm)


def _post_kernel(x_ref, os_ref, om_ref, mod_ref, n2_ref, wout_ref, wrh_ref, wrl_ref, br_ref,
                 xo_ref, h2_ref, meta_ref, cnt_ref, carry_ref):
    y = _dot(os_ref[...], wout_ref[0:512, :]) + _dot(om_ref[...], wout_ref[512:1024, :])
    _tail(x_ref[...], y, mod_ref[...], n2_ref[...], wrh_ref[...], wrl_ref[...], br_ref[...],
          xo_ref, h2_ref, meta_ref, cnt_ref, carry_ref)


def _post_layer(x, o_s, o_m, mod, n2, w_out, wr_hi, wr_lo, br):
    shapes, specs, scratch = _tail_out(T_TOK)
    return pl.pallas_call(
        _post_kernel,
        out_shape=shapes,
        grid=(SEQ // T_TOK,),
        in_specs=[pl.BlockSpec((T_TOK, D), lambda i: (i, 0)),
                  pl.BlockSpec((T_TOK, 512), lambda i: (i, 0)),
                  pl.BlockSpec((T_TOK, 512), lambda i: (i, 0)),
                  _full((8, D)), _full((1, D)), _full((D, D)),
                  _full((D, LANES)), _full((D, LANES)), _full((1, LANES))],
        out_specs=specs,
        scratch_shapes=scratch,
        compiler_params=_cparams(1),
        name="odd_post",
    )(x, o_s, o_m, mod, n2, w_out, wr_hi, wr_lo, br)


def _dispatch_plan(meta, cnt):
    grp = meta[:, EPG].astype(jnp.int32)
    rank = meta[:, EPG + 1].astype(jnp.int32)
    counts = cnt[0, :N_GROUPS].astype(jnp.int32)
    padded = ((counts + T_MOE - 1) // T_MOE) * T_MOE
    ends = jnp.cumsum(padded)
    pos = (ends - padded)[grp] + rank
    n_used = ends[-1] // T_MOE
    tile_start = jnp.arange(N_TILES, dtype=jnp.int32) * T_MOE
    tile_group = jnp.minimum(jnp.sum(tile_start[:, None] >= ends[None, :], axis=1), N_GROUPS - 1)
    return pos.reshape(SEQ // T_DISP, 1, T_DISP), tile_group.astype(jnp.int32), n_used.reshape(1)


def _row_copies(n, src_at, dst_at, sem):
    def body(r, carry):
        pltpu.make_async_copy(src_at(r), dst_at(r), sem).start()
        return carry
    lax.fori_loop(0, n, body, 0)


def _disp_kernel(pos_ref, x_ref, init_ref, o_ref, sem):
    del init_ref
    _row_copies(T_DISP,
                lambda r: x_ref.at[pl.ds(r, 1), :],
                lambda r: o_ref.at[pl.ds(pos_ref[0, 0, r], 1), :], sem)
    pltpu.make_async_copy(x_ref, o_ref.at[pl.ds(0, T_DISP), :], sem).wait()


def _dispatch(pos, h2a):
    return pl.pallas_call(
        _disp_kernel,
        out_shape=jax.ShapeDtypeStruct((N_SORT, ROW_W), F32),
        grid=(SEQ // T_DISP,),
        in_specs=[pl.BlockSpec((1, 1, T_DISP), lambda i: (i, 0, 0), memory_space=pltpu.SMEM),
                  pl.BlockSpec((T_DISP, ROW_W), lambda i: (i, 0)),
                  pl.BlockSpec(memory_space=pl.ANY)],
        out_specs=pl.BlockSpec(memory_space=pl.ANY),
        scratch_shapes=[pltpu.SemaphoreType.DMA(())],
        input_output_aliases={2: 0},
        compiler_params=_cparams(1),
        name="moe_dispatch",
    )(pos, h2a, jnp.zeros((N_SORT, ROW_W), F32))


def _moe_kernel(tg_ref, nu_ref, x_ref, wg_ref, wu_ref, wd_ref, o_ref, xb_ref, acc_ref):
    del tg_ref
    i = pl.program_id(0)
    e = pl.program_id(1)

    @pl.when(i < nu_ref[0])
    def _():
        @pl.when(e == 0)
        def _():
            xb_ref[...] = x_ref[:, 0:D].astype(BF16)

        xb = xb_ref[...]
        a = _dot(xb, wg_ref[0].astype(BF16))
        u = _dot(xb, wu_ref[0].astype(BF16))
        act = (a * jax.nn.sigmoid(a) * u).astype(BF16)
        y = _dot(act, wd_ref[0].astype(BF16))
        meta = x_ref[:, D:ROW_W]
        lane = lax.broadcasted_iota(jnp.int32, meta.shape, 1)
        contrib = jnp.sum(jnp.where(lane == e, meta, 0.0), axis=-1, keepdims=True) * y

        @pl.when(e == 0)
        def _():
            acc_ref[...] = contrib

        @pl.when((e > 0) & (e < EPG - 1))
        def _():
            acc_ref[...] += contrib

        @pl.when(e == EPG - 1)
        def _():
            o_ref[...] = acc_ref[...] + contrib

    @pl.when((i >= nu_ref[0]) & (e == EPG - 1))
    def _():
        o_ref[...] = jnp.zeros(o_ref.shape, F32)


def _moe_experts(tile_group, n_used, hs, wg, wu, wd):
    def tile(i, e, tg, nu):
        return (jnp.minimum(i, nu[0] - 1), 0)

    def expert(i, e, tg, nu):
        return (jnp.where(i < nu[0], tg[i] * EPG + e, tg[nu[0] - 1] * EPG + EPG - 1), 0, 0)

    grid_spec = pltpu.PrefetchScalarGridSpec(
        num_scalar_prefetch=2,
        grid=(N_TILES, EPG),
        in_specs=[pl.BlockSpec((T_MOE, ROW_W), tile),
                  pl.BlockSpec((1, D, FF), expert),
                  pl.BlockSpec((1, D, FF), expert),
                  pl.BlockSpec((1, FF, D), expert)],
        out_specs=pl.BlockSpec((T_MOE, D), lambda i, e, tg, nu: (i, 0)),
        scratch_shapes=[pltpu.VMEM((T_MOE, D), BF16), pltpu.VMEM((T_MOE, D), F32)],
    )
    return pl.pallas_call(
        _moe_kernel,
        out_shape=jax.ShapeDtypeStruct((N_SORT, D), F32),
        grid_spec=grid_spec,
        compiler_params=_cparams(2),
        name="moe_experts",
    )(tile_group, n_used, hs, wg, wu, wd)


def _comb_kernel(pos_ref, x_ref, mod_ref, y_ref, o_ref, buf_ref, sem):
    _row_copies(T_DISP,
                lambda r: y_ref.at[pl.ds(pos_ref[0, 0, r], 1), :],
                lambda r: buf_ref.at[pl.ds(r, 1), :], sem)
    pltpu.make_async_copy(y_ref.at[pl.ds(0, T_DISP), :], buf_ref, sem).wait()
    o_ref[...] = x_ref[...] + mod_ref[5:6, :] * buf_ref[...]


def _combine(pos, x, mod, ys):
    return pl.pallas_call(
        _comb_kernel,
        out_shape=jax.ShapeDtypeStruct((SEQ, D), F32),
        grid=(SEQ // T_DISP,),
        in_specs=[pl.BlockSpec((1, 1, T_DISP), lambda i: (i, 0, 0), memory_space=pltpu.SMEM),
                  pl.BlockSpec((T_DISP, D), lambda i: (i, 0)),
                  pl.BlockSpec((8, D), lambda i: (0, 0)),
                  pl.BlockSpec(memory_space=pl.ANY)],
        out_specs=pl.BlockSpec((T_DISP, D), lambda i: (i, 0)),
        scratch_shapes=[pltpu.VMEM((T_DISP, D), F32), pltpu.SemaphoreType.DMA(())],
        compiler_params=_cparams(1),
        name="moe_combine",
    )(pos, x, mod, ys)


def _moe_layer(x, h2a, meta, cnt, mod, wg, wu, wd):
    pos, tile_group, n_used = _dispatch_plan(meta, cnt)
    hs = _dispatch(pos, h2a)
    ys = _moe_experts(tile_group, n_used, hs, wg, wu, wd)
    return _combine(pos, x, mod, ys)


def _pad_heads(w, heads, dim):
    k = w.shape[0]
    w = w.reshape(k, heads, dim)
    return jnp.pad(w, ((0, 0), (0, 0), (0, LANES - dim))).reshape(k, heads * LANES)


def _pad_gain(g):
    return jnp.pad(g, (0, LANES - g.shape[0])).reshape(1, LANES)


def _odd_weights(w_in):
    q_s = _pad_heads(w_in[:, 0:512], SWA_HEADS, HEAD_DIM)
    k_s = _pad_heads(w_in[:, 512:640], SWA_KV, HEAD_DIM)
    v = w_in[:, 640:768]
    v_s = jnp.concatenate([v[:, 0:64], v[:, 0:64], v[:, 64:128], v[:, 64:128]], axis=1)
    c_q = w_in[:, 768:1152]
    c_kv = w_in[:, 1152:1408]
    k_r = jnp.pad(w_in[:, 1408:1440], ((0, 0), (MLA_NOPE, LANES - MLA_QK)))
    return jnp.concatenate([q_s, k_s, v_s, c_q, c_kv, k_r], axis=1).astype(BF16)


def _router_weights(w_group, b_group, w_expert, b_expert):
    w = jnp.pad(jnp.concatenate([w_expert, w_group], axis=1), ((0, 0), (0, LANES - N_EXPERTS - N_GROUPS)))
    hi = w.astype(BF16)
    lo = (w - hi.astype(F32)).astype(BF16)
    b = jnp.pad(jnp.concatenate([b_expert, b_group]), (0, LANES - N_EXPERTS - N_GROUPS)).reshape(1, LANES)
    return hi, lo, b


def kernel(x, c, positions, ada_w, ada_b, norm1_g, norm2_g, cp_w_in, conv_w, pool_w, pool_scale,
           cp_w_out, at_w_in, swa_q_g, swa_k_g, swa_sinks, mla_q_norm_g, mla_kv_norm_g, mla_w_uq,
           mla_w_ukv, mla_q_g, mla_k_g, at_w_out, moe_w_group, moe_b_group, moe_w_expert,
           moe_b_expert, moe_w_gate, moe_w_up, moe_w_down):
    xs = x.reshape(SEQ, D)
    mods = _ada_mod(c, ada_w, ada_b)
    tc, ts1, ts2 = _rope_tables(positions)
    for l in range(DEPTH):
        i = l // 2
        mod = mods[l]
        n1 = norm1_g[l].reshape(1, D)
        n2 = norm2_g[l].reshape(1, D)
        wr_hi, wr_lo, br = _router_weights(moe_w_group[l], moe_b_group[l], moe_w_expert[l], moe_b_expert[l])
        if l % 2 == 0:
            xs, h2a, meta, cnt = _even_layer(
                xs, mod, n1, n2, cp_w_in[i].astype(BF16), conv_w[i], pool_w[i].astype(BF16),
                pool_scale[i].reshape(1, 4 * POOL_G), cp_w_out[i].astype(BF16), wr_hi, wr_lo, br)
        else:
            ukv = mla_w_ukv[i].reshape(KV_LORA, MLA_HEADS, MLA_NOPE + MLA_V)
            wuk = _pad_heads(ukv[:, :, :MLA_NOPE].reshape(KV_LORA, MLA_HEADS * MLA_NOPE), MLA_HEADS, MLA_NOPE)
            wuv = ukv[:, :, MLA_NOPE:].reshape(KV_LORA, MLA_HEADS * MLA_V)
            wuq = _pad_heads(mla_w_uq[i], MLA_HEADS, MLA_QK)
            qs, ks, vs, qm, km, vm = _proj_layer(
                xs, mod, n1, _odd_weights(at_w_in[i]), _pad_gain(swa_q_g[i]), _pad_gain(swa_k_g[i]),
                mla_q_norm_g[i].reshape(1, Q_LORA), mla_kv_norm_g[i].reshape(1, KV_LORA),
                wuq.astype(BF16), wuk.astype(BF16), wuv.astype(BF16),
                _pad_gain(mla_q_g[i]), _pad_gain(mla_k_g[i]), tc, ts1, ts2)
            o_s = _swa_layer(swa_sinks[i], qs, ks, vs)
            o_m = _mla_layer(qm, km, vm)
            xs, h2a, meta, cnt = _post_layer(xs, o_s, o_m, mod, n2, at_w_out[i].astype(BF16), wr_hi, wr_lo, br)
        xs = _moe_layer(xs, h2a, meta, cnt, mod, moe_w_gate[l], moe_w_up[l], moe_w_down[l])
    return xs.reshape(1, SEQ, D)
```

```python
import functools

import numpy as np
import jax
import jax.numpy as jnp
from jax import lax
from jax.experimental import pallas as pl
from jax.experimental.pallas import tpu as pltpu

F32 = jnp.float32
BF16 = jnp.bfloat16

D = 1024
SEQ = 16384
DEPTH = 4
EPS = 1e-6
LANES = 128
CONV_CH = 512
POOL_WINDOWS = (2, 4, 8, 16)
POOL_G = 128
HALO = 16
SWA_HEADS = 8
SWA_KV = 2
HEAD_DIM = 64
WINDOW = 128
MLA_HEADS = 8
MLA_NOPE = 64
MLA_ROPE = 32
MLA_QK = MLA_NOPE + MLA_ROPE
MLA_V = 64
Q_LORA = 384
KV_LORA = 256
ROPE_THETA = 10000.0
N_GROUPS = 4
EPG = 8
N_EXPERTS = N_GROUPS * EPG
FF = 256
NEG = -1e30
LOG2E = 1.4426950408889634

T_TOK = 512
T_ATT = 512
T_MOE = 1024
T_DISP = 512
ROW_W = D + LANES
N_SORT = SEQ + N_GROUPS * T_MOE
N_TILES = N_SORT // T_MOE
E_STEP = 4
VMEM_LIMIT = 48 * 1024 * 1024
MOE_VMEM_LIMIT = 56 * 1024 * 1024

O_QS, O_KS, O_VS, O_CQ, O_CKV, O_KR, ODD_W = 0, 1024, 1280, 1536, 1920, 2176, 2304


def _cparams(n_axes=1):
    return pltpu.CompilerParams(dimension_semantics=("arbitrary",) * n_axes,
                                vmem_limit_bytes=VMEM_LIMIT)


def _rms(x):
    return x * lax.rsqrt(jnp.mean(x * x, axis=-1, keepdims=True) + EPS)


def _dot(a, b):
    return jnp.dot(a, b, preferred_element_type=F32)


def _dot_nt(a, b):
    return lax.dot_general(a, b, (((1,), (1,)), ((), ())), preferred_element_type=F32)


def _ada_kernel(c_ref, w_ref, b_ref, o_ref):
    c = c_ref[...]
    ca = c * jax.nn.sigmoid(c)
    o_ref[0] = jnp.sum(w_ref[0] * ca, axis=0, keepdims=True) + b_ref[0]


def _ada_mod(c, ada_w, ada_b):
    c_col = c.reshape(D, 1)
    b = ada_b.reshape(DEPTH * 6, 1, D)
    out = pl.pallas_call(
        _ada_kernel,
        out_shape=jax.ShapeDtypeStruct((DEPTH * 6, 1, D), F32),
        grid=(DEPTH, 6),
        in_specs=[pl.BlockSpec((D, 1), lambda l, j: (0, 0)),
                  pl.BlockSpec((1, D, D), lambda l, j: (l, 0, j)),
                  pl.BlockSpec((1, 1, D), lambda l, j: (l * 6 + j, 0, 0))],
        out_specs=pl.BlockSpec((1, 1, D), lambda l, j: (l * 6 + j, 0, 0)),
        compiler_params=_cparams(2),
        name="ada_mod",
    )(c_col, ada_w, b)
    mod = out.reshape(DEPTH, 6, D)
    return jnp.pad(mod, ((0, 0), (0, 2), (0, 0)))


def _rope_kernel(pos_ref, inv_ref, c_ref, s1_ref, s2_ref):
    pos = pos_ref[...].astype(F32)
    ang = pos * inv_ref[...]
    lane = lax.broadcasted_iota(jnp.int32, ang.shape, 1)
    cs = jnp.cos(ang)
    sn = jnp.sin(ang)
    c_ref[...] = jnp.where(lane < 64, 1.0, jnp.where(lane < 96, cs, 0.0))
    s1_ref[...] = jnp.where((lane >= 64) & (lane < 80), -sn, 0.0)
    s2_ref[...] = jnp.where((lane >= 80) & (lane < 96), sn, 0.0)


def _rope_tables(positions):
    half = MLA_ROPE // 2
    inv = jnp.power(ROPE_THETA, -jnp.arange(half, dtype=F32) / half)
    inv_lane = jnp.concatenate([jnp.zeros((64,), F32), inv, inv, jnp.zeros((32,), F32)]).reshape(1, LANES)
    pos = positions.reshape(SEQ, 1)
    shp = jax.ShapeDtypeStruct((SEQ, LANES), F32)
    spec = pl.BlockSpec((T_TOK, LANES), lambda i: (i, 0))
    return pl.pallas_call(
        _rope_kernel,
        out_shape=(shp, shp, shp),
        grid=(SEQ // T_TOK,),
        in_specs=[pl.BlockSpec((T_TOK, 1), lambda i: (i, 0)),
                  pl.BlockSpec((1, LANES), lambda i: (0, 0))],
        out_specs=(spec, spec, spec),
        compiler_params=_cparams(1),
        name="rope_tables",
    )(pos, inv_lane)


def _route(lg):
    lane = lax.broadcasted_iota(jnp.int32, lg.shape, 1)
    lane_f = lane.astype(F32)
    is_g = (lane >= N_EXPERTS) & (lane < N_EXPERTS + N_GROUPS)
    gl = jnp.where(is_g, lg, NEG)
    gmax = jnp.max(gl, axis=-1, keepdims=True)
    gidx = jnp.min(jnp.where(is_g & (gl == gmax), lane_f - N_EXPERTS, 1e3), axis=-1, keepdims=True)
    gsum = jnp.sum(jnp.where(is_g, jnp.exp(gl - gmax), 0.0), axis=-1, keepdims=True)
    gw = 1.0 / gsum
    grp_of_lane = (lane >> 3).astype(F32)
    in_grp = (lane < N_EXPERTS) & (grp_of_lane == gidx)
    el = jnp.where(in_grp, lg, NEG)
    m1 = jnp.max(el, axis=-1, keepdims=True)
    i1 = jnp.min(jnp.where(in_grp & (el == m1), lane_f, 1e3), axis=-1, keepdims=True)
    rest = in_grp & (lane_f != i1)
    el2 = jnp.where(rest, lg, NEG)
    m2 = jnp.max(el2, axis=-1, keepdims=True)
    i2 = jnp.min(jnp.where(rest & (el2 == m2), lane_f, 1e3), axis=-1, keepdims=True)
    r = jnp.exp(m2 - m1)
    w1 = gw / (1.0 + r)
    w2 = w1 * r
    return jnp.where(lane_f == i1, w1, jnp.where(lane_f == i2, w2, 0.0)), gidx


def _tail(x, y, mod, n2, wr_hi, wr_lo, br, xo_ref, h2_ref, meta_ref, cnt_ref, carry_ref):
    i = pl.program_id(0)

    @pl.when(i == 0)
    def _():
        carry_ref[...] = jnp.zeros(carry_ref.shape, F32)

    gate1 = mod[2:3]
    shift2, scale2 = mod[3:4], mod[4:5]
    xn = x + gate1 * y
    xo_ref[...] = xn
    h2 = _rms(xn) * n2 * (1.0 + scale2) + shift2
    hi = h2.astype(BF16)
    lo = (h2 - hi.astype(F32)).astype(BF16)
    lg = _dot(hi, wr_hi) + _dot(lo, wr_hi) + _dot(hi, wr_lo) + br
    gates, gidx = _route(lg)
    rows = lg.shape[0]
    lane = lax.broadcasted_iota(jnp.int32, (rows, LANES), 1)
    lane_f = lane.astype(F32)
    g8 = gates
    for g in range(1, N_GROUPS):
        g8 = g8 + pltpu.roll(gates, LANES - EPG * g, 1)
    onehot = (lane_f == gidx).astype(F32)
    r = lax.broadcasted_iota(jnp.int32, (rows, rows), 0)
    c = lax.broadcasted_iota(jnp.int32, (rows, rows), 1)
    before = jnp.where(c < r, 1.0, 0.0).astype(BF16)
    cum = _dot(before, onehot.astype(BF16)) + carry_ref[0:1, :]
    rank = jnp.sum(jnp.where(lane_f == gidx, cum, 0.0), axis=-1, keepdims=True)
    meta = jnp.where(lane < EPG, g8, jnp.where(lane == EPG, gidx, jnp.where(lane == EPG + 1, rank, 0.0)))
    h2_ref[:, 0:D] = h2
    h2_ref[:, D:D + LANES] = meta
    meta_ref[...] = meta
    total = carry_ref[0:1, :] + jnp.sum(onehot, axis=0, keepdims=True)
    carry_ref[...] = jnp.broadcast_to(total, carry_ref.shape)
    cnt_ref[...] = jnp.broadcast_to(total, cnt_ref.shape)


def _even_kernel(x_ref, xh_ref, mod_ref, n1_ref, n2_ref, win_ref, cw_ref, pw_ref, ps_ref,
                 wout_ref, wrh_ref, wrl_ref, br_ref, xo_ref, h2_ref, meta_ref, cnt_ref, carry_ref):
    i = pl.program_id(0)
    x = x_ref[...]
    mod = mod_ref[...]
    shift1, scale1 = mod[0:1], mod[1:2]
    xa = jnp.concatenate([xh_ref[...], x], axis=0)
    h = _rms(xa) * n1_ref[...] * (1.0 + scale1) + shift1
    z = _dot(h.astype(BF16), win_ref[...])
    rows = T_TOK + HALO
    row = lax.broadcasted_iota(jnp.int32, (rows, 1), 0)
    tpos = i * T_TOK + row - HALO
    live = (tpos >= 0).astype(F32)
    bg = z[:, 0:CONV_CH]
    v = z[:, CONV_CH:2 * CONV_CH] * z[:, 2 * CONV_CH:3 * CONV_CH] * live
    cw = cw_ref[...]
    conv = v * cw[0:1] + pltpu.roll(v, 1, 0) * cw[1:2] + pltpu.roll(v, 2, 0) * cw[2:3]
    parts = [(bg * conv)[HALO:]]
    ps = ps_ref[...]
    tcount = (tpos + 1).astype(F32)
    for gi, w in enumerate(POOL_WINDOWS):
        ug = z[:, 3 * CONV_CH + gi * POOL_G: 3 * CONV_CH + (gi + 1) * POOL_G] * live
        s = ug
        k = 1
        while k < w:
            s = s + pltpu.roll(s, k, 0)
            k *= 2
        inv = 1.0 / jnp.minimum(tcount, float(w))
        d = (s * inv - ug)[HALO:].astype(BF16)
        parts.append(_dot(d, pw_ref[gi]) * ps[:, gi * POOL_G:(gi + 1) * POOL_G])
    cat = jnp.concatenate(parts, axis=-1).astype(BF16)
    y = _dot(cat, wout_ref[...])
    _tail(x, y, mod, n2_ref[...], wrh_ref[...], wrl_ref[...], br_ref[...], xo_ref, h2_ref, meta_ref,
          cnt_ref, carry_ref)


def _tail_out(n):
    shapes = (jax.ShapeDtypeStruct((SEQ, D), F32), jax.ShapeDtypeStruct((SEQ, ROW_W), F32),
              jax.ShapeDtypeStruct((SEQ, LANES), F32), jax.ShapeDtypeStruct((8, LANES), F32))
    specs = (pl.BlockSpec((n, D), lambda i: (i, 0)), pl.BlockSpec((n, ROW_W), lambda i: (i, 0)),
             pl.BlockSpec((n, LANES), lambda i: (i, 0)), pl.BlockSpec((8, LANES), lambda i: (0, 0)))
    scratch = [pltpu.VMEM((8, LANES), F32)]
    return shapes, specs, scratch


def _full(shape):
    nd = len(shape)
    return pl.BlockSpec(shape, lambda i: (0,) * nd)


def _even_layer(x, mod, n1, n2, w_in, conv_w, pool_w, pool_scale, w_out, wr_hi, wr_lo, br):
    shapes, specs, scratch = _tail_out(T_TOK)
    hb = T_TOK // HALO
    return pl.pallas_call(
        _even_kernel,
        out_shape=shapes,
        grid=(SEQ // T_TOK,),
        in_specs=[pl.BlockSpec((T_TOK, D), lambda i: (i, 0)),
                  pl.BlockSpec((HALO, D), lambda i: (jnp.maximum(i * hb - 1, 0), 0)),
                  _full((8, D)), _full((1, D)), _full((1, D)),
                  _full((D, 4 * CONV_CH)), _full((3, CONV_CH)), _full((4, POOL_G, POOL_G)),
                  _full((1, 4 * POOL_G)), _full((D, D)),
                  _full((D, LANES)), _full((D, LANES)), _full((1, LANES))],
        out_specs=specs,
        scratch_shapes=scratch,
        compiler_params=_cparams(1),
        name="even_mixer",
    )(x, x, mod, n1, n2, w_in, conv_w, pool_w, pool_scale, w_out, wr_hi, wr_lo, br)


def _proj_kernel(x_ref, mod_ref, n1_ref, win_ref, gsq_ref, gsk_ref, gqn_ref, gkvn_ref,
                 wuq_ref, wuk_ref, wuv_ref, gmq_ref, gmk_ref, c_ref, s1_ref, s2_ref,
                 qs_ref, ks_ref, vs_ref, qm_ref, km_ref, vm_ref):
    x = x_ref[...]
    mod = mod_ref[...]
    shift1, scale1 = mod[0:1], mod[1:2]
    h = _rms(x) * n1_ref[...] * (1.0 + scale1) + shift1
    z = _dot(h.astype(BF16), win_ref[...])

    def head_norm(t, g, dim):
        ms = jnp.sum(t * t, axis=-1, keepdims=True) * (1.0 / dim)
        return t * lax.rsqrt(ms + EPS) * g

    gsq, gsk = gsq_ref[...], gsk_ref[...]
    for hd in range(SWA_HEADS):
        qh = head_norm(z[:, O_QS + hd * LANES: O_QS + (hd + 1) * LANES], gsq, HEAD_DIM)
        qs_ref[:, hd * LANES:(hd + 1) * LANES] = (qh * (HEAD_DIM ** -0.5)).astype(BF16)
    for kv in range(SWA_KV):
        kh = head_norm(z[:, O_KS + kv * LANES: O_KS + (kv + 1) * LANES], gsk, HEAD_DIM)
        ks_ref[:, kv * LANES:(kv + 1) * LANES] = kh.astype(BF16)
    vs_ref[...] = z[:, O_VS:O_CQ].astype(BF16)

    cq = (_rms(z[:, O_CQ:O_CKV]) * gqn_ref[...]).astype(BF16)
    ckv = (_rms(z[:, O_CKV:O_KR]) * gkvn_ref[...]).astype(BF16)
    qm = _dot(cq, wuq_ref[...])
    kn = _dot(ckv, wuk_ref[...])
    vm = _dot(ckv, wuv_ref[...])
    lane = lax.broadcasted_iota(jnp.int32, (T_TOK, LANES), 1)
    for j in range(MLA_HEADS // 2):
        vv = vm[:, j * LANES:(j + 1) * LANES]
        even = jnp.where(lane < 64, vv, jnp.where(lane == 64, 1.0, 0.0))
        odd = jnp.where(lane < 64, jnp.where(lane == 0, 1.0, 0.0), vv)
        vm_ref[:, (2 * j) * LANES:(2 * j + 1) * LANES] = even.astype(BF16)
        vm_ref[:, (2 * j + 1) * LANES:(2 * j + 2) * LANES] = odd.astype(BF16)
    kr = z[:, O_KR:ODD_W]
    cs, s1, s2 = c_ref[...], s1_ref[...], s2_ref[...]

    def rope(t):
        return t * cs + pltpu.roll(t, LANES - 16, 1) * s1 + pltpu.roll(t, 16, 1) * s2

    gmq, gmk = gmq_ref[...], gmk_ref[...]
    for hd in range(MLA_HEADS):
        sl = slice(hd * LANES, (hd + 1) * LANES)
        qh = rope(head_norm(qm[:, sl], gmq, MLA_QK))
        qm_ref[:, sl] = (qh * (MLA_QK ** -0.5 * LOG2E)).astype(BF16)
        kh = rope(head_norm(kn[:, sl] + kr, gmk, MLA_QK))
        km_ref[:, sl] = kh.astype(BF16)


def _proj_layer(x, mod, n1, w_in, gsq, gsk, gqn, gkvn, wuq, wuk, wuv, gmq, gmk, tc, ts1, ts2):
    def tok(wd):
        return pl.BlockSpec((T_TOK, wd), lambda i: (i, 0))
    widths = (1024, 256, 256, 1024, 1024, 1024)
    return pl.pallas_call(
        _proj_kernel,
        out_shape=tuple(jax.ShapeDtypeStruct((SEQ, wd), BF16) for wd in widths),
        grid=(SEQ // T_TOK,),
        in_specs=[tok(D), _full((8, D)), _full((1, D)), _full((D, ODD_W)),
                  _full((1, LANES)), _full((1, LANES)), _full((1, Q_LORA)), _full((1, KV_LORA)),
                  _full((Q_LORA, 1024)), _full((KV_LORA, 1024)), _full((KV_LORA, 512)),
                  _full((1, LANES)), _full((1, LANES)), tok(LANES), tok(LANES), tok(LANES)],
        out_specs=tuple(tok(wd) for wd in widths),
        compiler_params=_cparams(1),
        name="odd_proj",
    )(x, mod, n1, w_in, gsq, gsk, gqn, gkvn, wuq, wuk, wuv, gmq, gmk, tc, ts1, ts2)


def _swa_kernel(sink_ref, q_ref, k_ref, kh_ref, v_ref, vh_ref, o_ref):
    i = pl.program_id(0)
    kcat = jnp.concatenate([kh_ref[...], k_ref[...]], axis=0)
    vcat = jnp.concatenate([vh_ref[...], v_ref[...]], axis=0)
    r = lax.broadcasted_iota(jnp.int32, (WINDOW, 2 * WINDOW), 0)
    c = lax.broadcasted_iota(jnp.int32, (WINDOW, 2 * WINDOW), 1)
    rel = WINDOW + r - c
    lane = lax.broadcasted_iota(jnp.int32, (2 * WINDOW, LANES), 1)
    for sb in range(T_ATT // WINDOW):
        kb = kcat[sb * WINDOW: sb * WINDOW + 2 * WINDOW]
        vb = vcat[sb * WINDOW: sb * WINDOW + 2 * WINDOW]
        kpos = i * T_ATT + (sb - 1) * WINDOW + c
        ok = (rel >= 0) & (rel < WINDOW) & (kpos >= 0)
        for j in range(SWA_HEADS // 2):
            kv = (2 * j) // (SWA_HEADS // SWA_KV)
            kk = kb[:, kv * LANES:(kv + 1) * LANES]
            vv = vb[:, kv * LANES:(kv + 1) * LANES]
            vsel = (jnp.where(lane < 64, vv, jnp.zeros_like(vv)),
                    jnp.where(lane >= 64, vv, jnp.zeros_like(vv)))
            out = None
            for hh in range(2):
                hd = 2 * j + hh
                q = q_ref[sb * WINDOW:(sb + 1) * WINDOW, hd * LANES:(hd + 1) * LANES]
                s = jnp.where(ok, _dot_nt(q, kk), NEG)
                sink = sink_ref[hd]
                m = jnp.maximum(jnp.max(s, axis=-1, keepdims=True), sink)
                e = jnp.exp(s - m)
                den = jnp.sum(e, axis=-1, keepdims=True) + jnp.exp(sink - m)
                p = (e / den).astype(BF16)
                dd = _dot(p, vsel[hh])
                out = dd if out is None else out + dd
            o_ref[sb * WINDOW:(sb + 1) * WINDOW, j * LANES:(j + 1) * LANES] = out.astype(BF16)


def _swa_layer(sinks, qs, ks, vs):
    hb = T_ATT // WINDOW
    return pl.pallas_call(
        _swa_kernel,
        out_shape=jax.ShapeDtypeStruct((SEQ, 512), BF16),
        grid=(SEQ // T_ATT,),
        in_specs=[pl.BlockSpec(memory_space=pltpu.SMEM),
                  pl.BlockSpec((T_ATT, 1024), lambda i: (i, 0)),
                  pl.BlockSpec((T_ATT, 256), lambda i: (i, 0)),
                  pl.BlockSpec((WINDOW, 256), lambda i: (jnp.maximum(i * hb - 1, 0), 0)),
                  pl.BlockSpec((T_ATT, 256), lambda i: (i, 0)),
                  pl.BlockSpec((WINDOW, 256), lambda i: (jnp.maximum(i * hb - 1, 0), 0))],
        out_specs=pl.BlockSpec((T_ATT, 512), lambda i: (i, 0)),
        compiler_params=_cparams(1),
        name="swa_attn",
    )(sinks, qs, ks, ks, vs, vs)


def _mla_kernel(qi_ref, ki_ref, q_ref, k_ref, v_ref, o_ref, m_ref, acc_ref):
    step = pl.program_id(0)
    qi = qi_ref[step]
    ki = ki_ref[step]

    @pl.when(ki == 0)
    def _():
        m_ref[...] = jnp.full(m_ref.shape, NEG, F32)
        acc_ref[...] = jnp.zeros(acc_ref.shape, F32)

    lane = lax.broadcasted_iota(jnp.int32, (T_ATT, LANES), 1)
    lo = lane < 64
    nc = T_ATT // LANES

    def scores(hd):
        sl = slice(hd * LANES, (hd + 1) * LANES)
        return _dot_nt(q_ref[:, sl], k_ref[:, sl])

    def update(masked):
        if masked:
            r = lax.broadcasted_iota(jnp.int32, (T_ATT, LANES), 0)
        s_next = scores(0)
        for hd in range(MLA_HEADS):
            s = s_next
            if hd + 1 < MLA_HEADS:
                s_next = scores(hd + 1)
            cols = [s[:, c * LANES:(c + 1) * LANES] for c in range(nc)]
            if masked:
                cols = [jnp.where(r >= lane + c * LANES, cols[c], NEG) for c in range(nc)]
            cmax = cols[0]
            for c in range(1, nc):
                cmax = jnp.maximum(cmax, cols[c])
            m_prev = m_ref[hd]
            m_new = jnp.maximum(m_prev, jnp.max(cmax, axis=-1, keepdims=True))
            m_ref[hd] = m_new
            alpha = jnp.exp2(m_prev - m_new)
            p = jnp.concatenate([jnp.exp2(cols[c] - m_new).astype(BF16) for c in range(nc)], axis=-1)
            acc_ref[hd] = acc_ref[hd] * alpha + _dot(p, v_ref[:, hd * LANES:(hd + 1) * LANES])

    @pl.when(ki < qi)
    def _():
        update(False)

    @pl.when(ki == qi)
    def _():
        update(True)
        for j in range(MLA_HEADS // 2):
            ae = acc_ref[2 * j]
            ao = acc_ref[2 * j + 1]
            out = jnp.where(lo, ae * (1.0 / ae[:, 64:65]), ao * (1.0 / ao[:, 0:1]))
            o_ref[:, j * LANES:(j + 1) * LANES] = out.astype(BF16)


def _mla_layer(qm, km, vm):
    nb = SEQ // T_ATT
    qi = np.concatenate([np.full(n + 1, n, np.int32) for n in range(nb)])
    ki = np.concatenate([np.arange(n + 1, dtype=np.int32) for n in range(nb)])
    grid_spec = pltpu.PrefetchScalarGridSpec(
        num_scalar_prefetch=2,
        grid=(int(qi.shape[0]),),
        in_specs=[pl.BlockSpec((T_ATT, 1024), lambda s, qi, ki: (qi[s], 0)),
                  pl.BlockSpec((T_ATT, 1024), lambda s, qi, ki: (ki[s], 0)),
                  pl.BlockSpec((T_ATT, 1024), lambda s, qi, ki: (ki[s], 0))],
        out_specs=pl.BlockSpec((T_ATT, 512), lambda s, qi, ki: (qi[s], 0)),
        scratch_shapes=[pltpu.VMEM((MLA_HEADS, T_ATT, LANES), F32),
                        pltpu.VMEM((MLA_HEADS, T_ATT, LANES), F32)],
    )
    return pl.pallas_call(
        _mla_kernel,
        out_shape=jax.ShapeDtypeStruct((SEQ, 512), BF16),
        grid_spec=grid_spec,
        compiler_params=_cparams(1),
        name="mla_attn",
    )(jnp.asarray(qi), jnp.asarray(ki), qm, km, vm)


def _post_kernel(x_ref, os_ref, om_ref, mod_ref, n2_ref, wout_ref, wrh_ref, wrl_ref, br_ref,
                 xo_ref, h2_ref, meta_ref, cnt_ref, carry_ref):
    y = _dot(os_ref[...], wout_ref[0:512, :]) + _dot(om_ref[...], wout_ref[512:1024, :])
    _tail(x_ref[...], y, mod_ref[...], n2_ref[...], wrh_ref[...], wrl_ref[...], br_ref[...],
          xo_ref, h2_ref, meta_ref, cnt_ref, carry_ref)


def _post_layer(x, o_s, o_m, mod, n2, w_out, wr_hi, wr_lo, br):
    shapes, specs, scratch = _tail_out(T_TOK)
    return pl.pallas_call(
        _post_kernel,
        out_shape=shapes,
        grid=(SEQ // T_TOK,),
        in_specs=[pl.BlockSpec((T_TOK, D), lambda i: (i, 0)),
                  pl.BlockSpec((T_TOK, 512), lambda i: (i, 0)),
                  pl.BlockSpec((T_TOK, 512), lambda i: (i, 0)),
                  _full((8, D)), _full((1, D)), _full((D, D)),
                  _full((D, LANES)), _full((D, LANES)), _full((1, LANES))],
        out_specs=specs,
        scratch_shapes=scratch,
        compiler_params=_cparams(1),
        name="odd_post",
    )(x, o_s, o_m, mod, n2, w_out, wr_hi, wr_lo, br)


def _dispatch_plan(meta, cnt):
    grp = meta[:, EPG].astype(jnp.int32)
    rank = meta[:, EPG + 1].astype(jnp.int32)
    counts = cnt[0, :N_GROUPS].astype(jnp.int32)
    padded = ((counts + T_MOE - 1) // T_MOE) * T_MOE
    ends = jnp.cumsum(padded)
    pos = (ends - padded)[grp] + rank
    n_used = ends[-1] // T_MOE
    tile_start = jnp.arange(N_TILES, dtype=jnp.int32) * T_MOE
    tile_group = jnp.minimum(jnp.sum(tile_start[:, None] >= ends[None, :], axis=1), N_GROUPS - 1)
    return pos.reshape(SEQ // T_DISP, 1, T_DISP), tile_group.astype(jnp.int32), n_used.reshape(1)


def _row_copies(n, src_at, dst_at, sem):
    def body(r, carry):
        pltpu.make_async_copy(src_at(r), dst_at(r), sem).start()
        return carry
    lax.fori_loop(0, n, body, 0, unroll=8)


def _disp_kernel(pos_ref, x_ref, init_ref, o_ref, sem):
    del init_ref
    _row_copies(T_DISP,
                lambda r: x_ref.at[pl.ds(r, 1), :],
                lambda r: o_ref.at[pl.ds(pos_ref[0, 0, r], 1), :], sem)
    pltpu.make_async_copy(x_ref, o_ref.at[pl.ds(0, T_DISP), :], sem).wait()


def _dispatch(pos, h2a, hs):
    return pl.pallas_call(
        _disp_kernel,
        out_shape=jax.ShapeDtypeStruct((N_SORT, ROW_W), F32),
        grid=(SEQ // T_DISP,),
        in_specs=[pl.BlockSpec((1, 1, T_DISP), lambda i: (i, 0, 0), memory_space=pltpu.SMEM),
                  pl.BlockSpec((T_DISP, ROW_W), lambda i: (i, 0)),
                  pl.BlockSpec(memory_space=pl.ANY)],
        out_specs=pl.BlockSpec(memory_space=pl.ANY),
        scratch_shapes=[pltpu.SemaphoreType.DMA(())],
        input_output_aliases={2: 0},
        compiler_params=_cparams(1),
        name="moe_dispatch",
    )(pos, h2a, hs)


def _moe_kernel(tg_ref, nu_ref, x_ref, wg_ref, wu_ref, wd_ref, o_ref, xb_ref, act_ref):
    del tg_ref
    i = pl.program_id(0)
    j = pl.program_id(1)
    last = EPG // E_STEP - 1

    @pl.when(i < nu_ref[0])
    def _():
        @pl.when(j == 0)
        def _():
            xb_ref[...] = x_ref[:, 0:D].astype(BF16)

        xb = xb_ref[...]
        meta = x_ref[:, D:ROW_W]
        lane = lax.broadcasted_iota(jnp.int32, meta.shape, 1)
        for k in range(E_STEP):
            a = _dot(xb, wg_ref[0, k].astype(BF16))
            u = _dot(xb, wu_ref[0, k].astype(BF16))
            gate = jnp.sum(jnp.where(lane == j * E_STEP + k, meta, 0.0), axis=-1, keepdims=True)
            act = (a * jax.nn.sigmoid(a) * u * gate).astype(BF16)
            for jj in range(EPG // E_STEP):
                @pl.when(j == jj)
                def _(jj=jj, k=k, act=act):
                    act_ref[:, (jj * E_STEP + k) * FF:(jj * E_STEP + k + 1) * FF] = act

        @pl.when(j == last)
        def _():
            o_ref[...] = _dot(act_ref[...], wd_ref[0, 0].astype(BF16))

    @pl.when((i >= nu_ref[0]) & (j == last))
    def _():
        o_ref[...] = jnp.zeros(o_ref.shape, F32)


def _moe_experts(tile_group, n_used, hs, wg, wu, wd, layer):
    def tile(i, e, tg, nu):
        return (jnp.minimum(i, nu[0] - 1), 0)

    nj = EPG // E_STEP

    def expert(i, j, tg, nu):
        return (layer, jnp.where(i < nu[0], tg[i] * nj + j, tg[nu[0] - 1] * nj + nj - 1), 0, 0)

    def group(i, j, tg, nu):
        return (layer, tg[jnp.minimum(i, nu[0] - 1)], 0, 0)

    grid_spec = pltpu.PrefetchScalarGridSpec(
        num_scalar_prefetch=2,
        grid=(N_TILES, nj),
        in_specs=[pl.BlockSpec((T_MOE, ROW_W), tile),
                  pl.BlockSpec((1, E_STEP, D, FF), expert),
                  pl.BlockSpec((1, E_STEP, D, FF), expert),
                  pl.BlockSpec((1, 1, EPG * FF, D), group, pipeline_mode=pl.Buffered(1))],
        out_specs=pl.BlockSpec((T_MOE, D), lambda i, e, tg, nu: (i, 0)),
        scratch_shapes=[pltpu.VMEM((T_MOE, D), BF16), pltpu.VMEM((T_MOE, EPG * FF), BF16)],
    )
    return pl.pallas_call(
        _moe_kernel,
        out_shape=jax.ShapeDtypeStruct((N_SORT, D), F32),
        grid_spec=grid_spec,
        compiler_params=pltpu.CompilerParams(dimension_semantics=("arbitrary", "arbitrary"),
                                             vmem_limit_bytes=MOE_VMEM_LIMIT),
        name="moe_experts",
    )(tile_group, n_used, hs, wg, wu, wd.reshape(DEPTH, N_GROUPS, EPG * FF, D))


def _comb_kernel(pos_ref, x_ref, mod_ref, y_ref, o_ref, buf_ref, sem):
    _row_copies(T_DISP,
                lambda r: y_ref.at[pl.ds(pos_ref[0, 0, r], 1), :],
                lambda r: buf_ref.at[pl.ds(r, 1), :], sem)
    pltpu.make_async_copy(y_ref.at[pl.ds(0, T_DISP), :], buf_ref, sem).wait()
    o_ref[...] = x_ref[...] + mod_ref[5:6, :] * buf_ref[...]


def _combine(pos, x, mod, ys):
    return pl.pallas_call(
        _comb_kernel,
        out_shape=jax.ShapeDtypeStruct((SEQ, D), F32),
        grid=(SEQ // T_DISP,),
        in_specs=[pl.BlockSpec((1, 1, T_DISP), lambda i: (i, 0, 0), memory_space=pltpu.SMEM),
                  pl.BlockSpec((T_DISP, D), lambda i: (i, 0)),
                  pl.BlockSpec((8, D), lambda i: (0, 0)),
                  pl.BlockSpec(memory_space=pl.ANY)],
        out_specs=pl.BlockSpec((T_DISP, D), lambda i: (i, 0)),
        scratch_shapes=[pltpu.VMEM((T_DISP, D), F32), pltpu.SemaphoreType.DMA(())],
        compiler_params=_cparams(1),
        name="moe_combine",
    )(pos, x, mod, ys)


def _moe_layer(x, h2a, meta, cnt, mod, wg, wu, wd, layer, hs):
    pos, tile_group, n_used = _dispatch_plan(meta, cnt)
    hs = _dispatch(pos, h2a, hs)
    ys = _moe_experts(tile_group, n_used, hs, wg, wu, wd, layer)
    return _combine(pos, x, mod, ys), hs


def _pad_heads(w, heads, dim):
    k = w.shape[0]
    w = w.reshape(k, heads, dim)
    return jnp.pad(w, ((0, 0), (0, 0), (0, LANES - dim))).reshape(k, heads * LANES)


def _pad_gain(g):
    return jnp.pad(g, (0, LANES - g.shape[0])).reshape(1, LANES)


def _odd_weights(w_in):
    q_s = _pad_heads(w_in[:, 0:512], SWA_HEADS, HEAD_DIM)
    k_s = _pad_heads(w_in[:, 512:640], SWA_KV, HEAD_DIM)
    v = w_in[:, 640:768]
    v_s = jnp.concatenate([v[:, 0:64], v[:, 0:64], v[:, 64:128], v[:, 64:128]], axis=1)
    c_q = w_in[:, 768:1152]
    c_kv = w_in[:, 1152:1408]
    k_r = jnp.pad(w_in[:, 1408:1440], ((0, 0), (MLA_NOPE, LANES - MLA_QK)))
    return jnp.concatenate([q_s, k_s, v_s, c_q, c_kv, k_r], axis=1).astype(BF16)


def _router_weights(w_group, b_group, w_expert, b_expert):
    w = jnp.pad(jnp.concatenate([w_expert, w_group], axis=1), ((0, 0), (0, LANES - N_EXPERTS - N_GROUPS)))
    hi = w.astype(BF16)
    lo = (w - hi.astype(F32)).astype(BF16)
    b = jnp.pad(jnp.concatenate([b_expert, b_group]), (0, LANES - N_EXPERTS - N_GROUPS)).reshape(1, LANES)
    return hi, lo, b


def kernel(x, c, positions, ada_w, ada_b, norm1_g, norm2_g, cp_w_in, conv_w, pool_w, pool_scale,
           cp_w_out, at_w_in, swa_q_g, swa_k_g, swa_sinks, mla_q_norm_g, mla_kv_norm_g, mla_w_uq,
           mla_w_ukv, mla_q_g, mla_k_g, at_w_out, moe_w_group, moe_b_group, moe_w_expert,
           moe_b_expert, moe_w_gate, moe_w_up, moe_w_down):
    xs = x.reshape(SEQ, D)
    mods = _ada_mod(c, ada_w, ada_b)
    tc, ts1, ts2 = _rope_tables(positions)
    hs = jnp.zeros((N_SORT, ROW_W), F32)
    for l in range(DEPTH):
        i = l // 2
        mod = mods[l]
        n1 = norm1_g[l].reshape(1, D)
        n2 = norm2_g[l].reshape(1, D)
        wr_hi, wr_lo, br = _router_weights(moe_w_group[l], moe_b_group[l], moe_w_expert[l], moe_b_expert[l])
        if l % 2 == 0:
            xs, h2a, meta, cnt = _even_layer(
                xs, mod, n1, n2, cp_w_in[i].astype(BF16), conv_w[i], pool_w[i].astype(BF16),
                pool_scale[i].reshape(1, 4 * POOL_G), cp_w_out[i].astype(BF16), wr_hi, wr_lo, br)
        else:
            ukv = mla_w_ukv[i].reshape(KV_LORA, MLA_HEADS, MLA_NOPE + MLA_V)
            wuk = _pad_heads(ukv[:, :, :MLA_NOPE].reshape(KV_LORA, MLA_HEADS * MLA_NOPE), MLA_HEADS, MLA_NOPE)
            wuv = ukv[:, :, MLA_NOPE:].reshape(KV_LORA, MLA_HEADS * MLA_V)
            wuq = _pad_heads(mla_w_uq[i], MLA_HEADS, MLA_QK)
            qs, ks, vs, qm, km, vm = _proj_layer(
                xs, mod, n1, _odd_weights(at_w_in[i]), _pad_gain(swa_q_g[i]), _pad_gain(swa_k_g[i]),
                mla_q_norm_g[i].reshape(1, Q_LORA), mla_kv_norm_g[i].reshape(1, KV_LORA),
                wuq.astype(BF16), wuk.astype(BF16), wuv.astype(BF16),
                _pad_gain(mla_q_g[i]), _pad_gain(mla_k_g[i]), tc, ts1, ts2)
            o_s = _swa_layer(swa_sinks[i], qs, ks, vs)
            o_m = _mla_layer(qm, km, vm)
            xs, h2a, meta, cnt = _post_layer(xs, o_s, o_m, mod, n2, at_w_out[i].astype(BF16), wr_hi, wr_lo, br)
        xs, hs = _moe_layer(xs, h2a, meta, cnt, mod, moe_w_gate, moe_w_up, moe_w_down, l, hs)
    return xs.reshape(1, SEQ, D)
```

```python
import functools

import numpy as np
import jax
import jax.numpy as jnp
from jax import lax
from jax.experimental import pallas as pl
from jax.experimental.pallas import tpu as pltpu

F32 = jnp.float32
BF16 = jnp.bfloat16

D = 1024
SEQ = 16384
DEPTH = 4
EPS = 1e-6
LANES = 128
CONV_CH = 512
POOL_WINDOWS = (2, 4, 8, 16)
POOL_G = 128
HALO = 16
SWA_HEADS = 8
SWA_KV = 2
HEAD_DIM = 64
WINDOW = 128
MLA_HEADS = 8
MLA_NOPE = 64
MLA_ROPE = 32
MLA_QK = MLA_NOPE + MLA_ROPE
MLA_V = 64
Q_LORA = 384
KV_LORA = 256
ROPE_THETA = 10000.0
N_GROUPS = 4
EPG = 8
N_EXPERTS = N_GROUPS * EPG
FF = 256
NEG = -1e30
LOG2E = 1.4426950408889634

T_TOK = 512
T_ATT = 512
T_MOE = 1024
T_DISP = 512
ROW_W = D + LANES
N_SORT = SEQ + N_GROUPS * T_MOE
N_TILES = N_SORT // T_MOE
E_STEP = 4
VMEM_LIMIT = 48 * 1024 * 1024
MOE_VMEM_LIMIT = 56 * 1024 * 1024

O_QS, O_KS, O_VS, O_CQ, O_CKV, O_KR, ODD_W = 0, 1024, 1280, 1536, 1920, 2176, 2304


def _cparams(n_axes=1):
    return pltpu.CompilerParams(dimension_semantics=("arbitrary",) * n_axes,
                                vmem_limit_bytes=VMEM_LIMIT)


def _rms(x):
    return x * lax.rsqrt(jnp.mean(x * x, axis=-1, keepdims=True) + EPS)


def _dot(a, b):
    return jnp.dot(a, b, preferred_element_type=F32)


def _dot_nt(a, b):
    return lax.dot_general(a, b, (((1,), (1,)), ((), ())), preferred_element_type=F32)


def _ada_kernel(c_ref, w_ref, b_ref, o_ref):
    c = c_ref[...]
    ca = c * jax.nn.sigmoid(c)
    o_ref[0] = jnp.sum(w_ref[0] * ca, axis=0, keepdims=True) + b_ref[0]


def _ada_mod(c, ada_w, ada_b):
    c_col = c.reshape(D, 1)
    b = ada_b.reshape(DEPTH * 6, 1, D)
    out = pl.pallas_call(
        _ada_kernel,
        out_shape=jax.ShapeDtypeStruct((DEPTH * 6, 1, D), F32),
        grid=(DEPTH, 6),
        in_specs=[pl.BlockSpec((D, 1), lambda l, j: (0, 0)),
                  pl.BlockSpec((1, D, D), lambda l, j: (l, 0, j)),
                  pl.BlockSpec((1, 1, D), lambda l, j: (l * 6 + j, 0, 0))],
        out_specs=pl.BlockSpec((1, 1, D), lambda l, j: (l * 6 + j, 0, 0)),
        compiler_params=_cparams(2),
        name="ada_mod",
    )(c_col, ada_w, b)
    mod = out.reshape(DEPTH, 6, D)
    return jnp.pad(mod, ((0, 0), (0, 2), (0, 0)))


def _rope_kernel(pos_ref, inv_ref, c_ref, s1_ref, s2_ref):
    pos = pos_ref[...].astype(F32)
    ang = pos * inv_ref[...]
    lane = lax.broadcasted_iota(jnp.int32, ang.shape, 1)
    cs = jnp.cos(ang)
    sn = jnp.sin(ang)
    c_ref[...] = jnp.where(lane < 64, 1.0, jnp.where(lane < 96, cs, 0.0))
    s1_ref[...] = jnp.where((lane >= 64) & (lane < 80), -sn, 0.0)
    s2_ref[...] = jnp.where((lane >= 80) & (lane < 96), sn, 0.0)


def _rope_tables(positions):
    half = MLA_ROPE // 2
    inv = jnp.power(ROPE_THETA, -jnp.arange(half, dtype=F32) / half)
    inv_lane = jnp.concatenate([jnp.zeros((64,), F32), inv, inv, jnp.zeros((32,), F32)]).reshape(1, LANES)
    pos = positions.reshape(SEQ, 1)
    shp = jax.ShapeDtypeStruct((SEQ, LANES), F32)
    spec = pl.BlockSpec((T_TOK, LANES), lambda i: (i, 0))
    return pl.pallas_call(
        _rope_kernel,
        out_shape=(shp, shp, shp),
        grid=(SEQ // T_TOK,),
        in_specs=[pl.BlockSpec((T_TOK, 1), lambda i: (i, 0)),
                  pl.BlockSpec((1, LANES), lambda i: (0, 0))],
        out_specs=(spec, spec, spec),
        compiler_params=_cparams(1),
        name="rope_tables",
    )(pos, inv_lane)


def _route(lg):
    lane = lax.broadcasted_iota(jnp.int32, lg.shape, 1)
    lane_f = lane.astype(F32)
    is_g = (lane >= N_EXPERTS) & (lane < N_EXPERTS + N_GROUPS)
    gl = jnp.where(is_g, lg, NEG)
    gmax = jnp.max(gl, axis=-1, keepdims=True)
    gidx = jnp.min(jnp.where(is_g & (gl == gmax), lane_f - N_EXPERTS, 1e3), axis=-1, keepdims=True)
    gsum = jnp.sum(jnp.where(is_g, jnp.exp(gl - gmax), 0.0), axis=-1, keepdims=True)
    gw = 1.0 / gsum
    grp_of_lane = (lane >> 3).astype(F32)
    in_grp = (lane < N_EXPERTS) & (grp_of_lane == gidx)
    el = jnp.where(in_grp, lg, NEG)
    m1 = jnp.max(el, axis=-1, keepdims=True)
    i1 = jnp.min(jnp.where(in_grp & (el == m1), lane_f, 1e3), axis=-1, keepdims=True)
    rest = in_grp & (lane_f != i1)
    el2 = jnp.where(rest, lg, NEG)
    m2 = jnp.max(el2, axis=-1, keepdims=True)
    i2 = jnp.min(jnp.where(rest & (el2 == m2), lane_f, 1e3), axis=-1, keepdims=True)
    r = jnp.exp(m2 - m1)
    w1 = gw / (1.0 + r)
    w2 = w1 * r
    return jnp.where(lane_f == i1, w1, jnp.where(lane_f == i2, w2, 0.0)), gidx


def _tail(x, y, mod, n2, wr_hi, wr_lo, br, xo_ref, h2_ref, meta_ref, cnt_ref, carry_ref):
    i = pl.program_id(0)

    @pl.when(i == 0)
    def _():
        carry_ref[...] = jnp.zeros(carry_ref.shape, F32)

    gate1 = mod[2:3]
    shift2, scale2 = mod[3:4], mod[4:5]
    xn = x + gate1 * y
    xo_ref[...] = xn
    h2 = _rms(xn) * n2 * (1.0 + scale2) + shift2
    hi = h2.astype(BF16)
    lo = (h2 - hi.astype(F32)).astype(BF16)
    lg = _dot(hi, wr_hi) + _dot(lo, wr_hi) + _dot(hi, wr_lo) + br
    gates, gidx = _route(lg)
    rows = lg.shape[0]
    lane = lax.broadcasted_iota(jnp.int32, (rows, LANES), 1)
    lane_f = lane.astype(F32)
    g8 = gates
    for g in range(1, N_GROUPS):
        g8 = g8 + pltpu.roll(gates, LANES - EPG * g, 1)
    onehot = (lane_f == gidx).astype(F32)
    r = lax.broadcasted_iota(jnp.int32, (rows, rows), 0)
    c = lax.broadcasted_iota(jnp.int32, (rows, rows), 1)
    before = jnp.where(c < r, 1.0, 0.0).astype(BF16)
    cum = _dot(before, onehot.astype(BF16)) + carry_ref[0:1, :]
    rank = jnp.sum(jnp.where(lane_f == gidx, cum, 0.0), axis=-1, keepdims=True)
    meta = jnp.where(lane < EPG, g8, jnp.where(lane == EPG, gidx, jnp.where(lane == EPG + 1, rank, 0.0)))
    h2_ref[:, 0:D] = h2
    h2_ref[:, D:D + LANES] = meta
    meta_ref[...] = meta
    total = carry_ref[0:1, :] + jnp.sum(onehot, axis=0, keepdims=True)
    carry_ref[...] = jnp.broadcast_to(total, carry_ref.shape)
    cnt_ref[...] = jnp.broadcast_to(total, cnt_ref.shape)


def _even_kernel(x_ref, xh_ref, mod_ref, n1_ref, n2_ref, win_ref, cw_ref, pw_ref, ps_ref,
                 wout_ref, wrh_ref, wrl_ref, br_ref, xo_ref, h2_ref, meta_ref, cnt_ref, carry_ref):
    i = pl.program_id(0)
    x = x_ref[...]
    mod = mod_ref[...]
    shift1, scale1 = mod[0:1], mod[1:2]
    xa = jnp.concatenate([xh_ref[...], x], axis=0)
    h = _rms(xa) * n1_ref[...] * (1.0 + scale1) + shift1
    z = _dot(h.astype(BF16), win_ref[...])
    rows = T_TOK + HALO
    row = lax.broadcasted_iota(jnp.int32, (rows, 1), 0)
    tpos = i * T_TOK + row - HALO
    live = (tpos >= 0).astype(F32)
    bg = z[:, 0:CONV_CH]
    v = z[:, CONV_CH:2 * CONV_CH] * z[:, 2 * CONV_CH:3 * CONV_CH] * live
    cw = cw_ref[...]
    conv = v * cw[0:1] + pltpu.roll(v, 1, 0) * cw[1:2] + pltpu.roll(v, 2, 0) * cw[2:3]
    parts = [(bg * conv)[HALO:]]
    ps = ps_ref[...]
    tcount = (tpos + 1).astype(F32)
    for gi, w in enumerate(POOL_WINDOWS):
        ug = z[:, 3 * CONV_CH + gi * POOL_G: 3 * CONV_CH + (gi + 1) * POOL_G] * live
        s = ug
        k = 1
        while k < w:
            s = s + pltpu.roll(s, k, 0)
            k *= 2
        inv = 1.0 / jnp.minimum(tcount, float(w))
        d = (s * inv - ug)[HALO:].astype(BF16)
        parts.append(_dot(d, pw_ref[gi]) * ps[:, gi * POOL_G:(gi + 1) * POOL_G])
    cat = jnp.concatenate(parts, axis=-1).astype(BF16)
    y = _dot(cat, wout_ref[...])
    _tail(x, y, mod, n2_ref[...], wrh_ref[...], wrl_ref[...], br_ref[...], xo_ref, h2_ref, meta_ref,
          cnt_ref, carry_ref)


def _tail_out(n):
    shapes = (jax.ShapeDtypeStruct((SEQ, D), F32), jax.ShapeDtypeStruct((SEQ, ROW_W), F32),
              jax.ShapeDtypeStruct((SEQ, LANES), F32), jax.ShapeDtypeStruct((8, LANES), F32))
    specs = (pl.BlockSpec((n, D), lambda i: (i, 0)), pl.BlockSpec((n, ROW_W), lambda i: (i, 0)),
             pl.BlockSpec((n, LANES), lambda i: (i, 0)), pl.BlockSpec((8, LANES), lambda i: (0, 0)))
    scratch = [pltpu.VMEM((8, LANES), F32)]
    return shapes, specs, scratch


def _full(shape):
    nd = len(shape)
    return pl.BlockSpec(shape, lambda i: (0,) * nd)


def _even_layer(x, mod, n1, n2, w_in, conv_w, pool_w, pool_scale, w_out, wr_hi, wr_lo, br):
    shapes, specs, scratch = _tail_out(T_TOK)
    hb = T_TOK // HALO
    return pl.pallas_call(
        _even_kernel,
        out_shape=shapes,
        grid=(SEQ // T_TOK,),
        in_specs=[pl.BlockSpec((T_TOK, D), lambda i: (i, 0)),
                  pl.BlockSpec((HALO, D), lambda i: (jnp.maximum(i * hb - 1, 0), 0)),
                  _full((8, D)), _full((1, D)), _full((1, D)),
                  _full((D, 4 * CONV_CH)), _full((3, CONV_CH)), _full((4, POOL_G, POOL_G)),
                  _full((1, 4 * POOL_G)), _full((D, D)),
                  _full((D, LANES)), _full((D, LANES)), _full((1, LANES))],
        out_specs=specs,
        scratch_shapes=scratch,
        compiler_params=_cparams(1),
        name="even_mixer",
    )(x, x, mod, n1, n2, w_in, conv_w, pool_w, pool_scale, w_out, wr_hi, wr_lo, br)


def _proj_kernel(x_ref, mod_ref, n1_ref, win_ref, gsq_ref, gsk_ref, gqn_ref, gkvn_ref,
                 wuq_ref, wuk_ref, wuv_ref, gmq_ref, gmk_ref, c_ref, s1_ref, s2_ref,
                 qs_ref, ks_ref, vs_ref, qm_ref, km_ref, vm_ref):
    x = x_ref[...]
    mod = mod_ref[...]
    shift1, scale1 = mod[0:1], mod[1:2]
    h = _rms(x) * n1_ref[...] * (1.0 + scale1) + shift1
    z = _dot(h.astype(BF16), win_ref[...])

    def head_norm(t, g, dim):
        ms = jnp.sum(t * t, axis=-1, keepdims=True) * (1.0 / dim)
        return t * lax.rsqrt(ms + EPS) * g

    gsq, gsk = gsq_ref[...], gsk_ref[...]
    for hd in range(SWA_HEADS):
        qh = head_norm(z[:, O_QS + hd * LANES: O_QS + (hd + 1) * LANES], gsq, HEAD_DIM)
        qs_ref[:, hd * LANES:(hd + 1) * LANES] = (qh * (HEAD_DIM ** -0.5)).astype(BF16)
    for kv in range(SWA_KV):
        kh = head_norm(z[:, O_KS + kv * LANES: O_KS + (kv + 1) * LANES], gsk, HEAD_DIM)
        ks_ref[:, kv * LANES:(kv + 1) * LANES] = kh.astype(BF16)
    vs_ref[...] = z[:, O_VS:O_CQ].astype(BF16)

    cq = (_rms(z[:, O_CQ:O_CKV]) * gqn_ref[...]).astype(BF16)
    ckv = (_rms(z[:, O_CKV:O_KR]) * gkvn_ref[...]).astype(BF16)
    qm = _dot(cq, wuq_ref[...])
    kn = _dot(ckv, wuk_ref[...])
    vm = _dot(ckv, wuv_ref[...])
    lane = lax.broadcasted_iota(jnp.int32, (T_TOK, LANES), 1)
    for j in range(MLA_HEADS // 2):
        vv = vm[:, j * LANES:(j + 1) * LANES]
        even = jnp.where(lane < 64, vv, jnp.where(lane == 64, 1.0, 0.0))
        odd = jnp.where(lane < 64, jnp.where(lane == 0, 1.0, 0.0), vv)
        vm_ref[(2 * j) * LANES:(2 * j + 1) * LANES, :] = even.T.astype(BF16)
        vm_ref[(2 * j + 1) * LANES:(2 * j + 2) * LANES, :] = odd.T.astype(BF16)
    kr = z[:, O_KR:ODD_W]
    cs, s1, s2 = c_ref[...], s1_ref[...], s2_ref[...]

    def rope(t):
        return t * cs + pltpu.roll(t, LANES - 16, 1) * s1 + pltpu.roll(t, 16, 1) * s2

    gmq, gmk = gmq_ref[...], gmk_ref[...]
    for hd in range(MLA_HEADS):
        sl = slice(hd * LANES, (hd + 1) * LANES)
        qh = rope(head_norm(qm[:, sl], gmq, MLA_QK))
        qm_ref[:, sl] = (qh * (MLA_QK ** -0.5 * LOG2E)).astype(BF16)
        kh = rope(head_norm(kn[:, sl] + kr, gmk, MLA_QK))
        km_ref[:, sl] = kh.astype(BF16)


def _proj_layer(x, mod, n1, w_in, gsq, gsk, gqn, gkvn, wuq, wuk, wuv, gmq, gmk, tc, ts1, ts2):
    def tok(wd):
        return pl.BlockSpec((T_TOK, wd), lambda i: (i, 0))
    widths = (1024, 256, 256, 1024, 1024)
    return pl.pallas_call(
        _proj_kernel,
        out_shape=tuple(jax.ShapeDtypeStruct((SEQ, wd), BF16) for wd in widths)
        + (jax.ShapeDtypeStruct((MLA_HEADS * LANES, SEQ), BF16),),
        grid=(SEQ // T_TOK,),
        in_specs=[tok(D), _full((8, D)), _full((1, D)), _full((D, ODD_W)),
                  _full((1, LANES)), _full((1, LANES)), _full((1, Q_LORA)), _full((1, KV_LORA)),
                  _full((Q_LORA, 1024)), _full((KV_LORA, 1024)), _full((KV_LORA, 512)),
                  _full((1, LANES)), _full((1, LANES)), tok(LANES), tok(LANES), tok(LANES)],
        out_specs=tuple(tok(wd) for wd in widths)
        + (pl.BlockSpec((MLA_HEADS * LANES, T_TOK), lambda i: (0, i)),),
        compiler_params=_cparams(1),
        name="odd_proj",
    )(x, mod, n1, w_in, gsq, gsk, gqn, gkvn, wuq, wuk, wuv, gmq, gmk, tc, ts1, ts2)


def _swa_kernel(sink_ref, q_ref, k_ref, kh_ref, v_ref, vh_ref, o_ref):
    i = pl.program_id(0)
    kcat = jnp.concatenate([kh_ref[...], k_ref[...]], axis=0)
    vcat = jnp.concatenate([vh_ref[...], v_ref[...]], axis=0)
    grp = SWA_HEADS // SWA_KV
    r = lax.broadcasted_iota(jnp.int32, (grp * WINDOW, 2 * WINDOW), 0) & (WINDOW - 1)
    c = lax.broadcasted_iota(jnp.int32, (grp * WINDOW, 2 * WINDOW), 1)
    rel = WINDOW + r - c
    lane = lax.broadcasted_iota(jnp.int32, (WINDOW, LANES), 1)
    for sb in range(T_ATT // WINDOW):
        rows = slice(sb * WINDOW, (sb + 1) * WINDOW)
        kb = kcat[sb * WINDOW: sb * WINDOW + 2 * WINDOW]
        vb = vcat[sb * WINDOW: sb * WINDOW + 2 * WINDOW]
        kpos = i * T_ATT + (sb - 1) * WINDOW + c
        ok = (rel >= 0) & (rel < WINDOW) & (kpos >= 0)
        outs = []
        for kv in range(SWA_KV):
            q = jnp.concatenate([q_ref[rows, (kv * grp + g) * LANES:(kv * grp + g + 1) * LANES]
                                 for g in range(grp)], axis=0)
            sink = jnp.concatenate([jnp.full((WINDOW, 1), sink_ref[kv * grp + g], F32)
                                    for g in range(grp)], axis=0)
            s = jnp.where(ok, _dot_nt(q, kb[:, kv * LANES:(kv + 1) * LANES]), NEG)
            m = jnp.maximum(jnp.max(s, axis=-1, keepdims=True), sink)
            e = jnp.exp(s - m)
            den = jnp.sum(e, axis=-1, keepdims=True) + jnp.exp(sink - m)
            p = (e * (1.0 / den)).astype(BF16)
            o = _dot(p, vb[:, kv * LANES:(kv + 1) * LANES])
            outs += [o[g * WINDOW:(g + 1) * WINDOW] for g in range(grp)]
        for j in range(SWA_HEADS // 2):
            o_ref[rows, j * LANES:(j + 1) * LANES] = jnp.where(lane < 64, outs[2 * j], outs[2 * j + 1]).astype(BF16)


def _swa_layer(sinks, qs, ks, vs):
    hb = T_ATT // WINDOW
    return pl.pallas_call(
        _swa_kernel,
        out_shape=jax.ShapeDtypeStruct((SEQ, 512), BF16),
        grid=(SEQ // T_ATT,),
        in_specs=[pl.BlockSpec(memory_space=pltpu.SMEM),
                  pl.BlockSpec((T_ATT, 1024), lambda i: (i, 0)),
                  pl.BlockSpec((T_ATT, 256), lambda i: (i, 0)),
                  pl.BlockSpec((WINDOW, 256), lambda i: (jnp.maximum(i * hb - 1, 0), 0)),
                  pl.BlockSpec((T_ATT, 256), lambda i: (i, 0)),
                  pl.BlockSpec((WINDOW, 256), lambda i: (jnp.maximum(i * hb - 1, 0), 0))],
        out_specs=pl.BlockSpec((T_ATT, 512), lambda i: (i, 0)),
        compiler_params=_cparams(1),
        name="swa_attn",
    )(sinks, qs, ks, ks, vs, vs)


def _mla_kernel(qi_ref, ki_ref, q_ref, k_ref, vt_ref, o_ref, m_ref, acc_ref):
    step = pl.program_id(0)
    qi = qi_ref[step]
    ki = ki_ref[step]

    @pl.when(ki == 0)
    def _():
        m_ref[...] = jnp.full(m_ref.shape, NEG, F32)
        acc_ref[...] = jnp.zeros(acc_ref.shape, F32)

    def scores(hd):
        sl = slice(hd * LANES, (hd + 1) * LANES)
        return _dot_nt(k_ref[:, sl], q_ref[:, sl])

    def update(masked):
        if masked:
            kidx = lax.broadcasted_iota(jnp.int32, (T_ATT, T_ATT), 0)
            qidx = lax.broadcasted_iota(jnp.int32, (T_ATT, T_ATT), 1)
            keep = qidx >= kidx

        def softmax_part(hd, s):
            if masked:
                s = jnp.where(keep, s, NEG)
            m_prev = m_ref[hd]
            m_new = jnp.maximum(m_prev, jnp.max(s, axis=0, keepdims=True))
            m_ref[hd] = m_new
            alpha = jnp.exp2(m_prev - m_new)
            return jnp.exp2(s - m_new[0:1, :]).astype(BF16), alpha

        def value_part(hd, p, alpha):
            pv = _dot(vt_ref[hd * LANES:(hd + 1) * LANES, :], p)
            acc_ref[hd] = acc_ref[hd] * alpha[0:1, :] + pv

        depth = 2
        s_q = [scores(hd) for hd in range(depth)]
        pend = None
        for hd in range(MLA_HEADS):
            if hd + depth < MLA_HEADS:
                s_q.append(scores(hd + depth))
            cur = softmax_part(hd, s_q[hd])
            if pend is not None:
                value_part(hd - 1, *pend)
            pend = cur
        value_part(MLA_HEADS - 1, *pend)

    @pl.when(ki < qi)
    def _():
        update(False)

    @pl.when(ki == qi)
    def _():
        update(True)
        row = lax.broadcasted_iota(jnp.int32, (LANES, T_ATT), 0)
        for j in range(MLA_HEADS // 2):
            ae = acc_ref[2 * j]
            ao = acc_ref[2 * j + 1]
            out_t = jnp.where(row < 64, ae * (1.0 / ae[64:65, :]), ao * (1.0 / ao[0:1, :]))
            o_ref[:, j * LANES:(j + 1) * LANES] = out_t.T.astype(BF16)


def _mla_layer(qm, km, vm):
    nb = SEQ // T_ATT
    qi = np.concatenate([np.full(n + 1, n, np.int32) for n in range(nb)])
    ki = np.concatenate([np.arange(n + 1, dtype=np.int32) for n in range(nb)])
    grid_spec = pltpu.PrefetchScalarGridSpec(
        num_scalar_prefetch=2,
        grid=(int(qi.shape[0]),),
        in_specs=[pl.BlockSpec((T_ATT, 1024), lambda s, qi, ki: (qi[s], 0)),
                  pl.BlockSpec((T_ATT, 1024), lambda s, qi, ki: (ki[s], 0)),
                  pl.BlockSpec((1024, T_ATT), lambda s, qi, ki: (0, ki[s]))],
        out_specs=pl.BlockSpec((T_ATT, 512), lambda s, qi, ki: (qi[s], 0)),
        scratch_shapes=[pltpu.VMEM((MLA_HEADS, 8, T_ATT), F32),
                        pltpu.VMEM((MLA_HEADS, LANES, T_ATT), F32)],
    )
    return pl.pallas_call(
        _mla_kernel,
        out_shape=jax.ShapeDtypeStruct((SEQ, 512), BF16),
        grid_spec=grid_spec,
        compiler_params=_cparams(1),
        name="mla_attn",
    )(jnp.asarray(qi), jnp.asarray(ki), qm, km, vm)


def _post_kernel(x_ref, os_ref, om_ref, mod_ref, n2_ref, wout_ref, wrh_ref, wrl_ref, br_ref,
                 xo_ref, h2_ref, meta_ref, cnt_ref, carry_ref):
    y = _dot(os_ref[...], wout_ref[0:512, :]) + _dot(om_ref[...], wout_ref[512:1024, :])
    _tail(x_ref[...], y, mod_ref[...], n2_ref[...], wrh_ref[...], wrl_ref[...], br_ref[...],
          xo_ref, h2_ref, meta_ref, cnt_ref, carry_ref)


def _post_layer(x, o_s, o_m, mod, n2, w_out, wr_hi, wr_lo, br):
    shapes, specs, scratch = _tail_out(T_TOK)
    return pl.pallas_call(
        _post_kernel,
        out_shape=shapes,
        grid=(SEQ // T_TOK,),
        in_specs=[pl.BlockSpec((T_TOK, D), lambda i: (i, 0)),
                  pl.BlockSpec((T_TOK, 512), lambda i: (i, 0)),
                  pl.BlockSpec((T_TOK, 512), lambda i: (i, 0)),
                  _full((8, D)), _full((1, D)), _full((D, D)),
                  _full((D, LANES)), _full((D, LANES)), _full((1, LANES))],
        out_specs=specs,
        scratch_shapes=scratch,
        compiler_params=_cparams(1),
        name="odd_post",
    )(x, o_s, o_m, mod, n2, w_out, wr_hi, wr_lo, br)


def _dispatch_plan(meta, cnt):
    grp = meta[:, EPG].astype(jnp.int32)
    rank = meta[:, EPG + 1].astype(jnp.int32)
    counts = cnt[0, :N_GROUPS].astype(jnp.int32)
    padded = ((counts + T_MOE - 1) // T_MOE) * T_MOE
    ends = jnp.cumsum(padded)
    pos = (ends - padded)[grp] + rank
    n_used = ends[-1] // T_MOE
    tile_start = jnp.arange(N_TILES, dtype=jnp.int32) * T_MOE
    tile_group = jnp.minimum(jnp.sum(tile_start[:, None] >= ends[None, :], axis=1), N_GROUPS - 1)
    return pos.reshape(SEQ // T_DISP, 1, T_DISP), tile_group.astype(jnp.int32), n_used.reshape(1)


def _row_copies(n, src_at, dst_at, sem):
    def body(r, carry):
        pltpu.make_async_copy(src_at(r), dst_at(r), sem).start()
        return carry
    lax.fori_loop(0, n, body, 0, unroll=8)


def _disp_kernel(pos_ref, x_ref, init_ref, o_ref, sem):
    del init_ref
    _row_copies(T_DISP,
                lambda r: x_ref.at[pl.ds(r, 1), :],
                lambda r: o_ref.at[pl.ds(pos_ref[0, 0, r], 1), :], sem)
    pltpu.make_async_copy(x_ref, o_ref.at[pl.ds(0, T_DISP), :], sem).wait()


def _dispatch(pos, h2a, hs):
    return pl.pallas_call(
        _disp_kernel,
        out_shape=jax.ShapeDtypeStruct((N_SORT, ROW_W), F32),
        grid=(SEQ // T_DISP,),
        in_specs=[pl.BlockSpec((1, 1, T_DISP), lambda i: (i, 0, 0), memory_space=pltpu.SMEM),
                  pl.BlockSpec((T_DISP, ROW_W), lambda i: (i, 0)),
                  pl.BlockSpec(memory_space=pl.ANY)],
        out_specs=pl.BlockSpec(memory_space=pl.ANY),
        scratch_shapes=[pltpu.SemaphoreType.DMA(())],
        input_output_aliases={2: 0},
        compiler_params=_cparams(1),
        name="moe_dispatch",
    )(pos, h2a, hs)


def _moe_kernel(tg_ref, nu_ref, x_ref, wg_ref, wu_ref, wd_ref, o_ref, xb_ref, act_ref):
    del tg_ref
    i = pl.program_id(0)
    j = pl.program_id(1)
    last = EPG // E_STEP - 1

    @pl.when(i < nu_ref[0])
    def _():
        @pl.when(j == 0)
        def _():
            xb_ref[...] = x_ref[:, 0:D].astype(BF16)

        xb = xb_ref[...]
        meta = x_ref[:, D:ROW_W]
        lane = lax.broadcasted_iota(jnp.int32, meta.shape, 1)
        for k in range(E_STEP):
            a = _dot(xb, wg_ref[0, k].astype(BF16))
            u = _dot(xb, wu_ref[0, k].astype(BF16))
            gate = jnp.sum(jnp.where(lane == j * E_STEP + k, meta, 0.0), axis=-1, keepdims=True)
            act = (a * jax.nn.sigmoid(a) * u * gate).astype(BF16)
            for jj in range(EPG // E_STEP):
                @pl.when(j == jj)
                def _(jj=jj, k=k, act=act):
                    act_ref[:, (jj * E_STEP + k) * FF:(jj * E_STEP + k + 1) * FF] = act

        @pl.when(j == last)
        def _():
            o_ref[...] = _dot(act_ref[...], wd_ref[0, 0].astype(BF16))

    @pl.when((i >= nu_ref[0]) & (j == last))
    def _():
        o_ref[...] = jnp.zeros(o_ref.shape, F32)


def _moe_experts(tile_group, n_used, hs, wg, wu, wd, layer):
    def tile(i, e, tg, nu):
        return (jnp.minimum(i, nu[0] - 1), 0)

    nj = EPG // E_STEP

    def expert(i, j, tg, nu):
        return (layer, jnp.where(i < nu[0], tg[i] * nj + j, tg[nu[0] - 1] * nj + nj - 1), 0, 0)

    def group(i, j, tg, nu):
        return (layer, tg[jnp.minimum(i, nu[0] - 1)], 0, 0)

    grid_spec = pltpu.PrefetchScalarGridSpec(
        num_scalar_prefetch=2,
        grid=(N_TILES, nj),
        in_specs=[pl.BlockSpec((T_MOE, ROW_W), tile),
                  pl.BlockSpec((1, E_STEP, D, FF), expert),
                  pl.BlockSpec((1, E_STEP, D, FF), expert),
                  pl.BlockSpec((1, 1, EPG * FF, D), group, pipeline_mode=pl.Buffered(1))],
        out_specs=pl.BlockSpec((T_MOE, D), lambda i, e, tg, nu: (i, 0)),
        scratch_shapes=[pltpu.VMEM((T_MOE, D), BF16), pltpu.VMEM((T_MOE, EPG * FF), BF16)],
    )
    return pl.pallas_call(
        _moe_kernel,
        out_shape=jax.ShapeDtypeStruct((N_SORT, D), F32),
        grid_spec=grid_spec,
        compiler_params=pltpu.CompilerParams(dimension_semantics=("arbitrary", "arbitrary"),
                                             vmem_limit_bytes=MOE_VMEM_LIMIT),
        name="moe_experts",
    )(tile_group, n_used, hs, wg, wu, wd.reshape(DEPTH, N_GROUPS, EPG * FF, D))


def _comb_kernel(pos_ref, x_ref, mod_ref, y_ref, o_ref, buf_ref, sem):
    _row_copies(T_DISP,
                lambda r: y_ref.at[pl.ds(pos_ref[0, 0, r], 1), :],
                lambda r: buf_ref.at[pl.ds(r, 1), :], sem)
    pltpu.make_async_copy(y_ref.at[pl.ds(0, T_DISP), :], buf_ref, sem).wait()
    o_ref[...] = x_ref[...] + mod_ref[5:6, :] * buf_ref[...]


def _combine(pos, x, mod, ys):
    return pl.pallas_call(
        _comb_kernel,
        out_shape=jax.ShapeDtypeStruct((SEQ, D), F32),
        grid=(SEQ // T_DISP,),
        in_specs=[pl.BlockSpec((1, 1, T_DISP), lambda i: (i, 0, 0), memory_space=pltpu.SMEM),
                  pl.BlockSpec((T_DISP, D), lambda i: (i, 0)),
                  pl.BlockSpec((8, D), lambda i: (0, 0)),
                  pl.BlockSpec(memory_space=pl.ANY)],
        out_specs=pl.BlockSpec((T_DISP, D), lambda i: (i, 0)),
        scratch_shapes=[pltpu.VMEM((T_DISP, D), F32), pltpu.SemaphoreType.DMA(())],
        compiler_params=_cparams(1),
        name="moe_combine",
    )(pos, x, mod, ys)


def _moe_layer(x, h2a, meta, cnt, mod, wg, wu, wd, layer, hs):
    pos, tile_group, n_used = _dispatch_plan(meta, cnt)
    hs = _dispatch(pos, h2a, hs)
    ys = _moe_experts(tile_group, n_used, hs, wg, wu, wd, layer)
    return _combine(pos, x, mod, ys), hs


def _pad_heads(w, heads, dim):
    k = w.shape[0]
    w = w.reshape(k, heads, dim)
    return jnp.pad(w, ((0, 0), (0, 0), (0, LANES - dim))).reshape(k, heads * LANES)


def _pad_gain(g):
    return jnp.pad(g, (0, LANES - g.shape[0])).reshape(1, LANES)


def _odd_weights(w_in):
    q_s = _pad_heads(w_in[:, 0:512], SWA_HEADS, HEAD_DIM)
    k_s = _pad_heads(w_in[:, 512:640], SWA_KV, HEAD_DIM)
    v = w_in[:, 640:768]
    v_s = jnp.concatenate([v[:, 0:64], v[:, 0:64], v[:, 64:128], v[:, 64:128]], axis=1)
    c_q = w_in[:, 768:1152]
    c_kv = w_in[:, 1152:1408]
    k_r = jnp.pad(w_in[:, 1408:1440], ((0, 0), (MLA_NOPE, LANES - MLA_QK)))
    return jnp.concatenate([q_s, k_s, v_s, c_q, c_kv, k_r], axis=1).astype(BF16)


def _router_weights(w_group, b_group, w_expert, b_expert):
    w = jnp.pad(jnp.concatenate([w_expert, w_group], axis=1), ((0, 0), (0, LANES - N_EXPERTS - N_GROUPS)))
    hi = w.astype(BF16)
    lo = (w - hi.astype(F32)).astype(BF16)
    b = jnp.pad(jnp.concatenate([b_expert, b_group]), (0, LANES - N_EXPERTS - N_GROUPS)).reshape(1, LANES)
    return hi, lo, b


def kernel(x, c, positions, ada_w, ada_b, norm1_g, norm2_g, cp_w_in, conv_w, pool_w, pool_scale,
           cp_w_out, at_w_in, swa_q_g, swa_k_g, swa_sinks, mla_q_norm_g, mla_kv_norm_g, mla_w_uq,
           mla_w_ukv, mla_q_g, mla_k_g, at_w_out, moe_w_group, moe_b_group, moe_w_expert,
           moe_b_expert, moe_w_gate, moe_w_up, moe_w_down):
    xs = x.reshape(SEQ, D)
    mods = _ada_mod(c, ada_w, ada_b)
    tc, ts1, ts2 = _rope_tables(positions)
    hs = jnp.zeros((N_SORT, ROW_W), F32)
    for l in range(DEPTH):
        i = l // 2
        mod = mods[l]
        n1 = norm1_g[l].reshape(1, D)
        n2 = norm2_g[l].reshape(1, D)
        wr_hi, wr_lo, br = _router_weights(moe_w_group[l], moe_b_group[l], moe_w_expert[l], moe_b_expert[l])
        if l % 2 == 0:
            xs, h2a, meta, cnt = _even_layer(
                xs, mod, n1, n2, cp_w_in[i].astype(BF16), conv_w[i], pool_w[i].astype(BF16),
                pool_scale[i].reshape(1, 4 * POOL_G), cp_w_out[i].astype(BF16), wr_hi, wr_lo, br)
        else:
            ukv = mla_w_ukv[i].reshape(KV_LORA, MLA_HEADS, MLA_NOPE + MLA_V)
            wuk = _pad_heads(ukv[:, :, :MLA_NOPE].reshape(KV_LORA, MLA_HEADS * MLA_NOPE), MLA_HEADS, MLA_NOPE)
            wuv = ukv[:, :, MLA_NOPE:].reshape(KV_LORA, MLA_HEADS * MLA_V)
            wuq = _pad_heads(mla_w_uq[i], MLA_HEADS, MLA_QK)
            qs, ks, vs, qm, km, vm = _proj_layer(
                xs, mod, n1, _odd_weights(at_w_in[i]), _pad_gain(swa_q_g[i]), _pad_gain(swa_k_g[i]),
                mla_q_norm_g[i].reshape(1, Q_LORA), mla_kv_norm_g[i].reshape(1, KV_LORA),
                wuq.astype(BF16), wuk.astype(BF16), wuv.astype(BF16),
                _pad_gain(mla_q_g[i]), _pad_gain(mla_k_g[i]), tc, ts1, ts2)
            o_s = _swa_layer(swa_sinks[i], qs, ks, vs)
            o_m = _mla_layer(qm, km, vm)
            xs, h2a, meta, cnt = _post_layer(xs, o_s, o_m, mod, n2, at_w_out[i].astype(BF16), wr_hi, wr_lo, br)
        xs, hs = _moe_layer(xs, h2a, meta, cnt, mod, moe_w_gate, moe_w_up, moe_w_down, l, hs)
    return xs.reshape(1, SEQ, D)
```

```python
import functools

import numpy as np
import jax
import jax.numpy as jnp
from jax import lax
from jax.experimental import pallas as pl
from jax.experimental.pallas import tpu as pltpu

F32 = jnp.float32
BF16 = jnp.bfloat16

D = 1024
SEQ = 16384
DEPTH = 4
EPS = 1e-6
LANES = 128
CONV_CH = 512
POOL_WINDOWS = (2, 4, 8, 16)
POOL_G = 128
HALO = 16
SWA_HEADS = 8
SWA_KV = 2
HEAD_DIM = 64
WINDOW = 128
MLA_HEADS = 8
MLA_NOPE = 64
MLA_ROPE = 32
MLA_QK = MLA_NOPE + MLA_ROPE
MLA_V = 64
Q_LORA = 384
KV_LORA = 256
ROPE_THETA = 10000.0
N_GROUPS = 4
EPG = 8
N_EXPERTS = N_GROUPS * EPG
FF = 256
NEG = -1e30
LOG2E = 1.4426950408889634

T_TOK = 512
T_ATT = 512
T_MOE = 1024
T_DISP = 512
ROW_W = D + LANES
N_SORT = SEQ + N_GROUPS * T_MOE
N_TILES = N_SORT // T_MOE
E_STEP = 4
VMEM_LIMIT = 48 * 1024 * 1024
MOE_VMEM_LIMIT = 56 * 1024 * 1024

O_QS, O_KS, O_VS, O_CQ, O_CKV, O_KR, ODD_W = 0, 1024, 1280, 1536, 1920, 2176, 2304


def _cparams(n_axes=1):
    return pltpu.CompilerParams(dimension_semantics=("arbitrary",) * n_axes,
                                vmem_limit_bytes=VMEM_LIMIT)


def _rms(x):
    return x * lax.rsqrt(jnp.mean(x * x, axis=-1, keepdims=True) + EPS)


def _dot(a, b):
    return jnp.dot(a, b, preferred_element_type=F32)


def _dot_nt(a, b):
    return lax.dot_general(a, b, (((1,), (1,)), ((), ())), preferred_element_type=F32)


def _ada_kernel(c_ref, w_ref, b_ref, o_ref):
    c = c_ref[...]
    ca = c * jax.nn.sigmoid(c)
    o_ref[0] = jnp.sum(w_ref[0] * ca, axis=0, keepdims=True) + b_ref[0]


def _ada_mod(c, ada_w, ada_b):
    c_col = c.reshape(D, 1)
    b = ada_b.reshape(DEPTH * 6, 1, D)
    out = pl.pallas_call(
        _ada_kernel,
        out_shape=jax.ShapeDtypeStruct((DEPTH * 6, 1, D), F32),
        grid=(DEPTH, 6),
        in_specs=[pl.BlockSpec((D, 1), lambda l, j: (0, 0)),
                  pl.BlockSpec((1, D, D), lambda l, j: (l, 0, j)),
                  pl.BlockSpec((1, 1, D), lambda l, j: (l * 6 + j, 0, 0))],
        out_specs=pl.BlockSpec((1, 1, D), lambda l, j: (l * 6 + j, 0, 0)),
        compiler_params=_cparams(2),
        name="ada_mod",
    )(c_col, ada_w, b)
    mod = out.reshape(DEPTH, 6, D)
    return jnp.pad(mod, ((0, 0), (0, 2), (0, 0)))


def _rope_kernel(pos_ref, inv_ref, c_ref, s1_ref, s2_ref):
    pos = pos_ref[...].astype(F32)
    ang = pos * inv_ref[...]
    lane = lax.broadcasted_iota(jnp.int32, ang.shape, 1)
    cs = jnp.cos(ang)
    sn = jnp.sin(ang)
    c_ref[...] = jnp.where(lane < 64, 1.0, jnp.where(lane < 96, cs, 0.0))
    s1_ref[...] = jnp.where((lane >= 64) & (lane < 80), -sn, 0.0)
    s2_ref[...] = jnp.where((lane >= 80) & (lane < 96), sn, 0.0)


def _rope_tables(positions):
    half = MLA_ROPE // 2
    inv = jnp.power(ROPE_THETA, -jnp.arange(half, dtype=F32) / half)
    inv_lane = jnp.concatenate([jnp.zeros((64,), F32), inv, inv, jnp.zeros((32,), F32)]).reshape(1, LANES)
    pos = positions.reshape(SEQ, 1)
    shp = jax.ShapeDtypeStruct((SEQ, LANES), F32)
    spec = pl.BlockSpec((T_TOK, LANES), lambda i: (i, 0))
    return pl.pallas_call(
        _rope_kernel,
        out_shape=(shp, shp, shp),
        grid=(SEQ // T_TOK,),
        in_specs=[pl.BlockSpec((T_TOK, 1), lambda i: (i, 0)),
                  pl.BlockSpec((1, LANES), lambda i: (0, 0))],
        out_specs=(spec, spec, spec),
        compiler_params=_cparams(1),
        name="rope_tables",
    )(pos, inv_lane)


def _route(lg):
    lane = lax.broadcasted_iota(jnp.int32, lg.shape, 1)
    lane_f = lane.astype(F32)
    is_g = (lane >= N_EXPERTS) & (lane < N_EXPERTS + N_GROUPS)
    gl = jnp.where(is_g, lg, NEG)
    gmax = jnp.max(gl, axis=-1, keepdims=True)
    gidx = jnp.min(jnp.where(is_g & (gl == gmax), lane_f - N_EXPERTS, 1e3), axis=-1, keepdims=True)
    gsum = jnp.sum(jnp.where(is_g, jnp.exp(gl - gmax), 0.0), axis=-1, keepdims=True)
    gw = 1.0 / gsum
    grp_of_lane = (lane >> 3).astype(F32)
    in_grp = (lane < N_EXPERTS) & (grp_of_lane == gidx)
    el = jnp.where(in_grp, lg, NEG)
    m1 = jnp.max(el, axis=-1, keepdims=True)
    i1 = jnp.min(jnp.where(in_grp & (el == m1), lane_f, 1e3), axis=-1, keepdims=True)
    rest = in_grp & (lane_f != i1)
    el2 = jnp.where(rest, lg, NEG)
    m2 = jnp.max(el2, axis=-1, keepdims=True)
    i2 = jnp.min(jnp.where(rest & (el2 == m2), lane_f, 1e3), axis=-1, keepdims=True)
    r = jnp.exp(m2 - m1)
    w1 = gw / (1.0 + r)
    w2 = w1 * r
    return jnp.where(lane_f == i1, w1, jnp.where(lane_f == i2, w2, 0.0)), gidx


def _tail(x, y, mod, n2, wr_hi, wr_lo, br, xo_ref, h2_ref, meta_ref, cnt_ref, carry_ref):
    i = pl.program_id(0)

    @pl.when(i == 0)
    def _():
        carry_ref[...] = jnp.zeros(carry_ref.shape, F32)

    gate1 = mod[2:3]
    shift2, scale2 = mod[3:4], mod[4:5]
    xn = x + gate1 * y
    xo_ref[...] = xn
    h2 = _rms(xn) * n2 * (1.0 + scale2) + shift2
    hi = h2.astype(BF16)
    lo = (h2 - hi.astype(F32)).astype(BF16)
    lg = _dot(hi, wr_hi) + _dot(lo, wr_hi) + _dot(hi, wr_lo) + br
    gates, gidx = _route(lg)
    rows = lg.shape[0]
    lane = lax.broadcasted_iota(jnp.int32, (rows, LANES), 1)
    lane_f = lane.astype(F32)
    g8 = gates
    for g in range(1, N_GROUPS):
        g8 = g8 + pltpu.roll(gates, LANES - EPG * g, 1)
    onehot = (lane_f == gidx).astype(F32)
    r = lax.broadcasted_iota(jnp.int32, (rows, rows), 0)
    c = lax.broadcasted_iota(jnp.int32, (rows, rows), 1)
    before = jnp.where(c < r, 1.0, 0.0).astype(BF16)
    cum = _dot(before, onehot.astype(BF16)) + carry_ref[0:1, :]
    rank = jnp.sum(jnp.where(lane_f == gidx, cum, 0.0), axis=-1, keepdims=True)
    meta = jnp.where(lane < EPG, g8, jnp.where(lane == EPG, gidx, jnp.where(lane == EPG + 1, rank, 0.0)))
    h2_ref[:, 0:D] = h2
    h2_ref[:, D:D + LANES] = meta
    meta_ref[...] = meta
    total = carry_ref[0:1, :] + jnp.sum(onehot, axis=0, keepdims=True)
    carry_ref[...] = jnp.broadcast_to(total, carry_ref.shape)
    cnt_ref[...] = jnp.broadcast_to(total, cnt_ref.shape)


def _even_kernel(x_ref, xh_ref, mod_ref, n1_ref, n2_ref, win_ref, cw_ref, pw_ref, ps_ref,
                 wout_ref, wrh_ref, wrl_ref, br_ref, xo_ref, h2_ref, meta_ref, cnt_ref, carry_ref):
    i = pl.program_id(0)
    x = x_ref[...]
    mod = mod_ref[...]
    shift1, scale1 = mod[0:1], mod[1:2]
    xa = jnp.concatenate([xh_ref[...], x], axis=0)
    h = _rms(xa) * n1_ref[...] * (1.0 + scale1) + shift1
    z = _dot(h.astype(BF16), win_ref[...])
    rows = T_TOK + HALO
    row = lax.broadcasted_iota(jnp.int32, (rows, 1), 0)
    tpos = i * T_TOK + row - HALO
    live = (tpos >= 0).astype(F32)
    bg = z[:, 0:CONV_CH]
    v = z[:, CONV_CH:2 * CONV_CH] * z[:, 2 * CONV_CH:3 * CONV_CH] * live
    cw = cw_ref[...]
    conv = v * cw[0:1] + pltpu.roll(v, 1, 0) * cw[1:2] + pltpu.roll(v, 2, 0) * cw[2:3]
    parts = [(bg * conv)[HALO:]]
    ps = ps_ref[...]
    tcount = (tpos + 1).astype(F32)
    for gi, w in enumerate(POOL_WINDOWS):
        ug = z[:, 3 * CONV_CH + gi * POOL_G: 3 * CONV_CH + (gi + 1) * POOL_G] * live
        s = ug
        k = 1
        while k < w:
            s = s + pltpu.roll(s, k, 0)
            k *= 2
        inv = 1.0 / jnp.minimum(tcount, float(w))
        d = (s * inv - ug)[HALO:].astype(BF16)
        parts.append(_dot(d, pw_ref[gi]) * ps[:, gi * POOL_G:(gi + 1) * POOL_G])
    cat = jnp.concatenate(parts, axis=-1).astype(BF16)
    y = _dot(cat, wout_ref[...])
    _tail(x, y, mod, n2_ref[...], wrh_ref[...], wrl_ref[...], br_ref[...], xo_ref, h2_ref, meta_ref,
          cnt_ref, carry_ref)


def _tail_out(n):
    shapes = (jax.ShapeDtypeStruct((SEQ, D), F32), jax.ShapeDtypeStruct((SEQ, ROW_W), F32),
              jax.ShapeDtypeStruct((SEQ, LANES), F32), jax.ShapeDtypeStruct((8, LANES), F32))
    specs = (pl.BlockSpec((n, D), lambda i: (i, 0)), pl.BlockSpec((n, ROW_W), lambda i: (i, 0)),
             pl.BlockSpec((n, LANES), lambda i: (i, 0)), pl.BlockSpec((8, LANES), lambda i: (0, 0)))
    scratch = [pltpu.VMEM((8, LANES), F32)]
    return shapes, specs, scratch


def _full(shape):
    nd = len(shape)
    return pl.BlockSpec(shape, lambda i: (0,) * nd)


def _even_layer(x, mod, n1, n2, w_in, conv_w, pool_w, pool_scale, w_out, wr_hi, wr_lo, br):
    shapes, specs, scratch = _tail_out(T_TOK)
    hb = T_TOK // HALO
    return pl.pallas_call(
        _even_kernel,
        out_shape=shapes,
        grid=(SEQ // T_TOK,),
        in_specs=[pl.BlockSpec((T_TOK, D), lambda i: (i, 0)),
                  pl.BlockSpec((HALO, D), lambda i: (jnp.maximum(i * hb - 1, 0), 0)),
                  _full((8, D)), _full((1, D)), _full((1, D)),
                  _full((D, 4 * CONV_CH)), _full((3, CONV_CH)), _full((4, POOL_G, POOL_G)),
                  _full((1, 4 * POOL_G)), _full((D, D)),
                  _full((D, LANES)), _full((D, LANES)), _full((1, LANES))],
        out_specs=specs,
        scratch_shapes=scratch,
        compiler_params=_cparams(1),
        name="even_mixer",
    )(x, x, mod, n1, n2, w_in, conv_w, pool_w, pool_scale, w_out, wr_hi, wr_lo, br)


def _proj_kernel(x_ref, mod_ref, n1_ref, win_ref, gsq_ref, gsk_ref, gqn_ref, gkvn_ref,
                 wuq_ref, wuk_ref, wuv_ref, gmq_ref, gmk_ref, c_ref, s1_ref, s2_ref,
                 qs_ref, ks_ref, vs_ref, qm_ref, km_ref, vm_ref):
    x = x_ref[...]
    mod = mod_ref[...]
    shift1, scale1 = mod[0:1], mod[1:2]
    h = _rms(x) * n1_ref[...] * (1.0 + scale1) + shift1
    z = _dot(h.astype(BF16), win_ref[...])

    def head_norm(t, g, dim):
        ms = jnp.sum(t * t, axis=-1, keepdims=True) * (1.0 / dim)
        return t * lax.rsqrt(ms + EPS) * g

    gsq, gsk = gsq_ref[...], gsk_ref[...]
    for hd in range(SWA_HEADS):
        qh = head_norm(z[:, O_QS + hd * LANES: O_QS + (hd + 1) * LANES], gsq, HEAD_DIM)
        qs_ref[:, hd * LANES:(hd + 1) * LANES] = (qh * (HEAD_DIM ** -0.5)).astype(BF16)
    for kv in range(SWA_KV):
        kh = head_norm(z[:, O_KS + kv * LANES: O_KS + (kv + 1) * LANES], gsk, HEAD_DIM)
        ks_ref[:, kv * LANES:(kv + 1) * LANES] = kh.astype(BF16)
    vs_ref[...] = z[:, O_VS:O_CQ].astype(BF16)

    cq = (_rms(z[:, O_CQ:O_CKV]) * gqn_ref[...]).astype(BF16)
    ckv = (_rms(z[:, O_CKV:O_KR]) * gkvn_ref[...]).astype(BF16)
    qm = _dot(cq, wuq_ref[...])
    kn = _dot(ckv, wuk_ref[...])
    vm = _dot(ckv, wuv_ref[...])
    lane = lax.broadcasted_iota(jnp.int32, (T_TOK, LANES), 1)
    for j in range(MLA_HEADS // 2):
        vv = vm[:, j * LANES:(j + 1) * LANES]
        even = jnp.where(lane < 64, vv, jnp.where(lane == 64, 1.0, 0.0))
        odd = jnp.where(lane < 64, jnp.where(lane == 0, 1.0, 0.0), vv)
        vm_ref[0, (2 * j) * LANES:(2 * j + 1) * LANES, :] = even.T.astype(BF16)
        vm_ref[0, (2 * j + 1) * LANES:(2 * j + 2) * LANES, :] = odd.T.astype(BF16)
    kr = z[:, O_KR:ODD_W]
    cs, s1, s2 = c_ref[...], s1_ref[...], s2_ref[...]

    def rope(t):
        return t * cs + pltpu.roll(t, LANES - 16, 1) * s1 + pltpu.roll(t, 16, 1) * s2

    gmq, gmk = gmq_ref[...], gmk_ref[...]
    for hd in range(MLA_HEADS):
        sl = slice(hd * LANES, (hd + 1) * LANES)
        qh = rope(head_norm(qm[:, sl], gmq, MLA_QK))
        qm_ref[:, sl] = (qh * (MLA_QK ** -0.5 * LOG2E)).astype(BF16)
        kh = rope(head_norm(kn[:, sl] + kr, gmk, MLA_QK))
        km_ref[:, sl] = kh.astype(BF16)


def _proj_layer(x, mod, n1, w_in, gsq, gsk, gqn, gkvn, wuq, wuk, wuv, gmq, gmk, tc, ts1, ts2):
    def tok(wd):
        return pl.BlockSpec((T_TOK, wd), lambda i: (i, 0))
    widths = (1024, 256, 256, 1024, 1024)
    return pl.pallas_call(
        _proj_kernel,
        out_shape=tuple(jax.ShapeDtypeStruct((SEQ, wd), BF16) for wd in widths)
        + (jax.ShapeDtypeStruct((SEQ // T_TOK, MLA_HEADS * LANES, T_TOK), BF16),),
        grid=(SEQ // T_TOK,),
        in_specs=[tok(D), _full((8, D)), _full((1, D)), _full((D, ODD_W)),
                  _full((1, LANES)), _full((1, LANES)), _full((1, Q_LORA)), _full((1, KV_LORA)),
                  _full((Q_LORA, 1024)), _full((KV_LORA, 1024)), _full((KV_LORA, 512)),
                  _full((1, LANES)), _full((1, LANES)), tok(LANES), tok(LANES), tok(LANES)],
        out_specs=tuple(tok(wd) for wd in widths)
        + (pl.BlockSpec((1, MLA_HEADS * LANES, T_TOK), lambda i: (i, 0, 0)),),
        compiler_params=_cparams(1),
        name="odd_proj",
    )(x, mod, n1, w_in, gsq, gsk, gqn, gkvn, wuq, wuk, wuv, gmq, gmk, tc, ts1, ts2)


def _swa_kernel(sink_ref, q_ref, k_ref, kh_ref, v_ref, vh_ref, o_ref):
    i = pl.program_id(0)
    kcat = jnp.concatenate([kh_ref[...], k_ref[...]], axis=0)
    vcat = jnp.concatenate([vh_ref[...], v_ref[...]], axis=0)
    grp = SWA_HEADS // SWA_KV
    r = lax.broadcasted_iota(jnp.int32, (grp * WINDOW, 2 * WINDOW), 0) & (WINDOW - 1)
    c = lax.broadcasted_iota(jnp.int32, (grp * WINDOW, 2 * WINDOW), 1)
    rel = WINDOW + r - c
    lane = lax.broadcasted_iota(jnp.int32, (WINDOW, LANES), 1)
    for sb in range(T_ATT // WINDOW):
        rows = slice(sb * WINDOW, (sb + 1) * WINDOW)
        kb = kcat[sb * WINDOW: sb * WINDOW + 2 * WINDOW]
        vb = vcat[sb * WINDOW: sb * WINDOW + 2 * WINDOW]
        kpos = i * T_ATT + (sb - 1) * WINDOW + c
        ok = (rel >= 0) & (rel < WINDOW) & (kpos >= 0)
        outs = []
        for kv in range(SWA_KV):
            q = jnp.concatenate([q_ref[rows, (kv * grp + g) * LANES:(kv * grp + g + 1) * LANES]
                                 for g in range(grp)], axis=0)
            sink = jnp.concatenate([jnp.full((WINDOW, 1), sink_ref[kv * grp + g], F32)
                                    for g in range(grp)], axis=0)
            s = jnp.where(ok, _dot_nt(q, kb[:, kv * LANES:(kv + 1) * LANES]), NEG)
            m = jnp.maximum(jnp.max(s, axis=-1, keepdims=True), sink)
            e = jnp.exp(s - m)
            den = jnp.sum(e, axis=-1, keepdims=True) + jnp.exp(sink - m)
            p = (e * (1.0 / den)).astype(BF16)
            o = _dot(p, vb[:, kv * LANES:(kv + 1) * LANES])
            outs += [o[g * WINDOW:(g + 1) * WINDOW] for g in range(grp)]
        for j in range(SWA_HEADS // 2):
            o_ref[rows, j * LANES:(j + 1) * LANES] = jnp.where(lane < 64, outs[2 * j], outs[2 * j + 1]).astype(BF16)


def _swa_layer(sinks, qs, ks, vs):
    hb = T_ATT // WINDOW
    return pl.pallas_call(
        _swa_kernel,
        out_shape=jax.ShapeDtypeStruct((SEQ, 512), BF16),
        grid=(SEQ // T_ATT,),
        in_specs=[pl.BlockSpec(memory_space=pltpu.SMEM),
                  pl.BlockSpec((T_ATT, 1024), lambda i: (i, 0)),
                  pl.BlockSpec((T_ATT, 256), lambda i: (i, 0)),
                  pl.BlockSpec((WINDOW, 256), lambda i: (jnp.maximum(i * hb - 1, 0), 0)),
                  pl.BlockSpec((T_ATT, 256), lambda i: (i, 0)),
                  pl.BlockSpec((WINDOW, 256), lambda i: (jnp.maximum(i * hb - 1, 0), 0))],
        out_specs=pl.BlockSpec((T_ATT, 512), lambda i: (i, 0)),
        compiler_params=_cparams(1),
        name="swa_attn",
    )(sinks, qs, ks, ks, vs, vs)


def _mla_kernel(qi_ref, ki_ref, q_ref, k_ref, vt_ref, o_ref, m_ref, acc_ref):
    step = pl.program_id(0)
    qi = qi_ref[step]
    ki = ki_ref[step]

    @pl.when(ki == 0)
    def _():
        m_ref[...] = jnp.full(m_ref.shape, NEG, F32)
        acc_ref[...] = jnp.zeros(acc_ref.shape, F32)

    def scores(hd):
        sl = slice(hd * LANES, (hd + 1) * LANES)
        return _dot_nt(k_ref[:, sl], q_ref[:, sl])

    def update(masked):
        if masked:
            kidx = lax.broadcasted_iota(jnp.int32, (T_ATT, T_ATT), 0)
            qidx = lax.broadcasted_iota(jnp.int32, (T_ATT, T_ATT), 1)
            keep = qidx >= kidx

        def softmax_part(hd, s):
            if masked:
                s = jnp.where(keep, s, NEG)
            m_prev = m_ref[hd]
            m_new = jnp.maximum(m_prev, jnp.max(s, axis=0, keepdims=True))
            m_ref[hd] = m_new
            alpha = jnp.exp2(m_prev - m_new)
            return jnp.exp2(s - m_new[0:1, :]).astype(BF16), alpha

        def value_part(hd, p, alpha):
            pv = _dot(vt_ref[0, hd * LANES:(hd + 1) * LANES, :], p)
            acc_ref[hd] = acc_ref[hd] * alpha[0:1, :] + pv

        depth = 2
        s_q = [scores(hd) for hd in range(depth)]
        pend = None
        for hd in range(MLA_HEADS):
            if hd + depth < MLA_HEADS:
                s_q.append(scores(hd + depth))
            cur = softmax_part(hd, s_q[hd])
            if pend is not None:
                value_part(hd - 1, *pend)
            pend = cur
        value_part(MLA_HEADS - 1, *pend)

    @pl.when(ki < qi)
    def _():
        update(False)

    @pl.when(ki == qi)
    def _():
        update(True)
        row = lax.broadcasted_iota(jnp.int32, (LANES, T_ATT), 0)
        for j in range(MLA_HEADS // 2):
            ae = acc_ref[2 * j]
            ao = acc_ref[2 * j + 1]
            out_t = jnp.where(row < 64, ae * (1.0 / ae[64:65, :]), ao * (1.0 / ao[0:1, :]))
            o_ref[:, j * LANES:(j + 1) * LANES] = out_t.T.astype(BF16)


def _mla_layer(qm, km, vm):
    nb = SEQ // T_ATT
    qi = np.concatenate([np.full(n + 1, n, np.int32) for n in range(nb)])
    ki = np.concatenate([np.arange(n + 1, dtype=np.int32) for n in range(nb)])
    grid_spec = pltpu.PrefetchScalarGridSpec(
        num_scalar_prefetch=2,
        grid=(int(qi.shape[0]),),
        in_specs=[pl.BlockSpec((T_ATT, 1024), lambda s, qi, ki: (qi[s], 0)),
                  pl.BlockSpec((T_ATT, 1024), lambda s, qi, ki: (ki[s], 0)),
                  pl.BlockSpec((1, 1024, T_ATT), lambda s, qi, ki: (ki[s], 0, 0))],
        out_specs=pl.BlockSpec((T_ATT, 512), lambda s, qi, ki: (qi[s], 0)),
        scratch_shapes=[pltpu.VMEM((MLA_HEADS, 8, T_ATT), F32),
                        pltpu.VMEM((MLA_HEADS, LANES, T_ATT), F32)],
    )
    return pl.pallas_call(
        _mla_kernel,
        out_shape=jax.ShapeDtypeStruct((SEQ, 512), BF16),
        grid_spec=grid_spec,
        compiler_params=_cparams(1),
        name="mla_attn",
    )(jnp.asarray(qi), jnp.asarray(ki), qm, km, vm)


def _post_kernel(x_ref, os_ref, om_ref, mod_ref, n2_ref, wout_ref, wrh_ref, wrl_ref, br_ref,
                 xo_ref, h2_ref, meta_ref, cnt_ref, carry_ref):
    y = _dot(os_ref[...], wout_ref[0:512, :]) + _dot(om_ref[...], wout_ref[512:1024, :])
    _tail(x_ref[...], y, mod_ref[...], n2_ref[...], wrh_ref[...], wrl_ref[...], br_ref[...],
          xo_ref, h2_ref, meta_ref, cnt_ref, carry_ref)


def _post_layer(x, o_s, o_m, mod, n2, w_out, wr_hi, wr_lo, br):
    shapes, specs, scratch = _tail_out(T_TOK)
    return pl.pallas_call(
        _post_kernel,
        out_shape=shapes,
        grid=(SEQ // T_TOK,),
        in_specs=[pl.BlockSpec((T_TOK, D), lambda i: (i, 0)),
                  pl.BlockSpec((T_TOK, 512), lambda i: (i, 0)),
                  pl.BlockSpec((T_TOK, 512), lambda i: (i, 0)),
                  _full((8, D)), _full((1, D)), _full((D, D)),
                  _full((D, LANES)), _full((D, LANES)), _full((1, LANES))],
        out_specs=specs,
        scratch_shapes=scratch,
        compiler_params=_cparams(1),
        name="odd_post",
    )(x, o_s, o_m, mod, n2, w_out, wr_hi, wr_lo, br)


def _dispatch_plan(meta, cnt):
    grp = meta[:, EPG].astype(jnp.int32)
    rank = meta[:, EPG + 1].astype(jnp.int32)
    counts = cnt[0, :N_GROUPS].astype(jnp.int32)
    padded = ((counts + T_MOE - 1) // T_MOE) * T_MOE
    ends = jnp.cumsum(padded)
    pos = (ends - padded)[grp] + rank
    n_used = ends[-1] // T_MOE
    tile_start = jnp.arange(N_TILES, dtype=jnp.int32) * T_MOE
    tile_group = jnp.minimum(jnp.sum(tile_start[:, None] >= ends[None, :], axis=1), N_GROUPS - 1)
    return pos.reshape(SEQ // T_DISP, 1, T_DISP), tile_group.astype(jnp.int32), n_used.reshape(1)


def _row_copies(n, src_at, dst_at, sem):
    def body(r, carry):
        pltpu.make_async_copy(src_at(r), dst_at(r), sem).start()
        return carry
    lax.fori_loop(0, n, body, 0, unroll=8)


def _disp_kernel(pos_ref, x_ref, init_ref, o_ref, sem):
    del init_ref
    _row_copies(T_DISP,
                lambda r: x_ref.at[pl.ds(r, 1), :],
                lambda r: o_ref.at[pl.ds(pos_ref[0, 0, r], 1), :], sem)
    pltpu.make_async_copy(x_ref, o_ref.at[pl.ds(0, T_DISP), :], sem).wait()


def _dispatch(pos, h2a, hs):
    return pl.pallas_call(
        _disp_kernel,
        out_shape=jax.ShapeDtypeStruct((N_SORT, ROW_W), F32),
        grid=(SEQ // T_DISP,),
        in_specs=[pl.BlockSpec((1, 1, T_DISP), lambda i: (i, 0, 0), memory_space=pltpu.SMEM),
                  pl.BlockSpec((T_DISP, ROW_W), lambda i: (i, 0)),
                  pl.BlockSpec(memory_space=pl.ANY)],
        out_specs=pl.BlockSpec(memory_space=pl.ANY),
        scratch_shapes=[pltpu.SemaphoreType.DMA(())],
        input_output_aliases={2: 0},
        compiler_params=_cparams(1),
        name="moe_dispatch",
    )(pos, h2a, hs)


def _moe_kernel(tg_ref, nu_ref, x_ref, wg_ref, wu_ref, wd_ref, o_ref, xb_ref, act_ref):
    del tg_ref
    i = pl.program_id(0)
    j = pl.program_id(1)
    last = EPG // E_STEP - 1

    @pl.when(i < nu_ref[0])
    def _():
        @pl.when(j == 0)
        def _():
            xb_ref[...] = x_ref[:, 0:D].astype(BF16)

        xb = xb_ref[...]
        meta = x_ref[:, D:ROW_W]
        lane = lax.broadcasted_iota(jnp.int32, meta.shape, 1)
        for k in range(E_STEP):
            a = _dot(xb, wg_ref[0, k].astype(BF16))
            u = _dot(xb, wu_ref[0, k].astype(BF16))
            gate = jnp.sum(jnp.where(lane == j * E_STEP + k, meta, 0.0), axis=-1, keepdims=True)
            act = (a * jax.nn.sigmoid(a) * u * gate).astype(BF16)
            for jj in range(EPG // E_STEP):
                @pl.when(j == jj)
                def _(jj=jj, k=k, act=act):
                    act_ref[:, (jj * E_STEP + k) * FF:(jj * E_STEP + k + 1) * FF] = act

        @pl.when(j == last)
        def _():
            o_ref[...] = _dot(act_ref[...], wd_ref[0, 0].astype(BF16))

    @pl.when((i >= nu_ref[0]) & (j == last))
    def _():
        o_ref[...] = jnp.zeros(o_ref.shape, F32)


def _moe_experts(tile_group, n_used, hs, wg, wu, wd, layer):
    def tile(i, e, tg, nu):
        return (jnp.minimum(i, nu[0] - 1), 0)

    nj = EPG // E_STEP

    def expert(i, j, tg, nu):
        return (layer, jnp.where(i < nu[0], tg[i] * nj + j, tg[nu[0] - 1] * nj + nj - 1), 0, 0)

    def group(i, j, tg, nu):
        return (layer, tg[jnp.minimum(i, nu[0] - 1)], 0, 0)

    grid_spec = pltpu.PrefetchScalarGridSpec(
        num_scalar_prefetch=2,
        grid=(N_TILES, nj),
        in_specs=[pl.BlockSpec((T_MOE, ROW_W), tile),
                  pl.BlockSpec((1, E_STEP, D, FF), expert),
                  pl.BlockSpec((1, E_STEP, D, FF), expert),
                  pl.BlockSpec((1, 1, EPG * FF, D), group, pipeline_mode=pl.Buffered(1))],
        out_specs=pl.BlockSpec((T_MOE, D), lambda i, e, tg, nu: (i, 0)),
        scratch_shapes=[pltpu.VMEM((T_MOE, D), BF16), pltpu.VMEM((T_MOE, EPG * FF), BF16)],
    )
    return pl.pallas_call(
        _moe_kernel,
        out_shape=jax.ShapeDtypeStruct((N_SORT, D), F32),
        grid_spec=grid_spec,
        compiler_params=pltpu.CompilerParams(dimension_semantics=("arbitrary", "arbitrary"),
                                             vmem_limit_bytes=MOE_VMEM_LIMIT),
        name="moe_experts",
    )(tile_group, n_used, hs, wg, wu, wd.reshape(DEPTH, N_GROUPS, EPG * FF, D))


def _comb_kernel(pos_ref, x_ref, mod_ref, y_ref, o_ref, buf_ref, sem):
    _row_copies(T_DISP,
                lambda r: y_ref.at[pl.ds(pos_ref[0, 0, r], 1), :],
                lambda r: buf_ref.at[pl.ds(r, 1), :], sem)
    pltpu.make_async_copy(y_ref.at[pl.ds(0, T_DISP), :], buf_ref, sem).wait()
    o_ref[...] = x_ref[...] + mod_ref[5:6, :] * buf_ref[...]


def _combine(pos, x, mod, ys):
    return pl.pallas_call(
        _comb_kernel,
        out_shape=jax.ShapeDtypeStruct((SEQ, D), F32),
        grid=(SEQ // T_DISP,),
        in_specs=[pl.BlockSpec((1, 1, T_DISP), lambda i: (i, 0, 0), memory_space=pltpu.SMEM),
                  pl.BlockSpec((T_DISP, D), lambda i: (i, 0)),
                  pl.BlockSpec((8, D), lambda i: (0, 0)),
                  pl.BlockSpec(memory_space=pl.ANY)],
        out_specs=pl.BlockSpec((T_DISP, D), lambda i: (i, 0)),
        scratch_shapes=[pltpu.VMEM((T_DISP, D), F32), pltpu.SemaphoreType.DMA(())],
        compiler_params=_cparams(1),
        name="moe_combine",
    )(pos, x, mod, ys)


def _moe_layer(x, h2a, meta, cnt, mod, wg, wu, wd, layer, hs):
    pos, tile_group, n_used = _dispatch_plan(meta, cnt)
    hs = _dispatch(pos, h2a, hs)
    ys = _moe_experts(tile_group, n_used, hs, wg, wu, wd, layer)
    return _combine(pos, x, mod, ys), hs


def _pad_heads(w, heads, dim):
    k = w.shape[0]
    w = w.reshape(k, heads, dim)
    return jnp.pad(w, ((0, 0), (0, 0), (0, LANES - dim))).reshape(k, heads * LANES)


def _pad_gain(g):
    return jnp.pad(g, (0, LANES - g.shape[0])).reshape(1, LANES)


def _odd_weights(w_in):
    q_s = _pad_heads(w_in[:, 0:512], SWA_HEADS, HEAD_DIM)
    k_s = _pad_heads(w_in[:, 512:640], SWA_KV, HEAD_DIM)
    v = w_in[:, 640:768]
    v_s = jnp.concatenate([v[:, 0:64], v[:, 0:64], v[:, 64:128], v[:, 64:128]], axis=1)
    c_q = w_in[:, 768:1152]
    c_kv = w_in[:, 1152:1408]
    k_r = jnp.pad(w_in[:, 1408:1440], ((0, 0), (MLA_NOPE, LANES - MLA_QK)))
    return jnp.concatenate([q_s, k_s, v_s, c_q, c_kv, k_r], axis=1).astype(BF16)


def _router_weights(w_group, b_group, w_expert, b_expert):
    w = jnp.pad(jnp.concatenate([w_expert, w_group], axis=1), ((0, 0), (0, LANES - N_EXPERTS - N_GROUPS)))
    hi = w.astype(BF16)
    lo = (w - hi.astype(F32)).astype(BF16)
    b = jnp.pad(jnp.concatenate([b_expert, b_group]), (0, LANES - N_EXPERTS - N_GROUPS)).reshape(1, LANES)
    return hi, lo, b


def kernel(x, c, positions, ada_w, ada_b, norm1_g, norm2_g, cp_w_in, conv_w, pool_w, pool_scale,
           cp_w_out, at_w_in, swa_q_g, swa_k_g, swa_sinks, mla_q_norm_g, mla_kv_norm_g, mla_w_uq,
           mla_w_ukv, mla_q_g, mla_k_g, at_w_out, moe_w_group, moe_b_group, moe_w_expert,
           moe_b_expert, moe_w_gate, moe_w_up, moe_w_down):
    xs = x.reshape(SEQ, D)
    mods = _ada_mod(c, ada_w, ada_b)
    tc, ts1, ts2 = _rope_tables(positions)
    hs = jnp.zeros((N_SORT, ROW_W), F32)
    for l in range(DEPTH):
        i = l // 2
        mod = mods[l]
        n1 = norm1_g[l].reshape(1, D)
        n2 = norm2_g[l].reshape(1, D)
        wr_hi, wr_lo, br = _router_weights(moe_w_group[l], moe_b_group[l], moe_w_expert[l], moe_b_expert[l])
        if l % 2 == 0:
            xs, h2a, meta, cnt = _even_layer(
                xs, mod, n1, n2, cp_w_in[i].astype(BF16), conv_w[i], pool_w[i].astype(BF16),
                pool_scale[i].reshape(1, 4 * POOL_G), cp_w_out[i].astype(BF16), wr_hi, wr_lo, br)
        else:
            ukv = mla_w_ukv[i].reshape(KV_LORA, MLA_HEADS, MLA_NOPE + MLA_V)
            wuk = _pad_heads(ukv[:, :, :MLA_NOPE].reshape(KV_LORA, MLA_HEADS * MLA_NOPE), MLA_HEADS, MLA_NOPE)
            wuv = ukv[:, :, MLA_NOPE:].reshape(KV_LORA, MLA_HEADS * MLA_V)
            wuq = _pad_heads(mla_w_uq[i], MLA_HEADS, MLA_QK)
            qs, ks, vs, qm, km, vm = _proj_layer(
                xs, mod, n1, _odd_weights(at_w_in[i]), _pad_gain(swa_q_g[i]), _pad_gain(swa_k_g[i]),
                mla_q_norm_g[i].reshape(1, Q_LORA), mla_kv_norm_g[i].reshape(1, KV_LORA),
                wuq.astype(BF16), wuk.astype(BF16), wuv.astype(BF16),
                _pad_gain(mla_q_g[i]), _pad_gain(mla_k_g[i]), tc, ts1, ts2)
            o_s = _swa_layer(swa_sinks[i], qs, ks, vs)
            o_m = _mla_layer(qm, km, vm)
            xs, h2a, meta, cnt = _post_layer(xs, o_s, o_m, mod, n2, at_w_out[i].astype(BF16), wr_hi, wr_lo, br)
        xs, hs = _moe_layer(xs, h2a, meta, cnt, mod, moe_w_gate, moe_w_up, moe_w_down, l, hs)
    return xs.reshape(1, SEQ, D)
```

```python
import functools

import numpy as np
import jax
import jax.numpy as jnp
from jax import lax
from jax.experimental import pallas as pl
from jax.experimental.pallas import tpu as pltpu

F32 = jnp.float32
BF16 = jnp.bfloat16

D = 1024
SEQ = 16384
DEPTH = 4
EPS = 1e-6
LANES = 128
CONV_CH = 512
POOL_WINDOWS = (2, 4, 8, 16)
POOL_G = 128
HALO = 16
SWA_HEADS = 8
SWA_KV = 2
HEAD_DIM = 64
WINDOW = 128
MLA_HEADS = 8
MLA_NOPE = 64
MLA_ROPE = 32
MLA_QK = MLA_NOPE + MLA_ROPE
MLA_V = 64
Q_LORA = 384
KV_LORA = 256
ROPE_THETA = 10000.0
N_GROUPS = 4
EPG = 8
N_EXPERTS = N_GROUPS * EPG
FF = 256
NEG = -1e30
LOG2E = 1.4426950408889634

T_TOK = 512
T_ATT = 512
T_MOE = 1024
T_DISP = 512
ROW_W = D + LANES
N_SORT = SEQ + N_GROUPS * T_MOE
N_TILES = N_SORT // T_MOE
E_STEP = 4
MLA_DEPTH = 2
VMEM_LIMIT = 48 * 1024 * 1024
MOE_VMEM_LIMIT = 56 * 1024 * 1024

O_QS, O_KS, O_VS, O_CQ, O_CKV, O_KR, ODD_W = 0, 1024, 1280, 1536, 1920, 2176, 2304


def _cparams(n_axes=1):
    return pltpu.CompilerParams(dimension_semantics=("arbitrary",) * n_axes,
                                vmem_limit_bytes=VMEM_LIMIT)


def _rms(x):
    return x * lax.rsqrt(jnp.mean(x * x, axis=-1, keepdims=True) + EPS)


def _dot(a, b):
    return jnp.dot(a, b, preferred_element_type=F32)


def _dot_nt(a, b):
    return lax.dot_general(a, b, (((1,), (1,)), ((), ())), preferred_element_type=F32)


def _ada_kernel(c_ref, w_ref, b_ref, o_ref):
    c = c_ref[...]
    ca = c * jax.nn.sigmoid(c)
    o_ref[0] = jnp.sum(w_ref[0] * ca, axis=0, keepdims=True) + b_ref[0]


def _ada_mod(c, ada_w, ada_b):
    c_col = c.reshape(D, 1)
    b = ada_b.reshape(DEPTH * 6, 1, D)
    out = pl.pallas_call(
        _ada_kernel,
        out_shape=jax.ShapeDtypeStruct((DEPTH * 6, 1, D), F32),
        grid=(DEPTH, 6),
        in_specs=[pl.BlockSpec((D, 1), lambda l, j: (0, 0)),
                  pl.BlockSpec((1, D, D), lambda l, j: (l, 0, j)),
                  pl.BlockSpec((1, 1, D), lambda l, j: (l * 6 + j, 0, 0))],
        out_specs=pl.BlockSpec((1, 1, D), lambda l, j: (l * 6 + j, 0, 0)),
        compiler_params=_cparams(2),
        name="ada_mod",
    )(c_col, ada_w, b)
    mod = out.reshape(DEPTH, 6, D)
    return jnp.pad(mod, ((0, 0), (0, 2), (0, 0)))


def _rope_kernel(pos_ref, inv_ref, c_ref, s1_ref, s2_ref):
    pos = pos_ref[...].astype(F32)
    ang = pos * inv_ref[...]
    lane = lax.broadcasted_iota(jnp.int32, ang.shape, 1)
    cs = jnp.cos(ang)
    sn = jnp.sin(ang)
    c_ref[...] = jnp.where(lane < 64, 1.0, jnp.where(lane < 96, cs, 0.0))
    s1_ref[...] = jnp.where((lane >= 64) & (lane < 80), -sn, 0.0)
    s2_ref[...] = jnp.where((lane >= 80) & (lane < 96), sn, 0.0)


def _rope_tables(positions):
    half = MLA_ROPE // 2
    inv = jnp.power(ROPE_THETA, -jnp.arange(half, dtype=F32) / half)
    inv_lane = jnp.concatenate([jnp.zeros((64,), F32), inv, inv, jnp.zeros((32,), F32)]).reshape(1, LANES)
    pos = positions.reshape(SEQ, 1)
    shp = jax.ShapeDtypeStruct((SEQ, LANES), F32)
    spec = pl.BlockSpec((T_TOK, LANES), lambda i: (i, 0))
    return pl.pallas_call(
        _rope_kernel,
        out_shape=(shp, shp, shp),
        grid=(SEQ // T_TOK,),
        in_specs=[pl.BlockSpec((T_TOK, 1), lambda i: (i, 0)),
                  pl.BlockSpec((1, LANES), lambda i: (0, 0))],
        out_specs=(spec, spec, spec),
        compiler_params=_cparams(1),
        name="rope_tables",
    )(pos, inv_lane)


def _route(lg):
    lane = lax.broadcasted_iota(jnp.int32, lg.shape, 1)
    lane_f = lane.astype(F32)
    is_g = (lane >= N_EXPERTS) & (lane < N_EXPERTS + N_GROUPS)
    gl = jnp.where(is_g, lg, NEG)
    gmax = jnp.max(gl, axis=-1, keepdims=True)
    gidx = jnp.min(jnp.where(is_g & (gl == gmax), lane_f - N_EXPERTS, 1e3), axis=-1, keepdims=True)
    gsum = jnp.sum(jnp.where(is_g, jnp.exp(gl - gmax), 0.0), axis=-1, keepdims=True)
    gw = 1.0 / gsum
    grp_of_lane = (lane >> 3).astype(F32)
    in_grp = (lane < N_EXPERTS) & (grp_of_lane == gidx)
    el = jnp.where(in_grp, lg, NEG)
    m1 = jnp.max(el, axis=-1, keepdims=True)
    i1 = jnp.min(jnp.where(in_grp & (el == m1), lane_f, 1e3), axis=-1, keepdims=True)
    rest = in_grp & (lane_f != i1)
    el2 = jnp.where(rest, lg, NEG)
    m2 = jnp.max(el2, axis=-1, keepdims=True)
    i2 = jnp.min(jnp.where(rest & (el2 == m2), lane_f, 1e3), axis=-1, keepdims=True)
    r = jnp.exp(m2 - m1)
    w1 = gw / (1.0 + r)
    w2 = w1 * r
    return jnp.where(lane_f == i1, w1, jnp.where(lane_f == i2, w2, 0.0)), gidx


def _tail(x, y, mod, n2, wr_hi, wr_lo, br, xo_ref, h2_ref, meta_ref, cnt_ref, carry_ref):
    i = pl.program_id(0)

    @pl.when(i == 0)
    def _():
        carry_ref[...] = jnp.zeros(carry_ref.shape, F32)

    gate1 = mod[2:3]
    shift2, scale2 = mod[3:4], mod[4:5]
    xn = x + gate1 * y
    xo_ref[...] = xn
    h2 = _rms(xn) * n2 * (1.0 + scale2) + shift2
    hi = h2.astype(BF16)
    lo = (h2 - hi.astype(F32)).astype(BF16)
    lg = _dot(hi, wr_hi) + _dot(lo, wr_hi) + _dot(hi, wr_lo) + br
    gates, gidx = _route(lg)
    rows = lg.shape[0]
    lane = lax.broadcasted_iota(jnp.int32, (rows, LANES), 1)
    lane_f = lane.astype(F32)
    g8 = gates
    for g in range(1, N_GROUPS):
        g8 = g8 + pltpu.roll(gates, LANES - EPG * g, 1)
    onehot = (lane_f == gidx).astype(F32)
    r = lax.broadcasted_iota(jnp.int32, (rows, rows), 0)
    c = lax.broadcasted_iota(jnp.int32, (rows, rows), 1)
    before = jnp.where(c < r, 1.0, 0.0).astype(BF16)
    cum = _dot(before, onehot.astype(BF16)) + carry_ref[0:1, :]
    rank = jnp.sum(jnp.where(lane_f == gidx, cum, 0.0), axis=-1, keepdims=True)
    meta = jnp.where(lane < EPG, g8, jnp.where(lane == EPG, gidx, jnp.where(lane == EPG + 1, rank, 0.0)))
    h2_ref[:, 0:D] = h2
    h2_ref[:, D:D + LANES] = meta
    meta_ref[...] = meta
    total = carry_ref[0:1, :] + jnp.sum(onehot, axis=0, keepdims=True)
    carry_ref[...] = jnp.broadcast_to(total, carry_ref.shape)
    cnt_ref[...] = jnp.broadcast_to(total, cnt_ref.shape)


def _even_kernel(x_ref, xh_ref, mod_ref, n1_ref, n2_ref, win_ref, cw_ref, pw_ref, ps_ref,
                 wout_ref, wrh_ref, wrl_ref, br_ref, xo_ref, h2_ref, meta_ref, cnt_ref, carry_ref):
    i = pl.program_id(0)
    x = x_ref[...]
    mod = mod_ref[...]
    shift1, scale1 = mod[0:1], mod[1:2]
    xa = jnp.concatenate([xh_ref[...], x], axis=0)
    h = _rms(xa) * n1_ref[...] * (1.0 + scale1) + shift1
    z = _dot(h.astype(BF16), win_ref[...])
    rows = T_TOK + HALO
    row = lax.broadcasted_iota(jnp.int32, (rows, 1), 0)
    tpos = i * T_TOK + row - HALO
    live = (tpos >= 0).astype(F32)
    bg = z[:, 0:CONV_CH]
    v = z[:, CONV_CH:2 * CONV_CH] * z[:, 2 * CONV_CH:3 * CONV_CH] * live
    cw = cw_ref[...]
    conv = v * cw[0:1] + pltpu.roll(v, 1, 0) * cw[1:2] + pltpu.roll(v, 2, 0) * cw[2:3]
    parts = [(bg * conv)[HALO:]]
    ps = ps_ref[...]
    tcount = (tpos + 1).astype(F32)
    for gi, w in enumerate(POOL_WINDOWS):
        ug = z[:, 3 * CONV_CH + gi * POOL_G: 3 * CONV_CH + (gi + 1) * POOL_G] * live
        s = ug
        k = 1
        while k < w:
            s = s + pltpu.roll(s, k, 0)
            k *= 2
        inv = 1.0 / jnp.minimum(tcount, float(w))
        d = (s * inv - ug)[HALO:].astype(BF16)
        parts.append(_dot(d, pw_ref[gi]) * ps[:, gi * POOL_G:(gi + 1) * POOL_G])
    cat = jnp.concatenate(parts, axis=-1).astype(BF16)
    y = _dot(cat, wout_ref[...])
    _tail(x, y, mod, n2_ref[...], wrh_ref[...], wrl_ref[...], br_ref[...], xo_ref, h2_ref, meta_ref,
          cnt_ref, carry_ref)


def _tail_out(n):
    shapes = (jax.ShapeDtypeStruct((SEQ, D), F32), jax.ShapeDtypeStruct((SEQ, ROW_W), F32),
              jax.ShapeDtypeStruct((SEQ, LANES), F32), jax.ShapeDtypeStruct((8, LANES), F32))
    specs = (pl.BlockSpec((n, D), lambda i: (i, 0)), pl.BlockSpec((n, ROW_W), lambda i: (i, 0)),
             pl.BlockSpec((n, LANES), lambda i: (i, 0)), pl.BlockSpec((8, LANES), lambda i: (0, 0)))
    scratch = [pltpu.VMEM((8, LANES), F32)]
    return shapes, specs, scratch


def _full(shape):
    nd = len(shape)
    return pl.BlockSpec(shape, lambda i: (0,) * nd)


def _even_layer(x, mod, n1, n2, w_in, conv_w, pool_w, pool_scale, w_out, wr_hi, wr_lo, br):
    shapes, specs, scratch = _tail_out(T_TOK)
    hb = T_TOK // HALO
    return pl.pallas_call(
        _even_kernel,
        out_shape=shapes,
        grid=(SEQ // T_TOK,),
        in_specs=[pl.BlockSpec((T_TOK, D), lambda i: (i, 0)),
                  pl.BlockSpec((HALO, D), lambda i: (jnp.maximum(i * hb - 1, 0), 0)),
                  _full((8, D)), _full((1, D)), _full((1, D)),
                  _full((D, 4 * CONV_CH)), _full((3, CONV_CH)), _full((4, POOL_G, POOL_G)),
                  _full((1, 4 * POOL_G)), _full((D, D)),
                  _full((D, LANES)), _full((D, LANES)), _full((1, LANES))],
        out_specs=specs,
        scratch_shapes=scratch,
        compiler_params=_cparams(1),
        name="even_mixer",
    )(x, x, mod, n1, n2, w_in, conv_w, pool_w, pool_scale, w_out, wr_hi, wr_lo, br)


def _proj_kernel(x_ref, mod_ref, n1_ref, win_ref, gsq_ref, gsk_ref, gqn_ref, gkvn_ref,
                 wuq_ref, wuk_ref, wuv_ref, gmq_ref, gmk_ref, c_ref, s1_ref, s2_ref,
                 qs_ref, ks_ref, vs_ref, qm_ref, km_ref, vm_ref):
    x = x_ref[...]
    mod = mod_ref[...]
    shift1, scale1 = mod[0:1], mod[1:2]
    h = _rms(x) * n1_ref[...] * (1.0 + scale1) + shift1
    z = _dot(h.astype(BF16), win_ref[...])

    def head_norm(t, g, dim):
        ms = jnp.sum(t * t, axis=-1, keepdims=True) * (1.0 / dim)
        return t * lax.rsqrt(ms + EPS) * g

    gsq, gsk = gsq_ref[...], gsk_ref[...]
    for hd in range(SWA_HEADS):
        qh = head_norm(z[:, O_QS + hd * LANES: O_QS + (hd + 1) * LANES], gsq, HEAD_DIM)
        qs_ref[:, hd * LANES:(hd + 1) * LANES] = (qh * (HEAD_DIM ** -0.5)).astype(BF16)
    for kv in range(SWA_KV):
        kh = head_norm(z[:, O_KS + kv * LANES: O_KS + (kv + 1) * LANES], gsk, HEAD_DIM)
        ks_ref[:, kv * LANES:(kv + 1) * LANES] = kh.astype(BF16)
    vs_ref[...] = z[:, O_VS:O_CQ].astype(BF16)

    cq = (_rms(z[:, O_CQ:O_CKV]) * gqn_ref[...]).astype(BF16)
    ckv = (_rms(z[:, O_CKV:O_KR]) * gkvn_ref[...]).astype(BF16)
    qm = _dot(cq, wuq_ref[...])
    kn = _dot(ckv, wuk_ref[...])
    vm = _dot(ckv, wuv_ref[...])
    lane = lax.broadcasted_iota(jnp.int32, (T_TOK, LANES), 1)
    for j in range(MLA_HEADS // 2):
        vv = vm[:, j * LANES:(j + 1) * LANES]
        even = jnp.where(lane < 64, vv, jnp.where(lane == 64, 1.0, 0.0))
        odd = jnp.where(lane < 64, jnp.where(lane == 0, 1.0, 0.0), vv)
        vm_ref[:, (2 * j) * LANES:(2 * j + 1) * LANES] = even.astype(BF16)
        vm_ref[:, (2 * j + 1) * LANES:(2 * j + 2) * LANES] = odd.astype(BF16)
    kr = z[:, O_KR:ODD_W]
    cs, s1, s2 = c_ref[...], s1_ref[...], s2_ref[...]

    def rope(t):
        return t * cs + pltpu.roll(t, LANES - 16, 1) * s1 + pltpu.roll(t, 16, 1) * s2

    gmq, gmk = gmq_ref[...], gmk_ref[...]
    for hd in range(MLA_HEADS):
        sl = slice(hd * LANES, (hd + 1) * LANES)
        qh = rope(head_norm(qm[:, sl], gmq, MLA_QK))
        qm_ref[:, sl] = (qh * (MLA_QK ** -0.5 * LOG2E)).astype(BF16)
        kh = rope(head_norm(kn[:, sl] + kr, gmk, MLA_QK))
        km_ref[:, sl] = kh.astype(BF16)


def _proj_layer(x, mod, n1, w_in, gsq, gsk, gqn, gkvn, wuq, wuk, wuv, gmq, gmk, tc, ts1, ts2):
    def tok(wd):
        return pl.BlockSpec((T_TOK, wd), lambda i: (i, 0))
    widths = (1024, 256, 256, 1024, 1024, 1024)
    return pl.pallas_call(
        _proj_kernel,
        out_shape=tuple(jax.ShapeDtypeStruct((SEQ, wd), BF16) for wd in widths),
        grid=(SEQ // T_TOK,),
        in_specs=[tok(D), _full((8, D)), _full((1, D)), _full((D, ODD_W)),
                  _full((1, LANES)), _full((1, LANES)), _full((1, Q_LORA)), _full((1, KV_LORA)),
                  _full((Q_LORA, 1024)), _full((KV_LORA, 1024)), _full((KV_LORA, 512)),
                  _full((1, LANES)), _full((1, LANES)), tok(LANES), tok(LANES), tok(LANES)],
        out_specs=tuple(tok(wd) for wd in widths),
        compiler_params=_cparams(1),
        name="odd_proj",
    )(x, mod, n1, w_in, gsq, gsk, gqn, gkvn, wuq, wuk, wuv, gmq, gmk, tc, ts1, ts2)


def _swa_kernel(sink_ref, q_ref, k_ref, kh_ref, v_ref, vh_ref, o_ref):
    i = pl.program_id(0)
    kcat = jnp.concatenate([kh_ref[...], k_ref[...]], axis=0)
    vcat = jnp.concatenate([vh_ref[...], v_ref[...]], axis=0)
    grp = SWA_HEADS // SWA_KV
    r = lax.broadcasted_iota(jnp.int32, (grp * WINDOW, 2 * WINDOW), 0) & (WINDOW - 1)
    c = lax.broadcasted_iota(jnp.int32, (grp * WINDOW, 2 * WINDOW), 1)
    rel = WINDOW + r - c
    lane = lax.broadcasted_iota(jnp.int32, (WINDOW, LANES), 1)
    for sb in range(T_ATT // WINDOW):
        rows = slice(sb * WINDOW, (sb + 1) * WINDOW)
        kb = kcat[sb * WINDOW: sb * WINDOW + 2 * WINDOW]
        vb = vcat[sb * WINDOW: sb * WINDOW + 2 * WINDOW]
        kpos = i * T_ATT + (sb - 1) * WINDOW + c
        ok = (rel >= 0) & (rel < WINDOW) & (kpos >= 0)
        outs = []
        for kv in range(SWA_KV):
            q = jnp.concatenate([q_ref[rows, (kv * grp + g) * LANES:(kv * grp + g + 1) * LANES]
                                 for g in range(grp)], axis=0)
            sink = jnp.concatenate([jnp.full((WINDOW, 1), sink_ref[kv * grp + g], F32)
                                    for g in range(grp)], axis=0)
            s = jnp.where(ok, _dot_nt(q, kb[:, kv * LANES:(kv + 1) * LANES]), NEG)
            m = jnp.maximum(jnp.max(s, axis=-1, keepdims=True), sink)
            e = jnp.exp(s - m)
            den = jnp.sum(e, axis=-1, keepdims=True) + jnp.exp(sink - m)
            p = (e * (1.0 / den)).astype(BF16)
            o = _dot(p, vb[:, kv * LANES:(kv + 1) * LANES])
            outs += [o[g * WINDOW:(g + 1) * WINDOW] for g in range(grp)]
        for j in range(SWA_HEADS // 2):
            o_ref[rows, j * LANES:(j + 1) * LANES] = jnp.where(lane < 64, outs[2 * j], outs[2 * j + 1]).astype(BF16)


def _swa_layer(sinks, qs, ks, vs):
    hb = T_ATT // WINDOW
    return pl.pallas_call(
        _swa_kernel,
        out_shape=jax.ShapeDtypeStruct((SEQ, 512), BF16),
        grid=(SEQ // T_ATT,),
        in_specs=[pl.BlockSpec(memory_space=pltpu.SMEM),
                  pl.BlockSpec((T_ATT, 1024), lambda i: (i, 0)),
                  pl.BlockSpec((T_ATT, 256), lambda i: (i, 0)),
                  pl.BlockSpec((WINDOW, 256), lambda i: (jnp.maximum(i * hb - 1, 0), 0)),
                  pl.BlockSpec((T_ATT, 256), lambda i: (i, 0)),
                  pl.BlockSpec((WINDOW, 256), lambda i: (jnp.maximum(i * hb - 1, 0), 0))],
        out_specs=pl.BlockSpec((T_ATT, 512), lambda i: (i, 0)),
        compiler_params=_cparams(1),
        name="swa_attn",
    )(sinks, qs, ks, ks, vs, vs)


def _mla_kernel(qi_ref, ki_ref, q_ref, k_ref, v_ref, o_ref, m_ref, acc_ref):
    step = pl.program_id(0)
    qi = qi_ref[step]
    ki = ki_ref[step]

    @pl.when(ki == 0)
    def _():
        m_ref[...] = jnp.full(m_ref.shape, NEG, F32)
        acc_ref[...] = jnp.zeros(acc_ref.shape, F32)

    lane = lax.broadcasted_iota(jnp.int32, (T_ATT, LANES), 1)
    lo = lane < 64
    nc = T_ATT // LANES

    def scores(hd):
        sl = slice(hd * LANES, (hd + 1) * LANES)
        return _dot_nt(q_ref[:, sl], k_ref[:, sl])

    def update(masked):
        if masked:
            r = lax.broadcasted_iota(jnp.int32, (T_ATT, LANES), 0)

        def softmax_part(hd, s):
            cols = [s[:, c * LANES:(c + 1) * LANES] for c in range(nc)]
            if masked:
                cols = [jnp.where(r >= lane + c * LANES, cols[c], NEG) for c in range(nc)]
            cmax = cols[0]
            for c in range(1, nc):
                cmax = jnp.maximum(cmax, cols[c])
            m_prev = m_ref[hd]
            m_new = jnp.maximum(m_prev, jnp.max(cmax, axis=-1, keepdims=True))
            m_ref[hd] = m_new
            alpha = jnp.exp2(m_prev - m_new)
            p = jnp.concatenate([jnp.exp2(cols[c] - m_new).astype(BF16) for c in range(nc)], axis=-1)
            return p, alpha

        def value_part(hd, p, alpha):
            acc_ref[hd] = acc_ref[hd] * alpha + _dot(p, v_ref[:, hd * LANES:(hd + 1) * LANES])

        s_q = [scores(hd) for hd in range(MLA_DEPTH)]
        pend = None
        for hd in range(MLA_HEADS):
            if hd + MLA_DEPTH < MLA_HEADS:
                s_q.append(scores(hd + MLA_DEPTH))
            cur = softmax_part(hd, s_q[hd])
            if pend is not None:
                value_part(hd - 1, *pend)
            pend = cur
        value_part(MLA_HEADS - 1, *pend)

    @pl.when(ki < qi)
    def _():
        update(False)

    @pl.when(ki == qi)
    def _():
        update(True)
        for j in range(MLA_HEADS // 2):
            ae = acc_ref[2 * j]
            ao = acc_ref[2 * j + 1]
            out = jnp.where(lo, ae * (1.0 / ae[:, 64:65]), ao * (1.0 / ao[:, 0:1]))
            o_ref[:, j * LANES:(j + 1) * LANES] = out.astype(BF16)


def _mla_layer(qm, km, vm):
    nb = SEQ // T_ATT
    qi = np.concatenate([np.full(n + 1, n, np.int32) for n in range(nb)])
    ki = np.concatenate([np.arange(n + 1, dtype=np.int32) for n in range(nb)])
    grid_spec = pltpu.PrefetchScalarGridSpec(
        num_scalar_prefetch=2,
        grid=(int(qi.shape[0]),),
        in_specs=[pl.BlockSpec((T_ATT, 1024), lambda s, qi, ki: (qi[s], 0)),
                  pl.BlockSpec((T_ATT, 1024), lambda s, qi, ki: (ki[s], 0)),
                  pl.BlockSpec((T_ATT, 1024), lambda s, qi, ki: (ki[s], 0))],
        out_specs=pl.BlockSpec((T_ATT, 512), lambda s, qi, ki: (qi[s], 0)),
        scratch_shapes=[pltpu.VMEM((MLA_HEADS, T_ATT, LANES), F32),
                        pltpu.VMEM((MLA_HEADS, T_ATT, LANES), F32)],
    )
    return pl.pallas_call(
        _mla_kernel,
        out_shape=jax.ShapeDtypeStruct((SEQ, 512), BF16),
        grid_spec=grid_spec,
        compiler_params=_cparams(1),
        name="mla_attn",
    )(jnp.asarray(qi), jnp.asarray(ki), qm, km, vm)


def _post_kernel(x_ref, os_ref, om_ref, mod_ref, n2_ref, wout_ref, wrh_ref, wrl_ref, br_ref,
                 xo_ref, h2_ref, meta_ref, cnt_ref, carry_ref):
    y = _dot(os_ref[...], wout_ref[0:512, :]) + _dot(om_ref[...], wout_ref[512:1024, :])
    _tail(x_ref[...], y, mod_ref[...], n2_ref[...], wrh_ref[...], wrl_ref[...], br_ref[...],
          xo_ref, h2_ref, meta_ref, cnt_ref, carry_ref)


def _post_layer(x, o_s, o_m, mod, n2, w_out, wr_hi, wr_lo, br):
    shapes, specs, scratch = _tail_out(T_TOK)
    return pl.pallas_call(
        _post_kernel,
        out_shape=shapes,
        grid=(SEQ // T_TOK,),
        in_specs=[pl.BlockSpec((T_TOK, D), lambda i: (i, 0)),
                  pl.BlockSpec((T_TOK, 512), lambda i: (i, 0)),
                  pl.BlockSpec((T_TOK, 512), lambda i: (i, 0)),
                  _full((8, D)), _full((1, D)), _full((D, D)),
                  _full((D, LANES)), _full((D, LANES)), _full((1, LANES))],
        out_specs=specs,
        scratch_shapes=scratch,
        compiler_params=_cparams(1),
        name="odd_post",
    )(x, o_s, o_m, mod, n2, w_out, wr_hi, wr_lo, br)


def _dispatch_plan(meta, cnt):
    grp = meta[:, EPG].astype(jnp.int32)
    rank = meta[:, EPG + 1].astype(jnp.int32)
    counts = cnt[0, :N_GROUPS].astype(jnp.int32)
    padded = ((counts + T_MOE - 1) // T_MOE) * T_MOE
    ends = jnp.cumsum(padded)
    pos = (ends - padded)[grp] + rank
    n_used = ends[-1] // T_MOE
    tile_start = jnp.arange(N_TILES, dtype=jnp.int32) * T_MOE
    tile_group = jnp.minimum(jnp.sum(tile_start[:, None] >= ends[None, :], axis=1), N_GROUPS - 1)
    return pos.reshape(SEQ // T_DISP, 1, T_DISP), tile_group.astype(jnp.int32), n_used.reshape(1)


def _row_copies(n, src_at, dst_at, sem):
    def body(r, carry):
        pltpu.make_async_copy(src_at(r), dst_at(r), sem).start()
        return carry
    lax.fori_loop(0, n, body, 0, unroll=8)


def _disp_kernel(pos_ref, x_ref, init_ref, o_ref, sem):
    del init_ref
    _row_copies(T_DISP,
                lambda r: x_ref.at[pl.ds(r, 1), :],
                lambda r: o_ref.at[pl.ds(pos_ref[0, 0, r], 1), :], sem)
    pltpu.make_async_copy(x_ref, o_ref.at[pl.ds(0, T_DISP), :], sem).wait()


def _dispatch(pos, h2a, hs):
    return pl.pallas_call(
        _disp_kernel,
        out_shape=jax.ShapeDtypeStruct((N_SORT, ROW_W), F32),
        grid=(SEQ // T_DISP,),
        in_specs=[pl.BlockSpec((1, 1, T_DISP), lambda i: (i, 0, 0), memory_space=pltpu.SMEM),
                  pl.BlockSpec((T_DISP, ROW_W), lambda i: (i, 0)),
                  pl.BlockSpec(memory_space=pl.ANY)],
        out_specs=pl.BlockSpec(memory_space=pl.ANY),
        scratch_shapes=[pltpu.SemaphoreType.DMA(())],
        input_output_aliases={2: 0},
        compiler_params=_cparams(1),
        name="moe_dispatch",
    )(pos, h2a, hs)


def _moe_kernel(tg_ref, nu_ref, x_ref, wg_ref, wu_ref, wd_ref, o_ref, xb_ref, act_ref):
    del tg_ref
    i = pl.program_id(0)
    j = pl.program_id(1)
    last = EPG // E_STEP - 1

    @pl.when(i < nu_ref[0])
    def _():
        @pl.when(j == 0)
        def _():
            xb_ref[...] = x_ref[:, 0:D].astype(BF16)

        xb = xb_ref[...]
        meta = x_ref[:, D:ROW_W]
        lane = lax.broadcasted_iota(jnp.int32, meta.shape, 1)
        for k in range(E_STEP):
            a = _dot(xb, wg_ref[0, k].astype(BF16))
            u = _dot(xb, wu_ref[0, k].astype(BF16))
            gate = jnp.sum(jnp.where(lane == j * E_STEP + k, meta, 0.0), axis=-1, keepdims=True)
            act = (a * jax.nn.sigmoid(a) * u * gate).astype(BF16)
            for jj in range(EPG // E_STEP):
                @pl.when(j == jj)
                def _(jj=jj, k=k, act=act):
                    act_ref[:, (jj * E_STEP + k) * FF:(jj * E_STEP + k + 1) * FF] = act

        @pl.when(j == last)
        def _():
            o_ref[...] = _dot(act_ref[...], wd_ref[0, 0].astype(BF16))

    @pl.when((i >= nu_ref[0]) & (j == last))
    def _():
        o_ref[...] = jnp.zeros(o_ref.shape, F32)


def _moe_experts(tile_group, n_used, hs, wg, wu, wd, layer):
    def tile(i, e, tg, nu):
        return (jnp.minimum(i, nu[0] - 1), 0)

    nj = EPG // E_STEP

    def expert(i, j, tg, nu):
        return (layer, jnp.where(i < nu[0], tg[i] * nj + j, tg[nu[0] - 1] * nj + nj - 1), 0, 0)

    def group(i, j, tg, nu):
        return (layer, tg[jnp.minimum(i, nu[0] - 1)], 0, 0)

    grid_spec = pltpu.PrefetchScalarGridSpec(
        num_scalar_prefetch=2,
        grid=(N_TILES, nj),
        in_specs=[pl.BlockSpec((T_MOE, ROW_W), tile),
                  pl.BlockSpec((1, E_STEP, D, FF), expert),
                  pl.BlockSpec((1, E_STEP, D, FF), expert),
                  pl.BlockSpec((1, 1, EPG * FF, D), group, pipeline_mode=pl.Buffered(1))],
        out_specs=pl.BlockSpec((T_MOE, D), lambda i, e, tg, nu: (i, 0)),
        scratch_shapes=[pltpu.VMEM((T_MOE, D), BF16), pltpu.VMEM((T_MOE, EPG * FF), BF16)],
    )
    return pl.pallas_call(
        _moe_kernel,
        out_shape=jax.ShapeDtypeStruct((N_SORT, D), F32),
        grid_spec=grid_spec,
        compiler_params=pltpu.CompilerParams(dimension_semantics=("arbitrary", "arbitrary"),
                                             vmem_limit_bytes=MOE_VMEM_LIMIT),
        name="moe_experts",
    )(tile_group, n_used, hs, wg, wu, wd.reshape(DEPTH, N_GROUPS, EPG * FF, D))


def _comb_kernel(pos_ref, x_ref, mod_ref, y_ref, o_ref, buf_ref, sem):
    _row_copies(T_DISP,
                lambda r: y_ref.at[pl.ds(pos_ref[0, 0, r], 1), :],
                lambda r: buf_ref.at[pl.ds(r, 1), :], sem)
    pltpu.make_async_copy(y_ref.at[pl.ds(0, T_DISP), :], buf_ref, sem).wait()
    o_ref[...] = x_ref[...] + mod_ref[5:6, :] * buf_ref[...]


def _combine(pos, x, mod, ys):
    return pl.pallas_call(
        _comb_kernel,
        out_shape=jax.ShapeDtypeStruct((SEQ, D), F32),
        grid=(SEQ // T_DISP,),
        in_specs=[pl.BlockSpec((1, 1, T_DISP), lambda i: (i, 0, 0), memory_space=pltpu.SMEM),
                  pl.BlockSpec((T_DISP, D), lambda i: (i, 0)),
                  pl.BlockSpec((8, D), lambda i: (0, 0)),
                  pl.BlockSpec(memory_space=pl.ANY)],
        out_specs=pl.BlockSpec((T_DISP, D), lambda i: (i, 0)),
        scratch_shapes=[pltpu.VMEM((T_DISP, D), F32), pltpu.SemaphoreType.DMA(())],
        compiler_params=_cparams(1),
        name="moe_combine",
    )(pos, x, mod, ys)


def _moe_layer(x, h2a, meta, cnt, mod, wg, wu, wd, layer, hs):
    pos, tile_group, n_used = _dispatch_plan(meta, cnt)
    hs = _dispatch(pos, h2a, hs)
    ys = _moe_experts(tile_group, n_used, hs, wg, wu, wd, layer)
    return _combine(pos, x, mod, ys), hs


def _pad_heads(w, heads, dim):
    k = w.shape[0]
    w = w.reshape(k, heads, dim)
    return jnp.pad(w, ((0, 0), (0, 0), (0, LANES - dim))).reshape(k, heads * LANES)


def _pad_gain(g):
    return jnp.pad(g, (0, LANES - g.shape[0])).reshape(1, LANES)


def _odd_weights(w_in):
    q_s = _pad_heads(w_in[:, 0:512], SWA_HEADS, HEAD_DIM)
    k_s = _pad_heads(w_in[:, 512:640], SWA_KV, HEAD_DIM)
    v = w_in[:, 640:768]
    v_s = jnp.concatenate([v[:, 0:64], v[:, 0:64], v[:, 64:128], v[:, 64:128]], axis=1)
    c_q = w_in[:, 768:1152]
    c_kv = w_in[:, 1152:1408]
    k_r = jnp.pad(w_in[:, 1408:1440], ((0, 0), (MLA_NOPE, LANES - MLA_QK)))
    return jnp.concatenate([q_s, k_s, v_s, c_q, c_kv, k_r], axis=1).astype(BF16)


def _router_weights(w_group, b_group, w_expert, b_expert):
    w = jnp.pad(jnp.concatenate([w_expert, w_group], axis=1), ((0, 0), (0, LANES - N_EXPERTS - N_GROUPS)))
    hi = w.astype(BF16)
    lo = (w - hi.astype(F32)).astype(BF16)
    b = jnp.pad(jnp.concatenate([b_expert, b_group]), (0, LANES - N_EXPERTS - N_GROUPS)).reshape(1, LANES)
    return hi, lo, b


def kernel(x, c, positions, ada_w, ada_b, norm1_g, norm2_g, cp_w_in, conv_w, pool_w, pool_scale,
           cp_w_out, at_w_in, swa_q_g, swa_k_g, swa_sinks, mla_q_norm_g, mla_kv_norm_g, mla_w_uq,
           mla_w_ukv, mla_q_g, mla_k_g, at_w_out, moe_w_group, moe_b_group, moe_w_expert,
           moe_b_expert, moe_w_gate, moe_w_up, moe_w_down):
    xs = x.reshape(SEQ, D)
    mods = _ada_mod(c, ada_w, ada_b)
    tc, ts1, ts2 = _rope_tables(positions)
    hs = jnp.zeros((N_SORT, ROW_W), F32)
    for l in range(DEPTH):
        i = l // 2
        mod = mods[l]
        n1 = norm1_g[l].reshape(1, D)
        n2 = norm2_g[l].reshape(1, D)
        wr_hi, wr_lo, br = _router_weights(moe_w_group[l], moe_b_group[l], moe_w_expert[l], moe_b_expert[l])
        if l % 2 == 0:
            xs, h2a, meta, cnt = _even_layer(
                xs, mod, n1, n2, cp_w_in[i].astype(BF16), conv_w[i], pool_w[i].astype(BF16),
                pool_scale[i].reshape(1, 4 * POOL_G), cp_w_out[i].astype(BF16), wr_hi, wr_lo, br)
        else:
            ukv = mla_w_ukv[i].reshape(KV_LORA, MLA_HEADS, MLA_NOPE + MLA_V)
            wuk = _pad_heads(ukv[:, :, :MLA_NOPE].reshape(KV_LORA, MLA_HEADS * MLA_NOPE), MLA_HEADS, MLA_NOPE)
            wuv = ukv[:, :, MLA_NOPE:].reshape(KV_LORA, MLA_HEADS * MLA_V)
            wuq = _pad_heads(mla_w_uq[i], MLA_HEADS, MLA_QK)
            qs, ks, vs, qm, km, vm = _proj_layer(
                xs, mod, n1, _odd_weights(at_w_in[i]), _pad_gain(swa_q_g[i]), _pad_gain(swa_k_g[i]),
                mla_q_norm_g[i].reshape(1, Q_LORA), mla_kv_norm_g[i].reshape(1, KV_LORA),
                wuq.astype(BF16), wuk.astype(BF16), wuv.astype(BF16),
                _pad_gain(mla_q_g[i]), _pad_gain(mla_k_g[i]), tc, ts1, ts2)
            o_s = _swa_layer(swa_sinks[i], qs, ks, vs)
            o_m = _mla_layer(qm, km, vm)
            xs, h2a, meta, cnt = _post_layer(xs, o_s, o_m, mod, n2, at_w_out[i].astype(BF16), wr_hi, wr_lo, br)
        xs, hs = _moe_layer(xs, h2a, meta, cnt, mod, moe_w_gate, moe_w_up, moe_w_down, l, hs)
    return xs.reshape(1, SEQ, D)
```

```python
import functools

import numpy as np
import jax
import jax.numpy as jnp
from jax import lax
from jax.experimental import pallas as pl
from jax.experimental.pallas import tpu as pltpu

F32 = jnp.float32
BF16 = jnp.bfloat16

D = 1024
SEQ = 16384
DEPTH = 4
EPS = 1e-6
LANES = 128
CONV_CH = 512
POOL_WINDOWS = (2, 4, 8, 16)
POOL_G = 128
HALO = 16
SWA_HEADS = 8
SWA_KV = 2
HEAD_DIM = 64
WINDOW = 128
MLA_HEADS = 8
MLA_NOPE = 64
MLA_ROPE = 32
MLA_QK = MLA_NOPE + MLA_ROPE
MLA_V = 64
Q_LORA = 384
KV_LORA = 256
ROPE_THETA = 10000.0
N_GROUPS = 4
EPG = 8
N_EXPERTS = N_GROUPS * EPG
FF = 256
NEG = -1e30
LOG2E = 1.4426950408889634

T_TOK = 512
T_ATT = 512
T_MOE = 1024
T_DISP = 512
ROW_W = D + LANES
N_SORT = SEQ + N_GROUPS * T_MOE
N_TILES = N_SORT // T_MOE
E_STEP = 4
MLA_DEPTH = 3
VMEM_LIMIT = 48 * 1024 * 1024
MOE_VMEM_LIMIT = 56 * 1024 * 1024

O_QS, O_KS, O_VS, O_CQ, O_CKV, O_KR, ODD_W = 0, 1024, 1280, 1536, 1920, 2176, 2432


def _cparams(n_axes=1):
    return pltpu.CompilerParams(dimension_semantics=("arbitrary",) * n_axes,
                                vmem_limit_bytes=VMEM_LIMIT)


def _rms(x):
    return x * lax.rsqrt(jnp.mean(x * x, axis=-1, keepdims=True) + EPS)


def _dot(a, b):
    return jnp.dot(a, b, preferred_element_type=F32)


def _dot_nt(a, b):
    return lax.dot_general(a, b, (((1,), (1,)), ((), ())), preferred_element_type=F32)


def _ada_kernel(c_ref, w_ref, b_ref, o_ref):
    c = c_ref[...]
    ca = c * jax.nn.sigmoid(c)
    o_ref[0] = jnp.sum(w_ref[0] * ca, axis=0, keepdims=True) + b_ref[0]


def _ada_mod(c, ada_w, ada_b):
    c_col = c.reshape(D, 1)
    b = ada_b.reshape(DEPTH * 6, 1, D)
    out = pl.pallas_call(
        _ada_kernel,
        out_shape=jax.ShapeDtypeStruct((DEPTH * 6, 1, D), F32),
        grid=(DEPTH, 6),
        in_specs=[pl.BlockSpec((D, 1), lambda l, j: (0, 0)),
                  pl.BlockSpec((1, D, D), lambda l, j: (l, 0, j)),
                  pl.BlockSpec((1, 1, D), lambda l, j: (l * 6 + j, 0, 0))],
        out_specs=pl.BlockSpec((1, 1, D), lambda l, j: (l * 6 + j, 0, 0)),
        compiler_params=_cparams(2),
        name="ada_mod",
    )(c_col, ada_w, b)
    mod = out.reshape(DEPTH, 6, D)
    return jnp.pad(mod, ((0, 0), (0, 2), (0, 0)))


def _rope_kernel(pos_ref, inv_ref, c_ref, s1_ref, s2_ref):
    pos = pos_ref[...].astype(F32)
    ang = pos * inv_ref[...]
    lane = lax.broadcasted_iota(jnp.int32, ang.shape, 1)
    cs = jnp.cos(ang)
    sn = jnp.sin(ang)
    c_ref[...] = jnp.where(lane < 64, 1.0, jnp.where(lane < 96, cs, 0.0))
    s1_ref[...] = jnp.where((lane >= 64) & (lane < 80), -sn, 0.0)
    s2_ref[...] = jnp.where((lane >= 80) & (lane < 96), sn, 0.0)


def _rope_tables(positions):
    half = MLA_ROPE // 2
    inv = jnp.power(ROPE_THETA, -jnp.arange(half, dtype=F32) / half)
    inv_lane = jnp.concatenate([jnp.zeros((64,), F32), inv, inv, jnp.zeros((32,), F32)]).reshape(1, LANES)
    pos = positions.reshape(SEQ, 1)
    shp = jax.ShapeDtypeStruct((SEQ, LANES), F32)
    spec = pl.BlockSpec((T_TOK, LANES), lambda i: (i, 0))
    return pl.pallas_call(
        _rope_kernel,
        out_shape=(shp, shp, shp),
        grid=(SEQ // T_TOK,),
        in_specs=[pl.BlockSpec((T_TOK, 1), lambda i: (i, 0)),
                  pl.BlockSpec((1, LANES), lambda i: (0, 0))],
        out_specs=(spec, spec, spec),
        compiler_params=_cparams(1),
        name="rope_tables",
    )(pos, inv_lane)


def _route(lg):
    lane = lax.broadcasted_iota(jnp.int32, lg.shape, 1)
    lane_f = lane.astype(F32)
    is_g = (lane >= N_EXPERTS) & (lane < N_EXPERTS + N_GROUPS)
    gl = jnp.where(is_g, lg, NEG)
    gmax = jnp.max(gl, axis=-1, keepdims=True)
    gidx = jnp.min(jnp.where(is_g & (gl == gmax), lane_f - N_EXPERTS, 1e3), axis=-1, keepdims=True)
    gsum = jnp.sum(jnp.where(is_g, jnp.exp(gl - gmax), 0.0), axis=-1, keepdims=True)
    gw = 1.0 / gsum
    grp_of_lane = (lane >> 3).astype(F32)
    in_grp = (lane < N_EXPERTS) & (grp_of_lane == gidx)
    el = jnp.where(in_grp, lg, NEG)
    m1 = jnp.max(el, axis=-1, keepdims=True)
    i1 = jnp.min(jnp.where(in_grp & (el == m1), lane_f, 1e3), axis=-1, keepdims=True)
    rest = in_grp & (lane_f != i1)
    el2 = jnp.where(rest, lg, NEG)
    m2 = jnp.max(el2, axis=-1, keepdims=True)
    i2 = jnp.min(jnp.where(rest & (el2 == m2), lane_f, 1e3), axis=-1, keepdims=True)
    r = jnp.exp(m2 - m1)
    w1 = gw / (1.0 + r)
    w2 = w1 * r
    return jnp.where(lane_f == i1, w1, jnp.where(lane_f == i2, w2, 0.0)), gidx


def _tail(x, y, mod, n2, wr_hi, wr_lo, br, xo_ref, h2_ref, meta_ref, cnt_ref, carry_ref):
    i = pl.program_id(0)

    @pl.when(i == 0)
    def _():
        carry_ref[...] = jnp.zeros(carry_ref.shape, F32)

    gate1 = mod[2:3]
    shift2, scale2 = mod[3:4], mod[4:5]
    xn = x + gate1 * y
    xo_ref[...] = xn
    h2 = _rms(xn) * n2 * (1.0 + scale2) + shift2
    hi = h2.astype(BF16)
    lo = (h2 - hi.astype(F32)).astype(BF16)
    lg = _dot(hi, wr_hi) + _dot(lo, wr_hi) + _dot(hi, wr_lo) + br
    gates, gidx = _route(lg)
    rows = lg.shape[0]
    lane = lax.broadcasted_iota(jnp.int32, (rows, LANES), 1)
    lane_f = lane.astype(F32)
    g8 = gates
    for g in range(1, N_GROUPS):
        g8 = g8 + pltpu.roll(gates, LANES - EPG * g, 1)
    onehot = (lane_f == gidx).astype(F32)
    r = lax.broadcasted_iota(jnp.int32, (rows, rows), 0)
    c = lax.broadcasted_iota(jnp.int32, (rows, rows), 1)
    before = jnp.where(c < r, 1.0, 0.0).astype(BF16)
    cum = _dot(before, onehot.astype(BF16)) + carry_ref[0:1, :]
    rank = jnp.sum(jnp.where(lane_f == gidx, cum, 0.0), axis=-1, keepdims=True)
    meta = jnp.where(lane < EPG, g8, jnp.where(lane == EPG, gidx, jnp.where(lane == EPG + 1, rank, 0.0)))
    h2_ref[:, 0:D] = h2
    h2_ref[:, D:D + LANES] = meta
    meta_ref[...] = meta
    total = carry_ref[0:1, :] + jnp.sum(onehot, axis=0, keepdims=True)
    carry_ref[...] = jnp.broadcast_to(total, carry_ref.shape)
    cnt_ref[...] = jnp.broadcast_to(total, cnt_ref.shape)


def _even_kernel(x_ref, xh_ref, mod_ref, n1_ref, n2_ref, win_ref, cw_ref, pw_ref, ps_ref,
                 wout_ref, wrh_ref, wrl_ref, br_ref, xo_ref, h2_ref, meta_ref, cnt_ref, carry_ref):
    i = pl.program_id(0)
    x = x_ref[...]
    mod = mod_ref[...]
    shift1, scale1 = mod[0:1], mod[1:2]
    xa = jnp.concatenate([xh_ref[...], x], axis=0)
    h = _rms(xa) * n1_ref[...] * (1.0 + scale1) + shift1
    z = _dot(h.astype(BF16), win_ref[...])
    rows = T_TOK + HALO
    row = lax.broadcasted_iota(jnp.int32, (rows, 1), 0)
    tpos = i * T_TOK + row - HALO
    live = (tpos >= 0).astype(F32)
    bg = z[:, 0:CONV_CH]
    v = z[:, CONV_CH:2 * CONV_CH] * z[:, 2 * CONV_CH:3 * CONV_CH] * live
    cw = cw_ref[...]
    conv = v * cw[0:1] + pltpu.roll(v, 1, 0) * cw[1:2] + pltpu.roll(v, 2, 0) * cw[2:3]
    parts = [(bg * conv)[HALO:]]
    ps = ps_ref[...]
    tcount = (tpos + 1).astype(F32)
    for gi, w in enumerate(POOL_WINDOWS):
        ug = z[:, 3 * CONV_CH + gi * POOL_G: 3 * CONV_CH + (gi + 1) * POOL_G] * live
        s = ug
        k = 1
        while k < w:
            s = s + pltpu.roll(s, k, 0)
            k *= 2
        inv = 1.0 / jnp.minimum(tcount, float(w))
        d = (s * inv - ug)[HALO:].astype(BF16)
        parts.append(_dot(d, pw_ref[gi]) * ps[:, gi * POOL_G:(gi + 1) * POOL_G])
    cat = jnp.concatenate(parts, axis=-1).astype(BF16)
    y = _dot(cat, wout_ref[...])
    _tail(x, y, mod, n2_ref[...], wrh_ref[...], wrl_ref[...], br_ref[...], xo_ref, h2_ref, meta_ref,
          cnt_ref, carry_ref)


def _tail_out(n):
    shapes = (jax.ShapeDtypeStruct((SEQ, D), F32), jax.ShapeDtypeStruct((SEQ, ROW_W), F32),
              jax.ShapeDtypeStruct((SEQ, LANES), F32), jax.ShapeDtypeStruct((8, LANES), F32))
    specs = (pl.BlockSpec((n, D), lambda i: (i, 0)), pl.BlockSpec((n, ROW_W), lambda i: (i, 0)),
             pl.BlockSpec((n, LANES), lambda i: (i, 0)), pl.BlockSpec((8, LANES), lambda i: (0, 0)))
    scratch = [pltpu.VMEM((8, LANES), F32)]
    return shapes, specs, scratch


def _full(shape):
    nd = len(shape)
    return pl.BlockSpec(shape, lambda i: (0,) * nd)


def _even_layer(x, mod, n1, n2, w_in, conv_w, pool_w, pool_scale, w_out, wr_hi, wr_lo, br):
    shapes, specs, scratch = _tail_out(T_TOK)
    hb = T_TOK // HALO
    return pl.pallas_call(
        _even_kernel,
        out_shape=shapes,
        grid=(SEQ // T_TOK,),
        in_specs=[pl.BlockSpec((T_TOK, D), lambda i: (i, 0)),
                  pl.BlockSpec((HALO, D), lambda i: (jnp.maximum(i * hb - 1, 0), 0)),
                  _full((8, D)), _full((1, D)), _full((1, D)),
                  _full((D, 4 * CONV_CH)), _full((3, CONV_CH)), _full((4, POOL_G, POOL_G)),
                  _full((1, 4 * POOL_G)), _full((D, D)),
                  _full((D, LANES)), _full((D, LANES)), _full((1, LANES))],
        out_specs=specs,
        scratch_shapes=scratch,
        compiler_params=_cparams(1),
        name="even_mixer",
    )(x, x, mod, n1, n2, w_in, conv_w, pool_w, pool_scale, w_out, wr_hi, wr_lo, br)


def _proj_kernel(x_ref, mod_ref, n1_ref, win_ref, gsq_ref, gsk_ref, gqn_ref, gkvn_ref,
                 wuq_ref, wuk_ref, wuv_ref, gmq_ref, gmqs_ref, gmk_ref, gmks_ref, c_ref, s1_ref, s2_ref,
                 qs_ref, ks_ref, vs_ref, qm_ref, km_ref, vm_ref):
    x = x_ref[...]
    mod = mod_ref[...]
    shift1, scale1 = mod[0:1], mod[1:2]
    h = _rms(x) * n1_ref[...] * (1.0 + scale1) + shift1
    z = _dot(h.astype(BF16), win_ref[...])

    def head_norm(t, g, dim):
        ms = jnp.sum(t * t, axis=-1, keepdims=True) * (1.0 / dim)
        return t * lax.rsqrt(ms + EPS) * g

    gsq, gsk = gsq_ref[...] * (HEAD_DIM ** -0.5), gsk_ref[...]
    for hd in range(SWA_HEADS):
        qh = head_norm(z[:, O_QS + hd * LANES: O_QS + (hd + 1) * LANES], gsq, HEAD_DIM)
        qs_ref[:, hd * LANES:(hd + 1) * LANES] = qh.astype(BF16)
    for kv in range(SWA_KV):
        kh = head_norm(z[:, O_KS + kv * LANES: O_KS + (kv + 1) * LANES], gsk, HEAD_DIM)
        ks_ref[:, kv * LANES:(kv + 1) * LANES] = kh.astype(BF16)
    vs_ref[...] = z[:, O_VS:O_CQ].astype(BF16)

    cq = (_rms(z[:, O_CQ:O_CKV]) * gqn_ref[...]).astype(BF16)
    ckv = (_rms(z[:, O_CKV:O_KR]) * gkvn_ref[...]).astype(BF16)
    qm = _dot(cq, wuq_ref[...])
    kn = _dot(ckv, wuk_ref[...])
    vm = _dot(ckv, wuv_ref[...])
    lane = lax.broadcasted_iota(jnp.int32, (T_TOK, LANES), 1)
    for j in range(MLA_HEADS // 2):
        vv = vm[:, j * LANES:(j + 1) * LANES]
        even = jnp.where(lane < 64, vv, jnp.where(lane == 64, 1.0, 0.0))
        odd = jnp.where(lane < 64, jnp.where(lane == 0, 1.0, 0.0), vv)
        vm_ref[:, (2 * j) * LANES:(2 * j + 1) * LANES] = even.astype(BF16)
        vm_ref[:, (2 * j + 1) * LANES:(2 * j + 2) * LANES] = odd.astype(BF16)
    kr = z[:, O_KR:O_KR + LANES]
    krs = z[:, O_KR + LANES:ODD_W]
    cs = c_ref[...]
    sp = s1_ref[...] + s2_ref[...]
    qscale = MLA_QK ** -0.5 * LOG2E
    gcq = cs * (gmq_ref[...] * qscale)
    gsq_r = sp * (gmqs_ref[...] * qscale)
    gck = cs * gmk_ref[...]
    krot = krs * (sp * gmks_ref[...])

    def inv_rms(t):
        return lax.rsqrt(jnp.sum(t * t, axis=-1, keepdims=True) * (1.0 / MLA_QK) + EPS)

    for hd in range(MLA_HEADS):
        sl = slice(hd * LANES, (hd + 1) * LANES)
        sw = slice((MLA_HEADS + hd) * LANES, (MLA_HEADS + hd + 1) * LANES)
        qh = qm[:, sl]
        qm_ref[:, sl] = ((qh * gcq + qm[:, sw] * gsq_r) * inv_rms(qh)).astype(BF16)
        kh = kn[:, sl] + kr
        km_ref[:, sl] = ((kh * gck + krot) * inv_rms(kh)).astype(BF16)


def _proj_layer(x, mod, n1, w_in, gsq, gsk, gqn, gkvn, wuq, wuk, wuv, gmq, gmqs, gmk, gmks, tc, ts1, ts2):
    def tok(wd):
        return pl.BlockSpec((T_TOK, wd), lambda i: (i, 0))
    widths = (1024, 256, 256, 1024, 1024, 1024)
    return pl.pallas_call(
        _proj_kernel,
        out_shape=tuple(jax.ShapeDtypeStruct((SEQ, wd), BF16) for wd in widths),
        grid=(SEQ // T_TOK,),
        in_specs=[tok(D), _full((8, D)), _full((1, D)), _full((D, ODD_W)),
                  _full((1, LANES)), _full((1, LANES)), _full((1, Q_LORA)), _full((1, KV_LORA)),
                  _full((Q_LORA, 2048)), _full((KV_LORA, 1024)), _full((KV_LORA, 512)),
                  _full((1, LANES)), _full((1, LANES)), _full((1, LANES)), _full((1, LANES)),
                  tok(LANES), tok(LANES), tok(LANES)],
        out_specs=tuple(tok(wd) for wd in widths),
        compiler_params=_cparams(1),
        name="odd_proj",
    )(x, mod, n1, w_in, gsq, gsk, gqn, gkvn, wuq, wuk, wuv, gmq, gmqs, gmk, gmks, tc, ts1, ts2)


def _swa_kernel(sink_ref, q_ref, k_ref, kh_ref, v_ref, vh_ref, o_ref):
    i = pl.program_id(0)
    kcat = jnp.concatenate([kh_ref[...], k_ref[...]], axis=0)
    vcat = jnp.concatenate([vh_ref[...], v_ref[...]], axis=0)
    grp = SWA_HEADS // SWA_KV
    r = lax.broadcasted_iota(jnp.int32, (grp * WINDOW, 2 * WINDOW), 0) & (WINDOW - 1)
    c = lax.broadcasted_iota(jnp.int32, (grp * WINDOW, 2 * WINDOW), 1)
    rel = WINDOW + r - c
    lane = lax.broadcasted_iota(jnp.int32, (WINDOW, LANES), 1)
    for sb in range(T_ATT // WINDOW):
        rows = slice(sb * WINDOW, (sb + 1) * WINDOW)
        kb = kcat[sb * WINDOW: sb * WINDOW + 2 * WINDOW]
        vb = vcat[sb * WINDOW: sb * WINDOW + 2 * WINDOW]
        kpos = i * T_ATT + (sb - 1) * WINDOW + c
        ok = (rel >= 0) & (rel < WINDOW) & (kpos >= 0)
        outs = []
        for kv in range(SWA_KV):
            q = jnp.concatenate([q_ref[rows, (kv * grp + g) * LANES:(kv * grp + g + 1) * LANES]
                                 for g in range(grp)], axis=0)
            sink = jnp.concatenate([jnp.full((WINDOW, 1), sink_ref[kv * grp + g], F32)
                                    for g in range(grp)], axis=0)
            s = jnp.where(ok, _dot_nt(q, kb[:, kv * LANES:(kv + 1) * LANES]), NEG)
            m = jnp.maximum(jnp.max(s, axis=-1, keepdims=True), sink)
            e = jnp.exp(s - m)
            den = jnp.sum(e, axis=-1, keepdims=True) + jnp.exp(sink - m)
            p = (e * (1.0 / den)).astype(BF16)
            o = _dot(p, vb[:, kv * LANES:(kv + 1) * LANES])
            outs += [o[g * WINDOW:(g + 1) * WINDOW] for g in range(grp)]
        for j in range(SWA_HEADS // 2):
            o_ref[rows, j * LANES:(j + 1) * LANES] = jnp.where(lane < 64, outs[2 * j], outs[2 * j + 1]).astype(BF16)


def _swa_layer(sinks, qs, ks, vs):
    hb = T_ATT // WINDOW
    return pl.pallas_call(
        _swa_kernel,
        out_shape=jax.ShapeDtypeStruct((SEQ, 512), BF16),
        grid=(SEQ // T_ATT,),
        in_specs=[pl.BlockSpec(memory_space=pltpu.SMEM),
                  pl.BlockSpec((T_ATT, 1024), lambda i: (i, 0)),
                  pl.BlockSpec((T_ATT, 256), lambda i: (i, 0)),
                  pl.BlockSpec((WINDOW, 256), lambda i: (jnp.maximum(i * hb - 1, 0), 0)),
                  pl.BlockSpec((T_ATT, 256), lambda i: (i, 0)),
                  pl.BlockSpec((WINDOW, 256), lambda i: (jnp.maximum(i * hb - 1, 0), 0))],
        out_specs=pl.BlockSpec((T_ATT, 512), lambda i: (i, 0)),
        compiler_params=_cparams(1),
        name="swa_attn",
    )(sinks, qs, ks, ks, vs, vs)


def _mla_kernel(qi_ref, ki_ref, q_ref, k_ref, v_ref, o_ref, m_ref, acc_ref):
    step = pl.program_id(0)
    qi = qi_ref[step]
    ki = ki_ref[step]

    @pl.when(ki == 0)
    def _():
        m_ref[...] = jnp.full(m_ref.shape, NEG, F32)
        acc_ref[...] = jnp.zeros(acc_ref.shape, F32)

    lane = lax.broadcasted_iota(jnp.int32, (T_ATT, LANES), 1)
    lo = lane < 64
    nc = T_ATT // LANES

    def scores(hd):
        sl = slice(hd * LANES, (hd + 1) * LANES)
        return _dot_nt(q_ref[:, sl], k_ref[:, sl])

    def update(masked):
        if masked:
            r = lax.broadcasted_iota(jnp.int32, (T_ATT, LANES), 0)

        def softmax_part(hd, s):
            cols = [s[:, c * LANES:(c + 1) * LANES] for c in range(nc)]
            if masked:
                cols = [jnp.where(r >= lane + c * LANES, cols[c], NEG) for c in range(nc)]
            cmax = cols[0]
            for c in range(1, nc):
                cmax = jnp.maximum(cmax, cols[c])
            m_prev = m_ref[hd]
            m_new = jnp.maximum(m_prev, jnp.max(cmax, axis=-1, keepdims=True))
            m_ref[hd] = m_new
            alpha = jnp.exp2(m_prev - m_new)
            p = jnp.concatenate([jnp.exp2(cols[c] - m_new).astype(BF16) for c in range(nc)], axis=-1)
            return p, alpha

        def value_part(hd, p, alpha):
            acc_ref[hd] = acc_ref[hd] * alpha + _dot(p, v_ref[:, hd * LANES:(hd + 1) * LANES])

        s_q = [scores(hd) for hd in range(MLA_DEPTH)]
        pend = None
        for hd in range(MLA_HEADS):
            if hd + MLA_DEPTH < MLA_HEADS:
                s_q.append(scores(hd + MLA_DEPTH))
            cur = softmax_part(hd, s_q[hd])
            if pend is not None:
                value_part(hd - 1, *pend)
            pend = cur
        value_part(MLA_HEADS - 1, *pend)

    @pl.when(ki < qi)
    def _():
        update(False)

    @pl.when(ki == qi)
    def _():
        update(True)
        for j in range(MLA_HEADS // 2):
            ae = acc_ref[2 * j]
            ao = acc_ref[2 * j + 1]
            out = jnp.where(lo, ae * (1.0 / ae[:, 64:65]), ao * (1.0 / ao[:, 0:1]))
            o_ref[:, j * LANES:(j + 1) * LANES] = out.astype(BF16)


def _mla_layer(qm, km, vm):
    nb = SEQ // T_ATT
    qi = np.concatenate([np.full(n + 1, n, np.int32) for n in range(nb)])
    ki = np.concatenate([np.arange(n + 1, dtype=np.int32) for n in range(nb)])
    grid_spec = pltpu.PrefetchScalarGridSpec(
        num_scalar_prefetch=2,
        grid=(int(qi.shape[0]),),
        in_specs=[pl.BlockSpec((T_ATT, 1024), lambda s, qi, ki: (qi[s], 0)),
                  pl.BlockSpec((T_ATT, 1024), lambda s, qi, ki: (ki[s], 0)),
                  pl.BlockSpec((T_ATT, 1024), lambda s, qi, ki: (ki[s], 0))],
        out_specs=pl.BlockSpec((T_ATT, 512), lambda s, qi, ki: (qi[s], 0)),
        scratch_shapes=[pltpu.VMEM((MLA_HEADS, T_ATT, LANES), F32),
                        pltpu.VMEM((MLA_HEADS, T_ATT, LANES), F32)],
    )
    return pl.pallas_call(
        _mla_kernel,
        out_shape=jax.ShapeDtypeStruct((SEQ, 512), BF16),
        grid_spec=grid_spec,
        compiler_params=_cparams(1),
        name="mla_attn",
    )(jnp.asarray(qi), jnp.asarray(ki), qm, km, vm)


def _post_kernel(x_ref, os_ref, om_ref, mod_ref, n2_ref, wout_ref, wrh_ref, wrl_ref, br_ref,
                 xo_ref, h2_ref, meta_ref, cnt_ref, carry_ref):
    y = _dot(os_ref[...], wout_ref[0:512, :]) + _dot(om_ref[...], wout_ref[512:1024, :])
    _tail(x_ref[...], y, mod_ref[...], n2_ref[...], wrh_ref[...], wrl_ref[...], br_ref[...],
          xo_ref, h2_ref, meta_ref, cnt_ref, carry_ref)


def _post_layer(x, o_s, o_m, mod, n2, w_out, wr_hi, wr_lo, br):
    shapes, specs, scratch = _tail_out(T_TOK)
    return pl.pallas_call(
        _post_kernel,
        out_shape=shapes,
        grid=(SEQ // T_TOK,),
        in_specs=[pl.BlockSpec((T_TOK, D), lambda i: (i, 0)),
                  pl.BlockSpec((T_TOK, 512), lambda i: (i, 0)),
                  pl.BlockSpec((T_TOK, 512), lambda i: (i, 0)),
                  _full((8, D)), _full((1, D)), _full((D, D)),
                  _full((D, LANES)), _full((D, LANES)), _full((1, LANES))],
        out_specs=specs,
        scratch_shapes=scratch,
        compiler_params=_cparams(1),
        name="odd_post",
    )(x, o_s, o_m, mod, n2, w_out, wr_hi, wr_lo, br)


def _dispatch_plan(meta, cnt):
    grp = meta[:, EPG].astype(jnp.int32)
    rank = meta[:, EPG + 1].astype(jnp.int32)
    counts = cnt[0, :N_GROUPS].astype(jnp.int32)
    padded = ((counts + T_MOE - 1) // T_MOE) * T_MOE
    ends = jnp.cumsum(padded)
    pos = (ends - padded)[grp] + rank
    n_used = ends[-1] // T_MOE
    tile_start = jnp.arange(N_TILES, dtype=jnp.int32) * T_MOE
    tile_group = jnp.minimum(jnp.sum(tile_start[:, None] >= ends[None, :], axis=1), N_GROUPS - 1)
    return pos.reshape(SEQ // T_DISP, 1, T_DISP), tile_group.astype(jnp.int32), n_used.reshape(1)


def _row_copies(n, src_at, dst_at, sem):
    def body(r, carry):
        pltpu.make_async_copy(src_at(r), dst_at(r), sem).start()
        return carry
    lax.fori_loop(0, n, body, 0, unroll=8)


def _disp_kernel(pos_ref, x_ref, init_ref, o_ref, sem):
    del init_ref
    _row_copies(T_DISP,
                lambda r: x_ref.at[pl.ds(r, 1), :],
                lambda r: o_ref.at[pl.ds(pos_ref[0, 0, r], 1), :], sem)
    pltpu.make_async_copy(x_ref, o_ref.at[pl.ds(0, T_DISP), :], sem).wait()


def _dispatch(pos, h2a, hs):
    return pl.pallas_call(
        _disp_kernel,
        out_shape=jax.ShapeDtypeStruct((N_SORT, ROW_W), F32),
        grid=(SEQ // T_DISP,),
        in_specs=[pl.BlockSpec((1, 1, T_DISP), lambda i: (i, 0, 0), memory_space=pltpu.SMEM),
                  pl.BlockSpec((T_DISP, ROW_W), lambda i: (i, 0)),
                  pl.BlockSpec(memory_space=pl.ANY)],
        out_specs=pl.BlockSpec(memory_space=pl.ANY),
        scratch_shapes=[pltpu.SemaphoreType.DMA(())],
        input_output_aliases={2: 0},
        compiler_params=_cparams(1),
        name="moe_dispatch",
    )(pos, h2a, hs)


def _moe_kernel(tg_ref, nu_ref, x_ref, wg_ref, wu_ref, wd_ref, o_ref, xb_ref, act_ref):
    del tg_ref
    i = pl.program_id(0)
    j = pl.program_id(1)
    last = EPG // E_STEP - 1

    @pl.when(i < nu_ref[0])
    def _():
        @pl.when(j == 0)
        def _():
            xb_ref[...] = x_ref[:, 0:D].astype(BF16)

        xb = xb_ref[...]
        meta = x_ref[:, D:ROW_W]
        lane = lax.broadcasted_iota(jnp.int32, meta.shape, 1)
        for k in range(E_STEP):
            a = _dot(xb, wg_ref[0, k].astype(BF16))
            u = _dot(xb, wu_ref[0, k].astype(BF16))
            gate = jnp.sum(jnp.where(lane == j * E_STEP + k, meta, 0.0), axis=-1, keepdims=True)
            act = (a * jax.nn.sigmoid(a) * u * gate).astype(BF16)
            for jj in range(EPG // E_STEP):
                @pl.when(j == jj)
                def _(jj=jj, k=k, act=act):
                    act_ref[:, (jj * E_STEP + k) * FF:(jj * E_STEP + k + 1) * FF] = act

        @pl.when(j == last)
        def _():
            o_ref[...] = _dot(act_ref[...], wd_ref[0, 0].astype(BF16))

    @pl.when((i >= nu_ref[0]) & (j == last))
    def _():
        o_ref[...] = jnp.zeros(o_ref.shape, F32)


def _moe_experts(tile_group, n_used, hs, wg, wu, wd, layer):
    def tile(i, e, tg, nu):
        return (jnp.minimum(i, nu[0] - 1), 0)

    nj = EPG // E_STEP

    def expert(i, j, tg, nu):
        return (layer, jnp.where(i < nu[0], tg[i] * nj + j, tg[nu[0] - 1] * nj + nj - 1), 0, 0)

    def group(i, j, tg, nu):
        return (layer, tg[jnp.minimum(i, nu[0] - 1)], 0, 0)

    grid_spec = pltpu.PrefetchScalarGridSpec(
        num_scalar_prefetch=2,
        grid=(N_TILES, nj),
        in_specs=[pl.BlockSpec((T_MOE, ROW_W), tile),
                  pl.BlockSpec((1, E_STEP, D, FF), expert),
                  pl.BlockSpec((1, E_STEP, D, FF), expert),
                  pl.BlockSpec((1, 1, EPG * FF, D), group, pipeline_mode=pl.Buffered(1))],
        out_specs=pl.BlockSpec((T_MOE, D), lambda i, e, tg, nu: (i, 0)),
        scratch_shapes=[pltpu.VMEM((T_MOE, D), BF16), pltpu.VMEM((T_MOE, EPG * FF), BF16)],
    )
    return pl.pallas_call(
        _moe_kernel,
        out_shape=jax.ShapeDtypeStruct((N_SORT, D), F32),
        grid_spec=grid_spec,
        compiler_params=pltpu.CompilerParams(dimension_semantics=("arbitrary", "arbitrary"),
                                             vmem_limit_bytes=MOE_VMEM_LIMIT),
        name="moe_experts",
    )(tile_group, n_used, hs, wg, wu, wd.reshape(DEPTH, N_GROUPS, EPG * FF, D))


def _comb_kernel(pos_ref, x_ref, mod_ref, y_ref, o_ref, buf_ref, sem):
    _row_copies(T_DISP,
                lambda r: y_ref.at[pl.ds(pos_ref[0, 0, r], 1), :],
                lambda r: buf_ref.at[pl.ds(r, 1), :], sem)
    pltpu.make_async_copy(y_ref.at[pl.ds(0, T_DISP), :], buf_ref, sem).wait()
    o_ref[...] = x_ref[...] + mod_ref[5:6, :] * buf_ref[...]


def _combine(pos, x, mod, ys):
    return pl.pallas_call(
        _comb_kernel,
        out_shape=jax.ShapeDtypeStruct((SEQ, D), F32),
        grid=(SEQ // T_DISP,),
        in_specs=[pl.BlockSpec((1, 1, T_DISP), lambda i: (i, 0, 0), memory_space=pltpu.SMEM),
                  pl.BlockSpec((T_DISP, D), lambda i: (i, 0)),
                  pl.BlockSpec((8, D), lambda i: (0, 0)),
                  pl.BlockSpec(memory_space=pl.ANY)],
        out_specs=pl.BlockSpec((T_DISP, D), lambda i: (i, 0)),
        scratch_shapes=[pltpu.VMEM((T_DISP, D), F32), pltpu.SemaphoreType.DMA(())],
        compiler_params=_cparams(1),
        name="moe_combine",
    )(pos, x, mod, ys)


def _moe_layer(x, h2a, meta, cnt, mod, wg, wu, wd, layer, hs):
    pos, tile_group, n_used = _dispatch_plan(meta, cnt)
    hs = _dispatch(pos, h2a, hs)
    ys = _moe_experts(tile_group, n_used, hs, wg, wu, wd, layer)
    return _combine(pos, x, mod, ys), hs


def _pad_heads(w, heads, dim):
    k = w.shape[0]
    w = w.reshape(k, heads, dim)
    return jnp.pad(w, ((0, 0), (0, 0), (0, LANES - dim))).reshape(k, heads * LANES)


def _pad_gain(g):
    return jnp.pad(g, (0, LANES - g.shape[0])).reshape(1, LANES)


def _odd_weights(w_in):
    q_s = _pad_heads(w_in[:, 0:512], SWA_HEADS, HEAD_DIM)
    k_s = _pad_heads(w_in[:, 512:640], SWA_KV, HEAD_DIM)
    v = w_in[:, 640:768]
    v_s = jnp.concatenate([v[:, 0:64], v[:, 0:64], v[:, 64:128], v[:, 64:128]], axis=1)
    c_q = w_in[:, 768:1152]
    c_kv = w_in[:, 1152:1408]
    k_r = jnp.pad(w_in[:, 1408:1440], ((0, 0), (MLA_NOPE, LANES - MLA_QK)))
    k_r_sw = jnp.pad(_swap_halves(w_in[:, 1408:1440]), ((0, 0), (MLA_NOPE, LANES - MLA_QK)))
    return jnp.concatenate([q_s, k_s, v_s, c_q, c_kv, k_r, k_r_sw], axis=1).astype(BF16)


def _swap_halves(t):
    half = t.shape[-1] // 2
    return jnp.concatenate([t[..., half:], t[..., :half]], axis=-1)


def _uq_weights(w_uq):
    w = w_uq.reshape(Q_LORA, MLA_HEADS, MLA_QK)
    plain = jnp.pad(w, ((0, 0), (0, 0), (0, LANES - MLA_QK)))
    swapped = jnp.pad(_swap_halves(w[:, :, MLA_NOPE:]), ((0, 0), (0, 0), (MLA_NOPE, LANES - MLA_QK)))
    return jnp.concatenate([plain.reshape(Q_LORA, -1), swapped.reshape(Q_LORA, -1)], axis=1)


def _rope_gains(g):
    plain = jnp.pad(g, (0, LANES - MLA_QK)).reshape(1, LANES)
    partner = jnp.pad(_swap_halves(g[MLA_NOPE:]), (MLA_NOPE, LANES - MLA_QK)).reshape(1, LANES)
    return plain, partner


def _router_weights(w_group, b_group, w_expert, b_expert):
    w = jnp.pad(jnp.concatenate([w_expert, w_group], axis=1), ((0, 0), (0, LANES - N_EXPERTS - N_GROUPS)))
    hi = w.astype(BF16)
    lo = (w - hi.astype(F32)).astype(BF16)
    b = jnp.pad(jnp.concatenate([b_expert, b_group]), (0, LANES - N_EXPERTS - N_GROUPS)).reshape(1, LANES)
    return hi, lo, b


def kernel(x, c, positions, ada_w, ada_b, norm1_g, norm2_g, cp_w_in, conv_w, pool_w, pool_scale,
           cp_w_out, at_w_in, swa_q_g, swa_k_g, swa_sinks, mla_q_norm_g, mla_kv_norm_g, mla_w_uq,
           mla_w_ukv, mla_q_g, mla_k_g, at_w_out, moe_w_group, moe_b_group, moe_w_expert,
           moe_b_expert, moe_w_gate, moe_w_up, moe_w_down):
    xs = x.reshape(SEQ, D)
    mods = _ada_mod(c, ada_w, ada_b)
    tc, ts1, ts2 = _rope_tables(positions)
    hs = jnp.zeros((N_SORT, ROW_W), F32)
    for l in range(DEPTH):
        i = l // 2
        mod = mods[l]
        n1 = norm1_g[l].reshape(1, D)
        n2 = norm2_g[l].reshape(1, D)
        wr_hi, wr_lo, br = _router_weights(moe_w_group[l], moe_b_group[l], moe_w_expert[l], moe_b_expert[l])
        if l % 2 == 0:
            xs, h2a, meta, cnt = _even_layer(
                xs, mod, n1, n2, cp_w_in[i].astype(BF16), conv_w[i], pool_w[i].astype(BF16),
                pool_scale[i].reshape(1, 4 * POOL_G), cp_w_out[i].astype(BF16), wr_hi, wr_lo, br)
        else:
            ukv = mla_w_ukv[i].reshape(KV_LORA, MLA_HEADS, MLA_NOPE + MLA_V)
            wuk = _pad_heads(ukv[:, :, :MLA_NOPE].reshape(KV_LORA, MLA_HEADS * MLA_NOPE), MLA_HEADS, MLA_NOPE)
            wuv = ukv[:, :, MLA_NOPE:].reshape(KV_LORA, MLA_HEADS * MLA_V)
            wuq = _uq_weights(mla_w_uq[i])
            gmq, gmqs = _rope_gains(mla_q_g[i])
            gmk, gmks = _rope_gains(mla_k_g[i])
            qs, ks, vs, qm, km, vm = _proj_layer(
                xs, mod, n1, _odd_weights(at_w_in[i]), _pad_gain(swa_q_g[i]), _pad_gain(swa_k_g[i]),
                mla_q_norm_g[i].reshape(1, Q_LORA), mla_kv_norm_g[i].reshape(1, KV_LORA),
                wuq.astype(BF16), wuk.astype(BF16), wuv.astype(BF16),
                gmq, gmqs, gmk, gmks, tc, ts1, ts2)
            o_s = _swa_layer(swa_sinks[i], qs, ks, vs)
            o_m = _mla_layer(qm, km, vm)
            xs, h2a, meta, cnt = _post_layer(xs, o_s, o_m, mod, n2, at_w_out[i].astype(BF16), wr_hi, wr_lo, br)
        xs, hs = _moe_layer(xs, h2a, meta, cnt, mod, moe_w_gate, moe_w_up, moe_w_down, l, hs)
    return xs.reshape(1, SEQ, D)
```

```python
import functools

import numpy as np
import jax
import jax.numpy as jnp
from jax import lax
from jax.experimental import pallas as pl
from jax.experimental.pallas import tpu as pltpu

F32 = jnp.float32
BF16 = jnp.bfloat16

D = 1024
SEQ = 16384
DEPTH = 4
EPS = 1e-6
LANES = 128
CONV_CH = 512
POOL_WINDOWS = (2, 4, 8, 16)
POOL_G = 128
HALO = 16
SWA_HEADS = 8
SWA_KV = 2
HEAD_DIM = 64
WINDOW = 128
MLA_HEADS = 8
MLA_NOPE = 64
MLA_ROPE = 32
MLA_QK = MLA_NOPE + MLA_ROPE
MLA_V = 64
Q_LORA = 384
KV_LORA = 256
ROPE_THETA = 10000.0
N_GROUPS = 4
EPG = 8
N_EXPERTS = N_GROUPS * EPG
FF = 256
NEG = -1e30
LOG2E = 1.4426950408889634

T_TOK = 512
T_ATT = 512
T_MOE = 512
T_DISP = 512
ROW_W = D + LANES
N_SORT = SEQ + N_GROUPS * T_MOE
N_TILES = N_SORT // T_MOE
E_STEP = 8
MLA_SPLIT = 1
MLA_DEPTH = 3
VMEM_LIMIT = 48 * 1024 * 1024
MOE_VMEM_LIMIT = 56 * 1024 * 1024

O_QS, O_KS, O_VS, O_CQ, O_CKV, O_KR, ODD_W = 0, 1024, 1280, 1536, 1920, 2176, 2432


def _cparams(n_axes=1):
    return pltpu.CompilerParams(dimension_semantics=("arbitrary",) * n_axes,
                                vmem_limit_bytes=VMEM_LIMIT)


def _rms(x):
    return x * lax.rsqrt(jnp.mean(x * x, axis=-1, keepdims=True) + EPS)


def _dot(a, b):
    return jnp.dot(a, b, preferred_element_type=F32)


def _dot_nt(a, b):
    return lax.dot_general(a, b, (((1,), (1,)), ((), ())), preferred_element_type=F32)


def _ada_kernel(c_ref, w_ref, b_ref, o_ref):
    c = c_ref[...]
    ca = c * jax.nn.sigmoid(c)
    o_ref[0] = jnp.sum(w_ref[0] * ca, axis=0, keepdims=True) + b_ref[0]


def _ada_mod(c, ada_w, ada_b):
    c_col = c.reshape(D, 1)
    b = ada_b.reshape(DEPTH * 6, 1, D)
    out = pl.pallas_call(
        _ada_kernel,
        out_shape=jax.ShapeDtypeStruct((DEPTH * 6, 1, D), F32),
        grid=(DEPTH, 6),
        in_specs=[pl.BlockSpec((D, 1), lambda l, j: (0, 0)),
                  pl.BlockSpec((1, D, D), lambda l, j: (l, 0, j)),
                  pl.BlockSpec((1, 1, D), lambda l, j: (l * 6 + j, 0, 0))],
        out_specs=pl.BlockSpec((1, 1, D), lambda l, j: (l * 6 + j, 0, 0)),
        compiler_params=_cparams(2),
        name="ada_mod",
    )(c_col, ada_w, b)
    mod = out.reshape(DEPTH, 6, D)
    return jnp.pad(mod, ((0, 0), (0, 2), (0, 0)))


def _rope_kernel(pos_ref, inv_ref, c_ref, s1_ref, s2_ref):
    pos = pos_ref[...].astype(F32)
    ang = pos * inv_ref[...]
    lane = lax.broadcasted_iota(jnp.int32, ang.shape, 1)
    cs = jnp.cos(ang)
    sn = jnp.sin(ang)
    c_ref[...] = jnp.where(lane < 64, 1.0, jnp.where(lane < 96, cs, 0.0))
    s1_ref[...] = jnp.where((lane >= 64) & (lane < 80), -sn, 0.0)
    s2_ref[...] = jnp.where((lane >= 80) & (lane < 96), sn, 0.0)


def _rope_tables(positions):
    half = MLA_ROPE // 2
    inv = jnp.power(ROPE_THETA, -jnp.arange(half, dtype=F32) / half)
    inv_lane = jnp.concatenate([jnp.zeros((64,), F32), inv, inv, jnp.zeros((32,), F32)]).reshape(1, LANES)
    pos = positions.reshape(SEQ, 1)
    shp = jax.ShapeDtypeStruct((SEQ, LANES), F32)
    spec = pl.BlockSpec((T_TOK, LANES), lambda i: (i, 0))
    return pl.pallas_call(
        _rope_kernel,
        out_shape=(shp, shp, shp),
        grid=(SEQ // T_TOK,),
        in_specs=[pl.BlockSpec((T_TOK, 1), lambda i: (i, 0)),
                  pl.BlockSpec((1, LANES), lambda i: (0, 0))],
        out_specs=(spec, spec, spec),
        compiler_params=_cparams(1),
        name="rope_tables",
    )(pos, inv_lane)


def _route(lg):
    lane = lax.broadcasted_iota(jnp.int32, lg.shape, 1)
    lane_f = lane.astype(F32)
    is_g = (lane >= N_EXPERTS) & (lane < N_EXPERTS + N_GROUPS)
    gl = jnp.where(is_g, lg, NEG)
    gmax = jnp.max(gl, axis=-1, keepdims=True)
    gidx = jnp.min(jnp.where(is_g & (gl == gmax), lane_f - N_EXPERTS, 1e3), axis=-1, keepdims=True)
    gsum = jnp.sum(jnp.where(is_g, jnp.exp(gl - gmax), 0.0), axis=-1, keepdims=True)
    gw = 1.0 / gsum
    grp_of_lane = (lane >> 3).astype(F32)
    in_grp = (lane < N_EXPERTS) & (grp_of_lane == gidx)
    el = jnp.where(in_grp, lg, NEG)
    m1 = jnp.max(el, axis=-1, keepdims=True)
    i1 = jnp.min(jnp.where(in_grp & (el == m1), lane_f, 1e3), axis=-1, keepdims=True)
    rest = in_grp & (lane_f != i1)
    el2 = jnp.where(rest, lg, NEG)
    m2 = jnp.max(el2, axis=-1, keepdims=True)
    i2 = jnp.min(jnp.where(rest & (el2 == m2), lane_f, 1e3), axis=-1, keepdims=True)
    r = jnp.exp(m2 - m1)
    w1 = gw / (1.0 + r)
    w2 = w1 * r
    return jnp.where(lane_f == i1, w1, jnp.where(lane_f == i2, w2, 0.0)), gidx


def _tail(x, y, mod, n2, wr_hi, wr_lo, br, xo_ref, h2_ref, meta_ref, cnt_ref, carry_ref):
    i = pl.program_id(0)

    @pl.when(i == 0)
    def _():
        carry_ref[...] = jnp.zeros(carry_ref.shape, F32)

    gate1 = mod[2:3]
    shift2, scale2 = mod[3:4], mod[4:5]
    xn = x + gate1 * y
    xo_ref[...] = xn
    h2 = _rms(xn) * n2 * (1.0 + scale2) + shift2
    hi = h2.astype(BF16)
    lo = (h2 - hi.astype(F32)).astype(BF16)
    lg = _dot(hi, wr_hi) + _dot(lo, wr_hi) + _dot(hi, wr_lo) + br
    gates, gidx = _route(lg)
    rows = lg.shape[0]
    lane = lax.broadcasted_iota(jnp.int32, (rows, LANES), 1)
    lane_f = lane.astype(F32)
    g8 = gates
    for g in range(1, N_GROUPS):
        g8 = g8 + pltpu.roll(gates, LANES - EPG * g, 1)
    onehot = (lane_f == gidx).astype(F32)
    r = lax.broadcasted_iota(jnp.int32, (rows, rows), 0)
    c = lax.broadcasted_iota(jnp.int32, (rows, rows), 1)
    before = jnp.where(c < r, 1.0, 0.0).astype(BF16)
    cum = _dot(before, onehot.astype(BF16)) + carry_ref[0:1, :]
    rank = jnp.sum(jnp.where(lane_f == gidx, cum, 0.0), axis=-1, keepdims=True)
    meta = jnp.where(lane < EPG, g8, jnp.where(lane == EPG, gidx, jnp.where(lane == EPG + 1, rank, 0.0)))
    h2_ref[:, 0:D] = h2
    h2_ref[:, D:D + LANES] = meta
    meta_ref[...] = meta
    total = carry_ref[0:1, :] + jnp.sum(onehot, axis=0, keepdims=True)
    carry_ref[...] = jnp.broadcast_to(total, carry_ref.shape)
    cnt_ref[...] = jnp.broadcast_to(total, cnt_ref.shape)


def _even_kernel(x_ref, xh_ref, mod_ref, n1_ref, n2_ref, win_ref, cw_ref, pw_ref, ps_ref,
                 wout_ref, wrh_ref, wrl_ref, br_ref, xo_ref, h2_ref, meta_ref, cnt_ref, carry_ref):
    i = pl.program_id(0)
    x = x_ref[...]
    mod = mod_ref[...]
    shift1, scale1 = mod[0:1], mod[1:2]
    xa = jnp.concatenate([xh_ref[...], x], axis=0)
    h = _rms(xa) * n1_ref[...] * (1.0 + scale1) + shift1
    z = _dot(h.astype(BF16), win_ref[...])
    rows = T_TOK + HALO
    row = lax.broadcasted_iota(jnp.int32, (rows, 1), 0)
    tpos = i * T_TOK + row - HALO
    live = (tpos >= 0).astype(F32)
    bg = z[:, 0:CONV_CH]
    v = z[:, CONV_CH:2 * CONV_CH] * z[:, 2 * CONV_CH:3 * CONV_CH] * live
    cw = cw_ref[...]
    conv = v * cw[0:1] + pltpu.roll(v, 1, 0) * cw[1:2] + pltpu.roll(v, 2, 0) * cw[2:3]
    parts = [(bg * conv)[HALO:]]
    ps = ps_ref[...]
    tcount = (tpos + 1).astype(F32)
    for gi, w in enumerate(POOL_WINDOWS):
        ug = z[:, 3 * CONV_CH + gi * POOL_G: 3 * CONV_CH + (gi + 1) * POOL_G] * live
        s = ug
        k = 1
        while k < w:
            s = s + pltpu.roll(s, k, 0)
            k *= 2
        inv = 1.0 / jnp.minimum(tcount, float(w))
        d = (s * inv - ug)[HALO:].astype(BF16)
        parts.append(_dot(d, pw_ref[gi]) * ps[:, gi * POOL_G:(gi + 1) * POOL_G])
    cat = jnp.concatenate(parts, axis=-1).astype(BF16)
    y = _dot(cat, wout_ref[...])
    _tail(x, y, mod, n2_ref[...], wrh_ref[...], wrl_ref[...], br_ref[...], xo_ref, h2_ref, meta_ref,
          cnt_ref, carry_ref)


def _tail_out(n):
    shapes = (jax.ShapeDtypeStruct((SEQ, D), F32), jax.ShapeDtypeStruct((SEQ, ROW_W), F32),
              jax.ShapeDtypeStruct((SEQ, LANES), F32), jax.ShapeDtypeStruct((8, LANES), F32))
    specs = (pl.BlockSpec((n, D), lambda i: (i, 0)), pl.BlockSpec((n, ROW_W), lambda i: (i, 0)),
             pl.BlockSpec((n, LANES), lambda i: (i, 0)), pl.BlockSpec((8, LANES), lambda i: (0, 0)))
    scratch = [pltpu.VMEM((8, LANES), F32)]
    return shapes, specs, scratch


def _full(shape):
    nd = len(shape)
    return pl.BlockSpec(shape, lambda i: (0,) * nd)


def _even_layer(x, mod, n1, n2, w_in, conv_w, pool_w, pool_scale, w_out, wr_hi, wr_lo, br):
    shapes, specs, scratch = _tail_out(T_TOK)
    hb = T_TOK // HALO
    return pl.pallas_call(
        _even_kernel,
        out_shape=shapes,
        grid=(SEQ // T_TOK,),
        in_specs=[pl.BlockSpec((T_TOK, D), lambda i: (i, 0)),
                  pl.BlockSpec((HALO, D), lambda i: (jnp.maximum(i * hb - 1, 0), 0)),
                  _full((8, D)), _full((1, D)), _full((1, D)),
                  _full((D, 4 * CONV_CH)), _full((3, CONV_CH)), _full((4, POOL_G, POOL_G)),
                  _full((1, 4 * POOL_G)), _full((D, D)),
                  _full((D, LANES)), _full((D, LANES)), _full((1, LANES))],
        out_specs=specs,
        scratch_shapes=scratch,
        compiler_params=_cparams(1),
        name="even_mixer",
    )(x, x, mod, n1, n2, w_in, conv_w, pool_w, pool_scale, w_out, wr_hi, wr_lo, br)


def _proj_kernel(x_ref, mod_ref, n1_ref, win_ref, gsq_ref, gsk_ref, gqn_ref, gkvn_ref,
                 wuq_ref, wuk_ref, wuv_ref, gmq_ref, gmqs_ref, gmk_ref, gmks_ref, c_ref, s1_ref, s2_ref,
                 qs_ref, ks_ref, vs_ref, qm_ref, km_ref, vm_ref):
    x = x_ref[...]
    mod = mod_ref[...]
    shift1, scale1 = mod[0:1], mod[1:2]
    h = _rms(x) * n1_ref[...] * (1.0 + scale1) + shift1
    z = _dot(h.astype(BF16), win_ref[...])

    def head_norm(t, g, dim):
        ms = jnp.sum(t * t, axis=-1, keepdims=True) * (1.0 / dim)
        return t * lax.rsqrt(ms + EPS) * g

    gsq, gsk = gsq_ref[...] * (HEAD_DIM ** -0.5), gsk_ref[...]
    for hd in range(SWA_HEADS):
        qh = head_norm(z[:, O_QS + hd * LANES: O_QS + (hd + 1) * LANES], gsq, HEAD_DIM)
        qs_ref[:, hd * LANES:(hd + 1) * LANES] = qh.astype(BF16)
    for kv in range(SWA_KV):
        kh = head_norm(z[:, O_KS + kv * LANES: O_KS + (kv + 1) * LANES], gsk, HEAD_DIM)
        ks_ref[:, kv * LANES:(kv + 1) * LANES] = kh.astype(BF16)
    vs_ref[...] = z[:, O_VS:O_CQ].astype(BF16)

    cq = (_rms(z[:, O_CQ:O_CKV]) * gqn_ref[...]).astype(BF16)
    ckv = (_rms(z[:, O_CKV:O_KR]) * gkvn_ref[...]).astype(BF16)
    qm = _dot(cq, wuq_ref[...])
    kn = _dot(ckv, wuk_ref[...])
    vm = _dot(ckv, wuv_ref[...])
    lane = lax.broadcasted_iota(jnp.int32, (T_TOK, LANES), 1)
    for j in range(MLA_HEADS // 2):
        vv = vm[:, j * LANES:(j + 1) * LANES]
        even = jnp.where(lane < 64, vv, jnp.where(lane == 64, 1.0, 0.0))
        odd = jnp.where(lane < 64, jnp.where(lane == 0, 1.0, 0.0), vv)
        vm_ref[:, (2 * j) * LANES:(2 * j + 1) * LANES] = even.astype(BF16)
        vm_ref[:, (2 * j + 1) * LANES:(2 * j + 2) * LANES] = odd.astype(BF16)
    kr = z[:, O_KR:O_KR + LANES]
    krs = z[:, O_KR + LANES:ODD_W]
    cs = c_ref[...]
    sp = s1_ref[...] + s2_ref[...]
    qscale = MLA_QK ** -0.5 * LOG2E
    gcq = cs * (gmq_ref[...] * qscale)
    gsq_r = sp * (gmqs_ref[...] * qscale)
    gck = cs * gmk_ref[...]
    krot = krs * (sp * gmks_ref[...])

    def inv_rms(t):
        return lax.rsqrt(jnp.sum(t * t, axis=-1, keepdims=True) * (1.0 / MLA_QK) + EPS)

    for hd in range(MLA_HEADS):
        sl = slice(hd * LANES, (hd + 1) * LANES)
        sw = slice((MLA_HEADS + hd) * LANES, (MLA_HEADS + hd + 1) * LANES)
        qh = qm[:, sl]
        qm_ref[:, sl] = ((qh * gcq + qm[:, sw] * gsq_r) * inv_rms(qh)).astype(BF16)
        kh = kn[:, sl] + kr
        km_ref[:, sl] = ((kh * gck + krot) * inv_rms(kh)).astype(BF16)


def _proj_layer(x, mod, n1, w_in, gsq, gsk, gqn, gkvn, wuq, wuk, wuv, gmq, gmqs, gmk, gmks, tc, ts1, ts2):
    def tok(wd):
        return pl.BlockSpec((T_TOK, wd), lambda i: (i, 0))
    widths = (1024, 256, 256, 1024, 1024, 1024)
    return pl.pallas_call(
        _proj_kernel,
        out_shape=tuple(jax.ShapeDtypeStruct((SEQ, wd), BF16) for wd in widths),
        grid=(SEQ // T_TOK,),
        in_specs=[tok(D), _full((8, D)), _full((1, D)), _full((D, ODD_W)),
                  _full((1, LANES)), _full((1, LANES)), _full((1, Q_LORA)), _full((1, KV_LORA)),
                  _full((Q_LORA, 2048)), _full((KV_LORA, 1024)), _full((KV_LORA, 512)),
                  _full((1, LANES)), _full((1, LANES)), _full((1, LANES)), _full((1, LANES)),
                  tok(LANES), tok(LANES), tok(LANES)],
        out_specs=tuple(tok(wd) for wd in widths),
        compiler_params=_cparams(1),
        name="odd_proj",
    )(x, mod, n1, w_in, gsq, gsk, gqn, gkvn, wuq, wuk, wuv, gmq, gmqs, gmk, gmks, tc, ts1, ts2)


def _swa_kernel(sink_ref, q_ref, k_ref, kh_ref, v_ref, vh_ref, o_ref):
    i = pl.program_id(0)
    kcat = jnp.concatenate([kh_ref[...], k_ref[...]], axis=0)
    vcat = jnp.concatenate([vh_ref[...], v_ref[...]], axis=0)
    grp = SWA_HEADS // SWA_KV
    r = lax.broadcasted_iota(jnp.int32, (grp * WINDOW, 2 * WINDOW), 0) & (WINDOW - 1)
    c = lax.broadcasted_iota(jnp.int32, (grp * WINDOW, 2 * WINDOW), 1)
    rel = WINDOW + r - c
    lane = lax.broadcasted_iota(jnp.int32, (WINDOW, LANES), 1)
    for sb in range(T_ATT // WINDOW):
        rows = slice(sb * WINDOW, (sb + 1) * WINDOW)
        kb = kcat[sb * WINDOW: sb * WINDOW + 2 * WINDOW]
        vb = vcat[sb * WINDOW: sb * WINDOW + 2 * WINDOW]
        kpos = i * T_ATT + (sb - 1) * WINDOW + c
        ok = (rel >= 0) & (rel < WINDOW) & (kpos >= 0)
        outs = []
        for kv in range(SWA_KV):
            q = jnp.concatenate([q_ref[rows, (kv * grp + g) * LANES:(kv * grp + g + 1) * LANES]
                                 for g in range(grp)], axis=0)
            sink = jnp.concatenate([jnp.full((WINDOW, 1), sink_ref[kv * grp + g], F32)
                                    for g in range(grp)], axis=0)
            s = jnp.where(ok, _dot_nt(q, kb[:, kv * LANES:(kv + 1) * LANES]), NEG)
            m = jnp.maximum(jnp.max(s, axis=-1, keepdims=True), sink)
            e = jnp.exp(s - m)
            den = jnp.sum(e, axis=-1, keepdims=True) + jnp.exp(sink - m)
            p = (e * (1.0 / den)).astype(BF16)
            o = _dot(p, vb[:, kv * LANES:(kv + 1) * LANES])
            outs += [o[g * WINDOW:(g + 1) * WINDOW] for g in range(grp)]
        for j in range(SWA_HEADS // 2):
            o_ref[rows, j * LANES:(j + 1) * LANES] = jnp.where(lane < 64, outs[2 * j], outs[2 * j + 1]).astype(BF16)


def _swa_layer(sinks, qs, ks, vs):
    hb = T_ATT // WINDOW
    return pl.pallas_call(
        _swa_kernel,
        out_shape=jax.ShapeDtypeStruct((SEQ, 512), BF16),
        grid=(SEQ // T_ATT,),
        in_specs=[pl.BlockSpec(memory_space=pltpu.SMEM),
                  pl.BlockSpec((T_ATT, 1024), lambda i: (i, 0)),
                  pl.BlockSpec((T_ATT, 256), lambda i: (i, 0)),
                  pl.BlockSpec((WINDOW, 256), lambda i: (jnp.maximum(i * hb - 1, 0), 0)),
                  pl.BlockSpec((T_ATT, 256), lambda i: (i, 0)),
                  pl.BlockSpec((WINDOW, 256), lambda i: (jnp.maximum(i * hb - 1, 0), 0))],
        out_specs=pl.BlockSpec((T_ATT, 512), lambda i: (i, 0)),
        compiler_params=_cparams(1),
        name="swa_attn",
    )(sinks, qs, ks, ks, vs, vs)


def _mla_kernel(qi_ref, ki_ref, q_ref, k_ref, v_ref, o_ref, m_ref, acc_ref):
    step = pl.program_id(0)
    qi = qi_ref[step]
    ki = ki_ref[step]

    @pl.when(ki == 0)
    def _():
        m_ref[...] = jnp.full(m_ref.shape, NEG, F32)
        acc_ref[...] = jnp.zeros(acc_ref.shape, F32)

    lane = lax.broadcasted_iota(jnp.int32, (T_ATT, LANES), 1)
    lo = lane < 64
    nc = T_ATT // LANES
    sub = T_ATT // MLA_SPLIT
    units = [(hd, part) for hd in range(MLA_HEADS) for part in range(MLA_SPLIT)]

    def scores(unit):
        hd, part = unit
        sl = slice(hd * LANES, (hd + 1) * LANES)
        return _dot_nt(q_ref[part * sub:(part + 1) * sub, sl], k_ref[:, sl])

    def update(masked):
        if masked:
            r = lax.broadcasted_iota(jnp.int32, (sub, LANES), 0)
            lane_s = lax.broadcasted_iota(jnp.int32, (sub, LANES), 1)

        def softmax_part(unit, s):
            hd, part = unit
            rows = slice(part * sub, (part + 1) * sub)
            cols = [s[:, c * LANES:(c + 1) * LANES] for c in range(nc)]
            if masked:
                cols = [jnp.where(r + part * sub >= lane_s + c * LANES, cols[c], NEG) for c in range(nc)]
            cmax = cols[0]
            for c in range(1, nc):
                cmax = jnp.maximum(cmax, cols[c])
            m_prev = m_ref[hd, rows]
            m_new = jnp.maximum(m_prev, jnp.max(cmax, axis=-1, keepdims=True))
            m_ref[hd, rows] = m_new
            alpha = jnp.exp2(m_prev - m_new)
            p = jnp.concatenate([jnp.exp2(cols[c] - m_new).astype(BF16) for c in range(nc)], axis=-1)
            return p, alpha

        def value_part(unit, p, alpha):
            hd, part = unit
            rows = slice(part * sub, (part + 1) * sub)
            acc_ref[hd, rows] = acc_ref[hd, rows] * alpha + _dot(p, v_ref[:, hd * LANES:(hd + 1) * LANES])

        s_q = [scores(u) for u in units[:MLA_DEPTH]]
        pend = None
        for n, u in enumerate(units):
            if n + MLA_DEPTH < len(units):
                s_q.append(scores(units[n + MLA_DEPTH]))
            cur = softmax_part(u, s_q[n])
            if pend is not None:
                value_part(units[n - 1], *pend)
            pend = cur
        value_part(units[-1], *pend)

    @pl.when(ki < qi)
    def _():
        update(False)

    @pl.when(ki == qi)
    def _():
        update(True)
        for j in range(MLA_HEADS // 2):
            ae = acc_ref[2 * j]
            ao = acc_ref[2 * j + 1]
            out = jnp.where(lo, ae * (1.0 / ae[:, 64:65]), ao * (1.0 / ao[:, 0:1]))
            o_ref[:, j * LANES:(j + 1) * LANES] = out.astype(BF16)


def _mla_layer(qm, km, vm):
    nb = SEQ // T_ATT
    qi = np.concatenate([np.full(n + 1, n, np.int32) for n in range(nb)])
    ki = np.concatenate([np.arange(n + 1, dtype=np.int32) for n in range(nb)])
    grid_spec = pltpu.PrefetchScalarGridSpec(
        num_scalar_prefetch=2,
        grid=(int(qi.shape[0]),),
        in_specs=[pl.BlockSpec((T_ATT, 1024), lambda s, qi, ki: (qi[s], 0)),
                  pl.BlockSpec((T_ATT, 1024), lambda s, qi, ki: (ki[s], 0)),
                  pl.BlockSpec((T_ATT, 1024), lambda s, qi, ki: (ki[s], 0))],
        out_specs=pl.BlockSpec((T_ATT, 512), lambda s, qi, ki: (qi[s], 0)),
        scratch_shapes=[pltpu.VMEM((MLA_HEADS, T_ATT, LANES), F32),
                        pltpu.VMEM((MLA_HEADS, T_ATT, LANES), F32)],
    )
    return pl.pallas_call(
        _mla_kernel,
        out_shape=jax.ShapeDtypeStruct((SEQ, 512), BF16),
        grid_spec=grid_spec,
        compiler_params=_cparams(1),
        name="mla_attn",
    )(jnp.asarray(qi), jnp.asarray(ki), qm, km, vm)


def _post_kernel(x_ref, os_ref, om_ref, mod_ref, n2_ref, wout_ref, wrh_ref, wrl_ref, br_ref,
                 xo_ref, h2_ref, meta_ref, cnt_ref, carry_ref):
    y = _dot(os_ref[...], wout_ref[0:512, :]) + _dot(om_ref[...], wout_ref[512:1024, :])
    _tail(x_ref[...], y, mod_ref[...], n2_ref[...], wrh_ref[...], wrl_ref[...], br_ref[...],
          xo_ref, h2_ref, meta_ref, cnt_ref, carry_ref)


def _post_layer(x, o_s, o_m, mod, n2, w_out, wr_hi, wr_lo, br):
    shapes, specs, scratch = _tail_out(T_TOK)
    return pl.pallas_call(
        _post_kernel,
        out_shape=shapes,
        grid=(SEQ // T_TOK,),
        in_specs=[pl.BlockSpec((T_TOK, D), lambda i: (i, 0)),
                  pl.BlockSpec((T_TOK, 512), lambda i: (i, 0)),
                  pl.BlockSpec((T_TOK, 512), lambda i: (i, 0)),
                  _full((8, D)), _full((1, D)), _full((D, D)),
                  _full((D, LANES)), _full((D, LANES)), _full((1, LANES))],
        out_specs=specs,
        scratch_shapes=scratch,
        compiler_params=_cparams(1),
        name="odd_post",
    )(x, o_s, o_m, mod, n2, w_out, wr_hi, wr_lo, br)


def _dispatch_plan(meta, cnt):
    grp = meta[:, EPG].astype(jnp.int32)
    rank = meta[:, EPG + 1].astype(jnp.int32)
    counts = cnt[0, :N_GROUPS].astype(jnp.int32)
    padded = ((counts + T_MOE - 1) // T_MOE) * T_MOE
    ends = jnp.cumsum(padded)
    pos = (ends - padded)[grp] + rank
    n_used = ends[-1] // T_MOE
    tile_start = jnp.arange(N_TILES, dtype=jnp.int32) * T_MOE
    tile_group = jnp.minimum(jnp.sum(tile_start[:, None] >= ends[None, :], axis=1), N_GROUPS - 1)
    return pos.reshape(SEQ // T_DISP, 1, T_DISP), tile_group.astype(jnp.int32), n_used.reshape(1)


def _row_copies(n, src_at, dst_at, sem):
    def body(r, carry):
        pltpu.make_async_copy(src_at(r), dst_at(r), sem).start()
        return carry
    lax.fori_loop(0, n, body, 0, unroll=8)


def _disp_kernel(pos_ref, x_ref, init_ref, o_ref, sem):
    del init_ref
    _row_copies(T_DISP,
                lambda r: x_ref.at[pl.ds(r, 1), :],
                lambda r: o_ref.at[pl.ds(pos_ref[0, 0, r], 1), :], sem)
    pltpu.make_async_copy(x_ref, o_ref.at[pl.ds(0, T_DISP), :], sem).wait()


def _dispatch(pos, h2a, hs):
    return pl.pallas_call(
        _disp_kernel,
        out_shape=jax.ShapeDtypeStruct((N_SORT, ROW_W), F32),
        grid=(SEQ // T_DISP,),
        in_specs=[pl.BlockSpec((1, 1, T_DISP), lambda i: (i, 0, 0), memory_space=pltpu.SMEM),
                  pl.BlockSpec((T_DISP, ROW_W), lambda i: (i, 0)),
                  pl.BlockSpec(memory_space=pl.ANY)],
        out_specs=pl.BlockSpec(memory_space=pl.ANY),
        scratch_shapes=[pltpu.SemaphoreType.DMA(())],
        input_output_aliases={2: 0},
        compiler_params=_cparams(1),
        name="moe_dispatch",
    )(pos, h2a, hs)


def _moe_kernel(tg_ref, nu_ref, x_ref, wg_ref, wu_ref, wd_ref, o_ref, xb_ref, act_ref):
    del tg_ref
    i = pl.program_id(0)
    j = pl.program_id(1)
    last = EPG // E_STEP - 1

    @pl.when(i < nu_ref[0])
    def _():
        @pl.when(j == 0)
        def _():
            xb_ref[...] = x_ref[:, 0:D].astype(BF16)

        xb = xb_ref[...]
        meta = x_ref[:, D:ROW_W]
        lane = lax.broadcasted_iota(jnp.int32, meta.shape, 1)
        for k in range(E_STEP):
            a = _dot(xb, wg_ref[0, k].astype(BF16))
            u = _dot(xb, wu_ref[0, k].astype(BF16))
            gate = jnp.sum(jnp.where(lane == j * E_STEP + k, meta, 0.0), axis=-1, keepdims=True)
            act = (a * jax.nn.sigmoid(a) * u * gate).astype(BF16)
            for jj in range(EPG // E_STEP):
                @pl.when(j == jj)
                def _(jj=jj, k=k, act=act):
                    act_ref[:, (jj * E_STEP + k) * FF:(jj * E_STEP + k + 1) * FF] = act

        @pl.when(j == last)
        def _():
            o_ref[...] = _dot(act_ref[...], wd_ref[0, 0].astype(BF16))

    @pl.when((i >= nu_ref[0]) & (j == last))
    def _():
        o_ref[...] = jnp.zeros(o_ref.shape, F32)


def _moe_experts(tile_group, n_used, hs, wg, wu, wd, layer):
    def tile(i, e, tg, nu):
        return (jnp.minimum(i, nu[0] - 1), 0)

    nj = EPG // E_STEP

    def expert(i, j, tg, nu):
        return (layer, jnp.where(i < nu[0], tg[i] * nj + j, tg[nu[0] - 1] * nj + nj - 1), 0, 0)

    def group(i, j, tg, nu):
        return (layer, tg[jnp.minimum(i, nu[0] - 1)], 0, 0)

    grid_spec = pltpu.PrefetchScalarGridSpec(
        num_scalar_prefetch=2,
        grid=(N_TILES, nj),
        in_specs=[pl.BlockSpec((T_MOE, ROW_W), tile),
                  pl.BlockSpec((1, E_STEP, D, FF), expert),
                  pl.BlockSpec((1, E_STEP, D, FF), expert),
                  pl.BlockSpec((1, 1, EPG * FF, D), group, pipeline_mode=pl.Buffered(1))],
        out_specs=pl.BlockSpec((T_MOE, D), lambda i, e, tg, nu: (i, 0)),
        scratch_shapes=[pltpu.VMEM((T_MOE, D), BF16), pltpu.VMEM((T_MOE, EPG * FF), BF16)],
    )
    return pl.pallas_call(
        _moe_kernel,
        out_shape=jax.ShapeDtypeStruct((N_SORT, D), F32),
        grid_spec=grid_spec,
        compiler_params=pltpu.CompilerParams(dimension_semantics=("arbitrary", "arbitrary"),
                                             vmem_limit_bytes=MOE_VMEM_LIMIT),
        name="moe_experts",
    )(tile_group, n_used, hs, wg, wu, wd.reshape(DEPTH, N_GROUPS, EPG * FF, D))


def _comb_kernel(pos_ref, x_ref, mod_ref, y_ref, o_ref, buf_ref, sem):
    _row_copies(T_DISP,
                lambda r: y_ref.at[pl.ds(pos_ref[0, 0, r], 1), :],
                lambda r: buf_ref.at[pl.ds(r, 1), :], sem)
    pltpu.make_async_copy(y_ref.at[pl.ds(0, T_DISP), :], buf_ref, sem).wait()
    o_ref[...] = x_ref[...] + mod_ref[5:6, :] * buf_ref[...]


def _combine(pos, x, mod, ys):
    return pl.pallas_call(
        _comb_kernel,
        out_shape=jax.ShapeDtypeStruct((SEQ, D), F32),
        grid=(SEQ // T_DISP,),
        in_specs=[pl.BlockSpec((1, 1, T_DISP), lambda i: (i, 0, 0), memory_space=pltpu.SMEM),
                  pl.BlockSpec((T_DISP, D), lambda i: (i, 0)),
                  pl.BlockSpec((8, D), lambda i: (0, 0)),
                  pl.BlockSpec(memory_space=pl.ANY)],
        out_specs=pl.BlockSpec((T_DISP, D), lambda i: (i, 0)),
        scratch_shapes=[pltpu.VMEM((T_DISP, D), F32), pltpu.SemaphoreType.DMA(())],
        compiler_params=_cparams(1),
        name="moe_combine",
    )(pos, x, mod, ys)


def _moe_layer(x, h2a, meta, cnt, mod, wg, wu, wd, layer, hs):
    pos, tile_group, n_used = _dispatch_plan(meta, cnt)
    hs = _dispatch(pos, h2a, hs)
    ys = _moe_experts(tile_group, n_used, hs, wg, wu, wd, layer)
    return _combine(pos, x, mod, ys), hs


def _pad_heads(w, heads, dim):
    k = w.shape[0]
    w = w.reshape(k, heads, dim)
    return jnp.pad(w, ((0, 0), (0, 0), (0, LANES - dim))).reshape(k, heads * LANES)


def _pad_gain(g):
    return jnp.pad(g, (0, LANES - g.shape[0])).reshape(1, LANES)


def _odd_weights(w_in):
    q_s = _pad_heads(w_in[:, 0:512], SWA_HEADS, HEAD_DIM)
    k_s = _pad_heads(w_in[:, 512:640], SWA_KV, HEAD_DIM)
    v = w_in[:, 640:768]
    v_s = jnp.concatenate([v[:, 0:64], v[:, 0:64], v[:, 64:128], v[:, 64:128]], axis=1)
    c_q = w_in[:, 768:1152]
    c_kv = w_in[:, 1152:1408]
    k_r = jnp.pad(w_in[:, 1408:1440], ((0, 0), (MLA_NOPE, LANES - MLA_QK)))
    k_r_sw = jnp.pad(_swap_halves(w_in[:, 1408:1440]), ((0, 0), (MLA_NOPE, LANES - MLA_QK)))
    return jnp.concatenate([q_s, k_s, v_s, c_q, c_kv, k_r, k_r_sw], axis=1).astype(BF16)


def _swap_halves(t):
    half = t.shape[-1] // 2
    return jnp.concatenate([t[..., half:], t[..., :half]], axis=-1)


def _uq_weights(w_uq):
    w = w_uq.reshape(Q_LORA, MLA_HEADS, MLA_QK)
    plain = jnp.pad(w, ((0, 0), (0, 0), (0, LANES - MLA_QK)))
    swapped = jnp.pad(_swap_halves(w[:, :, MLA_NOPE:]), ((0, 0), (0, 0), (MLA_NOPE, LANES - MLA_QK)))
    return jnp.concatenate([plain.reshape(Q_LORA, -1), swapped.reshape(Q_LORA, -1)], axis=1)


def _rope_gains(g):
    plain = jnp.pad(g, (0, LANES - MLA_QK)).reshape(1, LANES)
    partner = jnp.pad(_swap_halves(g[MLA_NOPE:]), (MLA_NOPE, LANES - MLA_QK)).reshape(1, LANES)
    return plain, partner


def _router_weights(w_group, b_group, w_expert, b_expert):
    w = jnp.pad(jnp.concatenate([w_expert, w_group], axis=1), ((0, 0), (0, LANES - N_EXPERTS - N_GROUPS)))
    hi = w.astype(BF16)
    lo = (w - hi.astype(F32)).astype(BF16)
    b = jnp.pad(jnp.concatenate([b_expert, b_group]), (0, LANES - N_EXPERTS - N_GROUPS)).reshape(1, LANES)
    return hi, lo, b


def kernel(x, c, positions, ada_w, ada_b, norm1_g, norm2_g, cp_w_in, conv_w, pool_w, pool_scale,
           cp_w_out, at_w_in, swa_q_g, swa_k_g, swa_sinks, mla_q_norm_g, mla_kv_norm_g, mla_w_uq,
           mla_w_ukv, mla_q_g, mla_k_g, at_w_out, moe_w_group, moe_b_group, moe_w_expert,
           moe_b_expert, moe_w_gate, moe_w_up, moe_w_down):
    xs = x.reshape(SEQ, D)
    mods = _ada_mod(c, ada_w, ada_b)
    tc, ts1, ts2 = _rope_tables(positions)
    hs = jnp.zeros((N_SORT, ROW_W), F32)
    for l in range(DEPTH):
        i = l // 2
        mod = mods[l]
        n1 = norm1_g[l].reshape(1, D)
        n2 = norm2_g[l].reshape(1, D)
        wr_hi, wr_lo, br = _router_weights(moe_w_group[l], moe_b_group[l], moe_w_expert[l], moe_b_expert[l])
        if l % 2 == 0:
            xs, h2a, meta, cnt = _even_layer(
                xs, mod, n1, n2, cp_w_in[i].astype(BF16), conv_w[i], pool_w[i].astype(BF16),
                pool_scale[i].reshape(1, 4 * POOL_G), cp_w_out[i].astype(BF16), wr_hi, wr_lo, br)
        else:
            ukv = mla_w_ukv[i].reshape(KV_LORA, MLA_HEADS, MLA_NOPE + MLA_V)
            wuk = _pad_heads(ukv[:, :, :MLA_NOPE].reshape(KV_LORA, MLA_HEADS * MLA_NOPE), MLA_HEADS, MLA_NOPE)
            wuv = ukv[:, :, MLA_NOPE:].reshape(KV_LORA, MLA_HEADS * MLA_V)
            wuq = _uq_weights(mla_w_uq[i])
            gmq, gmqs = _rope_gains(mla_q_g[i])
            gmk, gmks = _rope_gains(mla_k_g[i])
            qs, ks, vs, qm, km, vm = _proj_layer(
                xs, mod, n1, _odd_weights(at_w_in[i]), _pad_gain(swa_q_g[i]), _pad_gain(swa_k_g[i]),
                mla_q_norm_g[i].reshape(1, Q_LORA), mla_kv_norm_g[i].reshape(1, KV_LORA),
                wuq.astype(BF16), wuk.astype(BF16), wuv.astype(BF16),
                gmq, gmqs, gmk, gmks, tc, ts1, ts2)
            o_s = _swa_layer(swa_sinks[i], qs, ks, vs)
            o_m = _mla_layer(qm, km, vm)
            xs, h2a, meta, cnt = _post_layer(xs, o_s, o_m, mod, n2, at_w_out[i].astype(BF16), wr_hi, wr_lo, br)
        xs, hs = _moe_layer(xs, h2a, meta, cnt, mod, moe_w_gate, moe_w_up, moe_w_down, l, hs)
    return xs.reshape(1, SEQ, D)
```

```python
import functools

import numpy as np
import jax
import jax.numpy as jnp
from jax import lax
from jax.experimental import pallas as pl
from jax.experimental.pallas import tpu as pltpu

F32 = jnp.float32
BF16 = jnp.bfloat16

D = 1024
SEQ = 16384
DEPTH = 4
EPS = 1e-6
LANES = 128
CONV_CH = 512
POOL_WINDOWS = (2, 4, 8, 16)
POOL_G = 128
HALO = 16
SWA_HEADS = 8
SWA_KV = 2
HEAD_DIM = 64
WINDOW = 128
MLA_HEADS = 8
MLA_NOPE = 64
MLA_ROPE = 32
MLA_QK = MLA_NOPE + MLA_ROPE
MLA_V = 64
Q_LORA = 384
KV_LORA = 256
ROPE_THETA = 10000.0
N_GROUPS = 4
EPG = 8
N_EXPERTS = N_GROUPS * EPG
FF = 256
NEG = -1e30
LOG2E = 1.4426950408889634

T_TOK = 512
T_ATT = 512
T_MOE = 512
T_DISP = 512
ROW_W = D + LANES
N_SORT = SEQ + N_GROUPS * T_MOE
N_TILES = N_SORT // T_MOE
E_STEP = 8
MLA_SPLIT = 1
MLA_DEPTH = 3
VMEM_LIMIT = 48 * 1024 * 1024
MOE_VMEM_LIMIT = 56 * 1024 * 1024

O_QS, O_KS, O_VS, O_CQ, O_CKV, O_KR, ODD_W = 0, 1024, 1280, 1536, 1920, 2176, 2432


def _cparams(n_axes=1):
    return pltpu.CompilerParams(dimension_semantics=("arbitrary",) * n_axes,
                                vmem_limit_bytes=VMEM_LIMIT)


def _rms(x):
    return x * lax.rsqrt(jnp.mean(x * x, axis=-1, keepdims=True) + EPS)


def _dot(a, b):
    return jnp.dot(a, b, preferred_element_type=F32)


def _dot_nt(a, b):
    return lax.dot_general(a, b, (((1,), (1,)), ((), ())), preferred_element_type=F32)


def _ada_kernel(c_ref, w_ref, b_ref, o_ref):
    c = c_ref[...]
    ca = c * jax.nn.sigmoid(c)
    o_ref[0] = jnp.sum(w_ref[0] * ca, axis=0, keepdims=True) + b_ref[0]


def _ada_mod(c, ada_w, ada_b):
    c_col = c.reshape(D, 1)
    b = ada_b.reshape(DEPTH * 6, 1, D)
    out = pl.pallas_call(
        _ada_kernel,
        out_shape=jax.ShapeDtypeStruct((DEPTH * 6, 1, D), F32),
        grid=(DEPTH, 6),
        in_specs=[pl.BlockSpec((D, 1), lambda l, j: (0, 0)),
                  pl.BlockSpec((1, D, D), lambda l, j: (l, 0, j)),
                  pl.BlockSpec((1, 1, D), lambda l, j: (l * 6 + j, 0, 0))],
        out_specs=pl.BlockSpec((1, 1, D), lambda l, j: (l * 6 + j, 0, 0)),
        compiler_params=_cparams(2),
        name="ada_mod",
    )(c_col, ada_w, b)
    mod = out.reshape(DEPTH, 6, D)
    return jnp.pad(mod, ((0, 0), (0, 2), (0, 0)))


def _rope_kernel(pos_ref, inv_ref, c_ref, s1_ref, s2_ref):
    pos = pos_ref[...].astype(F32)
    ang = pos * inv_ref[...]
    lane = lax.broadcasted_iota(jnp.int32, ang.shape, 1)
    cs = jnp.cos(ang)
    sn = jnp.sin(ang)
    c_ref[...] = jnp.where(lane < 64, 1.0, jnp.where(lane < 96, cs, 0.0))
    s1_ref[...] = jnp.where((lane >= 64) & (lane < 80), -sn, 0.0)
    s2_ref[...] = jnp.where((lane >= 80) & (lane < 96), sn, 0.0)


def _rope_tables(positions):
    half = MLA_ROPE // 2
    inv = jnp.power(ROPE_THETA, -jnp.arange(half, dtype=F32) / half)
    inv_lane = jnp.concatenate([jnp.zeros((64,), F32), inv, inv, jnp.zeros((32,), F32)]).reshape(1, LANES)
    pos = positions.reshape(SEQ, 1)
    shp = jax.ShapeDtypeStruct((SEQ, LANES), F32)
    spec = pl.BlockSpec((T_TOK, LANES), lambda i: (i, 0))
    return pl.pallas_call(
        _rope_kernel,
        out_shape=(shp, shp, shp),
        grid=(SEQ // T_TOK,),
        in_specs=[pl.BlockSpec((T_TOK, 1), lambda i: (i, 0)),
                  pl.BlockSpec((1, LANES), lambda i: (0, 0))],
        out_specs=(spec, spec, spec),
        compiler_params=_cparams(1),
        name="rope_tables",
    )(pos, inv_lane)


def _route(lg):
    lane = lax.broadcasted_iota(jnp.int32, lg.shape, 1)
    lane_f = lane.astype(F32)
    is_g = (lane >= N_EXPERTS) & (lane < N_EXPERTS + N_GROUPS)
    gl = jnp.where(is_g, lg, NEG)
    gmax = jnp.max(gl, axis=-1, keepdims=True)
    gidx = jnp.min(jnp.where(is_g & (gl == gmax), lane_f - N_EXPERTS, 1e3), axis=-1, keepdims=True)
    gsum = jnp.sum(jnp.where(is_g, jnp.exp(gl - gmax), 0.0), axis=-1, keepdims=True)
    gw = 1.0 / gsum
    grp_of_lane = (lane >> 3).astype(F32)
    in_grp = (lane < N_EXPERTS) & (grp_of_lane == gidx)
    el = jnp.where(in_grp, lg, NEG)
    m1 = jnp.max(el, axis=-1, keepdims=True)
    i1 = jnp.min(jnp.where(in_grp & (el == m1), lane_f, 1e3), axis=-1, keepdims=True)
    rest = in_grp & (lane_f != i1)
    el2 = jnp.where(rest, lg, NEG)
    m2 = jnp.max(el2, axis=-1, keepdims=True)
    i2 = jnp.min(jnp.where(rest & (el2 == m2), lane_f, 1e3), axis=-1, keepdims=True)
    r = jnp.exp(m2 - m1)
    w1 = gw / (1.0 + r)
    w2 = w1 * r
    return jnp.where(lane_f == i1, w1, jnp.where(lane_f == i2, w2, 0.0)), gidx


def _tail(x, y, mod, n2, wr_hi, wr_lo, br, xo_ref, h2_ref, meta_ref, cnt_ref, carry_ref):
    i = pl.program_id(0)

    @pl.when(i == 0)
    def _():
        carry_ref[...] = jnp.zeros(carry_ref.shape, F32)

    gate1 = mod[2:3]
    shift2, scale2 = mod[3:4], mod[4:5]
    xn = x + gate1 * y
    xo_ref[...] = xn
    h2 = _rms(xn) * n2 * (1.0 + scale2) + shift2
    hi = h2.astype(BF16)
    lo = (h2 - hi.astype(F32)).astype(BF16)
    lg = _dot(hi, wr_hi) + _dot(lo, wr_hi) + _dot(hi, wr_lo) + br
    gates, gidx = _route(lg)
    rows = lg.shape[0]
    lane = lax.broadcasted_iota(jnp.int32, (rows, LANES), 1)
    lane_f = lane.astype(F32)
    g8 = gates
    for g in range(1, N_GROUPS):
        g8 = g8 + pltpu.roll(gates, LANES - EPG * g, 1)
    onehot = (lane_f == gidx).astype(F32)
    r = lax.broadcasted_iota(jnp.int32, (rows, rows), 0)
    c = lax.broadcasted_iota(jnp.int32, (rows, rows), 1)
    before = jnp.where(c < r, 1.0, 0.0).astype(BF16)
    cum = _dot(before, onehot.astype(BF16)) + carry_ref[0:1, :]
    rank = jnp.sum(jnp.where(lane_f == gidx, cum, 0.0), axis=-1, keepdims=True)
    meta = jnp.where(lane < EPG, g8, jnp.where(lane == EPG, gidx, jnp.where(lane == EPG + 1, rank, 0.0)))
    h2_ref[:, 0:D] = h2
    h2_ref[:, D:D + LANES] = meta
    meta_ref[...] = meta
    total = carry_ref[0:1, :] + jnp.sum(onehot, axis=0, keepdims=True)
    carry_ref[...] = jnp.broadcast_to(total, carry_ref.shape)
    cnt_ref[...] = jnp.broadcast_to(total, cnt_ref.shape)


def _even_kernel(x_ref, xh_ref, mod_ref, n1_ref, n2_ref, win_ref, cw_ref, pw_ref, ps_ref,
                 wout_ref, wrh_ref, wrl_ref, br_ref, xo_ref, h2_ref, meta_ref, cnt_ref, carry_ref):
    i = pl.program_id(0)
    x = x_ref[...]
    mod = mod_ref[...]
    shift1, scale1 = mod[0:1], mod[1:2]
    xa = jnp.concatenate([xh_ref[...], x], axis=0)
    h = _rms(xa) * n1_ref[...] * (1.0 + scale1) + shift1
    z = _dot(h.astype(BF16), win_ref[...])
    rows = T_TOK + HALO
    row = lax.broadcasted_iota(jnp.int32, (rows, 1), 0)
    tpos = i * T_TOK + row - HALO
    live = (tpos >= 0).astype(F32)
    bg = z[:, 0:CONV_CH]
    v = z[:, CONV_CH:2 * CONV_CH] * z[:, 2 * CONV_CH:3 * CONV_CH] * live
    cw = cw_ref[...]
    conv = v * cw[0:1] + pltpu.roll(v, 1, 0) * cw[1:2] + pltpu.roll(v, 2, 0) * cw[2:3]
    parts = [(bg * conv)[HALO:]]
    ps = ps_ref[...]
    tcount = (tpos + 1).astype(F32)
    for gi, w in enumerate(POOL_WINDOWS):
        ug = z[:, 3 * CONV_CH + gi * POOL_G: 3 * CONV_CH + (gi + 1) * POOL_G] * live
        s = ug
        k = 1
        while k < w:
            s = s + pltpu.roll(s, k, 0)
            k *= 2
        inv = 1.0 / jnp.minimum(tcount, float(w))
        d = (s * inv - ug)[HALO:].astype(BF16)
        parts.append(_dot(d, pw_ref[gi]) * ps[:, gi * POOL_G:(gi + 1) * POOL_G])
    cat = jnp.concatenate(parts, axis=-1).astype(BF16)
    y = _dot(cat, wout_ref[...])
    _tail(x, y, mod, n2_ref[...], wrh_ref[...], wrl_ref[...], br_ref[...], xo_ref, h2_ref, meta_ref,
          cnt_ref, carry_ref)


def _tail_out(n):
    shapes = (jax.ShapeDtypeStruct((SEQ, D), F32), jax.ShapeDtypeStruct((SEQ, ROW_W), F32),
              jax.ShapeDtypeStruct((SEQ, LANES), F32), jax.ShapeDtypeStruct((8, LANES), F32))
    specs = (pl.BlockSpec((n, D), lambda i: (i, 0)), pl.BlockSpec((n, ROW_W), lambda i: (i, 0)),
             pl.BlockSpec((n, LANES), lambda i: (i, 0)), pl.BlockSpec((8, LANES), lambda i: (0, 0)))
    scratch = [pltpu.VMEM((8, LANES), F32)]
    return shapes, specs, scratch


def _full(shape):
    nd = len(shape)
    return pl.BlockSpec(shape, lambda i: (0,) * nd)


def _even_layer(x, mod, n1, n2, w_in, conv_w, pool_w, pool_scale, w_out, wr_hi, wr_lo, br):
    shapes, specs, scratch = _tail_out(T_TOK)
    hb = T_TOK // HALO
    return pl.pallas_call(
        _even_kernel,
        out_shape=shapes,
        grid=(SEQ // T_TOK,),
        in_specs=[pl.BlockSpec((T_TOK, D), lambda i: (i, 0)),
                  pl.BlockSpec((HALO, D), lambda i: (jnp.maximum(i * hb - 1, 0), 0)),
                  _full((8, D)), _full((1, D)), _full((1, D)),
                  _full((D, 4 * CONV_CH)), _full((3, CONV_CH)), _full((4, POOL_G, POOL_G)),
                  _full((1, 4 * POOL_G)), _full((D, D)),
                  _full((D, LANES)), _full((D, LANES)), _full((1, LANES))],
        out_specs=specs,
        scratch_shapes=scratch,
        compiler_params=_cparams(1),
        name="even_mixer",
    )(x, x, mod, n1, n2, w_in, conv_w, pool_w, pool_scale, w_out, wr_hi, wr_lo, br)


def _proj_kernel(x_ref, mod_ref, n1_ref, win_ref, gsq_ref, gsk_ref, gqn_ref, gkvn_ref,
                 wuq_ref, wuk_ref, wuv_ref, gmq_ref, gmqs_ref, gmk_ref, gmks_ref, c_ref, s1_ref, s2_ref,
                 qs_ref, ks_ref, vs_ref, qm_ref, km_ref, vm_ref):
    x = x_ref[...]
    mod = mod_ref[...]
    shift1, scale1 = mod[0:1], mod[1:2]
    h = _rms(x) * n1_ref[...] * (1.0 + scale1) + shift1
    z = _dot(h.astype(BF16), win_ref[...])

    def head_norm(t, g, dim):
        ms = jnp.sum(t * t, axis=-1, keepdims=True) * (1.0 / dim)
        return t * lax.rsqrt(ms + EPS) * g

    gsq, gsk = gsq_ref[...] * (HEAD_DIM ** -0.5), gsk_ref[...]
    for hd in range(SWA_HEADS):
        qh = head_norm(z[:, O_QS + hd * LANES: O_QS + (hd + 1) * LANES], gsq, HEAD_DIM)
        qs_ref[:, hd * LANES:(hd + 1) * LANES] = qh.astype(BF16)
    for kv in range(SWA_KV):
        kh = head_norm(z[:, O_KS + kv * LANES: O_KS + (kv + 1) * LANES], gsk, HEAD_DIM)
        ks_ref[:, kv * LANES:(kv + 1) * LANES] = kh.astype(BF16)
    vs_ref[...] = z[:, O_VS:O_CQ].astype(BF16)

    cq = (_rms(z[:, O_CQ:O_CKV]) * gqn_ref[...]).astype(BF16)
    ckv = (_rms(z[:, O_CKV:O_KR]) * gkvn_ref[...]).astype(BF16)
    qm = _dot(cq, wuq_ref[...])
    kn = _dot(ckv, wuk_ref[...])
    vm = _dot(ckv, wuv_ref[...])
    lane = lax.broadcasted_iota(jnp.int32, (T_TOK, LANES), 1)
    for j in range(MLA_HEADS // 2):
        vv = vm[:, j * LANES:(j + 1) * LANES]
        even = jnp.where(lane < 64, vv, jnp.where(lane == 64, 1.0, 0.0))
        odd = jnp.where(lane < 64, jnp.where(lane == 0, 1.0, 0.0), vv)
        vm_ref[:, (2 * j) * LANES:(2 * j + 1) * LANES] = even.astype(BF16)
        vm_ref[:, (2 * j + 1) * LANES:(2 * j + 2) * LANES] = odd.astype(BF16)
    kr = z[:, O_KR:O_KR + LANES]
    krs = z[:, O_KR + LANES:ODD_W]
    cs = c_ref[...]
    sp = s1_ref[...] + s2_ref[...]
    qscale = MLA_QK ** -0.5 * LOG2E
    gcq = cs * (gmq_ref[...] * qscale)
    gsq_r = sp * (gmqs_ref[...] * qscale)
    gck = cs * gmk_ref[...]
    krot = krs * (sp * gmks_ref[...])

    def inv_rms(t):
        return lax.rsqrt(jnp.sum(t * t, axis=-1, keepdims=True) * (1.0 / MLA_QK) + EPS)

    for hd in range(MLA_HEADS):
        sl = slice(hd * LANES, (hd + 1) * LANES)
        sw = slice((MLA_HEADS + hd) * LANES, (MLA_HEADS + hd + 1) * LANES)
        qh = qm[:, sl]
        qm_ref[:, sl] = ((qh * gcq + qm[:, sw] * gsq_r) * inv_rms(qh)).astype(BF16)
        kh = kn[:, sl] + kr
        km_ref[:, sl] = ((kh * gck + krot) * inv_rms(kh)).astype(BF16)


def _proj_layer(x, mod, n1, w_in, gsq, gsk, gqn, gkvn, wuq, wuk, wuv, gmq, gmqs, gmk, gmks, tc, ts1, ts2):
    def tok(wd):
        return pl.BlockSpec((T_TOK, wd), lambda i: (i, 0))
    widths = (1024, 256, 256, 1024, 1024, 1024)
    return pl.pallas_call(
        _proj_kernel,
        out_shape=tuple(jax.ShapeDtypeStruct((SEQ, wd), BF16) for wd in widths),
        grid=(SEQ // T_TOK,),
        in_specs=[tok(D), _full((8, D)), _full((1, D)), _full((D, ODD_W)),
                  _full((1, LANES)), _full((1, LANES)), _full((1, Q_LORA)), _full((1, KV_LORA)),
                  _full((Q_LORA, 2048)), _full((KV_LORA, 1024)), _full((KV_LORA, 512)),
                  _full((1, LANES)), _full((1, LANES)), _full((1, LANES)), _full((1, LANES)),
                  tok(LANES), tok(LANES), tok(LANES)],
        out_specs=tuple(tok(wd) for wd in widths),
        compiler_params=_cparams(1),
        name="odd_proj",
    )(x, mod, n1, w_in, gsq, gsk, gqn, gkvn, wuq, wuk, wuv, gmq, gmqs, gmk, gmks, tc, ts1, ts2)


def _swa_kernel(sink_ref, q_ref, k_ref, kh_ref, v_ref, vh_ref, o_ref):
    i = pl.program_id(0)
    kcat = jnp.concatenate([kh_ref[...], k_ref[...]], axis=0)
    vcat = jnp.concatenate([vh_ref[...], v_ref[...]], axis=0)
    grp = SWA_HEADS // SWA_KV
    r = lax.broadcasted_iota(jnp.int32, (grp * WINDOW, 2 * WINDOW), 0) & (WINDOW - 1)
    c = lax.broadcasted_iota(jnp.int32, (grp * WINDOW, 2 * WINDOW), 1)
    rel = WINDOW + r - c
    lane = lax.broadcasted_iota(jnp.int32, (WINDOW, LANES), 1)
    for sb in range(T_ATT // WINDOW):
        rows = slice(sb * WINDOW, (sb + 1) * WINDOW)
        kb = kcat[sb * WINDOW: sb * WINDOW + 2 * WINDOW]
        vb = vcat[sb * WINDOW: sb * WINDOW + 2 * WINDOW]
        kpos = i * T_ATT + (sb - 1) * WINDOW + c
        ok = (rel >= 0) & (rel < WINDOW) & (kpos >= 0)
        outs = []
        for kv in range(SWA_KV):
            q = jnp.concatenate([q_ref[rows, (kv * grp + g) * LANES:(kv * grp + g + 1) * LANES]
                                 for g in range(grp)], axis=0)
            sink = jnp.concatenate([jnp.full((WINDOW, 1), sink_ref[kv * grp + g], F32)
                                    for g in range(grp)], axis=0)
            s = jnp.where(ok, _dot_nt(q, kb[:, kv * LANES:(kv + 1) * LANES]), NEG)
            m = jnp.maximum(jnp.max(s, axis=-1, keepdims=True), sink)
            e = jnp.exp(s - m)
            den = jnp.sum(e, axis=-1, keepdims=True) + jnp.exp(sink - m)
            p = (e * (1.0 / den)).astype(BF16)
            o = _dot(p, vb[:, kv * LANES:(kv + 1) * LANES])
            outs += [o[g * WINDOW:(g + 1) * WINDOW] for g in range(grp)]
        for j in range(SWA_HEADS // 2):
            o_ref[rows, j * LANES:(j + 1) * LANES] = jnp.where(lane < 64, outs[2 * j], outs[2 * j + 1]).astype(BF16)


def _swa_layer(sinks, qs, ks, vs):
    hb = T_ATT // WINDOW
    return pl.pallas_call(
        _swa_kernel,
        out_shape=jax.ShapeDtypeStruct((SEQ, 512), BF16),
        grid=(SEQ // T_ATT,),
        in_specs=[pl.BlockSpec(memory_space=pltpu.SMEM),
                  pl.BlockSpec((T_ATT, 1024), lambda i: (i, 0)),
                  pl.BlockSpec((T_ATT, 256), lambda i: (i, 0)),
                  pl.BlockSpec((WINDOW, 256), lambda i: (jnp.maximum(i * hb - 1, 0), 0)),
                  pl.BlockSpec((T_ATT, 256), lambda i: (i, 0)),
                  pl.BlockSpec((WINDOW, 256), lambda i: (jnp.maximum(i * hb - 1, 0), 0))],
        out_specs=pl.BlockSpec((T_ATT, 512), lambda i: (i, 0)),
        compiler_params=_cparams(1),
        name="swa_attn",
    )(sinks, qs, ks, ks, vs, vs)


def _mla_kernel(qi_ref, ki_ref, q_ref, k_ref, v_ref, o_ref, m_ref, acc_ref):
    step = pl.program_id(0)
    qi = qi_ref[step]
    ki = ki_ref[step]

    @pl.when(ki == 0)
    def _():
        m_ref[...] = jnp.full(m_ref.shape, NEG, F32)
        acc_ref[...] = jnp.zeros(acc_ref.shape, F32)

    lane = lax.broadcasted_iota(jnp.int32, (T_ATT, LANES), 1)
    lo = lane < 64
    nc = T_ATT // LANES
    sub = T_ATT // MLA_SPLIT
    units = [(hd, part) for hd in range(MLA_HEADS) for part in range(MLA_SPLIT)]

    def scores(unit):
        hd, part = unit
        sl = slice(hd * LANES, (hd + 1) * LANES)
        return _dot_nt(q_ref[part * sub:(part + 1) * sub, sl], k_ref[:, sl])

    def update(masked):
        if masked:
            r = lax.broadcasted_iota(jnp.int32, (sub, LANES), 0)
            lane_s = lax.broadcasted_iota(jnp.int32, (sub, LANES), 1)

        def softmax_part(unit, s):
            hd, part = unit
            rows = slice(part * sub, (part + 1) * sub)
            cols = [s[:, c * LANES:(c + 1) * LANES] for c in range(nc)]
            if masked:
                cols = [jnp.where(r + part * sub >= lane_s + c * LANES, cols[c], NEG) for c in range(nc)]
            cmax = cols[0]
            for c in range(1, nc):
                cmax = jnp.maximum(cmax, cols[c])
            m_prev = m_ref[hd, rows]
            m_new = jnp.maximum(m_prev, jnp.max(cmax, axis=-1, keepdims=True))
            m_ref[hd, rows] = m_new
            alpha = jnp.exp2(m_prev - m_new)
            p = jnp.concatenate([jnp.exp2(cols[c] - m_new).astype(BF16) for c in range(nc)], axis=-1)
            return p, alpha

        def value_part(unit, p, alpha):
            hd, part = unit
            rows = slice(part * sub, (part + 1) * sub)
            acc_ref[hd, rows] = acc_ref[hd, rows] * alpha + _dot(p, v_ref[:, hd * LANES:(hd + 1) * LANES])

        s_q = [scores(u) for u in units[:MLA_DEPTH]]
        pend = None
        for n, u in enumerate(units):
            if n + MLA_DEPTH < len(units):
                s_q.append(scores(units[n + MLA_DEPTH]))
            cur = softmax_part(u, s_q[n])
            if pend is not None:
                value_part(units[n - 1], *pend)
            pend = cur
        value_part(units[-1], *pend)

    @pl.when(ki < qi)
    def _():
        update(False)

    @pl.when(ki == qi)
    def _():
        update(True)
        for j in range(MLA_HEADS // 2):
            ae = acc_ref[2 * j]
            ao = acc_ref[2 * j + 1]
            out = jnp.where(lo, ae * (1.0 / ae[:, 64:65]), ao * (1.0 / ao[:, 0:1]))
            o_ref[:, j * LANES:(j + 1) * LANES] = out.astype(BF16)


def _mla_layer(qm, km, vm):
    nb = SEQ // T_ATT
    qi = np.concatenate([np.full(n + 1, n, np.int32) for n in range(nb)])
    ki = np.concatenate([np.arange(n + 1, dtype=np.int32) for n in range(nb)])
    grid_spec = pltpu.PrefetchScalarGridSpec(
        num_scalar_prefetch=2,
        grid=(int(qi.shape[0]),),
        in_specs=[pl.BlockSpec((T_ATT, 1024), lambda s, qi, ki: (qi[s], 0)),
                  pl.BlockSpec((T_ATT, 1024), lambda s, qi, ki: (ki[s], 0)),
                  pl.BlockSpec((T_ATT, 1024), lambda s, qi, ki: (ki[s], 0))],
        out_specs=pl.BlockSpec((T_ATT, 512), lambda s, qi, ki: (qi[s], 0)),
        scratch_shapes=[pltpu.VMEM((MLA_HEADS, T_ATT, LANES), F32),
                        pltpu.VMEM((MLA_HEADS, T_ATT, LANES), F32)],
    )
    return pl.pallas_call(
        _mla_kernel,
        out_shape=jax.ShapeDtypeStruct((SEQ, 512), BF16),
        grid_spec=grid_spec,
        compiler_params=_cparams(1),
        name="mla_attn",
    )(jnp.asarray(qi), jnp.asarray(ki), qm, km, vm)


def _post_kernel(x_ref, os_ref, om_ref, mod_ref, n2_ref, wout_ref, wrh_ref, wrl_ref, br_ref,
                 xo_ref, h2_ref, meta_ref, cnt_ref, carry_ref):
    y = _dot(os_ref[...], wout_ref[0:512, :]) + _dot(om_ref[...], wout_ref[512:1024, :])
    _tail(x_ref[...], y, mod_ref[...], n2_ref[...], wrh_ref[...], wrl_ref[...], br_ref[...],
          xo_ref, h2_ref, meta_ref, cnt_ref, carry_ref)


def _post_layer(x, o_s, o_m, mod, n2, w_out, wr_hi, wr_lo, br):
    shapes, specs, scratch = _tail_out(T_TOK)
    return pl.pallas_call(
        _post_kernel,
        out_shape=shapes,
        grid=(SEQ // T_TOK,),
        in_specs=[pl.BlockSpec((T_TOK, D), lambda i: (i, 0)),
                  pl.BlockSpec((T_TOK, 512), lambda i: (i, 0)),
                  pl.BlockSpec((T_TOK, 512), lambda i: (i, 0)),
                  _full((8, D)), _full((1, D)), _full((D, D)),
                  _full((D, LANES)), _full((D, LANES)), _full((1, LANES))],
        out_specs=specs,
        scratch_shapes=scratch,
        compiler_params=_cparams(1),
        name="odd_post",
    )(x, o_s, o_m, mod, n2, w_out, wr_hi, wr_lo, br)


def _dispatch_plan(meta, cnt):
    grp = meta[:, EPG].astype(jnp.int32)
    rank = meta[:, EPG + 1].astype(jnp.int32)
    counts = cnt[0, :N_GROUPS].astype(jnp.int32)
    padded = ((counts + T_MOE - 1) // T_MOE) * T_MOE
    ends = jnp.cumsum(padded)
    pos = (ends - padded)[grp] + rank
    n_used = ends[-1] // T_MOE
    tile_start = jnp.arange(N_TILES, dtype=jnp.int32) * T_MOE
    tile_group = jnp.minimum(jnp.sum(tile_start[:, None] >= ends[None, :], axis=1), N_GROUPS - 1)
    return pos.reshape(SEQ // T_DISP, 1, T_DISP), tile_group.astype(jnp.int32), n_used.reshape(1)


def _row_copies(n, src_at, dst_at, sem):
    for r in range(n):
        pltpu.make_async_copy(src_at(r), dst_at(r), sem).start(priority=r % 2)


def _disp_kernel(pos_ref, x_ref, init_ref, o_ref, sem):
    del init_ref
    _row_copies(T_DISP,
                lambda r: x_ref.at[pl.ds(r, 1), :],
                lambda r: o_ref.at[pl.ds(pos_ref[0, 0, r], 1), :], sem)
    pltpu.make_async_copy(x_ref, o_ref.at[pl.ds(0, T_DISP), :], sem).wait()


def _dispatch(pos, h2a, hs):
    return pl.pallas_call(
        _disp_kernel,
        out_shape=jax.ShapeDtypeStruct((N_SORT, ROW_W), F32),
        grid=(SEQ // T_DISP,),
        in_specs=[pl.BlockSpec((1, 1, T_DISP), lambda i: (i, 0, 0), memory_space=pltpu.SMEM),
                  pl.BlockSpec((T_DISP, ROW_W), lambda i: (i, 0)),
                  pl.BlockSpec(memory_space=pl.ANY)],
        out_specs=pl.BlockSpec(memory_space=pl.ANY),
        scratch_shapes=[pltpu.SemaphoreType.DMA(())],
        input_output_aliases={2: 0},
        compiler_params=_cparams(1),
        name="moe_dispatch",
    )(pos, h2a, hs)


def _moe_kernel(tg_ref, nu_ref, x_ref, wg_ref, wu_ref, wd_ref, o_ref, xb_ref, act_ref):
    del tg_ref
    i = pl.program_id(0)
    j = pl.program_id(1)
    last = EPG // E_STEP - 1

    @pl.when(i < nu_ref[0])
    def _():
        @pl.when(j == 0)
        def _():
            xb_ref[...] = x_ref[:, 0:D].astype(BF16)

        xb = xb_ref[...]
        meta = x_ref[:, D:ROW_W]
        lane = lax.broadcasted_iota(jnp.int32, meta.shape, 1)
        for k in range(E_STEP):
            a = _dot(xb, wg_ref[0, k].astype(BF16))
            u = _dot(xb, wu_ref[0, k].astype(BF16))
            gate = jnp.sum(jnp.where(lane == j * E_STEP + k, meta, 0.0), axis=-1, keepdims=True)
            act = (a * jax.nn.sigmoid(a) * u * gate).astype(BF16)
            for jj in range(EPG // E_STEP):
                @pl.when(j == jj)
                def _(jj=jj, k=k, act=act):
                    act_ref[:, (jj * E_STEP + k) * FF:(jj * E_STEP + k + 1) * FF] = act

        @pl.when(j == last)
        def _():
            o_ref[...] = _dot(act_ref[...], wd_ref[0, 0].astype(BF16))

    @pl.when((i >= nu_ref[0]) & (j == last))
    def _():
        o_ref[...] = jnp.zeros(o_ref.shape, F32)


def _moe_experts(tile_group, n_used, hs, wg, wu, wd, layer):
    def tile(i, e, tg, nu):
        return (jnp.minimum(i, nu[0] - 1), 0)

    nj = EPG // E_STEP

    def expert(i, j, tg, nu):
        return (layer, jnp.where(i < nu[0], tg[i] * nj + j, tg[nu[0] - 1] * nj + nj - 1), 0, 0)

    def group(i, j, tg, nu):
        return (layer, tg[jnp.minimum(i, nu[0] - 1)], 0, 0)

    grid_spec = pltpu.PrefetchScalarGridSpec(
        num_scalar_prefetch=2,
        grid=(N_TILES, nj),
        in_specs=[pl.BlockSpec((T_MOE, ROW_W), tile),
                  pl.BlockSpec((1, E_STEP, D, FF), expert),
                  pl.BlockSpec((1, E_STEP, D, FF), expert),
                  pl.BlockSpec((1, 1, EPG * FF, D), group, pipeline_mode=pl.Buffered(1))],
        out_specs=pl.BlockSpec((T_MOE, D), lambda i, e, tg, nu: (i, 0)),
        scratch_shapes=[pltpu.VMEM((T_MOE, D), BF16), pltpu.VMEM((T_MOE, EPG * FF), BF16)],
    )
    return pl.pallas_call(
        _moe_kernel,
        out_shape=jax.ShapeDtypeStruct((N_SORT, D), F32),
        grid_spec=grid_spec,
        compiler_params=pltpu.CompilerParams(dimension_semantics=("arbitrary", "arbitrary"),
                                             vmem_limit_bytes=MOE_VMEM_LIMIT),
        name="moe_experts",
    )(tile_group, n_used, hs, wg, wu, wd.reshape(DEPTH, N_GROUPS, EPG * FF, D))


def _comb_kernel(pos_ref, x_ref, mod_ref, y_ref, o_ref, buf_ref, sem):
    _row_copies(T_DISP,
                lambda r: y_ref.at[pl.ds(pos_ref[0, 0, r], 1), :],
                lambda r: buf_ref.at[pl.ds(r, 1), :], sem)
    pltpu.make_async_copy(y_ref.at[pl.ds(0, T_DISP), :], buf_ref, sem).wait()
    o_ref[...] = x_ref[...] + mod_ref[5:6, :] * buf_ref[...]


def _combine(pos, x, mod, ys):
    return pl.pallas_call(
        _comb_kernel,
        out_shape=jax.ShapeDtypeStruct((SEQ, D), F32),
        grid=(SEQ // T_DISP,),
        in_specs=[pl.BlockSpec((1, 1, T_DISP), lambda i: (i, 0, 0), memory_space=pltpu.SMEM),
                  pl.BlockSpec((T_DISP, D), lambda i: (i, 0)),
                  pl.BlockSpec((8, D), lambda i: (0, 0)),
                  pl.BlockSpec(memory_space=pl.ANY)],
        out_specs=pl.BlockSpec((T_DISP, D), lambda i: (i, 0)),
        scratch_shapes=[pltpu.VMEM((T_DISP, D), F32), pltpu.SemaphoreType.DMA(())],
        compiler_params=_cparams(1),
        name="moe_combine",
    )(pos, x, mod, ys)


def _moe_layer(x, h2a, meta, cnt, mod, wg, wu, wd, layer, hs):
    pos, tile_group, n_used = _dispatch_plan(meta, cnt)
    hs = _dispatch(pos, h2a, hs)
    ys = _moe_experts(tile_group, n_used, hs, wg, wu, wd, layer)
    return _combine(pos, x, mod, ys), hs


def _pad_heads(w, heads, dim):
    k = w.shape[0]
    w = w.reshape(k, heads, dim)
    return jnp.pad(w, ((0, 0), (0, 0), (0, LANES - dim))).reshape(k, heads * LANES)


def _pad_gain(g):
    return jnp.pad(g, (0, LANES - g.shape[0])).reshape(1, LANES)


def _odd_weights(w_in):
    q_s = _pad_heads(w_in[:, 0:512], SWA_HEADS, HEAD_DIM)
    k_s = _pad_heads(w_in[:, 512:640], SWA_KV, HEAD_DIM)
    v = w_in[:, 640:768]
    v_s = jnp.concatenate([v[:, 0:64], v[:, 0:64], v[:, 64:128], v[:, 64:128]], axis=1)
    c_q = w_in[:, 768:1152]
    c_kv = w_in[:, 1152:1408]
    k_r = jnp.pad(w_in[:, 1408:1440], ((0, 0), (MLA_NOPE, LANES - MLA_QK)))
    k_r_sw = jnp.pad(_swap_halves(w_in[:, 1408:1440]), ((0, 0), (MLA_NOPE, LANES - MLA_QK)))
    return jnp.concatenate([q_s, k_s, v_s, c_q, c_kv, k_r, k_r_sw], axis=1).astype(BF16)


def _swap_halves(t):
    half = t.shape[-1] // 2
    return jnp.concatenate([t[..., half:], t[..., :half]], axis=-1)


def _uq_weights(w_uq):
    w = w_uq.reshape(Q_LORA, MLA_HEADS, MLA_QK)
    plain = jnp.pad(w, ((0, 0), (0, 0), (0, LANES - MLA_QK)))
    swapped = jnp.pad(_swap_halves(w[:, :, MLA_NOPE:]), ((0, 0), (0, 0), (MLA_NOPE, LANES - MLA_QK)))
    return jnp.concatenate([plain.reshape(Q_LORA, -1), swapped.reshape(Q_LORA, -1)], axis=1)


def _rope_gains(g):
    plain = jnp.pad(g, (0, LANES - MLA_QK)).reshape(1, LANES)
    partner = jnp.pad(_swap_halves(g[MLA_NOPE:]), (MLA_NOPE, LANES - MLA_QK)).reshape(1, LANES)
    return plain, partner


def _router_weights(w_group, b_group, w_expert, b_expert):
    w = jnp.pad(jnp.concatenate([w_expert, w_group], axis=1), ((0, 0), (0, LANES - N_EXPERTS - N_GROUPS)))
    hi = w.astype(BF16)
    lo = (w - hi.astype(F32)).astype(BF16)
    b = jnp.pad(jnp.concatenate([b_expert, b_group]), (0, LANES - N_EXPERTS - N_GROUPS)).reshape(1, LANES)
    return hi, lo, b


def kernel(x, c, positions, ada_w, ada_b, norm1_g, norm2_g, cp_w_in, conv_w, pool_w, pool_scale,
           cp_w_out, at_w_in, swa_q_g, swa_k_g, swa_sinks, mla_q_norm_g, mla_kv_norm_g, mla_w_uq,
           mla_w_ukv, mla_q_g, mla_k_g, at_w_out, moe_w_group, moe_b_group, moe_w_expert,
           moe_b_expert, moe_w_gate, moe_w_up, moe_w_down):
    xs = x.reshape(SEQ, D)
    mods = _ada_mod(c, ada_w, ada_b)
    tc, ts1, ts2 = _rope_tables(positions)
    hs = jnp.zeros((N_SORT, ROW_W), F32)
    for l in range(DEPTH):
        i = l // 2
        mod = mods[l]
        n1 = norm1_g[l].reshape(1, D)
        n2 = norm2_g[l].reshape(1, D)
        wr_hi, wr_lo, br = _router_weights(moe_w_group[l], moe_b_group[l], moe_w_expert[l], moe_b_expert[l])
        if l % 2 == 0:
            xs, h2a, meta, cnt = _even_layer(
                xs, mod, n1, n2, cp_w_in[i].astype(BF16), conv_w[i], pool_w[i].astype(BF16),
                pool_scale[i].reshape(1, 4 * POOL_G), cp_w_out[i].astype(BF16), wr_hi, wr_lo, br)
        else:
            ukv = mla_w_ukv[i].reshape(KV_LORA, MLA_HEADS, MLA_NOPE + MLA_V)
            wuk = _pad_heads(ukv[:, :, :MLA_NOPE].reshape(KV_LORA, MLA_HEADS * MLA_NOPE), MLA_HEADS, MLA_NOPE)
            wuv = ukv[:, :, MLA_NOPE:].reshape(KV_LORA, MLA_HEADS * MLA_V)
            wuq = _uq_weights(mla_w_uq[i])
            gmq, gmqs = _rope_gains(mla_q_g[i])
            gmk, gmks = _rope_gains(mla_k_g[i])
            qs, ks, vs, qm, km, vm = _proj_layer(
                xs, mod, n1, _odd_weights(at_w_in[i]), _pad_gain(swa_q_g[i]), _pad_gain(swa_k_g[i]),
                mla_q_norm_g[i].reshape(1, Q_LORA), mla_kv_norm_g[i].reshape(1, KV_LORA),
                wuq.astype(BF16), wuk.astype(BF16), wuv.astype(BF16),
                gmq, gmqs, gmk, gmks, tc, ts1, ts2)
            o_s = _swa_layer(swa_sinks[i], qs, ks, vs)
            o_m = _mla_layer(qm, km, vm)
            xs, h2a, meta, cnt = _post_layer(xs, o_s, o_m, mod, n2, at_w_out[i].astype(BF16), wr_hi, wr_lo, br)
        xs, hs = _moe_layer(xs, h2a, meta, cnt, mod, moe_w_gate, moe_w_up, moe_w_down, l, hs)
    return xs.reshape(1, SEQ, D)
```

```python
import functools

import numpy as np
import jax
import jax.numpy as jnp
from jax import lax
from jax.experimental import pallas as pl
from jax.experimental.pallas import tpu as pltpu

F32 = jnp.float32
BF16 = jnp.bfloat16

D = 1024
SEQ = 16384
DEPTH = 4
EPS = 1e-6
LANES = 128
CONV_CH = 512
POOL_WINDOWS = (2, 4, 8, 16)
POOL_G = 128
HALO = 16
SWA_HEADS = 8
SWA_KV = 2
HEAD_DIM = 64
WINDOW = 128
MLA_HEADS = 8
MLA_NOPE = 64
MLA_ROPE = 32
MLA_QK = MLA_NOPE + MLA_ROPE
MLA_V = 64
Q_LORA = 384
KV_LORA = 256
ROPE_THETA = 10000.0
N_GROUPS = 4
EPG = 8
N_EXPERTS = N_GROUPS * EPG
FF = 256
NEG = -1e30
LOG2E = 1.4426950408889634

T_TOK = 512
T_ATT = 512
T_MOE = 512
T_DISP = 512
ROW_W = D + LANES
N_SORT = SEQ + N_GROUPS * T_MOE
N_TILES = N_SORT // T_MOE
E_STEP = 8
T_MQ = 1024
T_MK = 512
MLA_SPLIT = 1
MLA_DEPTH = 3
VMEM_LIMIT = 48 * 1024 * 1024
MOE_VMEM_LIMIT = 56 * 1024 * 1024

O_QS, O_KS, O_VS, O_CQ, O_CKV, O_KR, ODD_W = 0, 1024, 1280, 1536, 1920, 2176, 2432


def _cparams(n_axes=1):
    return pltpu.CompilerParams(dimension_semantics=("arbitrary",) * n_axes,
                                vmem_limit_bytes=VMEM_LIMIT)


def _rms(x):
    return x * lax.rsqrt(jnp.mean(x * x, axis=-1, keepdims=True) + EPS)


def _dot(a, b):
    return jnp.dot(a, b, preferred_element_type=F32)


def _dot_nt(a, b):
    return lax.dot_general(a, b, (((1,), (1,)), ((), ())), preferred_element_type=F32)


def _ada_kernel(c_ref, w_ref, b_ref, o_ref):
    c = c_ref[...]
    ca = c * jax.nn.sigmoid(c)
    o_ref[0] = jnp.sum(w_ref[0] * ca, axis=0, keepdims=True) + b_ref[0]


def _ada_mod(c, ada_w, ada_b):
    c_col = c.reshape(D, 1)
    b = ada_b.reshape(DEPTH * 6, 1, D)
    out = pl.pallas_call(
        _ada_kernel,
        out_shape=jax.ShapeDtypeStruct((DEPTH * 6, 1, D), F32),
        grid=(DEPTH, 6),
        in_specs=[pl.BlockSpec((D, 1), lambda l, j: (0, 0)),
                  pl.BlockSpec((1, D, D), lambda l, j: (l, 0, j)),
                  pl.BlockSpec((1, 1, D), lambda l, j: (l * 6 + j, 0, 0))],
        out_specs=pl.BlockSpec((1, 1, D), lambda l, j: (l * 6 + j, 0, 0)),
        compiler_params=_cparams(2),
        name="ada_mod",
    )(c_col, ada_w, b)
    mod = out.reshape(DEPTH, 6, D)
    return jnp.pad(mod, ((0, 0), (0, 2), (0, 0)))


def _rope_kernel(pos_ref, inv_ref, c_ref, s1_ref, s2_ref):
    pos = pos_ref[...].astype(F32)
    ang = pos * inv_ref[...]
    lane = lax.broadcasted_iota(jnp.int32, ang.shape, 1)
    cs = jnp.cos(ang)
    sn = jnp.sin(ang)
    c_ref[...] = jnp.where(lane < 64, 1.0, jnp.where(lane < 96, cs, 0.0))
    s1_ref[...] = jnp.where((lane >= 64) & (lane < 80), -sn, 0.0)
    s2_ref[...] = jnp.where((lane >= 80) & (lane < 96), sn, 0.0)


def _rope_tables(positions):
    half = MLA_ROPE // 2
    inv = jnp.power(ROPE_THETA, -jnp.arange(half, dtype=F32) / half)
    inv_lane = jnp.concatenate([jnp.zeros((64,), F32), inv, inv, jnp.zeros((32,), F32)]).reshape(1, LANES)
    pos = positions.reshape(SEQ, 1)
    shp = jax.ShapeDtypeStruct((SEQ, LANES), F32)
    spec = pl.BlockSpec((T_TOK, LANES), lambda i: (i, 0))
    return pl.pallas_call(
        _rope_kernel,
        out_shape=(shp, shp, shp),
        grid=(SEQ // T_TOK,),
        in_specs=[pl.BlockSpec((T_TOK, 1), lambda i: (i, 0)),
                  pl.BlockSpec((1, LANES), lambda i: (0, 0))],
        out_specs=(spec, spec, spec),
        compiler_params=_cparams(1),
        name="rope_tables",
    )(pos, inv_lane)


def _route(lg):
    lane = lax.broadcasted_iota(jnp.int32, lg.shape, 1)
    lane_f = lane.astype(F32)
    is_g = (lane >= N_EXPERTS) & (lane < N_EXPERTS + N_GROUPS)
    gl = jnp.where(is_g, lg, NEG)
    gmax = jnp.max(gl, axis=-1, keepdims=True)
    gidx = jnp.min(jnp.where(is_g & (gl == gmax), lane_f - N_EXPERTS, 1e3), axis=-1, keepdims=True)
    gsum = jnp.sum(jnp.where(is_g, jnp.exp(gl - gmax), 0.0), axis=-1, keepdims=True)
    gw = 1.0 / gsum
    grp_of_lane = (lane >> 3).astype(F32)
    in_grp = (lane < N_EXPERTS) & (grp_of_lane == gidx)
    el = jnp.where(in_grp, lg, NEG)
    m1 = jnp.max(el, axis=-1, keepdims=True)
    i1 = jnp.min(jnp.where(in_grp & (el == m1), lane_f, 1e3), axis=-1, keepdims=True)
    rest = in_grp & (lane_f != i1)
    el2 = jnp.where(rest, lg, NEG)
    m2 = jnp.max(el2, axis=-1, keepdims=True)
    i2 = jnp.min(jnp.where(rest & (el2 == m2), lane_f, 1e3), axis=-1, keepdims=True)
    r = jnp.exp(m2 - m1)
    w1 = gw / (1.0 + r)
    w2 = w1 * r
    return jnp.where(lane_f == i1, w1, jnp.where(lane_f == i2, w2, 0.0)), gidx


def _tail(x, y, mod, n2, wr_hi, wr_lo, br, xo_ref, h2_ref, meta_ref, cnt_ref, carry_ref):
    i = pl.program_id(0)

    @pl.when(i == 0)
    def _():
        carry_ref[...] = jnp.zeros(carry_ref.shape, F32)

    gate1 = mod[2:3]
    shift2, scale2 = mod[3:4], mod[4:5]
    xn = x + gate1 * y
    xo_ref[...] = xn
    h2 = _rms(xn) * n2 * (1.0 + scale2) + shift2
    hi = h2.astype(BF16)
    lo = (h2 - hi.astype(F32)).astype(BF16)
    lg = _dot(hi, wr_hi) + _dot(lo, wr_hi) + _dot(hi, wr_lo) + br
    gates, gidx = _route(lg)
    rows = lg.shape[0]
    lane = lax.broadcasted_iota(jnp.int32, (rows, LANES), 1)
    lane_f = lane.astype(F32)
    g8 = gates
    for g in range(1, N_GROUPS):
        g8 = g8 + pltpu.roll(gates, LANES - EPG * g, 1)
    onehot = (lane_f == gidx).astype(F32)
    r = lax.broadcasted_iota(jnp.int32, (rows, rows), 0)
    c = lax.broadcasted_iota(jnp.int32, (rows, rows), 1)
    before = jnp.where(c < r, 1.0, 0.0).astype(BF16)
    cum = _dot(before, onehot.astype(BF16)) + carry_ref[0:1, :]
    rank = jnp.sum(jnp.where(lane_f == gidx, cum, 0.0), axis=-1, keepdims=True)
    meta = jnp.where(lane < EPG, g8, jnp.where(lane == EPG, gidx, jnp.where(lane == EPG + 1, rank, 0.0)))
    h2_ref[:, 0:D] = h2
    h2_ref[:, D:D + LANES] = meta
    meta_ref[...] = meta
    total = carry_ref[0:1, :] + jnp.sum(onehot, axis=0, keepdims=True)
    carry_ref[...] = jnp.broadcast_to(total, carry_ref.shape)
    cnt_ref[...] = jnp.broadcast_to(total, cnt_ref.shape)


def _even_kernel(x_ref, xh_ref, mod_ref, n1_ref, n2_ref, win_ref, cw_ref, pw_ref, ps_ref,
                 wout_ref, wrh_ref, wrl_ref, br_ref, xo_ref, h2_ref, meta_ref, cnt_ref, carry_ref):
    i = pl.program_id(0)
    x = x_ref[...]
    mod = mod_ref[...]
    shift1, scale1 = mod[0:1], mod[1:2]
    xa = jnp.concatenate([xh_ref[...], x], axis=0)
    h = _rms(xa) * n1_ref[...] * (1.0 + scale1) + shift1
    z = _dot(h.astype(BF16), win_ref[...])
    rows = T_TOK + HALO
    row = lax.broadcasted_iota(jnp.int32, (rows, 1), 0)
    tpos = i * T_TOK + row - HALO
    live = (tpos >= 0).astype(F32)
    bg = z[:, 0:CONV_CH]
    v = z[:, CONV_CH:2 * CONV_CH] * z[:, 2 * CONV_CH:3 * CONV_CH] * live
    cw = cw_ref[...]
    conv = v * cw[0:1] + pltpu.roll(v, 1, 0) * cw[1:2] + pltpu.roll(v, 2, 0) * cw[2:3]
    parts = [(bg * conv)[HALO:]]
    ps = ps_ref[...]
    tcount = (tpos + 1).astype(F32)
    for gi, w in enumerate(POOL_WINDOWS):
        ug = z[:, 3 * CONV_CH + gi * POOL_G: 3 * CONV_CH + (gi + 1) * POOL_G] * live
        s = ug
        k = 1
        while k < w:
            s = s + pltpu.roll(s, k, 0)
            k *= 2
        inv = 1.0 / jnp.minimum(tcount, float(w))
        d = (s * inv - ug)[HALO:].astype(BF16)
        parts.append(_dot(d, pw_ref[gi]) * ps[:, gi * POOL_G:(gi + 1) * POOL_G])
    cat = jnp.concatenate(parts, axis=-1).astype(BF16)
    y = _dot(cat, wout_ref[...])
    _tail(x, y, mod, n2_ref[...], wrh_ref[...], wrl_ref[...], br_ref[...], xo_ref, h2_ref, meta_ref,
          cnt_ref, carry_ref)


def _tail_out(n):
    shapes = (jax.ShapeDtypeStruct((SEQ, D), F32), jax.ShapeDtypeStruct((SEQ, ROW_W), F32),
              jax.ShapeDtypeStruct((SEQ, LANES), F32), jax.ShapeDtypeStruct((8, LANES), F32))
    specs = (pl.BlockSpec((n, D), lambda i: (i, 0)), pl.BlockSpec((n, ROW_W), lambda i: (i, 0)),
             pl.BlockSpec((n, LANES), lambda i: (i, 0)), pl.BlockSpec((8, LANES), lambda i: (0, 0)))
    scratch = [pltpu.VMEM((8, LANES), F32)]
    return shapes, specs, scratch


def _full(shape):
    nd = len(shape)
    return pl.BlockSpec(shape, lambda i: (0,) * nd)


def _even_layer(x, mod, n1, n2, w_in, conv_w, pool_w, pool_scale, w_out, wr_hi, wr_lo, br):
    shapes, specs, scratch = _tail_out(T_TOK)
    hb = T_TOK // HALO
    return pl.pallas_call(
        _even_kernel,
        out_shape=shapes,
        grid=(SEQ // T_TOK,),
        in_specs=[pl.BlockSpec((T_TOK, D), lambda i: (i, 0)),
                  pl.BlockSpec((HALO, D), lambda i: (jnp.maximum(i * hb - 1, 0), 0)),
                  _full((8, D)), _full((1, D)), _full((1, D)),
                  _full((D, 4 * CONV_CH)), _full((3, CONV_CH)), _full((4, POOL_G, POOL_G)),
                  _full((1, 4 * POOL_G)), _full((D, D)),
                  _full((D, LANES)), _full((D, LANES)), _full((1, LANES))],
        out_specs=specs,
        scratch_shapes=scratch,
        compiler_params=_cparams(1),
        name="even_mixer",
    )(x, x, mod, n1, n2, w_in, conv_w, pool_w, pool_scale, w_out, wr_hi, wr_lo, br)


def _proj_kernel(x_ref, mod_ref, n1_ref, win_ref, gsq_ref, gsk_ref, gqn_ref, gkvn_ref,
                 wuq_ref, wuk_ref, wuv_ref, gmq_ref, gmqs_ref, gmk_ref, gmks_ref, c_ref, s1_ref, s2_ref,
                 qs_ref, ks_ref, vs_ref, qm_ref, km_ref, vm_ref):
    x = x_ref[...]
    mod = mod_ref[...]
    shift1, scale1 = mod[0:1], mod[1:2]
    h = _rms(x) * n1_ref[...] * (1.0 + scale1) + shift1
    z = _dot(h.astype(BF16), win_ref[...])

    def head_norm(t, g, dim):
        ms = jnp.sum(t * t, axis=-1, keepdims=True) * (1.0 / dim)
        return t * lax.rsqrt(ms + EPS) * g

    gsq, gsk = gsq_ref[...] * (HEAD_DIM ** -0.5), gsk_ref[...]
    for hd in range(SWA_HEADS):
        qh = head_norm(z[:, O_QS + hd * LANES: O_QS + (hd + 1) * LANES], gsq, HEAD_DIM)
        qs_ref[:, hd * LANES:(hd + 1) * LANES] = qh.astype(BF16)
    for kv in range(SWA_KV):
        kh = head_norm(z[:, O_KS + kv * LANES: O_KS + (kv + 1) * LANES], gsk, HEAD_DIM)
        ks_ref[:, kv * LANES:(kv + 1) * LANES] = kh.astype(BF16)
    vs_ref[...] = z[:, O_VS:O_CQ].astype(BF16)

    cq = (_rms(z[:, O_CQ:O_CKV]) * gqn_ref[...]).astype(BF16)
    ckv = (_rms(z[:, O_CKV:O_KR]) * gkvn_ref[...]).astype(BF16)
    qm = _dot(cq, wuq_ref[...])
    kn = _dot(ckv, wuk_ref[...])
    vm = _dot(ckv, wuv_ref[...])
    lane = lax.broadcasted_iota(jnp.int32, (T_TOK, LANES), 1)
    for j in range(MLA_HEADS // 2):
        vv = vm[:, j * LANES:(j + 1) * LANES]
        even = jnp.where(lane < 64, vv, jnp.where(lane == 64, 1.0, 0.0))
        odd = jnp.where(lane < 64, jnp.where(lane == 0, 1.0, 0.0), vv)
        vm_ref[:, (2 * j) * LANES:(2 * j + 1) * LANES] = even.astype(BF16)
        vm_ref[:, (2 * j + 1) * LANES:(2 * j + 2) * LANES] = odd.astype(BF16)
    kr = z[:, O_KR:O_KR + LANES]
    krs = z[:, O_KR + LANES:ODD_W]
    cs = c_ref[...]
    sp = s1_ref[...] + s2_ref[...]
    qscale = MLA_QK ** -0.5 * LOG2E
    gcq = cs * (gmq_ref[...] * qscale)
    gsq_r = sp * (gmqs_ref[...] * qscale)
    gck = cs * gmk_ref[...]
    krot = krs * (sp * gmks_ref[...])

    def inv_rms(t):
        return lax.rsqrt(jnp.sum(t * t, axis=-1, keepdims=True) * (1.0 / MLA_QK) + EPS)

    for hd in range(MLA_HEADS):
        sl = slice(hd * LANES, (hd + 1) * LANES)
        sw = slice((MLA_HEADS + hd) * LANES, (MLA_HEADS + hd + 1) * LANES)
        qh = qm[:, sl]
        qm_ref[:, sl] = ((qh * gcq + qm[:, sw] * gsq_r) * inv_rms(qh)).astype(BF16)
        kh = kn[:, sl] + kr
        km_ref[:, sl] = ((kh * gck + krot) * inv_rms(kh)).astype(BF16)


def _proj_layer(x, mod, n1, w_in, gsq, gsk, gqn, gkvn, wuq, wuk, wuv, gmq, gmqs, gmk, gmks, tc, ts1, ts2):
    def tok(wd):
        return pl.BlockSpec((T_TOK, wd), lambda i: (i, 0))
    widths = (1024, 256, 256, 1024, 1024, 1024)
    return pl.pallas_call(
        _proj_kernel,
        out_shape=tuple(jax.ShapeDtypeStruct((SEQ, wd), BF16) for wd in widths),
        grid=(SEQ // T_TOK,),
        in_specs=[tok(D), _full((8, D)), _full((1, D)), _full((D, ODD_W)),
                  _full((1, LANES)), _full((1, LANES)), _full((1, Q_LORA)), _full((1, KV_LORA)),
                  _full((Q_LORA, 2048)), _full((KV_LORA, 1024)), _full((KV_LORA, 512)),
                  _full((1, LANES)), _full((1, LANES)), _full((1, LANES)), _full((1, LANES)),
                  tok(LANES), tok(LANES), tok(LANES)],
        out_specs=tuple(tok(wd) for wd in widths),
        compiler_params=_cparams(1),
        name="odd_proj",
    )(x, mod, n1, w_in, gsq, gsk, gqn, gkvn, wuq, wuk, wuv, gmq, gmqs, gmk, gmks, tc, ts1, ts2)


def _swa_kernel(sink_ref, q_ref, k_ref, kh_ref, v_ref, vh_ref, o_ref):
    i = pl.program_id(0)
    kcat = jnp.concatenate([kh_ref[...], k_ref[...]], axis=0)
    vcat = jnp.concatenate([vh_ref[...], v_ref[...]], axis=0)
    grp = SWA_HEADS // SWA_KV
    r = lax.broadcasted_iota(jnp.int32, (grp * WINDOW, 2 * WINDOW), 0) & (WINDOW - 1)
    c = lax.broadcasted_iota(jnp.int32, (grp * WINDOW, 2 * WINDOW), 1)
    rel = WINDOW + r - c
    lane = lax.broadcasted_iota(jnp.int32, (WINDOW, LANES), 1)
    for sb in range(T_ATT // WINDOW):
        rows = slice(sb * WINDOW, (sb + 1) * WINDOW)
        kb = kcat[sb * WINDOW: sb * WINDOW + 2 * WINDOW]
        vb = vcat[sb * WINDOW: sb * WINDOW + 2 * WINDOW]
        kpos = i * T_ATT + (sb - 1) * WINDOW + c
        ok = (rel >= 0) & (rel < WINDOW) & (kpos >= 0)
        outs = []
        for kv in range(SWA_KV):
            q = jnp.concatenate([q_ref[rows, (kv * grp + g) * LANES:(kv * grp + g + 1) * LANES]
                                 for g in range(grp)], axis=0)
            sink = jnp.concatenate([jnp.full((WINDOW, 1), sink_ref[kv * grp + g], F32)
                                    for g in range(grp)], axis=0)
            s = jnp.where(ok, _dot_nt(q, kb[:, kv * LANES:(kv + 1) * LANES]), NEG)
            m = jnp.maximum(jnp.max(s, axis=-1, keepdims=True), sink)
            e = jnp.exp(s - m)
            den = jnp.sum(e, axis=-1, keepdims=True) + jnp.exp(sink - m)
            p = (e * (1.0 / den)).astype(BF16)
            o = _dot(p, vb[:, kv * LANES:(kv + 1) * LANES])
            outs += [o[g * WINDOW:(g + 1) * WINDOW] for g in range(grp)]
        for j in range(SWA_HEADS // 2):
            o_ref[rows, j * LANES:(j + 1) * LANES] = jnp.where(lane < 64, outs[2 * j], outs[2 * j + 1]).astype(BF16)


def _swa_layer(sinks, qs, ks, vs):
    hb = T_ATT // WINDOW
    return pl.pallas_call(
        _swa_kernel,
        out_shape=jax.ShapeDtypeStruct((SEQ, 512), BF16),
        grid=(SEQ // T_ATT,),
        in_specs=[pl.BlockSpec(memory_space=pltpu.SMEM),
                  pl.BlockSpec((T_ATT, 1024), lambda i: (i, 0)),
                  pl.BlockSpec((T_ATT, 256), lambda i: (i, 0)),
                  pl.BlockSpec((WINDOW, 256), lambda i: (jnp.maximum(i * hb - 1, 0), 0)),
                  pl.BlockSpec((T_ATT, 256), lambda i: (i, 0)),
                  pl.BlockSpec((WINDOW, 256), lambda i: (jnp.maximum(i * hb - 1, 0), 0))],
        out_specs=pl.BlockSpec((T_ATT, 512), lambda i: (i, 0)),
        compiler_params=_cparams(1),
        name="swa_attn",
    )(sinks, qs, ks, ks, vs, vs)


def _mla_kernel(qi_ref, ki_ref, q_ref, k_ref, v_ref, o_ref, m_ref, acc_ref):
    step = pl.program_id(0)
    qi = qi_ref[step]
    ki = ki_ref[step]

    @pl.when(ki == 0)
    def _():
        m_ref[...] = jnp.full(m_ref.shape, NEG, F32)
        acc_ref[...] = jnp.zeros(acc_ref.shape, F32)

    lane = lax.broadcasted_iota(jnp.int32, (T_MQ, LANES), 1)
    lo = lane < 64
    nc = T_MK // LANES
    sub = T_MQ // MLA_SPLIT
    ratio = T_MQ // T_MK
    units = [(hd, part) for hd in range(MLA_HEADS) for part in range(MLA_SPLIT)]

    def scores(unit):
        hd, part = unit
        sl = slice(hd * LANES, (hd + 1) * LANES)
        return _dot_nt(q_ref[part * sub:(part + 1) * sub, sl], k_ref[:, sl])

    def update(masked):
        if masked:
            r = lax.broadcasted_iota(jnp.int32, (sub, LANES), 0)
            lane_s = lax.broadcasted_iota(jnp.int32, (sub, LANES), 1)

        def softmax_part(unit, s):
            hd, part = unit
            rows = slice(part * sub, (part + 1) * sub)
            cols = [s[:, c * LANES:(c + 1) * LANES] for c in range(nc)]
            if masked:
                off = qi * T_MQ - ki * T_MK + part * sub
                cols = [jnp.where(r + off >= lane_s + c * LANES, cols[c], NEG) for c in range(nc)]
            cmax = cols[0]
            for c in range(1, nc):
                cmax = jnp.maximum(cmax, cols[c])
            m_prev = m_ref[hd, rows]
            m_new = jnp.maximum(m_prev, jnp.max(cmax, axis=-1, keepdims=True))
            m_ref[hd, rows] = m_new
            alpha = jnp.exp2(m_prev - m_new)
            p = jnp.concatenate([jnp.exp2(cols[c] - m_new).astype(BF16) for c in range(nc)], axis=-1)
            return p, alpha

        def value_part(unit, p, alpha):
            hd, part = unit
            rows = slice(part * sub, (part + 1) * sub)
            acc_ref[hd, rows] = acc_ref[hd, rows] * alpha + _dot(p, v_ref[:, hd * LANES:(hd + 1) * LANES])

        s_q = [scores(u) for u in units[:MLA_DEPTH]]
        pend = None
        for n, u in enumerate(units):
            if n + MLA_DEPTH < len(units):
                s_q.append(scores(units[n + MLA_DEPTH]))
            cur = softmax_part(u, s_q[n])
            if pend is not None:
                value_part(units[n - 1], *pend)
            pend = cur
        value_part(units[-1], *pend)

    @pl.when(ki < qi * ratio)
    def _():
        update(False)

    @pl.when(ki >= qi * ratio)
    def _():
        update(True)

    @pl.when(ki == qi * ratio + ratio - 1)
    def _():
        for j in range(MLA_HEADS // 2):
            ae = acc_ref[2 * j]
            ao = acc_ref[2 * j + 1]
            out = jnp.where(lo, ae * (1.0 / ae[:, 64:65]), ao * (1.0 / ao[:, 0:1]))
            o_ref[:, j * LANES:(j + 1) * LANES] = out.astype(BF16)


def _mla_layer(qm, km, vm):
    nb = SEQ // T_MQ
    ratio = T_MQ // T_MK
    qi = np.concatenate([np.full(ratio * (n + 1), n, np.int32) for n in range(nb)])
    ki = np.concatenate([np.arange(ratio * (n + 1), dtype=np.int32) for n in range(nb)])
    grid_spec = pltpu.PrefetchScalarGridSpec(
        num_scalar_prefetch=2,
        grid=(int(qi.shape[0]),),
        in_specs=[pl.BlockSpec((T_MQ, 1024), lambda s, qi, ki: (qi[s], 0)),
                  pl.BlockSpec((T_MK, 1024), lambda s, qi, ki: (ki[s], 0)),
                  pl.BlockSpec((T_MK, 1024), lambda s, qi, ki: (ki[s], 0))],
        out_specs=pl.BlockSpec((T_MQ, 512), lambda s, qi, ki: (qi[s], 0)),
        scratch_shapes=[pltpu.VMEM((MLA_HEADS, T_MQ, LANES), F32),
                        pltpu.VMEM((MLA_HEADS, T_MQ, LANES), F32)],
    )
    return pl.pallas_call(
        _mla_kernel,
        out_shape=jax.ShapeDtypeStruct((SEQ, 512), BF16),
        grid_spec=grid_spec,
        compiler_params=_cparams(1),
        name="mla_attn",
    )(jnp.asarray(qi), jnp.asarray(ki), qm, km, vm)


def _post_kernel(x_ref, os_ref, om_ref, mod_ref, n2_ref, wout_ref, wrh_ref, wrl_ref, br_ref,
                 xo_ref, h2_ref, meta_ref, cnt_ref, carry_ref):
    y = _dot(os_ref[...], wout_ref[0:512, :]) + _dot(om_ref[...], wout_ref[512:1024, :])
    _tail(x_ref[...], y, mod_ref[...], n2_ref[...], wrh_ref[...], wrl_ref[...], br_ref[...],
          xo_ref, h2_ref, meta_ref, cnt_ref, carry_ref)


def _post_layer(x, o_s, o_m, mod, n2, w_out, wr_hi, wr_lo, br):
    shapes, specs, scratch = _tail_out(T_TOK)
    return pl.pallas_call(
        _post_kernel,
        out_shape=shapes,
        grid=(SEQ // T_TOK,),
        in_specs=[pl.BlockSpec((T_TOK, D), lambda i: (i, 0)),
                  pl.BlockSpec((T_TOK, 512), lambda i: (i, 0)),
                  pl.BlockSpec((T_TOK, 512), lambda i: (i, 0)),
                  _full((8, D)), _full((1, D)), _full((D, D)),
                  _full((D, LANES)), _full((D, LANES)), _full((1, LANES))],
        out_specs=specs,
        scratch_shapes=scratch,
        compiler_params=_cparams(1),
        name="odd_post",
    )(x, o_s, o_m, mod, n2, w_out, wr_hi, wr_lo, br)


def _dispatch_plan(meta, cnt):
    grp = meta[:, EPG].astype(jnp.int32)
    rank = meta[:, EPG + 1].astype(jnp.int32)
    counts = cnt[0, :N_GROUPS].astype(jnp.int32)
    padded = ((counts + T_MOE - 1) // T_MOE) * T_MOE
    ends = jnp.cumsum(padded)
    pos = (ends - padded)[grp] + rank
    n_used = ends[-1] // T_MOE
    tile_start = jnp.arange(N_TILES, dtype=jnp.int32) * T_MOE
    tile_group = jnp.minimum(jnp.sum(tile_start[:, None] >= ends[None, :], axis=1), N_GROUPS - 1)
    return pos.reshape(SEQ // T_DISP, 1, T_DISP), tile_group.astype(jnp.int32), n_used.reshape(1)


def _row_copies(n, src_at, dst_at, sem):
    for r in range(n):
        pltpu.make_async_copy(src_at(r), dst_at(r), sem).start(priority=r % 2)


def _disp_kernel(pos_ref, x_ref, init_ref, o_ref, sem):
    del init_ref
    _row_copies(T_DISP,
                lambda r: x_ref.at[pl.ds(r, 1), :],
                lambda r: o_ref.at[pl.ds(pos_ref[0, 0, r], 1), :], sem)
    pltpu.make_async_copy(x_ref, o_ref.at[pl.ds(0, T_DISP), :], sem).wait()


def _dispatch(pos, h2a, hs):
    return pl.pallas_call(
        _disp_kernel,
        out_shape=jax.ShapeDtypeStruct((N_SORT, ROW_W), F32),
        grid=(SEQ // T_DISP,),
        in_specs=[pl.BlockSpec((1, 1, T_DISP), lambda i: (i, 0, 0), memory_space=pltpu.SMEM),
                  pl.BlockSpec((T_DISP, ROW_W), lambda i: (i, 0)),
                  pl.BlockSpec(memory_space=pl.ANY)],
        out_specs=pl.BlockSpec(memory_space=pl.ANY),
        scratch_shapes=[pltpu.SemaphoreType.DMA(())],
        input_output_aliases={2: 0},
        compiler_params=_cparams(1),
        name="moe_dispatch",
    )(pos, h2a, hs)


def _moe_kernel(tg_ref, nu_ref, x_ref, wg_ref, wu_ref, wd_ref, o_ref, xb_ref, act_ref):
    del tg_ref
    i = pl.program_id(0)
    j = pl.program_id(1)
    last = EPG // E_STEP - 1

    @pl.when(i < nu_ref[0])
    def _():
        @pl.when(j == 0)
        def _():
            xb_ref[...] = x_ref[:, 0:D].astype(BF16)

        xb = xb_ref[...]
        meta = x_ref[:, D:ROW_W]
        lane = lax.broadcasted_iota(jnp.int32, meta.shape, 1)
        for k in range(E_STEP):
            a = _dot(xb, wg_ref[0, k].astype(BF16))
            u = _dot(xb, wu_ref[0, k].astype(BF16))
            gate = jnp.sum(jnp.where(lane == j * E_STEP + k, meta, 0.0), axis=-1, keepdims=True)
            act = (a * jax.nn.sigmoid(a) * u * gate).astype(BF16)
            for jj in range(EPG // E_STEP):
                @pl.when(j == jj)
                def _(jj=jj, k=k, act=act):
                    act_ref[:, (jj * E_STEP + k) * FF:(jj * E_STEP + k + 1) * FF] = act

        @pl.when(j == last)
        def _():
            o_ref[...] = _dot(act_ref[...], wd_ref[0, 0].astype(BF16))

    @pl.when((i >= nu_ref[0]) & (j == last))
    def _():
        o_ref[...] = jnp.zeros(o_ref.shape, F32)


def _moe_experts(tile_group, n_used, hs, wg, wu, wd, layer):
    def tile(i, e, tg, nu):
        return (jnp.minimum(i, nu[0] - 1), 0)

    nj = EPG // E_STEP

    def expert(i, j, tg, nu):
        return (layer, jnp.where(i < nu[0], tg[i] * nj + j, tg[nu[0] - 1] * nj + nj - 1), 0, 0)

    def group(i, j, tg, nu):
        return (layer, tg[jnp.minimum(i, nu[0] - 1)], 0, 0)

    grid_spec = pltpu.PrefetchScalarGridSpec(
        num_scalar_prefetch=2,
        grid=(N_TILES, nj),
        in_specs=[pl.BlockSpec((T_MOE, ROW_W), tile),
                  pl.BlockSpec((1, E_STEP, D, FF), expert),
                  pl.BlockSpec((1, E_STEP, D, FF), expert),
                  pl.BlockSpec((1, 1, EPG * FF, D), group, pipeline_mode=pl.Buffered(1))],
        out_specs=pl.BlockSpec((T_MOE, D), lambda i, e, tg, nu: (i, 0)),
        scratch_shapes=[pltpu.VMEM((T_MOE, D), BF16), pltpu.VMEM((T_MOE, EPG * FF), BF16)],
    )
    return pl.pallas_call(
        _moe_kernel,
        out_shape=jax.ShapeDtypeStruct((N_SORT, D), F32),
        grid_spec=grid_spec,
        compiler_params=pltpu.CompilerParams(dimension_semantics=("arbitrary", "arbitrary"),
                                             vmem_limit_bytes=MOE_VMEM_LIMIT),
        name="moe_experts",
    )(tile_group, n_used, hs, wg, wu, wd.reshape(DEPTH, N_GROUPS, EPG * FF, D))


def _comb_kernel(pos_ref, x_ref, mod_ref, y_ref, o_ref, buf_ref, sem):
    _row_copies(T_DISP,
                lambda r: y_ref.at[pl.ds(pos_ref[0, 0, r], 1), :],
                lambda r: buf_ref.at[pl.ds(r, 1), :], sem)
    pltpu.make_async_copy(y_ref.at[pl.ds(0, T_DISP), :], buf_ref, sem).wait()
    o_ref[...] = x_ref[...] + mod_ref[5:6, :] * buf_ref[...]


def _combine(pos, x, mod, ys):
    return pl.pallas_call(
        _comb_kernel,
        out_shape=jax.ShapeDtypeStruct((SEQ, D), F32),
        grid=(SEQ // T_DISP,),
        in_specs=[pl.BlockSpec((1, 1, T_DISP), lambda i: (i, 0, 0), memory_space=pltpu.SMEM),
                  pl.BlockSpec((T_DISP, D), lambda i: (i, 0)),
                  pl.BlockSpec((8, D), lambda i: (0, 0)),
                  pl.BlockSpec(memory_space=pl.ANY)],
        out_specs=pl.BlockSpec((T_DISP, D), lambda i: (i, 0)),
        scratch_shapes=[pltpu.VMEM((T_DISP, D), F32), pltpu.SemaphoreType.DMA(())],
        compiler_params=_cparams(1),
        name="moe_combine",
    )(pos, x, mod, ys)


def _moe_layer(x, h2a, meta, cnt, mod, wg, wu, wd, layer, hs):
    pos, tile_group, n_used = _dispatch_plan(meta, cnt)
    hs = _dispatch(pos, h2a, hs)
    ys = _moe_experts(tile_group, n_used, hs, wg, wu, wd, layer)
    return _combine(pos, x, mod, ys), hs


def _pad_heads(w, heads, dim):
    k = w.shape[0]
    w = w.reshape(k, heads, dim)
    return jnp.pad(w, ((0, 0), (0, 0), (0, LANES - dim))).reshape(k, heads * LANES)


def _pad_gain(g):
    return jnp.pad(g, (0, LANES - g.shape[0])).reshape(1, LANES)


def _odd_weights(w_in):
    q_s = _pad_heads(w_in[:, 0:512], SWA_HEADS, HEAD_DIM)
    k_s = _pad_heads(w_in[:, 512:640], SWA_KV, HEAD_DIM)
    v = w_in[:, 640:768]
    v_s = jnp.concatenate([v[:, 0:64], v[:, 0:64], v[:, 64:128], v[:, 64:128]], axis=1)
    c_q = w_in[:, 768:1152]
    c_kv = w_in[:, 1152:1408]
    k_r = jnp.pad(w_in[:, 1408:1440], ((0, 0), (MLA_NOPE, LANES - MLA_QK)))
    k_r_sw = jnp.pad(_swap_halves(w_in[:, 1408:1440]), ((0, 0), (MLA_NOPE, LANES - MLA_QK)))
    return jnp.concatenate([q_s, k_s, v_s, c_q, c_kv, k_r, k_r_sw], axis=1).astype(BF16)


def _swap_halves(t):
    half = t.shape[-1] // 2
    return jnp.concatenate([t[..., half:], t[..., :half]], axis=-1)


def _uq_weights(w_uq):
    w = w_uq.reshape(Q_LORA, MLA_HEADS, MLA_QK)
    plain = jnp.pad(w, ((0, 0), (0, 0), (0, LANES - MLA_QK)))
    swapped = jnp.pad(_swap_halves(w[:, :, MLA_NOPE:]), ((0, 0), (0, 0), (MLA_NOPE, LANES - MLA_QK)))
    return jnp.concatenate([plain.reshape(Q_LORA, -1), swapped.reshape(Q_LORA, -1)], axis=1)


def _rope_gains(g):
    plain = jnp.pad(g, (0, LANES - MLA_QK)).reshape(1, LANES)
    partner = jnp.pad(_swap_halves(g[MLA_NOPE:]), (MLA_NOPE, LANES - MLA_QK)).reshape(1, LANES)
    return plain, partner


def _router_weights(w_group, b_group, w_expert, b_expert):
    w = jnp.pad(jnp.concatenate([w_expert, w_group], axis=1), ((0, 0), (0, LANES - N_EXPERTS - N_GROUPS)))
    hi = w.astype(BF16)
    lo = (w - hi.astype(F32)).astype(BF16)
    b = jnp.pad(jnp.concatenate([b_expert, b_group]), (0, LANES - N_EXPERTS - N_GROUPS)).reshape(1, LANES)
    return hi, lo, b


def kernel(x, c, positions, ada_w, ada_b, norm1_g, norm2_g, cp_w_in, conv_w, pool_w, pool_scale,
           cp_w_out, at_w_in, swa_q_g, swa_k_g, swa_sinks, mla_q_norm_g, mla_kv_norm_g, mla_w_uq,
           mla_w_ukv, mla_q_g, mla_k_g, at_w_out, moe_w_group, moe_b_group, moe_w_expert,
           moe_b_expert, moe_w_gate, moe_w_up, moe_w_down):
    xs = x.reshape(SEQ, D)
    mods = _ada_mod(c, ada_w, ada_b)
    tc, ts1, ts2 = _rope_tables(positions)
    hs = jnp.zeros((N_SORT, ROW_W), F32)
    for l in range(DEPTH):
        i = l // 2
        mod = mods[l]
        n1 = norm1_g[l].reshape(1, D)
        n2 = norm2_g[l].reshape(1, D)
        wr_hi, wr_lo, br = _router_weights(moe_w_group[l], moe_b_group[l], moe_w_expert[l], moe_b_expert[l])
        if l % 2 == 0:
            xs, h2a, meta, cnt = _even_layer(
                xs, mod, n1, n2, cp_w_in[i].astype(BF16), conv_w[i], pool_w[i].astype(BF16),
                pool_scale[i].reshape(1, 4 * POOL_G), cp_w_out[i].astype(BF16), wr_hi, wr_lo, br)
        else:
            ukv = mla_w_ukv[i].reshape(KV_LORA, MLA_HEADS, MLA_NOPE + MLA_V)
            wuk = _pad_heads(ukv[:, :, :MLA_NOPE].reshape(KV_LORA, MLA_HEADS * MLA_NOPE), MLA_HEADS, MLA_NOPE)
            wuv = ukv[:, :, MLA_NOPE:].reshape(KV_LORA, MLA_HEADS * MLA_V)
            wuq = _uq_weights(mla_w_uq[i])
            gmq, gmqs = _rope_gains(mla_q_g[i])
            gmk, gmks = _rope_gains(mla_k_g[i])
            qs, ks, vs, qm, km, vm = _proj_layer(
                xs, mod, n1, _odd_weights(at_w_in[i]), _pad_gain(swa_q_g[i]), _pad_gain(swa_k_g[i]),
                mla_q_norm_g[i].reshape(1, Q_LORA), mla_kv_norm_g[i].reshape(1, KV_LORA),
                wuq.astype(BF16), wuk.astype(BF16), wuv.astype(BF16),
                gmq, gmqs, gmk, gmks, tc, ts1, ts2)
            o_s = _swa_layer(swa_sinks[i], qs, ks, vs)
            o_m = _mla_layer(qm, km, vm)
            xs, h2a, meta, cnt = _post_layer(xs, o_s, o_m, mod, n2, at_w_out[i].astype(BF16), wr_hi, wr_lo, br)
        xs, hs = _moe_layer(xs, h2a, meta, cnt, mod, moe_w_gate, moe_w_up, moe_w_down, l, hs)
    return xs.reshape(1, SEQ, D)
```

```python
import functools

import numpy as np
import jax
import jax.numpy as jnp
from jax import lax
from jax.experimental import pallas as pl
from jax.experimental.pallas import tpu as pltpu

F32 = jnp.float32
BF16 = jnp.bfloat16

D = 1024
SEQ = 16384
DEPTH = 4
EPS = 1e-6
LANES = 128
CONV_CH = 512
POOL_WINDOWS = (2, 4, 8, 16)
POOL_G = 128
HALO = 16
SWA_HEADS = 8
SWA_KV = 2
HEAD_DIM = 64
WINDOW = 128
MLA_HEADS = 8
MLA_NOPE = 64
MLA_ROPE = 32
MLA_QK = MLA_NOPE + MLA_ROPE
MLA_V = 64
Q_LORA = 384
KV_LORA = 256
ROPE_THETA = 10000.0
N_GROUPS = 4
EPG = 8
N_EXPERTS = N_GROUPS * EPG
FF = 256
NEG = -1e30
LOG2E = 1.4426950408889634

T_TOK = 512
T_ATT = 512
T_MOE = 512
T_DISP = 1024
ROW_W = D + LANES
N_SORT = SEQ + N_GROUPS * T_MOE
N_TILES = N_SORT // T_MOE
E_STEP = 8
T_MQ = 1024
T_MK = 512
MLA_SPLIT = 1
MLA_DEPTH = 3
VMEM_LIMIT = 48 * 1024 * 1024
MOE_VMEM_LIMIT = 56 * 1024 * 1024

O_QS, O_KS, O_VS, O_CQ, O_CKV, O_KR, ODD_W = 0, 1024, 1280, 1536, 1920, 2176, 2432


def _cparams(n_axes=1):
    return pltpu.CompilerParams(dimension_semantics=("arbitrary",) * n_axes,
                                vmem_limit_bytes=VMEM_LIMIT)


def _rms(x):
    return x * lax.rsqrt(jnp.mean(x * x, axis=-1, keepdims=True) + EPS)


def _dot(a, b):
    return jnp.dot(a, b, preferred_element_type=F32)


def _dot_nt(a, b):
    return lax.dot_general(a, b, (((1,), (1,)), ((), ())), preferred_element_type=F32)


def _ada_kernel(c_ref, w_ref, b_ref, o_ref):
    c = c_ref[...]
    ca = c * jax.nn.sigmoid(c)
    o_ref[0] = jnp.sum(w_ref[0] * ca, axis=0, keepdims=True) + b_ref[0]


def _ada_mod(c, ada_w, ada_b):
    c_col = c.reshape(D, 1)
    b = ada_b.reshape(DEPTH * 6, 1, D)
    out = pl.pallas_call(
        _ada_kernel,
        out_shape=jax.ShapeDtypeStruct((DEPTH * 6, 1, D), F32),
        grid=(DEPTH, 6),
        in_specs=[pl.BlockSpec((D, 1), lambda l, j: (0, 0)),
                  pl.BlockSpec((1, D, D), lambda l, j: (l, 0, j)),
                  pl.BlockSpec((1, 1, D), lambda l, j: (l * 6 + j, 0, 0))],
        out_specs=pl.BlockSpec((1, 1, D), lambda l, j: (l * 6 + j, 0, 0)),
        compiler_params=_cparams(2),
        name="ada_mod",
    )(c_col, ada_w, b)
    mod = out.reshape(DEPTH, 6, D)
    return jnp.pad(mod, ((0, 0), (0, 2), (0, 0)))


def _rope_kernel(pos_ref, inv_ref, c_ref, s1_ref, s2_ref):
    pos = pos_ref[...].astype(F32)
    ang = pos * inv_ref[...]
    lane = lax.broadcasted_iota(jnp.int32, ang.shape, 1)
    cs = jnp.cos(ang)
    sn = jnp.sin(ang)
    c_ref[...] = jnp.where(lane < 64, 1.0, jnp.where(lane < 96, cs, 0.0))
    s1_ref[...] = jnp.where((lane >= 64) & (lane < 80), -sn, 0.0)
    s2_ref[...] = jnp.where((lane >= 80) & (lane < 96), sn, 0.0)


def _rope_tables(positions):
    half = MLA_ROPE // 2
    inv = jnp.power(ROPE_THETA, -jnp.arange(half, dtype=F32) / half)
    inv_lane = jnp.concatenate([jnp.zeros((64,), F32), inv, inv, jnp.zeros((32,), F32)]).reshape(1, LANES)
    pos = positions.reshape(SEQ, 1)
    shp = jax.ShapeDtypeStruct((SEQ, LANES), F32)
    spec = pl.BlockSpec((T_TOK, LANES), lambda i: (i, 0))
    return pl.pallas_call(
        _rope_kernel,
        out_shape=(shp, shp, shp),
        grid=(SEQ // T_TOK,),
        in_specs=[pl.BlockSpec((T_TOK, 1), lambda i: (i, 0)),
                  pl.BlockSpec((1, LANES), lambda i: (0, 0))],
        out_specs=(spec, spec, spec),
        compiler_params=_cparams(1),
        name="rope_tables",
    )(pos, inv_lane)


def _route(lg):
    lane = lax.broadcasted_iota(jnp.int32, lg.shape, 1)
    lane_f = lane.astype(F32)
    is_g = (lane >= N_EXPERTS) & (lane < N_EXPERTS + N_GROUPS)
    gl = jnp.where(is_g, lg, NEG)
    gmax = jnp.max(gl, axis=-1, keepdims=True)
    gidx = jnp.min(jnp.where(is_g & (gl == gmax), lane_f - N_EXPERTS, 1e3), axis=-1, keepdims=True)
    gsum = jnp.sum(jnp.where(is_g, jnp.exp(gl - gmax), 0.0), axis=-1, keepdims=True)
    gw = 1.0 / gsum
    grp_of_lane = (lane >> 3).astype(F32)
    in_grp = (lane < N_EXPERTS) & (grp_of_lane == gidx)
    el = jnp.where(in_grp, lg, NEG)
    m1 = jnp.max(el, axis=-1, keepdims=True)
    i1 = jnp.min(jnp.where(in_grp & (el == m1), lane_f, 1e3), axis=-1, keepdims=True)
    rest = in_grp & (lane_f != i1)
    el2 = jnp.where(rest, lg, NEG)
    m2 = jnp.max(el2, axis=-1, keepdims=True)
    i2 = jnp.min(jnp.where(rest & (el2 == m2), lane_f, 1e3), axis=-1, keepdims=True)
    r = jnp.exp(m2 - m1)
    w1 = gw / (1.0 + r)
    w2 = w1 * r
    return jnp.where(lane_f == i1, w1, jnp.where(lane_f == i2, w2, 0.0)), gidx


def _tail(x, y, mod, n2, wr_hi, wr_lo, br, xo_ref, h2_ref, meta_ref, cnt_ref, carry_ref):
    i = pl.program_id(0)

    @pl.when(i == 0)
    def _():
        carry_ref[...] = jnp.zeros(carry_ref.shape, F32)

    gate1 = mod[2:3]
    shift2, scale2 = mod[3:4], mod[4:5]
    xn = x + gate1 * y
    xo_ref[...] = xn
    h2 = _rms(xn) * n2 * (1.0 + scale2) + shift2
    hi = h2.astype(BF16)
    lo = (h2 - hi.astype(F32)).astype(BF16)
    lg = _dot(hi, wr_hi) + _dot(lo, wr_hi) + _dot(hi, wr_lo) + br
    gates, gidx = _route(lg)
    rows = lg.shape[0]
    lane = lax.broadcasted_iota(jnp.int32, (rows, LANES), 1)
    lane_f = lane.astype(F32)
    g8 = gates
    for g in range(1, N_GROUPS):
        g8 = g8 + pltpu.roll(gates, LANES - EPG * g, 1)
    onehot = (lane_f == gidx).astype(F32)
    r = lax.broadcasted_iota(jnp.int32, (rows, rows), 0)
    c = lax.broadcasted_iota(jnp.int32, (rows, rows), 1)
    before = jnp.where(c < r, 1.0, 0.0).astype(BF16)
    cum = _dot(before, onehot.astype(BF16)) + carry_ref[0:1, :]
    rank = jnp.sum(jnp.where(lane_f == gidx, cum, 0.0), axis=-1, keepdims=True)
    meta = jnp.where(lane < EPG, g8, jnp.where(lane == EPG, gidx, jnp.where(lane == EPG + 1, rank, 0.0)))
    h2_ref[:, 0:D] = h2
    h2_ref[:, D:D + LANES] = meta
    meta_ref[...] = meta
    total = carry_ref[0:1, :] + jnp.sum(onehot, axis=0, keepdims=True)
    carry_ref[...] = jnp.broadcast_to(total, carry_ref.shape)
    cnt_ref[...] = jnp.broadcast_to(total, cnt_ref.shape)


def _even_kernel(x_ref, xh_ref, mod_ref, n1_ref, n2_ref, win_ref, cw_ref, pw_ref, ps_ref,
                 wout_ref, wrh_ref, wrl_ref, br_ref, xo_ref, h2_ref, meta_ref, cnt_ref, carry_ref):
    i = pl.program_id(0)
    x = x_ref[...]
    mod = mod_ref[...]
    shift1, scale1 = mod[0:1], mod[1:2]
    xa = jnp.concatenate([xh_ref[...], x], axis=0)
    h = _rms(xa) * n1_ref[...] * (1.0 + scale1) + shift1
    z = _dot(h.astype(BF16), win_ref[...])
    rows = T_TOK + HALO
    row = lax.broadcasted_iota(jnp.int32, (rows, 1), 0)
    tpos = i * T_TOK + row - HALO
    live = (tpos >= 0).astype(F32)
    bg = z[:, 0:CONV_CH]
    v = z[:, CONV_CH:2 * CONV_CH] * z[:, 2 * CONV_CH:3 * CONV_CH] * live
    cw = cw_ref[...]
    conv = v * cw[0:1] + pltpu.roll(v, 1, 0) * cw[1:2] + pltpu.roll(v, 2, 0) * cw[2:3]
    parts = [(bg * conv)[HALO:]]
    ps = ps_ref[...]
    tcount = (tpos + 1).astype(F32)
    for gi, w in enumerate(POOL_WINDOWS):
        ug = z[:, 3 * CONV_CH + gi * POOL_G: 3 * CONV_CH + (gi + 1) * POOL_G] * live
        s = ug
        k = 1
        while k < w:
            s = s + pltpu.roll(s, k, 0)
            k *= 2
        inv = 1.0 / jnp.minimum(tcount, float(w))
        d = (s * inv - ug)[HALO:].astype(BF16)
        parts.append(_dot(d, pw_ref[gi]) * ps[:, gi * POOL_G:(gi + 1) * POOL_G])
    cat = jnp.concatenate(parts, axis=-1).astype(BF16)
    y = _dot(cat, wout_ref[...])
    _tail(x, y, mod, n2_ref[...], wrh_ref[...], wrl_ref[...], br_ref[...], xo_ref, h2_ref, meta_ref,
          cnt_ref, carry_ref)


def _tail_out(n):
    shapes = (jax.ShapeDtypeStruct((SEQ, D), F32), jax.ShapeDtypeStruct((SEQ, ROW_W), F32),
              jax.ShapeDtypeStruct((SEQ, LANES), F32), jax.ShapeDtypeStruct((8, LANES), F32))
    specs = (pl.BlockSpec((n, D), lambda i: (i, 0)), pl.BlockSpec((n, ROW_W), lambda i: (i, 0)),
             pl.BlockSpec((n, LANES), lambda i: (i, 0)), pl.BlockSpec((8, LANES), lambda i: (0, 0)))
    scratch = [pltpu.VMEM((8, LANES), F32)]
    return shapes, specs, scratch


def _full(shape):
    nd = len(shape)
    return pl.BlockSpec(shape, lambda i: (0,) * nd)


def _even_layer(x, mod, n1, n2, w_in, conv_w, pool_w, pool_scale, w_out, wr_hi, wr_lo, br):
    shapes, specs, scratch = _tail_out(T_TOK)
    hb = T_TOK // HALO
    return pl.pallas_call(
        _even_kernel,
        out_shape=shapes,
        grid=(SEQ // T_TOK,),
        in_specs=[pl.BlockSpec((T_TOK, D), lambda i: (i, 0)),
                  pl.BlockSpec((HALO, D), lambda i: (jnp.maximum(i * hb - 1, 0), 0)),
                  _full((8, D)), _full((1, D)), _full((1, D)),
                  _full((D, 4 * CONV_CH)), _full((3, CONV_CH)), _full((4, POOL_G, POOL_G)),
                  _full((1, 4 * POOL_G)), _full((D, D)),
                  _full((D, LANES)), _full((D, LANES)), _full((1, LANES))],
        out_specs=specs,
        scratch_shapes=scratch,
        compiler_params=_cparams(1),
        name="even_mixer",
    )(x, x, mod, n1, n2, w_in, conv_w, pool_w, pool_scale, w_out, wr_hi, wr_lo, br)


def _proj_kernel(x_ref, mod_ref, n1_ref, win_ref, gsq_ref, gsk_ref, gqn_ref, gkvn_ref,
                 wuq_ref, wuk_ref, wuv_ref, gmq_ref, gmqs_ref, gmk_ref, gmks_ref, c_ref, s1_ref, s2_ref,
                 qs_ref, ks_ref, vs_ref, qm_ref, km_ref, vm_ref):
    x = x_ref[...]
    mod = mod_ref[...]
    shift1, scale1 = mod[0:1], mod[1:2]
    h = _rms(x) * n1_ref[...] * (1.0 + scale1) + shift1
    z = _dot(h.astype(BF16), win_ref[...])

    def head_norm(t, g, dim):
        ms = jnp.sum(t * t, axis=-1, keepdims=True) * (1.0 / dim)
        return t * lax.rsqrt(ms + EPS) * g

    gsq, gsk = gsq_ref[...] * (HEAD_DIM ** -0.5), gsk_ref[...]
    for hd in range(SWA_HEADS):
        qh = head_norm(z[:, O_QS + hd * LANES: O_QS + (hd + 1) * LANES], gsq, HEAD_DIM)
        qs_ref[:, hd * LANES:(hd + 1) * LANES] = qh.astype(BF16)
    for kv in range(SWA_KV):
        kh = head_norm(z[:, O_KS + kv * LANES: O_KS + (kv + 1) * LANES], gsk, HEAD_DIM)
        ks_ref[:, kv * LANES:(kv + 1) * LANES] = kh.astype(BF16)
    vs_ref[...] = z[:, O_VS:O_CQ].astype(BF16)

    cq = (_rms(z[:, O_CQ:O_CKV]) * gqn_ref[...]).astype(BF16)
    ckv = (_rms(z[:, O_CKV:O_KR]) * gkvn_ref[...]).astype(BF16)
    qm = _dot(cq, wuq_ref[...])
    kn = _dot(ckv, wuk_ref[...])
    vm = _dot(ckv, wuv_ref[...])
    lane = lax.broadcasted_iota(jnp.int32, (T_TOK, LANES), 1)
    for j in range(MLA_HEADS // 2):
        vv = vm[:, j * LANES:(j + 1) * LANES]
        even = jnp.where(lane < 64, vv, jnp.where(lane == 64, 1.0, 0.0))
        odd = jnp.where(lane < 64, jnp.where(lane == 0, 1.0, 0.0), vv)
        vm_ref[:, (2 * j) * LANES:(2 * j + 1) * LANES] = even.astype(BF16)
        vm_ref[:, (2 * j + 1) * LANES:(2 * j + 2) * LANES] = odd.astype(BF16)
    kr = z[:, O_KR:O_KR + LANES]
    krs = z[:, O_KR + LANES:ODD_W]
    cs = c_ref[...]
    sp = s1_ref[...] + s2_ref[...]
    qscale = MLA_QK ** -0.5 * LOG2E
    gcq = cs * (gmq_ref[...] * qscale)
    gsq_r = sp * (gmqs_ref[...] * qscale)
    gck = cs * gmk_ref[...]
    krot = krs * (sp * gmks_ref[...])

    def inv_rms(t):
        return lax.rsqrt(jnp.sum(t * t, axis=-1, keepdims=True) * (1.0 / MLA_QK) + EPS)

    for hd in range(MLA_HEADS):
        sl = slice(hd * LANES, (hd + 1) * LANES)
        sw = slice((MLA_HEADS + hd) * LANES, (MLA_HEADS + hd + 1) * LANES)
        qh = qm[:, sl]
        qm_ref[:, sl] = ((qh * gcq + qm[:, sw] * gsq_r) * inv_rms(qh)).astype(BF16)
        kh = kn[:, sl] + kr
        km_ref[:, sl] = ((kh * gck + krot) * inv_rms(kh)).astype(BF16)


def _proj_layer(x, mod, n1, w_in, gsq, gsk, gqn, gkvn, wuq, wuk, wuv, gmq, gmqs, gmk, gmks, tc, ts1, ts2):
    def tok(wd):
        return pl.BlockSpec((T_TOK, wd), lambda i: (i, 0))
    widths = (1024, 256, 256, 1024, 1024, 1024)
    return pl.pallas_call(
        _proj_kernel,
        out_shape=tuple(jax.ShapeDtypeStruct((SEQ, wd), BF16) for wd in widths),
        grid=(SEQ // T_TOK,),
        in_specs=[tok(D), _full((8, D)), _full((1, D)), _full((D, ODD_W)),
                  _full((1, LANES)), _full((1, LANES)), _full((1, Q_LORA)), _full((1, KV_LORA)),
                  _full((Q_LORA, 2048)), _full((KV_LORA, 1024)), _full((KV_LORA, 512)),
                  _full((1, LANES)), _full((1, LANES)), _full((1, LANES)), _full((1, LANES)),
                  tok(LANES), tok(LANES), tok(LANES)],
        out_specs=tuple(tok(wd) for wd in widths),
        compiler_params=_cparams(1),
        name="odd_proj",
    )(x, mod, n1, w_in, gsq, gsk, gqn, gkvn, wuq, wuk, wuv, gmq, gmqs, gmk, gmks, tc, ts1, ts2)


def _swa_kernel(sink_ref, q_ref, k_ref, kh_ref, v_ref, vh_ref, o_ref):
    i = pl.program_id(0)
    kcat = jnp.concatenate([kh_ref[...], k_ref[...]], axis=0)
    vcat = jnp.concatenate([vh_ref[...], v_ref[...]], axis=0)
    grp = SWA_HEADS // SWA_KV
    r = lax.broadcasted_iota(jnp.int32, (grp * WINDOW, 2 * WINDOW), 0) & (WINDOW - 1)
    c = lax.broadcasted_iota(jnp.int32, (grp * WINDOW, 2 * WINDOW), 1)
    rel = WINDOW + r - c
    lane = lax.broadcasted_iota(jnp.int32, (WINDOW, LANES), 1)
    for sb in range(T_ATT // WINDOW):
        rows = slice(sb * WINDOW, (sb + 1) * WINDOW)
        kb = kcat[sb * WINDOW: sb * WINDOW + 2 * WINDOW]
        vb = vcat[sb * WINDOW: sb * WINDOW + 2 * WINDOW]
        kpos = i * T_ATT + (sb - 1) * WINDOW + c
        ok = (rel >= 0) & (rel < WINDOW) & (kpos >= 0)
        outs = []
        for kv in range(SWA_KV):
            q = jnp.concatenate([q_ref[rows, (kv * grp + g) * LANES:(kv * grp + g + 1) * LANES]
                                 for g in range(grp)], axis=0)
            sink = jnp.concatenate([jnp.full((WINDOW, 1), sink_ref[kv * grp + g], F32)
                                    for g in range(grp)], axis=0)
            s = jnp.where(ok, _dot_nt(q, kb[:, kv * LANES:(kv + 1) * LANES]), NEG)
            m = jnp.maximum(jnp.max(s, axis=-1, keepdims=True), sink)
            e = jnp.exp(s - m)
            den = jnp.sum(e, axis=-1, keepdims=True) + jnp.exp(sink - m)
            p = (e * (1.0 / den)).astype(BF16)
            o = _dot(p, vb[:, kv * LANES:(kv + 1) * LANES])
            outs += [o[g * WINDOW:(g + 1) * WINDOW] for g in range(grp)]
        for j in range(SWA_HEADS // 2):
            o_ref[rows, j * LANES:(j + 1) * LANES] = jnp.where(lane < 64, outs[2 * j], outs[2 * j + 1]).astype(BF16)


def _swa_layer(sinks, qs, ks, vs):
    hb = T_ATT // WINDOW
    return pl.pallas_call(
        _swa_kernel,
        out_shape=jax.ShapeDtypeStruct((SEQ, 512), BF16),
        grid=(SEQ // T_ATT,),
        in_specs=[pl.BlockSpec(memory_space=pltpu.SMEM),
                  pl.BlockSpec((T_ATT, 1024), lambda i: (i, 0)),
                  pl.BlockSpec((T_ATT, 256), lambda i: (i, 0)),
                  pl.BlockSpec((WINDOW, 256), lambda i: (jnp.maximum(i * hb - 1, 0), 0)),
                  pl.BlockSpec((T_ATT, 256), lambda i: (i, 0)),
                  pl.BlockSpec((WINDOW, 256), lambda i: (jnp.maximum(i * hb - 1, 0), 0))],
        out_specs=pl.BlockSpec((T_ATT, 512), lambda i: (i, 0)),
        compiler_params=_cparams(1),
        name="swa_attn",
    )(sinks, qs, ks, ks, vs, vs)


def _mla_kernel(qi_ref, ki_ref, q_ref, k_ref, v_ref, o_ref, m_ref, acc_ref):
    step = pl.program_id(0)
    qi = qi_ref[step]
    ki = ki_ref[step]

    @pl.when(ki == 0)
    def _():
        m_ref[...] = jnp.full(m_ref.shape, NEG, F32)
        acc_ref[...] = jnp.zeros(acc_ref.shape, F32)

    lane = lax.broadcasted_iota(jnp.int32, (T_MQ, LANES), 1)
    lo = lane < 64
    nc = T_MK // LANES
    sub = T_MQ // MLA_SPLIT
    ratio = T_MQ // T_MK
    units = [(hd, part) for hd in range(MLA_HEADS) for part in range(MLA_SPLIT)]

    def scores(unit):
        hd, part = unit
        sl = slice(hd * LANES, (hd + 1) * LANES)
        return _dot_nt(q_ref[part * sub:(part + 1) * sub, sl], k_ref[:, sl])

    def update(masked):
        if masked:
            r = lax.broadcasted_iota(jnp.int32, (sub, LANES), 0)
            lane_s = lax.broadcasted_iota(jnp.int32, (sub, LANES), 1)

        def softmax_part(unit, s):
            hd, part = unit
            rows = slice(part * sub, (part + 1) * sub)
            cols = [s[:, c * LANES:(c + 1) * LANES] for c in range(nc)]
            if masked:
                off = qi * T_MQ - ki * T_MK + part * sub
                cols = [jnp.where(r + off >= lane_s + c * LANES, cols[c], NEG) for c in range(nc)]
            cmax = cols[0]
            for c in range(1, nc):
                cmax = jnp.maximum(cmax, cols[c])
            m_prev = m_ref[hd, rows]
            m_new = jnp.maximum(m_prev, jnp.max(cmax, axis=-1, keepdims=True))
            m_ref[hd, rows] = m_new
            alpha = jnp.exp2(m_prev - m_new)
            p = jnp.concatenate([jnp.exp2(cols[c] - m_new).astype(BF16) for c in range(nc)], axis=-1)
            return p, alpha

        def value_part(unit, p, alpha):
            hd, part = unit
            rows = slice(part * sub, (part + 1) * sub)
            acc_ref[hd, rows] = acc_ref[hd, rows] * alpha + _dot(p, v_ref[:, hd * LANES:(hd + 1) * LANES])

        s_q = [scores(u) for u in units[:MLA_DEPTH]]
        pend = None
        for n, u in enumerate(units):
            if n + MLA_DEPTH < len(units):
                s_q.append(scores(units[n + MLA_DEPTH]))
            cur = softmax_part(u, s_q[n])
            if pend is not None:
                value_part(units[n - 1], *pend)
            pend = cur
        value_part(units[-1], *pend)

    @pl.when(ki < qi * ratio)
    def _():
        update(False)

    @pl.when(ki >= qi * ratio)
    def _():
        update(True)

    @pl.when(ki == qi * ratio + ratio - 1)
    def _():
        for j in range(MLA_HEADS // 2):
            ae = acc_ref[2 * j]
            ao = acc_ref[2 * j + 1]
            out = jnp.where(lo, ae * (1.0 / ae[:, 64:65]), ao * (1.0 / ao[:, 0:1]))
            o_ref[:, j * LANES:(j + 1) * LANES] = out.astype(BF16)


def _mla_layer(qm, km, vm):
    nb = SEQ // T_MQ
    ratio = T_MQ // T_MK
    qi = np.concatenate([np.full(ratio * (n + 1), n, np.int32) for n in range(nb)])
    ki = np.concatenate([np.arange(ratio * (n + 1), dtype=np.int32) for n in range(nb)])
    grid_spec = pltpu.PrefetchScalarGridSpec(
        num_scalar_prefetch=2,
        grid=(int(qi.shape[0]),),
        in_specs=[pl.BlockSpec((T_MQ, 1024), lambda s, qi, ki: (qi[s], 0)),
                  pl.BlockSpec((T_MK, 1024), lambda s, qi, ki: (ki[s], 0)),
                  pl.BlockSpec((T_MK, 1024), lambda s, qi, ki: (ki[s], 0))],
        out_specs=pl.BlockSpec((T_MQ, 512), lambda s, qi, ki: (qi[s], 0)),
        scratch_shapes=[pltpu.VMEM((MLA_HEADS, T_MQ, LANES), F32),
                        pltpu.VMEM((MLA_HEADS, T_MQ, LANES), F32)],
    )
    return pl.pallas_call(
        _mla_kernel,
        out_shape=jax.ShapeDtypeStruct((SEQ, 512), BF16),
        grid_spec=grid_spec,
        compiler_params=_cparams(1),
        name="mla_attn",
    )(jnp.asarray(qi), jnp.asarray(ki), qm, km, vm)


def _post_kernel(x_ref, os_ref, om_ref, mod_ref, n2_ref, wout_ref, wrh_ref, wrl_ref, br_ref,
                 xo_ref, h2_ref, meta_ref, cnt_ref, carry_ref):
    y = _dot(os_ref[...], wout_ref[0:512, :]) + _dot(om_ref[...], wout_ref[512:1024, :])
    _tail(x_ref[...], y, mod_ref[...], n2_ref[...], wrh_ref[...], wrl_ref[...], br_ref[...],
          xo_ref, h2_ref, meta_ref, cnt_ref, carry_ref)


def _post_layer(x, o_s, o_m, mod, n2, w_out, wr_hi, wr_lo, br):
    shapes, specs, scratch = _tail_out(T_TOK)
    return pl.pallas_call(
        _post_kernel,
        out_shape=shapes,
        grid=(SEQ // T_TOK,),
        in_specs=[pl.BlockSpec((T_TOK, D), lambda i: (i, 0)),
                  pl.BlockSpec((T_TOK, 512), lambda i: (i, 0)),
                  pl.BlockSpec((T_TOK, 512), lambda i: (i, 0)),
                  _full((8, D)), _full((1, D)), _full((D, D)),
                  _full((D, LANES)), _full((D, LANES)), _full((1, LANES))],
        out_specs=specs,
        scratch_shapes=scratch,
        compiler_params=_cparams(1),
        name="odd_post",
    )(x, o_s, o_m, mod, n2, w_out, wr_hi, wr_lo, br)


def _dispatch_plan(meta, cnt):
    grp = meta[:, EPG].astype(jnp.int32)
    rank = meta[:, EPG + 1].astype(jnp.int32)
    counts = cnt[0, :N_GROUPS].astype(jnp.int32)
    padded = ((counts + T_MOE - 1) // T_MOE) * T_MOE
    ends = jnp.cumsum(padded)
    pos = (ends - padded)[grp] + rank
    n_used = ends[-1] // T_MOE
    tile_start = jnp.arange(N_TILES, dtype=jnp.int32) * T_MOE
    tile_group = jnp.minimum(jnp.sum(tile_start[:, None] >= ends[None, :], axis=1), N_GROUPS - 1)
    return pos.reshape(SEQ // T_DISP, 1, T_DISP), tile_group.astype(jnp.int32), n_used.reshape(1)


def _row_copies(n, src_at, dst_at, sem):
    for r in range(n):
        pltpu.make_async_copy(src_at(r), dst_at(r), sem).start(priority=r % 2)


def _disp_kernel(pos_ref, x_ref, init_ref, o_ref, sem):
    del init_ref
    _row_copies(T_DISP,
                lambda r: x_ref.at[pl.ds(r, 1), :],
                lambda r: o_ref.at[pl.ds(pos_ref[0, 0, r], 1), :], sem)
    pltpu.make_async_copy(x_ref, o_ref.at[pl.ds(0, T_DISP), :], sem).wait()


def _dispatch(pos, h2a, hs):
    return pl.pallas_call(
        _disp_kernel,
        out_shape=jax.ShapeDtypeStruct((N_SORT, ROW_W), F32),
        grid=(SEQ // T_DISP,),
        in_specs=[pl.BlockSpec((1, 1, T_DISP), lambda i: (i, 0, 0), memory_space=pltpu.SMEM),
                  pl.BlockSpec((T_DISP, ROW_W), lambda i: (i, 0)),
                  pl.BlockSpec(memory_space=pl.ANY)],
        out_specs=pl.BlockSpec(memory_space=pl.ANY),
        scratch_shapes=[pltpu.SemaphoreType.DMA(())],
        input_output_aliases={2: 0},
        compiler_params=_cparams(1),
        name="moe_dispatch",
    )(pos, h2a, hs)


def _moe_kernel(tg_ref, nu_ref, x_ref, wg_ref, wu_ref, wd_ref, o_ref, xb_ref, act_ref):
    del tg_ref
    i = pl.program_id(0)
    j = pl.program_id(1)
    last = EPG // E_STEP - 1

    @pl.when(i < nu_ref[0])
    def _():
        @pl.when(j == 0)
        def _():
            xb_ref[...] = x_ref[:, 0:D].astype(BF16)

        xb = xb_ref[...]
        meta = x_ref[:, D:ROW_W]
        lane = lax.broadcasted_iota(jnp.int32, meta.shape, 1)
        for k in range(E_STEP):
            a = _dot(xb, wg_ref[0, k].astype(BF16))
            u = _dot(xb, wu_ref[0, k].astype(BF16))
            gate = jnp.sum(jnp.where(lane == j * E_STEP + k, meta, 0.0), axis=-1, keepdims=True)
            act = (a * jax.nn.sigmoid(a) * u * gate).astype(BF16)
            for jj in range(EPG // E_STEP):
                @pl.when(j == jj)
                def _(jj=jj, k=k, act=act):
                    act_ref[:, (jj * E_STEP + k) * FF:(jj * E_STEP + k + 1) * FF] = act

        @pl.when(j == last)
        def _():
            o_ref[...] = _dot(act_ref[...], wd_ref[0, 0].astype(BF16))

    @pl.when((i >= nu_ref[0]) & (j == last))
    def _():
        o_ref[...] = jnp.zeros(o_ref.shape, F32)


def _moe_experts(tile_group, n_used, hs, wg, wu, wd, layer):
    def tile(i, e, tg, nu):
        return (jnp.minimum(i, nu[0] - 1), 0)

    nj = EPG // E_STEP

    def expert(i, j, tg, nu):
        return (layer, jnp.where(i < nu[0], tg[i] * nj + j, tg[nu[0] - 1] * nj + nj - 1), 0, 0)

    def group(i, j, tg, nu):
        return (layer, tg[jnp.minimum(i, nu[0] - 1)], 0, 0)

    grid_spec = pltpu.PrefetchScalarGridSpec(
        num_scalar_prefetch=2,
        grid=(N_TILES, nj),
        in_specs=[pl.BlockSpec((T_MOE, ROW_W), tile),
                  pl.BlockSpec((1, E_STEP, D, FF), expert),
                  pl.BlockSpec((1, E_STEP, D, FF), expert),
                  pl.BlockSpec((1, 1, EPG * FF, D), group, pipeline_mode=pl.Buffered(1))],
        out_specs=pl.BlockSpec((T_MOE, D), lambda i, e, tg, nu: (i, 0)),
        scratch_shapes=[pltpu.VMEM((T_MOE, D), BF16), pltpu.VMEM((T_MOE, EPG * FF), BF16)],
    )
    return pl.pallas_call(
        _moe_kernel,
        out_shape=jax.ShapeDtypeStruct((N_SORT, D), F32),
        grid_spec=grid_spec,
        compiler_params=pltpu.CompilerParams(dimension_semantics=("arbitrary", "arbitrary"),
                                             vmem_limit_bytes=MOE_VMEM_LIMIT),
        name="moe_experts",
    )(tile_group, n_used, hs, wg, wu, wd.reshape(DEPTH, N_GROUPS, EPG * FF, D))


def _comb_kernel(pos_ref, x_ref, mod_ref, y_ref, o_ref, buf_ref, sem):
    _row_copies(T_DISP,
                lambda r: y_ref.at[pl.ds(pos_ref[0, 0, r], 1), :],
                lambda r: buf_ref.at[pl.ds(r, 1), :], sem)
    pltpu.make_async_copy(y_ref.at[pl.ds(0, T_DISP), :], buf_ref, sem).wait()
    o_ref[...] = x_ref[...] + mod_ref[5:6, :] * buf_ref[...]


def _combine(pos, x, mod, ys):
    return pl.pallas_call(
        _comb_kernel,
        out_shape=jax.ShapeDtypeStruct((SEQ, D), F32),
        grid=(SEQ // T_DISP,),
        in_specs=[pl.BlockSpec((1, 1, T_DISP), lambda i: (i, 0, 0), memory_space=pltpu.SMEM),
                  pl.BlockSpec((T_DISP, D), lambda i: (i, 0)),
                  pl.BlockSpec((8, D), lambda i: (0, 0)),
                  pl.BlockSpec(memory_space=pl.ANY)],
        out_specs=pl.BlockSpec((T_DISP, D), lambda i: (i, 0)),
        scratch_shapes=[pltpu.VMEM((T_DISP, D), F32), pltpu.SemaphoreType.DMA(())],
        compiler_params=_cparams(1),
        name="moe_combine",
    )(pos, x, mod, ys)


def _moe_layer(x, h2a, meta, cnt, mod, wg, wu, wd, layer, hs):
    pos, tile_group, n_used = _dispatch_plan(meta, cnt)
    hs = _dispatch(pos, h2a, hs)
    ys = _moe_experts(tile_group, n_used, hs, wg, wu, wd, layer)
    return _combine(pos, x, mod, ys), hs


def _pad_heads(w, heads, dim):
    k = w.shape[0]
    w = w.reshape(k, heads, dim)
    return jnp.pad(w, ((0, 0), (0, 0), (0, LANES - dim))).reshape(k, heads * LANES)


def _pad_gain(g):
    return jnp.pad(g, (0, LANES - g.shape[0])).reshape(1, LANES)


def _odd_weights(w_in):
    q_s = _pad_heads(w_in[:, 0:512], SWA_HEADS, HEAD_DIM)
    k_s = _pad_heads(w_in[:, 512:640], SWA_KV, HEAD_DIM)
    v = w_in[:, 640:768]
    v_s = jnp.concatenate([v[:, 0:64], v[:, 0:64], v[:, 64:128], v[:, 64:128]], axis=1)
    c_q = w_in[:, 768:1152]
    c_kv = w_in[:, 1152:1408]
    k_r = jnp.pad(w_in[:, 1408:1440], ((0, 0), (MLA_NOPE, LANES - MLA_QK)))
    k_r_sw = jnp.pad(_swap_halves(w_in[:, 1408:1440]), ((0, 0), (MLA_NOPE, LANES - MLA_QK)))
    return jnp.concatenate([q_s, k_s, v_s, c_q, c_kv, k_r, k_r_sw], axis=1).astype(BF16)


def _swap_halves(t):
    half = t.shape[-1] // 2
    return jnp.concatenate([t[..., half:], t[..., :half]], axis=-1)


def _uq_weights(w_uq):
    w = w_uq.reshape(Q_LORA, MLA_HEADS, MLA_QK)
    plain = jnp.pad(w, ((0, 0), (0, 0), (0, LANES - MLA_QK)))
    swapped = jnp.pad(_swap_halves(w[:, :, MLA_NOPE:]), ((0, 0), (0, 0), (MLA_NOPE, LANES - MLA_QK)))
    return jnp.concatenate([plain.reshape(Q_LORA, -1), swapped.reshape(Q_LORA, -1)], axis=1)


def _rope_gains(g):
    plain = jnp.pad(g, (0, LANES - MLA_QK)).reshape(1, LANES)
    partner = jnp.pad(_swap_halves(g[MLA_NOPE:]), (MLA_NOPE, LANES - MLA_QK)).reshape(1, LANES)
    return plain, partner


def _router_weights(w_group, b_group, w_expert, b_expert):
    w = jnp.pad(jnp.concatenate([w_expert, w_group], axis=1), ((0, 0), (0, LANES - N_EXPERTS - N_GROUPS)))
    hi = w.astype(BF16)
    lo = (w - hi.astype(F32)).astype(BF16)
    b = jnp.pad(jnp.concatenate([b_expert, b_group]), (0, LANES - N_EXPERTS - N_GROUPS)).reshape(1, LANES)
    return hi, lo, b


def kernel(x, c, positions, ada_w, ada_b, norm1_g, norm2_g, cp_w_in, conv_w, pool_w, pool_scale,
           cp_w_out, at_w_in, swa_q_g, swa_k_g, swa_sinks, mla_q_norm_g, mla_kv_norm_g, mla_w_uq,
           mla_w_ukv, mla_q_g, mla_k_g, at_w_out, moe_w_group, moe_b_group, moe_w_expert,
           moe_b_expert, moe_w_gate, moe_w_up, moe_w_down):
    xs = x.reshape(SEQ, D)
    mods = _ada_mod(c, ada_w, ada_b)
    tc, ts1, ts2 = _rope_tables(positions)
    hs = jnp.zeros((N_SORT, ROW_W), F32)
    for l in range(DEPTH):
        i = l // 2
        mod = mods[l]
        n1 = norm1_g[l].reshape(1, D)
        n2 = norm2_g[l].reshape(1, D)
        wr_hi, wr_lo, br = _router_weights(moe_w_group[l], moe_b_group[l], moe_w_expert[l], moe_b_expert[l])
        if l % 2 == 0:
            xs, h2a, meta, cnt = _even_layer(
                xs, mod, n1, n2, cp_w_in[i].astype(BF16), conv_w[i], pool_w[i].astype(BF16),
                pool_scale[i].reshape(1, 4 * POOL_G), cp_w_out[i].astype(BF16), wr_hi, wr_lo, br)
        else:
            ukv = mla_w_ukv[i].reshape(KV_LORA, MLA_HEADS, MLA_NOPE + MLA_V)
            wuk = _pad_heads(ukv[:, :, :MLA_NOPE].reshape(KV_LORA, MLA_HEADS * MLA_NOPE), MLA_HEADS, MLA_NOPE)
            wuv = ukv[:, :, MLA_NOPE:].reshape(KV_LORA, MLA_HEADS * MLA_V)
            wuq = _uq_weights(mla_w_uq[i])
            gmq, gmqs = _rope_gains(mla_q_g[i])
            gmk, gmks = _rope_gains(mla_k_g[i])
            qs, ks, vs, qm, km, vm = _proj_layer(
                xs, mod, n1, _odd_weights(at_w_in[i]), _pad_gain(swa_q_g[i]), _pad_gain(swa_k_g[i]),
                mla_q_norm_g[i].reshape(1, Q_LORA), mla_kv_norm_g[i].reshape(1, KV_LORA),
                wuq.astype(BF16), wuk.astype(BF16), wuv.astype(BF16),
                gmq, gmqs, gmk, gmks, tc, ts1, ts2)
            o_s = _swa_layer(swa_sinks[i], qs, ks, vs)
            o_m = _mla_layer(qm, km, vm)
            xs, h2a, meta, cnt = _post_layer(xs, o_s, o_m, mod, n2, at_w_out[i].astype(BF16), wr_hi, wr_lo, br)
        xs, hs = _moe_layer(xs, h2a, meta, cnt, mod, moe_w_gate, moe_w_up, moe_w_down, l, hs)
    return xs.reshape(1, SEQ, D)
```

```python
import functools

import numpy as np
import jax
import jax.numpy as jnp
from jax import lax
from jax.experimental import pallas as pl
from jax.experimental.pallas import tpu as pltpu

F32 = jnp.float32
BF16 = jnp.bfloat16

D = 1024
SEQ = 16384
DEPTH = 4
EPS = 1e-6
LANES = 128
CONV_CH = 512
POOL_WINDOWS = (2, 4, 8, 16)
POOL_G = 128
HALO = 16
SWA_HEADS = 8
SWA_KV = 2
HEAD_DIM = 64
WINDOW = 128
MLA_HEADS = 8
MLA_NOPE = 64
MLA_ROPE = 32
MLA_QK = MLA_NOPE + MLA_ROPE
MLA_V = 64
Q_LORA = 384
KV_LORA = 256
ROPE_THETA = 10000.0
N_GROUPS = 4
EPG = 8
N_EXPERTS = N_GROUPS * EPG
FF = 256
NEG = -1e30
LOG2E = 1.4426950408889634

T_TOK = 512
N_SUB = 2
T_SUB = T_TOK // N_SUB
T_ATT = 512
T_MOE = 512
T_DISP = 1024
ROW_W = D + LANES
N_SORT = SEQ + N_GROUPS * T_MOE
N_TILES = N_SORT // T_MOE
E_STEP = 8
T_MQ = 1024
T_MK = 512
MLA_SPLIT = 1
MLA_DEPTH = 3
VMEM_LIMIT = 48 * 1024 * 1024
MOE_VMEM_LIMIT = 56 * 1024 * 1024

O_QS, O_KS, O_VS, O_CQ, O_CKV, O_KR, ODD_W = 0, 1024, 1280, 1536, 1920, 2176, 2432


def _cparams(n_axes=1):
    return pltpu.CompilerParams(dimension_semantics=("arbitrary",) * n_axes,
                                vmem_limit_bytes=VMEM_LIMIT)


def _rms(x):
    return x * lax.rsqrt(jnp.mean(x * x, axis=-1, keepdims=True) + EPS)


def _dot(a, b):
    return jnp.dot(a, b, preferred_element_type=F32)


def _dot_nt(a, b):
    return lax.dot_general(a, b, (((1,), (1,)), ((), ())), preferred_element_type=F32)


def _ada_kernel(c_ref, w_ref, b_ref, o_ref):
    c = c_ref[...]
    ca = c * jax.nn.sigmoid(c)
    o_ref[0] = jnp.sum(w_ref[0] * ca, axis=0, keepdims=True) + b_ref[0]


def _ada_mod(c, ada_w, ada_b):
    c_col = c.reshape(D, 1)
    b = ada_b.reshape(DEPTH * 6, 1, D)
    out = pl.pallas_call(
        _ada_kernel,
        out_shape=jax.ShapeDtypeStruct((DEPTH * 6, 1, D), F32),
        grid=(DEPTH, 6),
        in_specs=[pl.BlockSpec((D, 1), lambda l, j: (0, 0)),
                  pl.BlockSpec((1, D, D), lambda l, j: (l, 0, j)),
                  pl.BlockSpec((1, 1, D), lambda l, j: (l * 6 + j, 0, 0))],
        out_specs=pl.BlockSpec((1, 1, D), lambda l, j: (l * 6 + j, 0, 0)),
        compiler_params=_cparams(2),
        name="ada_mod",
    )(c_col, ada_w, b)
    mod = out.reshape(DEPTH, 6, D)
    return jnp.pad(mod, ((0, 0), (0, 2), (0, 0)))


def _rope_kernel(pos_ref, inv_ref, c_ref, s1_ref, s2_ref):
    pos = pos_ref[...].astype(F32)
    ang = pos * inv_ref[...]
    lane = lax.broadcasted_iota(jnp.int32, ang.shape, 1)
    cs = jnp.cos(ang)
    sn = jnp.sin(ang)
    c_ref[...] = jnp.where(lane < 64, 1.0, jnp.where(lane < 96, cs, 0.0))
    s1_ref[...] = jnp.where((lane >= 64) & (lane < 80), -sn, 0.0)
    s2_ref[...] = jnp.where((lane >= 80) & (lane < 96), sn, 0.0)


def _rope_tables(positions):
    half = MLA_ROPE // 2
    inv = jnp.power(ROPE_THETA, -jnp.arange(half, dtype=F32) / half)
    inv_lane = jnp.concatenate([jnp.zeros((64,), F32), inv, inv, jnp.zeros((32,), F32)]).reshape(1, LANES)
    pos = positions.reshape(SEQ, 1)
    shp = jax.ShapeDtypeStruct((SEQ, LANES), F32)
    spec = pl.BlockSpec((T_TOK, LANES), lambda i: (i, 0))
    return pl.pallas_call(
        _rope_kernel,
        out_shape=(shp, shp, shp),
        grid=(SEQ // T_TOK,),
        in_specs=[pl.BlockSpec((T_TOK, 1), lambda i: (i, 0)),
                  pl.BlockSpec((1, LANES), lambda i: (0, 0))],
        out_specs=(spec, spec, spec),
        compiler_params=_cparams(1),
        name="rope_tables",
    )(pos, inv_lane)


def _route(lg):
    lane = lax.broadcasted_iota(jnp.int32, lg.shape, 1)
    lane_f = lane.astype(F32)
    is_g = (lane >= N_EXPERTS) & (lane < N_EXPERTS + N_GROUPS)
    gl = jnp.where(is_g, lg, NEG)
    gmax = jnp.max(gl, axis=-1, keepdims=True)
    gidx = jnp.min(jnp.where(is_g & (gl == gmax), lane_f - N_EXPERTS, 1e3), axis=-1, keepdims=True)
    gsum = jnp.sum(jnp.where(is_g, jnp.exp(gl - gmax), 0.0), axis=-1, keepdims=True)
    gw = 1.0 / gsum
    grp_of_lane = (lane >> 3).astype(F32)
    in_grp = (lane < N_EXPERTS) & (grp_of_lane == gidx)
    el = jnp.where(in_grp, lg, NEG)
    m1 = jnp.max(el, axis=-1, keepdims=True)
    i1 = jnp.min(jnp.where(in_grp & (el == m1), lane_f, 1e3), axis=-1, keepdims=True)
    rest = in_grp & (lane_f != i1)
    el2 = jnp.where(rest, lg, NEG)
    m2 = jnp.max(el2, axis=-1, keepdims=True)
    i2 = jnp.min(jnp.where(rest & (el2 == m2), lane_f, 1e3), axis=-1, keepdims=True)
    r = jnp.exp(m2 - m1)
    w1 = gw / (1.0 + r)
    w2 = w1 * r
    return jnp.where(lane_f == i1, w1, jnp.where(lane_f == i2, w2, 0.0)), gidx


def _tail(x, y, mod, n2, wr_hi, wr_lo, br, rows_at, xo_ref, h2_ref, meta_ref, cnt_ref, carry_ref):
    gate1 = mod[2:3]
    shift2, scale2 = mod[3:4], mod[4:5]
    xn = x + gate1 * y
    xo_ref[rows_at, :] = xn
    h2 = _rms(xn) * n2 * (1.0 + scale2) + shift2
    hi = h2.astype(BF16)
    lo = (h2 - hi.astype(F32)).astype(BF16)
    lg = _dot(hi, wr_hi) + _dot(lo, wr_hi) + _dot(hi, wr_lo) + br
    gates, gidx = _route(lg)
    rows = lg.shape[0]
    lane = lax.broadcasted_iota(jnp.int32, (rows, LANES), 1)
    lane_f = lane.astype(F32)
    g8 = gates
    for g in range(1, N_GROUPS):
        g8 = g8 + pltpu.roll(gates, LANES - EPG * g, 1)
    onehot = (lane_f == gidx).astype(F32)
    r = lax.broadcasted_iota(jnp.int32, (rows, rows), 0)
    c = lax.broadcasted_iota(jnp.int32, (rows, rows), 1)
    before = jnp.where(c < r, 1.0, 0.0).astype(BF16)
    cum = _dot(before, onehot.astype(BF16)) + carry_ref[0:1, :]
    rank = jnp.sum(jnp.where(lane_f == gidx, cum, 0.0), axis=-1, keepdims=True)
    meta = jnp.where(lane < EPG, g8, jnp.where(lane == EPG, gidx, jnp.where(lane == EPG + 1, rank, 0.0)))
    h2_ref[rows_at, 0:D] = h2
    h2_ref[rows_at, D:D + LANES] = meta
    meta_ref[rows_at, :] = meta
    total = carry_ref[0:1, :] + jnp.sum(onehot, axis=0, keepdims=True)
    carry_ref[...] = jnp.broadcast_to(total, carry_ref.shape)
    cnt_ref[...] = jnp.broadcast_to(total, cnt_ref.shape)


def _even_kernel(x_ref, xh_ref, mod_ref, n1_ref, n2_ref, win_ref, cw_ref, pw_ref, ps_ref,
                 wout_ref, wrh_ref, wrl_ref, br_ref, xo_ref, h2_ref, meta_ref, cnt_ref, carry_ref):
    i = pl.program_id(0)
    _init_carry(i, carry_ref)
    for sub in range(N_SUB):
        rows_at = slice(sub * T_SUB, (sub + 1) * T_SUB)
        halo = xh_ref[...] if sub == 0 else x_ref[sub * T_SUB - HALO: sub * T_SUB, :]
        _even_rows(i * T_TOK + sub * T_SUB, x_ref[rows_at, :], halo, rows_at, mod_ref, n1_ref, n2_ref, win_ref,
                   cw_ref, pw_ref, ps_ref, wout_ref, wrh_ref, wrl_ref, br_ref, xo_ref, h2_ref, meta_ref,
                   cnt_ref, carry_ref)


def _init_carry(i, carry_ref):
    @pl.when(i == 0)
    def _():
        carry_ref[...] = jnp.zeros(carry_ref.shape, F32)


def _even_rows(tok0, x, halo, rows_at, mod_ref, n1_ref, n2_ref, win_ref, cw_ref, pw_ref, ps_ref,
               wout_ref, wrh_ref, wrl_ref, br_ref, xo_ref, h2_ref, meta_ref, cnt_ref, carry_ref):
    mod = mod_ref[...]
    shift1, scale1 = mod[0:1], mod[1:2]
    xa = jnp.concatenate([halo, x], axis=0)
    h = _rms(xa) * n1_ref[...] * (1.0 + scale1) + shift1
    z = _dot(h.astype(BF16), win_ref[...])
    rows = T_SUB + HALO
    row = lax.broadcasted_iota(jnp.int32, (rows, 1), 0)
    tpos = tok0 + row - HALO
    live = (tpos >= 0).astype(F32)
    bg = z[:, 0:CONV_CH]
    v = z[:, CONV_CH:2 * CONV_CH] * z[:, 2 * CONV_CH:3 * CONV_CH] * live
    cw = cw_ref[...]
    conv = v * cw[0:1] + pltpu.roll(v, 1, 0) * cw[1:2] + pltpu.roll(v, 2, 0) * cw[2:3]
    parts = [(bg * conv)[HALO:]]
    ps = ps_ref[...]
    tcount = (tpos + 1).astype(F32)
    for gi, w in enumerate(POOL_WINDOWS):
        ug = z[:, 3 * CONV_CH + gi * POOL_G: 3 * CONV_CH + (gi + 1) * POOL_G] * live
        s = ug
        k = 1
        while k < w:
            s = s + pltpu.roll(s, k, 0)
            k *= 2
        inv = 1.0 / jnp.minimum(tcount, float(w))
        d = (s * inv - ug)[HALO:].astype(BF16)
        parts.append(_dot(d, pw_ref[gi]) * ps[:, gi * POOL_G:(gi + 1) * POOL_G])
    cat = jnp.concatenate(parts, axis=-1).astype(BF16)
    y = _dot(cat, wout_ref[...])
    _tail(x, y, mod, n2_ref[...], wrh_ref[...], wrl_ref[...], br_ref[...], rows_at, xo_ref, h2_ref, meta_ref,
          cnt_ref, carry_ref)


def _tail_out(n):
    shapes = (jax.ShapeDtypeStruct((SEQ, D), F32), jax.ShapeDtypeStruct((SEQ, ROW_W), F32),
              jax.ShapeDtypeStruct((SEQ, LANES), F32), jax.ShapeDtypeStruct((8, LANES), F32))
    specs = (pl.BlockSpec((n, D), lambda i: (i, 0)), pl.BlockSpec((n, ROW_W), lambda i: (i, 0)),
             pl.BlockSpec((n, LANES), lambda i: (i, 0)), pl.BlockSpec((8, LANES), lambda i: (0, 0)))
    scratch = [pltpu.VMEM((8, LANES), F32)]
    return shapes, specs, scratch


def _full(shape):
    nd = len(shape)
    return pl.BlockSpec(shape, lambda i: (0,) * nd)


def _even_layer(x, mod, n1, n2, w_in, conv_w, pool_w, pool_scale, w_out, wr_hi, wr_lo, br):
    shapes, specs, scratch = _tail_out(T_TOK)
    hb = T_TOK // HALO
    return pl.pallas_call(
        _even_kernel,
        out_shape=shapes,
        grid=(SEQ // T_TOK,),
        in_specs=[pl.BlockSpec((T_TOK, D), lambda i: (i, 0)),
                  pl.BlockSpec((HALO, D), lambda i: (jnp.maximum(i * hb - 1, 0), 0)),
                  _full((8, D)), _full((1, D)), _full((1, D)),
                  _full((D, 4 * CONV_CH)), _full((3, CONV_CH)), _full((4, POOL_G, POOL_G)),
                  _full((1, 4 * POOL_G)), _full((D, D)),
                  _full((D, LANES)), _full((D, LANES)), _full((1, LANES))],
        out_specs=specs,
        scratch_shapes=scratch,
        compiler_params=_cparams(1),
        name="even_mixer",
    )(x, x, mod, n1, n2, w_in, conv_w, pool_w, pool_scale, w_out, wr_hi, wr_lo, br)


def _proj_kernel(x_ref, mod_ref, n1_ref, win_ref, gsq_ref, gsk_ref, gqn_ref, gkvn_ref,
                 wuq_ref, wuk_ref, wuv_ref, gmq_ref, gmqs_ref, gmk_ref, gmks_ref, c_ref, s1_ref, s2_ref,
                 qs_ref, ks_ref, vs_ref, qm_ref, km_ref, vm_ref):
    x = x_ref[...]
    mod = mod_ref[...]
    shift1, scale1 = mod[0:1], mod[1:2]
    h = _rms(x) * n1_ref[...] * (1.0 + scale1) + shift1
    z = _dot(h.astype(BF16), win_ref[...])

    def head_norm(t, g, dim):
        ms = jnp.sum(t * t, axis=-1, keepdims=True) * (1.0 / dim)
        return t * lax.rsqrt(ms + EPS) * g

    gsq, gsk = gsq_ref[...] * (HEAD_DIM ** -0.5), gsk_ref[...]
    for hd in range(SWA_HEADS):
        qh = head_norm(z[:, O_QS + hd * LANES: O_QS + (hd + 1) * LANES], gsq, HEAD_DIM)
        qs_ref[:, hd * LANES:(hd + 1) * LANES] = qh.astype(BF16)
    for kv in range(SWA_KV):
        kh = head_norm(z[:, O_KS + kv * LANES: O_KS + (kv + 1) * LANES], gsk, HEAD_DIM)
        ks_ref[:, kv * LANES:(kv + 1) * LANES] = kh.astype(BF16)
    vs_ref[...] = z[:, O_VS:O_CQ].astype(BF16)

    cq = (_rms(z[:, O_CQ:O_CKV]) * gqn_ref[...]).astype(BF16)
    ckv = (_rms(z[:, O_CKV:O_KR]) * gkvn_ref[...]).astype(BF16)
    qm = _dot(cq, wuq_ref[...])
    kn = _dot(ckv, wuk_ref[...])
    vm = _dot(ckv, wuv_ref[...])
    lane = lax.broadcasted_iota(jnp.int32, (T_TOK, LANES), 1)
    for j in range(MLA_HEADS // 2):
        vv = vm[:, j * LANES:(j + 1) * LANES]
        even = jnp.where(lane < 64, vv, jnp.where(lane == 64, 1.0, 0.0))
        odd = jnp.where(lane < 64, jnp.where(lane == 0, 1.0, 0.0), vv)
        vm_ref[:, (2 * j) * LANES:(2 * j + 1) * LANES] = even.astype(BF16)
        vm_ref[:, (2 * j + 1) * LANES:(2 * j + 2) * LANES] = odd.astype(BF16)
    kr = z[:, O_KR:O_KR + LANES]
    krs = z[:, O_KR + LANES:ODD_W]
    cs = c_ref[...]
    sp = s1_ref[...] + s2_ref[...]
    qscale = MLA_QK ** -0.5 * LOG2E
    gcq = cs * (gmq_ref[...] * qscale)
    gsq_r = sp * (gmqs_ref[...] * qscale)
    gck = cs * gmk_ref[...]
    krot = krs * (sp * gmks_ref[...])

    def inv_rms(t):
        return lax.rsqrt(jnp.sum(t * t, axis=-1, keepdims=True) * (1.0 / MLA_QK) + EPS)

    for hd in range(MLA_HEADS):
        sl = slice(hd * LANES, (hd + 1) * LANES)
        sw = slice((MLA_HEADS + hd) * LANES, (MLA_HEADS + hd + 1) * LANES)
        qh = qm[:, sl]
        qm_ref[:, sl] = ((qh * gcq + qm[:, sw] * gsq_r) * inv_rms(qh)).astype(BF16)
        kh = kn[:, sl] + kr
        km_ref[:, sl] = ((kh * gck + krot) * inv_rms(kh)).astype(BF16)


def _proj_layer(x, mod, n1, w_in, gsq, gsk, gqn, gkvn, wuq, wuk, wuv, gmq, gmqs, gmk, gmks, tc, ts1, ts2):
    def tok(wd):
        return pl.BlockSpec((T_TOK, wd), lambda i: (i, 0))
    widths = (1024, 256, 256, 1024, 1024, 1024)
    return pl.pallas_call(
        _proj_kernel,
        out_shape=tuple(jax.ShapeDtypeStruct((SEQ, wd), BF16) for wd in widths),
        grid=(SEQ // T_TOK,),
        in_specs=[tok(D), _full((8, D)), _full((1, D)), _full((D, ODD_W)),
                  _full((1, LANES)), _full((1, LANES)), _full((1, Q_LORA)), _full((1, KV_LORA)),
                  _full((Q_LORA, 2048)), _full((KV_LORA, 1024)), _full((KV_LORA, 512)),
                  _full((1, LANES)), _full((1, LANES)), _full((1, LANES)), _full((1, LANES)),
                  tok(LANES), tok(LANES), tok(LANES)],
        out_specs=tuple(tok(wd) for wd in widths),
        compiler_params=_cparams(1),
        name="odd_proj",
    )(x, mod, n1, w_in, gsq, gsk, gqn, gkvn, wuq, wuk, wuv, gmq, gmqs, gmk, gmks, tc, ts1, ts2)


def _swa_kernel(sink_ref, q_ref, k_ref, kh_ref, v_ref, vh_ref, o_ref):
    i = pl.program_id(0)
    kcat = jnp.concatenate([kh_ref[...], k_ref[...]], axis=0)
    vcat = jnp.concatenate([vh_ref[...], v_ref[...]], axis=0)
    grp = SWA_HEADS // SWA_KV
    r = lax.broadcasted_iota(jnp.int32, (grp * WINDOW, 2 * WINDOW), 0) & (WINDOW - 1)
    c = lax.broadcasted_iota(jnp.int32, (grp * WINDOW, 2 * WINDOW), 1)
    rel = WINDOW + r - c
    lane = lax.broadcasted_iota(jnp.int32, (WINDOW, LANES), 1)
    for sb in range(T_ATT // WINDOW):
        rows = slice(sb * WINDOW, (sb + 1) * WINDOW)
        kb = kcat[sb * WINDOW: sb * WINDOW + 2 * WINDOW]
        vb = vcat[sb * WINDOW: sb * WINDOW + 2 * WINDOW]
        kpos = i * T_ATT + (sb - 1) * WINDOW + c
        ok = (rel >= 0) & (rel < WINDOW) & (kpos >= 0)
        outs = []
        for kv in range(SWA_KV):
            q = jnp.concatenate([q_ref[rows, (kv * grp + g) * LANES:(kv * grp + g + 1) * LANES]
                                 for g in range(grp)], axis=0)
            sink = jnp.concatenate([jnp.full((WINDOW, 1), sink_ref[kv * grp + g], F32)
                                    for g in range(grp)], axis=0)
            s = jnp.where(ok, _dot_nt(q, kb[:, kv * LANES:(kv + 1) * LANES]), NEG)
            m = jnp.maximum(jnp.max(s, axis=-1, keepdims=True), sink)
            e = jnp.exp(s - m)
            den = jnp.sum(e, axis=-1, keepdims=True) + jnp.exp(sink - m)
            p = (e * (1.0 / den)).astype(BF16)
            o = _dot(p, vb[:, kv * LANES:(kv + 1) * LANES])
            outs += [o[g * WINDOW:(g + 1) * WINDOW] for g in range(grp)]
        for j in range(SWA_HEADS // 2):
            o_ref[rows, j * LANES:(j + 1) * LANES] = jnp.where(lane < 64, outs[2 * j], outs[2 * j + 1]).astype(BF16)


def _swa_layer(sinks, qs, ks, vs):
    hb = T_ATT // WINDOW
    return pl.pallas_call(
        _swa_kernel,
        out_shape=jax.ShapeDtypeStruct((SEQ, 512), BF16),
        grid=(SEQ // T_ATT,),
        in_specs=[pl.BlockSpec(memory_space=pltpu.SMEM),
                  pl.BlockSpec((T_ATT, 1024), lambda i: (i, 0)),
                  pl.BlockSpec((T_ATT, 256), lambda i: (i, 0)),
                  pl.BlockSpec((WINDOW, 256), lambda i: (jnp.maximum(i * hb - 1, 0), 0)),
                  pl.BlockSpec((T_ATT, 256), lambda i: (i, 0)),
                  pl.BlockSpec((WINDOW, 256), lambda i: (jnp.maximum(i * hb - 1, 0), 0))],
        out_specs=pl.BlockSpec((T_ATT, 512), lambda i: (i, 0)),
        compiler_params=_cparams(1),
        name="swa_attn",
    )(sinks, qs, ks, ks, vs, vs)


def _mla_kernel(qi_ref, ki_ref, q_ref, k_ref, v_ref, o_ref, m_ref, acc_ref):
    step = pl.program_id(0)
    qi = qi_ref[step]
    ki = ki_ref[step]

    @pl.when(ki == 0)
    def _():
        m_ref[...] = jnp.full(m_ref.shape, NEG, F32)
        acc_ref[...] = jnp.zeros(acc_ref.shape, F32)

    lane = lax.broadcasted_iota(jnp.int32, (T_MQ, LANES), 1)
    lo = lane < 64
    nc = T_MK // LANES
    sub = T_MQ // MLA_SPLIT
    ratio = T_MQ // T_MK
    units = [(hd, part) for hd in range(MLA_HEADS) for part in range(MLA_SPLIT)]

    def scores(unit):
        hd, part = unit
        sl = slice(hd * LANES, (hd + 1) * LANES)
        return _dot_nt(q_ref[part * sub:(part + 1) * sub, sl], k_ref[:, sl])

    def update(masked):
        if masked:
            r = lax.broadcasted_iota(jnp.int32, (sub, LANES), 0)
            lane_s = lax.broadcasted_iota(jnp.int32, (sub, LANES), 1)

        def softmax_part(unit, s):
            hd, part = unit
            rows = slice(part * sub, (part + 1) * sub)
            cols = [s[:, c * LANES:(c + 1) * LANES] for c in range(nc)]
            if masked:
                off = qi * T_MQ - ki * T_MK + part * sub
                cols = [jnp.where(r + off >= lane_s + c * LANES, cols[c], NEG) for c in range(nc)]
            cmax = cols[0]
            for c in range(1, nc):
                cmax = jnp.maximum(cmax, cols[c])
            m_prev = m_ref[hd, rows]
            m_new = jnp.maximum(m_prev, jnp.max(cmax, axis=-1, keepdims=True))
            m_ref[hd, rows] = m_new
            alpha = jnp.exp2(m_prev - m_new)
            p = jnp.concatenate([jnp.exp2(cols[c] - m_new).astype(BF16) for c in range(nc)], axis=-1)
            return p, alpha

        def value_part(unit, p, alpha):
            hd, part = unit
            rows = slice(part * sub, (part + 1) * sub)
            acc_ref[hd, rows] = acc_ref[hd, rows] * alpha + _dot(p, v_ref[:, hd * LANES:(hd + 1) * LANES])

        s_q = [scores(u) for u in units[:MLA_DEPTH]]
        pend = None
        for n, u in enumerate(units):
            if n + MLA_DEPTH < len(units):
                s_q.append(scores(units[n + MLA_DEPTH]))
            cur = softmax_part(u, s_q[n])
            if pend is not None:
                value_part(units[n - 1], *pend)
            pend = cur
        value_part(units[-1], *pend)

    @pl.when(ki < qi * ratio)
    def _():
        update(False)

    @pl.when(ki >= qi * ratio)
    def _():
        update(True)

    @pl.when(ki == qi * ratio + ratio - 1)
    def _():
        for j in range(MLA_HEADS // 2):
            ae = acc_ref[2 * j]
            ao = acc_ref[2 * j + 1]
            out = jnp.where(lo, ae * (1.0 / ae[:, 64:65]), ao * (1.0 / ao[:, 0:1]))
            o_ref[:, j * LANES:(j + 1) * LANES] = out.astype(BF16)


def _mla_layer(qm, km, vm):
    nb = SEQ // T_MQ
    ratio = T_MQ // T_MK
    qi = np.concatenate([np.full(ratio * (n + 1), n, np.int32) for n in range(nb)])
    ki = np.concatenate([np.arange(ratio * (n + 1), dtype=np.int32) for n in range(nb)])
    grid_spec = pltpu.PrefetchScalarGridSpec(
        num_scalar_prefetch=2,
        grid=(int(qi.shape[0]),),
        in_specs=[pl.BlockSpec((T_MQ, 1024), lambda s, qi, ki: (qi[s], 0)),
                  pl.BlockSpec((T_MK, 1024), lambda s, qi, ki: (ki[s], 0)),
                  pl.BlockSpec((T_MK, 1024), lambda s, qi, ki: (ki[s], 0))],
        out_specs=pl.BlockSpec((T_MQ, 512), lambda s, qi, ki: (qi[s], 0)),
        scratch_shapes=[pltpu.VMEM((MLA_HEADS, T_MQ, LANES), F32),
                        pltpu.VMEM((MLA_HEADS, T_MQ, LANES), F32)],
    )
    return pl.pallas_call(
        _mla_kernel,
        out_shape=jax.ShapeDtypeStruct((SEQ, 512), BF16),
        grid_spec=grid_spec,
        compiler_params=_cparams(1),
        name="mla_attn",
    )(jnp.asarray(qi), jnp.asarray(ki), qm, km, vm)


def _post_kernel(x_ref, os_ref, om_ref, mod_ref, n2_ref, wout_ref, wrh_ref, wrl_ref, br_ref,
                 xo_ref, h2_ref, meta_ref, cnt_ref, carry_ref):
    _init_carry(pl.program_id(0), carry_ref)
    for sub in range(N_SUB):
        rows_at = slice(sub * T_SUB, (sub + 1) * T_SUB)
        y = _dot(os_ref[rows_at, :], wout_ref[0:512, :]) + _dot(om_ref[rows_at, :], wout_ref[512:1024, :])
        _tail(x_ref[rows_at, :], y, mod_ref[...], n2_ref[...], wrh_ref[...], wrl_ref[...], br_ref[...],
              rows_at, xo_ref, h2_ref, meta_ref, cnt_ref, carry_ref)


def _post_layer(x, o_s, o_m, mod, n2, w_out, wr_hi, wr_lo, br):
    shapes, specs, scratch = _tail_out(T_TOK)
    return pl.pallas_call(
        _post_kernel,
        out_shape=shapes,
        grid=(SEQ // T_TOK,),
        in_specs=[pl.BlockSpec((T_TOK, D), lambda i: (i, 0)),
                  pl.BlockSpec((T_TOK, 512), lambda i: (i, 0)),
                  pl.BlockSpec((T_TOK, 512), lambda i: (i, 0)),
                  _full((8, D)), _full((1, D)), _full((D, D)),
                  _full((D, LANES)), _full((D, LANES)), _full((1, LANES))],
        out_specs=specs,
        scratch_shapes=scratch,
        compiler_params=_cparams(1),
        name="odd_post",
    )(x, o_s, o_m, mod, n2, w_out, wr_hi, wr_lo, br)


def _dispatch_plan(meta, cnt):
    grp = meta[:, EPG].astype(jnp.int32)
    rank = meta[:, EPG + 1].astype(jnp.int32)
    counts = cnt[0, :N_GROUPS].astype(jnp.int32)
    padded = ((counts + T_MOE - 1) // T_MOE) * T_MOE
    ends = jnp.cumsum(padded)
    pos = (ends - padded)[grp] + rank
    n_used = ends[-1] // T_MOE
    tile_start = jnp.arange(N_TILES, dtype=jnp.int32) * T_MOE
    tile_group = jnp.minimum(jnp.sum(tile_start[:, None] >= ends[None, :], axis=1), N_GROUPS - 1)
    return pos.reshape(SEQ // T_DISP, 1, T_DISP), tile_group.astype(jnp.int32), n_used.reshape(1)


def _row_copies(n, src_at, dst_at, sem):
    for r in range(n):
        pltpu.make_async_copy(src_at(r), dst_at(r), sem).start(priority=r % 2)


def _disp_kernel(pos_ref, x_ref, init_ref, o_ref, sem):
    del init_ref
    _row_copies(T_DISP,
                lambda r: x_ref.at[pl.ds(r, 1), :],
                lambda r: o_ref.at[pl.ds(pos_ref[0, 0, r], 1), :], sem)
    pltpu.make_async_copy(x_ref, o_ref.at[pl.ds(0, T_DISP), :], sem).wait()


def _dispatch(pos, h2a, hs):
    return pl.pallas_call(
        _disp_kernel,
        out_shape=jax.ShapeDtypeStruct((N_SORT, ROW_W), F32),
        grid=(SEQ // T_DISP,),
        in_specs=[pl.BlockSpec((1, 1, T_DISP), lambda i: (i, 0, 0), memory_space=pltpu.SMEM),
                  pl.BlockSpec((T_DISP, ROW_W), lambda i: (i, 0)),
                  pl.BlockSpec(memory_space=pl.ANY)],
        out_specs=pl.BlockSpec(memory_space=pl.ANY),
        scratch_shapes=[pltpu.SemaphoreType.DMA(())],
        input_output_aliases={2: 0},
        compiler_params=_cparams(1),
        name="moe_dispatch",
    )(pos, h2a, hs)


def _moe_kernel(tg_ref, nu_ref, x_ref, wg_ref, wu_ref, wd_ref, o_ref, xb_ref, act_ref):
    del tg_ref
    i = pl.program_id(0)
    j = pl.program_id(1)
    last = EPG // E_STEP - 1

    @pl.when(i < nu_ref[0])
    def _():
        @pl.when(j == 0)
        def _():
            xb_ref[...] = x_ref[:, 0:D].astype(BF16)

        xb = xb_ref[...]
        meta = x_ref[:, D:ROW_W]
        lane = lax.broadcasted_iota(jnp.int32, meta.shape, 1)
        for k in range(E_STEP):
            a = _dot(xb, wg_ref[0, k].astype(BF16))
            u = _dot(xb, wu_ref[0, k].astype(BF16))
            gate = jnp.sum(jnp.where(lane == j * E_STEP + k, meta, 0.0), axis=-1, keepdims=True)
            act = (a * jax.nn.sigmoid(a) * u * gate).astype(BF16)
            for jj in range(EPG // E_STEP):
                @pl.when(j == jj)
                def _(jj=jj, k=k, act=act):
                    act_ref[:, (jj * E_STEP + k) * FF:(jj * E_STEP + k + 1) * FF] = act

        @pl.when(j == last)
        def _():
            o_ref[...] = _dot(act_ref[...], wd_ref[0, 0].astype(BF16))

    @pl.when((i >= nu_ref[0]) & (j == last))
    def _():
        o_ref[...] = jnp.zeros(o_ref.shape, F32)


def _moe_experts(tile_group, n_used, hs, wg, wu, wd, layer):
    def tile(i, e, tg, nu):
        return (jnp.minimum(i, nu[0] - 1), 0)

    nj = EPG // E_STEP

    def expert(i, j, tg, nu):
        return (layer, jnp.where(i < nu[0], tg[i] * nj + j, tg[nu[0] - 1] * nj + nj - 1), 0, 0)

    def group(i, j, tg, nu):
        return (layer, tg[jnp.minimum(i, nu[0] - 1)], 0, 0)

    grid_spec = pltpu.PrefetchScalarGridSpec(
        num_scalar_prefetch=2,
        grid=(N_TILES, nj),
        in_specs=[pl.BlockSpec((T_MOE, ROW_W), tile),
                  pl.BlockSpec((1, E_STEP, D, FF), expert),
                  pl.BlockSpec((1, E_STEP, D, FF), expert),
                  pl.BlockSpec((1, 1, EPG * FF, D), group, pipeline_mode=pl.Buffered(1))],
        out_specs=pl.BlockSpec((T_MOE, D), lambda i, e, tg, nu: (i, 0)),
        scratch_shapes=[pltpu.VMEM((T_MOE, D), BF16), pltpu.VMEM((T_MOE, EPG * FF), BF16)],
    )
    return pl.pallas_call(
        _moe_kernel,
        out_shape=jax.ShapeDtypeStruct((N_SORT, D), F32),
        grid_spec=grid_spec,
        compiler_params=pltpu.CompilerParams(dimension_semantics=("arbitrary", "arbitrary"),
                                             vmem_limit_bytes=MOE_VMEM_LIMIT),
        name="moe_experts",
    )(tile_group, n_used, hs, wg, wu, wd.reshape(DEPTH, N_GROUPS, EPG * FF, D))


def _comb_kernel(pos_ref, x_ref, mod_ref, y_ref, o_ref, buf_ref, sem):
    _row_copies(T_DISP,
                lambda r: y_ref.at[pl.ds(pos_ref[0, 0, r], 1), :],
                lambda r: buf_ref.at[pl.ds(r, 1), :], sem)
    pltpu.make_async_copy(y_ref.at[pl.ds(0, T_DISP), :], buf_ref, sem).wait()
    o_ref[...] = x_ref[...] + mod_ref[5:6, :] * buf_ref[...]


def _combine(pos, x, mod, ys):
    return pl.pallas_call(
        _comb_kernel,
        out_shape=jax.ShapeDtypeStruct((SEQ, D), F32),
        grid=(SEQ // T_DISP,),
        in_specs=[pl.BlockSpec((1, 1, T_DISP), lambda i: (i, 0, 0), memory_space=pltpu.SMEM),
                  pl.BlockSpec((T_DISP, D), lambda i: (i, 0)),
                  pl.BlockSpec((8, D), lambda i: (0, 0)),
                  pl.BlockSpec(memory_space=pl.ANY)],
        out_specs=pl.BlockSpec((T_DISP, D), lambda i: (i, 0)),
        scratch_shapes=[pltpu.VMEM((T_DISP, D), F32), pltpu.SemaphoreType.DMA(())],
        compiler_params=_cparams(1),
        name="moe_combine",
    )(pos, x, mod, ys)


def _moe_layer(x, h2a, meta, cnt, mod, wg, wu, wd, layer, hs):
    pos, tile_group, n_used = _dispatch_plan(meta, cnt)
    hs = _dispatch(pos, h2a, hs)
    ys = _moe_experts(tile_group, n_used, hs, wg, wu, wd, layer)
    return _combine(pos, x, mod, ys), hs


def _pad_heads(w, heads, dim):
    k = w.shape[0]
    w = w.reshape(k, heads, dim)
    return jnp.pad(w, ((0, 0), (0, 0), (0, LANES - dim))).reshape(k, heads * LANES)


def _pad_gain(g):
    return jnp.pad(g, (0, LANES - g.shape[0])).reshape(1, LANES)


def _odd_weights(w_in):
    q_s = _pad_heads(w_in[:, 0:512], SWA_HEADS, HEAD_DIM)
    k_s = _pad_heads(w_in[:, 512:640], SWA_KV, HEAD_DIM)
    v = w_in[:, 640:768]
    v_s = jnp.concatenate([v[:, 0:64], v[:, 0:64], v[:, 64:128], v[:, 64:128]], axis=1)
    c_q = w_in[:, 768:1152]
    c_kv = w_in[:, 1152:1408]
    k_r = jnp.pad(w_in[:, 1408:1440], ((0, 0), (MLA_NOPE, LANES - MLA_QK)))
    k_r_sw = jnp.pad(_swap_halves(w_in[:, 1408:1440]), ((0, 0), (MLA_NOPE, LANES - MLA_QK)))
    return jnp.concatenate([q_s, k_s, v_s, c_q, c_kv, k_r, k_r_sw], axis=1).astype(BF16)


def _swap_halves(t):
    half = t.shape[-1] // 2
    return jnp.concatenate([t[..., half:], t[..., :half]], axis=-1)


def _uq_weights(w_uq):
    w = w_uq.reshape(Q_LORA, MLA_HEADS, MLA_QK)
    plain = jnp.pad(w, ((0, 0), (0, 0), (0, LANES - MLA_QK)))
    swapped = jnp.pad(_swap_halves(w[:, :, MLA_NOPE:]), ((0, 0), (0, 0), (MLA_NOPE, LANES - MLA_QK)))
    return jnp.concatenate([plain.reshape(Q_LORA, -1), swapped.reshape(Q_LORA, -1)], axis=1)


def _rope_gains(g):
    plain = jnp.pad(g, (0, LANES - MLA_QK)).reshape(1, LANES)
    partner = jnp.pad(_swap_halves(g[MLA_NOPE:]), (MLA_NOPE, LANES - MLA_QK)).reshape(1, LANES)
    return plain, partner


def _router_weights(w_group, b_group, w_expert, b_expert):
    w = jnp.pad(jnp.concatenate([w_expert, w_group], axis=1), ((0, 0), (0, LANES - N_EXPERTS - N_GROUPS)))
    hi = w.astype(BF16)
    lo = (w - hi.astype(F32)).astype(BF16)
    b = jnp.pad(jnp.concatenate([b_expert, b_group]), (0, LANES - N_EXPERTS - N_GROUPS)).reshape(1, LANES)
    return hi, lo, b


def kernel(x, c, positions, ada_w, ada_b, norm1_g, norm2_g, cp_w_in, conv_w, pool_w, pool_scale,
           cp_w_out, at_w_in, swa_q_g, swa_k_g, swa_sinks, mla_q_norm_g, mla_kv_norm_g, mla_w_uq,
           mla_w_ukv, mla_q_g, mla_k_g, at_w_out, moe_w_group, moe_b_group, moe_w_expert,
           moe_b_expert, moe_w_gate, moe_w_up, moe_w_down):
    xs = x.reshape(SEQ, D)
    mods = _ada_mod(c, ada_w, ada_b)
    tc, ts1, ts2 = _rope_tables(positions)
    hs = jnp.zeros((N_SORT, ROW_W), F32)
    for l in range(DEPTH):
        i = l // 2
        mod = mods[l]
        n1 = norm1_g[l].reshape(1, D)
        n2 = norm2_g[l].reshape(1, D)
        wr_hi, wr_lo, br = _router_weights(moe_w_group[l], moe_b_group[l], moe_w_expert[l], moe_b_expert[l])
        if l % 2 == 0:
            xs, h2a, meta, cnt = _even_layer(
                xs, mod, n1, n2, cp_w_in[i].astype(BF16), conv_w[i], pool_w[i].astype(BF16),
                pool_scale[i].reshape(1, 4 * POOL_G), cp_w_out[i].astype(BF16), wr_hi, wr_lo, br)
        else:
            ukv = mla_w_ukv[i].reshape(KV_LORA, MLA_HEADS, MLA_NOPE + MLA_V)
            wuk = _pad_heads(ukv[:, :, :MLA_NOPE].reshape(KV_LORA, MLA_HEADS * MLA_NOPE), MLA_HEADS, MLA_NOPE)
            wuv = ukv[:, :, MLA_NOPE:].reshape(KV_LORA, MLA_HEADS * MLA_V)
            wuq = _uq_weights(mla_w_uq[i])
            gmq, gmqs = _rope_gains(mla_q_g[i])
            gmk, gmks = _rope_gains(mla_k_g[i])
            qs, ks, vs, qm, km, vm = _proj_layer(
                xs, mod, n1, _odd_weights(at_w_in[i]), _pad_gain(swa_q_g[i]), _pad_gain(swa_k_g[i]),
                mla_q_norm_g[i].reshape(1, Q_LORA), mla_kv_norm_g[i].reshape(1, KV_LORA),
                wuq.astype(BF16), wuk.astype(BF16), wuv.astype(BF16),
                gmq, gmqs, gmk, gmks, tc, ts1, ts2)
            o_s = _swa_layer(swa_sinks[i], qs, ks, vs)
            o_m = _mla_layer(qm, km, vm)
            xs, h2a, meta, cnt = _post_layer(xs, o_s, o_m, mod, n2, at_w_out[i].astype(BF16), wr_hi, wr_lo, br)
        xs, hs = _moe_layer(xs, h2a, meta, cnt, mod, moe_w_gate, moe_w_up, moe_w_down, l, hs)
    return xs.reshape(1, SEQ, D)
```

```python
import functools

import numpy as np
import jax
import jax.numpy as jnp
from jax import lax
from jax.experimental import pallas as pl
from jax.experimental.pallas import tpu as pltpu

F32 = jnp.float32
BF16 = jnp.bfloat16

D = 1024
SEQ = 16384
DEPTH = 4
EPS = 1e-6
LANES = 128
CONV_CH = 512
POOL_WINDOWS = (2, 4, 8, 16)
POOL_G = 128
HALO = 16
SWA_HEADS = 8
SWA_KV = 2
HEAD_DIM = 64
WINDOW = 128
MLA_HEADS = 8
MLA_NOPE = 64
MLA_ROPE = 32
MLA_QK = MLA_NOPE + MLA_ROPE
MLA_V = 64
Q_LORA = 384
KV_LORA = 256
ROPE_THETA = 10000.0
N_GROUPS = 4
EPG = 8
N_EXPERTS = N_GROUPS * EPG
FF = 256
NEG = -1e30
LOG2E = 1.4426950408889634

T_TOK = 512
N_SUB = 2
T_SUB = T_TOK // N_SUB
T_ATT = 512
T_MOE = 512
T_DISP = 1024
ROW_W = D + LANES
N_SORT = SEQ + N_GROUPS * T_MOE
N_TILES = N_SORT // T_MOE
E_STEP = 8
T_MQ = 1024
T_MK = 512
MLA_SPLIT = 1
MLA_DEPTH = 3
VMEM_LIMIT = 48 * 1024 * 1024
MOE_VMEM_LIMIT = 56 * 1024 * 1024

O_QS, O_KS, O_VS, O_CQ, O_CKV, O_KR, ODD_W = 0, 1024, 1280, 1536, 1920, 2176, 2432


def _cparams(n_axes=1):
    return pltpu.CompilerParams(dimension_semantics=("arbitrary",) * n_axes,
                                vmem_limit_bytes=VMEM_LIMIT)


def _rms(x):
    return x * lax.rsqrt(jnp.mean(x * x, axis=-1, keepdims=True) + EPS)


def _dot(a, b):
    return jnp.dot(a, b, preferred_element_type=F32)


def _dot_nt(a, b):
    return lax.dot_general(a, b, (((1,), (1,)), ((), ())), preferred_element_type=F32)


def _ada_kernel(c_ref, w_ref, b_ref, o_ref):
    c = c_ref[...]
    ca = c * jax.nn.sigmoid(c)
    o_ref[0] = jnp.sum(w_ref[0] * ca, axis=0, keepdims=True) + b_ref[0]


def _ada_mod(c, ada_w, ada_b):
    c_col = c.reshape(D, 1)
    b = ada_b.reshape(DEPTH * 6, 1, D)
    out = pl.pallas_call(
        _ada_kernel,
        out_shape=jax.ShapeDtypeStruct((DEPTH * 6, 1, D), F32),
        grid=(DEPTH, 6),
        in_specs=[pl.BlockSpec((D, 1), lambda l, j: (0, 0)),
                  pl.BlockSpec((1, D, D), lambda l, j: (l, 0, j)),
                  pl.BlockSpec((1, 1, D), lambda l, j: (l * 6 + j, 0, 0))],
        out_specs=pl.BlockSpec((1, 1, D), lambda l, j: (l * 6 + j, 0, 0)),
        compiler_params=_cparams(2),
        name="ada_mod",
    )(c_col, ada_w, b)
    mod = out.reshape(DEPTH, 6, D)
    return jnp.pad(mod, ((0, 0), (0, 2), (0, 0)))


def _rope_kernel(pos_ref, inv_ref, c_ref, s1_ref, s2_ref):
    pos = pos_ref[...].astype(F32)
    ang = pos * inv_ref[...]
    lane = lax.broadcasted_iota(jnp.int32, ang.shape, 1)
    cs = jnp.cos(ang)
    sn = jnp.sin(ang)
    c_ref[...] = jnp.where(lane < 64, 1.0, jnp.where(lane < 96, cs, 0.0))
    s1_ref[...] = jnp.where((lane >= 64) & (lane < 80), -sn, 0.0)
    s2_ref[...] = jnp.where((lane >= 80) & (lane < 96), sn, 0.0)


def _rope_tables(positions):
    half = MLA_ROPE // 2
    inv = jnp.power(ROPE_THETA, -jnp.arange(half, dtype=F32) / half)
    inv_lane = jnp.concatenate([jnp.zeros((64,), F32), inv, inv, jnp.zeros((32,), F32)]).reshape(1, LANES)
    pos = positions.reshape(SEQ, 1)
    shp = jax.ShapeDtypeStruct((SEQ, LANES), F32)
    spec = pl.BlockSpec((T_TOK, LANES), lambda i: (i, 0))
    return pl.pallas_call(
        _rope_kernel,
        out_shape=(shp, shp, shp),
        grid=(SEQ // T_TOK,),
        in_specs=[pl.BlockSpec((T_TOK, 1), lambda i: (i, 0)),
                  pl.BlockSpec((1, LANES), lambda i: (0, 0))],
        out_specs=(spec, spec, spec),
        compiler_params=_cparams(1),
        name="rope_tables",
    )(pos, inv_lane)


def _route(lg):
    lane = lax.broadcasted_iota(jnp.int32, lg.shape, 1)
    lane_f = lane.astype(F32)
    is_g = (lane >= N_EXPERTS) & (lane < N_EXPERTS + N_GROUPS)
    gl = jnp.where(is_g, lg, NEG)
    gmax = jnp.max(gl, axis=-1, keepdims=True)
    gidx = jnp.min(jnp.where(is_g & (gl == gmax), lane_f - N_EXPERTS, 1e3), axis=-1, keepdims=True)
    gsum = jnp.sum(jnp.where(is_g, jnp.exp(gl - gmax), 0.0), axis=-1, keepdims=True)
    gw = 1.0 / gsum
    grp_of_lane = (lane >> 3).astype(F32)
    in_grp = (lane < N_EXPERTS) & (grp_of_lane == gidx)
    el = jnp.where(in_grp, lg, NEG)
    m1 = jnp.max(el, axis=-1, keepdims=True)
    i1 = jnp.min(jnp.where(in_grp & (el == m1), lane_f, 1e3), axis=-1, keepdims=True)
    rest = in_grp & (lane_f != i1)
    el2 = jnp.where(rest, lg, NEG)
    m2 = jnp.max(el2, axis=-1, keepdims=True)
    i2 = jnp.min(jnp.where(rest & (el2 == m2), lane_f, 1e3), axis=-1, keepdims=True)
    r = jnp.exp(m2 - m1)
    w1 = gw / (1.0 + r)
    w2 = w1 * r
    return jnp.where(lane_f == i1, w1, jnp.where(lane_f == i2, w2, 0.0)), gidx


def _tail(x, y, mod, n2, wr_hi, wr_lo, br, rows_at, xo_ref, h2_ref, meta_ref, cnt_ref, carry_ref):
    gate1 = mod[2:3]
    shift2, scale2 = mod[3:4], mod[4:5]
    xn = x + gate1 * y
    xo_ref[rows_at, :] = xn
    h2 = _rms(xn) * n2 * (1.0 + scale2) + shift2
    hi = h2.astype(BF16)
    lo = (h2 - hi.astype(F32)).astype(BF16)
    both = _dot(hi, jnp.concatenate([wr_hi, wr_lo], axis=1))
    lg = both[:, 0:LANES] + both[:, LANES:2 * LANES] + _dot(lo, wr_hi) + br
    gates, gidx = _route(lg)
    rows = lg.shape[0]
    lane = lax.broadcasted_iota(jnp.int32, (rows, LANES), 1)
    lane_f = lane.astype(F32)
    g8 = gates
    for g in range(1, N_GROUPS):
        g8 = g8 + pltpu.roll(gates, LANES - EPG * g, 1)
    onehot = (lane_f == gidx).astype(F32)
    r = lax.broadcasted_iota(jnp.int32, (rows, rows), 0)
    c = lax.broadcasted_iota(jnp.int32, (rows, rows), 1)
    before = jnp.where(c < r, 1.0, 0.0).astype(BF16)
    cum = _dot(before, onehot.astype(BF16)) + carry_ref[0:1, :]
    rank = jnp.sum(jnp.where(lane_f == gidx, cum, 0.0), axis=-1, keepdims=True)
    meta = jnp.where(lane < EPG, g8, jnp.where(lane == EPG, gidx, jnp.where(lane == EPG + 1, rank, 0.0)))
    h2_ref[rows_at, 0:D] = h2
    h2_ref[rows_at, D:D + LANES] = meta
    meta_ref[rows_at, :] = meta
    total = carry_ref[0:1, :] + jnp.sum(onehot, axis=0, keepdims=True)
    carry_ref[...] = jnp.broadcast_to(total, carry_ref.shape)
    cnt_ref[...] = jnp.broadcast_to(total, cnt_ref.shape)


def _even_kernel(x_ref, xh_ref, mod_ref, n1_ref, n2_ref, win_ref, cw_ref, pw_ref, ps_ref,
                 wout_ref, wrh_ref, wrl_ref, br_ref, xo_ref, h2_ref, meta_ref, cnt_ref, carry_ref):
    i = pl.program_id(0)
    _init_carry(i, carry_ref)
    for sub in range(N_SUB):
        rows_at = slice(sub * T_SUB, (sub + 1) * T_SUB)
        halo = xh_ref[...] if sub == 0 else x_ref[sub * T_SUB - HALO: sub * T_SUB, :]
        _even_rows(i * T_TOK + sub * T_SUB, x_ref[rows_at, :], halo, rows_at, mod_ref, n1_ref, n2_ref, win_ref,
                   cw_ref, pw_ref, ps_ref, wout_ref, wrh_ref, wrl_ref, br_ref, xo_ref, h2_ref, meta_ref,
                   cnt_ref, carry_ref)


def _init_carry(i, carry_ref):
    @pl.when(i == 0)
    def _():
        carry_ref[...] = jnp.zeros(carry_ref.shape, F32)


def _even_rows(tok0, x, halo, rows_at, mod_ref, n1_ref, n2_ref, win_ref, cw_ref, pw_ref, ps_ref,
               wout_ref, wrh_ref, wrl_ref, br_ref, xo_ref, h2_ref, meta_ref, cnt_ref, carry_ref):
    mod = mod_ref[...]
    shift1, scale1 = mod[0:1], mod[1:2]
    xa = jnp.concatenate([halo, x], axis=0)
    h = _rms(xa) * n1_ref[...] * (1.0 + scale1) + shift1
    z = _dot(h.astype(BF16), win_ref[...])
    rows = T_SUB + HALO
    row = lax.broadcasted_iota(jnp.int32, (rows, 1), 0)
    tpos = tok0 + row - HALO
    live = (tpos >= 0).astype(F32)
    bg = z[:, 0:CONV_CH]
    v = z[:, CONV_CH:2 * CONV_CH] * z[:, 2 * CONV_CH:3 * CONV_CH] * live
    cw = cw_ref[...]
    conv = v * cw[0:1] + pltpu.roll(v, 1, 0) * cw[1:2] + pltpu.roll(v, 2, 0) * cw[2:3]
    parts = [(bg * conv)[HALO:]]
    ps = ps_ref[...]
    tcount = (tpos + 1).astype(F32)
    for gi, w in enumerate(POOL_WINDOWS):
        ug = z[:, 3 * CONV_CH + gi * POOL_G: 3 * CONV_CH + (gi + 1) * POOL_G] * live
        s = ug
        k = 1
        while k < w:
            s = s + pltpu.roll(s, k, 0)
            k *= 2
        inv = 1.0 / jnp.minimum(tcount, float(w))
        d = (s * inv - ug)[HALO:].astype(BF16)
        parts.append(_dot(d, pw_ref[gi]) * ps[:, gi * POOL_G:(gi + 1) * POOL_G])
    cat = jnp.concatenate(parts, axis=-1).astype(BF16)
    y = _dot(cat, wout_ref[...])
    _tail(x, y, mod, n2_ref[...], wrh_ref[...], wrl_ref[...], br_ref[...], rows_at, xo_ref, h2_ref, meta_ref,
          cnt_ref, carry_ref)


def _tail_out(n):
    shapes = (jax.ShapeDtypeStruct((SEQ, D), F32), jax.ShapeDtypeStruct((SEQ, ROW_W), F32),
              jax.ShapeDtypeStruct((SEQ, LANES), F32), jax.ShapeDtypeStruct((8, LANES), F32))
    specs = (pl.BlockSpec((n, D), lambda i: (i, 0)), pl.BlockSpec((n, ROW_W), lambda i: (i, 0)),
             pl.BlockSpec((n, LANES), lambda i: (i, 0)), pl.BlockSpec((8, LANES), lambda i: (0, 0)))
    scratch = [pltpu.VMEM((8, LANES), F32)]
    return shapes, specs, scratch


def _full(shape):
    nd = len(shape)
    return pl.BlockSpec(shape, lambda i: (0,) * nd)


def _even_layer(x, mod, n1, n2, w_in, conv_w, pool_w, pool_scale, w_out, wr_hi, wr_lo, br):
    shapes, specs, scratch = _tail_out(T_TOK)
    hb = T_TOK // HALO
    return pl.pallas_call(
        _even_kernel,
        out_shape=shapes,
        grid=(SEQ // T_TOK,),
        in_specs=[pl.BlockSpec((T_TOK, D), lambda i: (i, 0)),
                  pl.BlockSpec((HALO, D), lambda i: (jnp.maximum(i * hb - 1, 0), 0)),
                  _full((8, D)), _full((1, D)), _full((1, D)),
                  _full((D, 4 * CONV_CH)), _full((3, CONV_CH)), _full((4, POOL_G, POOL_G)),
                  _full((1, 4 * POOL_G)), _full((D, D)),
                  _full((D, LANES)), _full((D, LANES)), _full((1, LANES))],
        out_specs=specs,
        scratch_shapes=scratch,
        compiler_params=_cparams(1),
        name="even_mixer",
    )(x, x, mod, n1, n2, w_in, conv_w, pool_w, pool_scale, w_out, wr_hi, wr_lo, br)


def _proj_kernel(x_ref, mod_ref, n1_ref, win_ref, gsq_ref, gsk_ref, gqn_ref, gkvn_ref,
                 wuq_ref, wuk_ref, wuv_ref, gmq_ref, gmqs_ref, gmk_ref, gmks_ref, c_ref, s1_ref, s2_ref,
                 qs_ref, ks_ref, vs_ref, qm_ref, km_ref, vm_ref):
    refs = (x_ref, mod_ref, n1_ref, win_ref, gsq_ref, gsk_ref, gqn_ref, gkvn_ref, wuq_ref, wuk_ref, wuv_ref,
            gmq_ref, gmqs_ref, gmk_ref, gmks_ref, c_ref, s1_ref, s2_ref, qs_ref, ks_ref, vs_ref, qm_ref, km_ref, vm_ref)
    _proj_rows(slice(0, T_TOK), *refs)


def _proj_rows(ra, x_ref, mod_ref, n1_ref, win_ref, gsq_ref, gsk_ref, gqn_ref, gkvn_ref,
               wuq_ref, wuk_ref, wuv_ref, gmq_ref, gmqs_ref, gmk_ref, gmks_ref, c_ref, s1_ref, s2_ref,
               qs_ref, ks_ref, vs_ref, qm_ref, km_ref, vm_ref):
    x = x_ref[ra, :]
    mod = mod_ref[...]
    shift1, scale1 = mod[0:1], mod[1:2]
    h = _rms(x) * n1_ref[...] * (1.0 + scale1) + shift1
    z = _dot(h.astype(BF16), win_ref[...])

    def head_norm(t, g, dim):
        ms = jnp.sum(t * t, axis=-1, keepdims=True) * (1.0 / dim)
        return t * lax.rsqrt(ms + EPS) * g

    gsq, gsk = gsq_ref[...] * (HEAD_DIM ** -0.5), gsk_ref[...]
    for hd in range(SWA_HEADS):
        qh = head_norm(z[:, O_QS + hd * LANES: O_QS + (hd + 1) * LANES], gsq, HEAD_DIM)
        qs_ref[ra, hd * LANES:(hd + 1) * LANES] = qh.astype(BF16)
    for kv in range(SWA_KV):
        kh = head_norm(z[:, O_KS + kv * LANES: O_KS + (kv + 1) * LANES], gsk, HEAD_DIM)
        ks_ref[ra, kv * LANES:(kv + 1) * LANES] = kh.astype(BF16)
    vs_ref[ra, :] = z[:, O_VS:O_CQ].astype(BF16)

    cq = (_rms(z[:, O_CQ:O_CKV]) * gqn_ref[...]).astype(BF16)
    ckv = (_rms(z[:, O_CKV:O_KR]) * gkvn_ref[...]).astype(BF16)
    qm = _dot(cq, wuq_ref[...])
    kn = _dot(ckv, wuk_ref[...])
    vm = _dot(ckv, wuv_ref[...])
    lane = lax.broadcasted_iota(jnp.int32, (ra.stop - ra.start, LANES), 1)
    for j in range(MLA_HEADS // 2):
        vv = vm[:, j * LANES:(j + 1) * LANES]
        even = jnp.where(lane < 64, vv, jnp.where(lane == 64, 1.0, 0.0))
        odd = jnp.where(lane < 64, jnp.where(lane == 0, 1.0, 0.0), vv)
        vm_ref[ra, (2 * j) * LANES:(2 * j + 1) * LANES] = even.astype(BF16)
        vm_ref[ra, (2 * j + 1) * LANES:(2 * j + 2) * LANES] = odd.astype(BF16)
    kr = z[:, O_KR:O_KR + LANES]
    krs = z[:, O_KR + LANES:ODD_W]
    cs = c_ref[ra, :]
    sp = s1_ref[ra, :] + s2_ref[ra, :]
    qscale = MLA_QK ** -0.5 * LOG2E
    gcq = cs * (gmq_ref[...] * qscale)
    gsq_r = sp * (gmqs_ref[...] * qscale)
    gck = cs * gmk_ref[...]
    krot = krs * (sp * gmks_ref[...])

    def inv_rms(t):
        return lax.rsqrt(jnp.sum(t * t, axis=-1, keepdims=True) * (1.0 / MLA_QK) + EPS)

    for hd in range(MLA_HEADS):
        sl = slice(hd * LANES, (hd + 1) * LANES)
        sw = slice((MLA_HEADS + hd) * LANES, (MLA_HEADS + hd + 1) * LANES)
        qh = qm[:, sl]
        qm_ref[ra, sl] = ((qh * gcq + qm[:, sw] * gsq_r) * inv_rms(qh)).astype(BF16)
        kh = kn[:, sl] + kr
        km_ref[ra, sl] = ((kh * gck + krot) * inv_rms(kh)).astype(BF16)


def _proj_layer(x, mod, n1, w_in, gsq, gsk, gqn, gkvn, wuq, wuk, wuv, gmq, gmqs, gmk, gmks, tc, ts1, ts2):
    def tok(wd):
        return pl.BlockSpec((T_TOK, wd), lambda i: (i, 0))
    widths = (1024, 256, 256, 1024, 1024, 1024)
    return pl.pallas_call(
        _proj_kernel,
        out_shape=tuple(jax.ShapeDtypeStruct((SEQ, wd), BF16) for wd in widths),
        grid=(SEQ // T_TOK,),
        in_specs=[tok(D), _full((8, D)), _full((1, D)), _full((D, ODD_W)),
                  _full((1, LANES)), _full((1, LANES)), _full((1, Q_LORA)), _full((1, KV_LORA)),
                  _full((Q_LORA, 2048)), _full((KV_LORA, 1024)), _full((KV_LORA, 512)),
                  _full((1, LANES)), _full((1, LANES)), _full((1, LANES)), _full((1, LANES)),
                  tok(LANES), tok(LANES), tok(LANES)],
        out_specs=tuple(tok(wd) for wd in widths),
        compiler_params=_cparams(1),
        name="odd_proj",
    )(x, mod, n1, w_in, gsq, gsk, gqn, gkvn, wuq, wuk, wuv, gmq, gmqs, gmk, gmks, tc, ts1, ts2)


def _swa_kernel(sink_ref, q_ref, k_ref, kh_ref, v_ref, vh_ref, o_ref):
    i = pl.program_id(0)
    kcat = jnp.concatenate([kh_ref[...], k_ref[...]], axis=0)
    vcat = jnp.concatenate([vh_ref[...], v_ref[...]], axis=0)
    grp = SWA_HEADS // SWA_KV
    r = lax.broadcasted_iota(jnp.int32, (grp * WINDOW, 2 * WINDOW), 0) & (WINDOW - 1)
    c = lax.broadcasted_iota(jnp.int32, (grp * WINDOW, 2 * WINDOW), 1)
    rel = WINDOW + r - c
    lane = lax.broadcasted_iota(jnp.int32, (WINDOW, LANES), 1)
    for sb in range(T_ATT // WINDOW):
        rows = slice(sb * WINDOW, (sb + 1) * WINDOW)
        kb = kcat[sb * WINDOW: sb * WINDOW + 2 * WINDOW]
        vb = vcat[sb * WINDOW: sb * WINDOW + 2 * WINDOW]
        kpos = i * T_ATT + (sb - 1) * WINDOW + c
        ok = (rel >= 0) & (rel < WINDOW) & (kpos >= 0)
        outs = []
        for kv in range(SWA_KV):
            q = jnp.concatenate([q_ref[rows, (kv * grp + g) * LANES:(kv * grp + g + 1) * LANES]
                                 for g in range(grp)], axis=0)
            sink = jnp.concatenate([jnp.full((WINDOW, 1), sink_ref[kv * grp + g], F32)
                                    for g in range(grp)], axis=0)
            s = jnp.where(ok, _dot_nt(q, kb[:, kv * LANES:(kv + 1) * LANES]), NEG)
            m = jnp.maximum(jnp.max(s, axis=-1, keepdims=True), sink)
            e = jnp.exp(s - m)
            den = jnp.sum(e, axis=-1, keepdims=True) + jnp.exp(sink - m)
            p = (e * (1.0 / den)).astype(BF16)
            o = _dot(p, vb[:, kv * LANES:(kv + 1) * LANES])
            outs += [o[g * WINDOW:(g + 1) * WINDOW] for g in range(grp)]
        for j in range(SWA_HEADS // 2):
            o_ref[rows, j * LANES:(j + 1) * LANES] = jnp.where(lane < 64, outs[2 * j], outs[2 * j + 1]).astype(BF16)


def _swa_layer(sinks, qs, ks, vs):
    hb = T_ATT // WINDOW
    return pl.pallas_call(
        _swa_kernel,
        out_shape=jax.ShapeDtypeStruct((SEQ, 512), BF16),
        grid=(SEQ // T_ATT,),
        in_specs=[pl.BlockSpec(memory_space=pltpu.SMEM),
                  pl.BlockSpec((T_ATT, 1024), lambda i: (i, 0)),
                  pl.BlockSpec((T_ATT, 256), lambda i: (i, 0)),
                  pl.BlockSpec((WINDOW, 256), lambda i: (jnp.maximum(i * hb - 1, 0), 0)),
                  pl.BlockSpec((T_ATT, 256), lambda i: (i, 0)),
                  pl.BlockSpec((WINDOW, 256), lambda i: (jnp.maximum(i * hb - 1, 0), 0))],
        out_specs=pl.BlockSpec((T_ATT, 512), lambda i: (i, 0)),
        compiler_params=_cparams(1),
        name="swa_attn",
    )(sinks, qs, ks, ks, vs, vs)


def _mla_kernel(qi_ref, ki_ref, q_ref, k_ref, v_ref, o_ref, m_ref, acc_ref):
    step = pl.program_id(0)
    qi = qi_ref[step]
    ki = ki_ref[step]

    @pl.when(ki == 0)
    def _():
        m_ref[...] = jnp.full(m_ref.shape, NEG, F32)
        acc_ref[...] = jnp.zeros(acc_ref.shape, F32)

    lane = lax.broadcasted_iota(jnp.int32, (T_MQ, LANES), 1)
    lo = lane < 64
    nc = T_MK // LANES
    sub = T_MQ // MLA_SPLIT
    ratio = T_MQ // T_MK
    units = [(hd, part) for hd in range(MLA_HEADS) for part in range(MLA_SPLIT)]

    def scores(unit):
        hd, part = unit
        sl = slice(hd * LANES, (hd + 1) * LANES)
        return _dot_nt(q_ref[part * sub:(part + 1) * sub, sl], k_ref[:, sl])

    def update(masked):
        if masked:
            r = lax.broadcasted_iota(jnp.int32, (sub, LANES), 0)
            lane_s = lax.broadcasted_iota(jnp.int32, (sub, LANES), 1)

        def softmax_part(unit, s):
            hd, part = unit
            rows = slice(part * sub, (part + 1) * sub)
            cols = [s[:, c * LANES:(c + 1) * LANES] for c in range(nc)]
            if masked:
                off = qi * T_MQ - ki * T_MK + part * sub
                cols = [jnp.where(r + off >= lane_s + c * LANES, cols[c], NEG) for c in range(nc)]
            cmax = cols[0]
            for c in range(1, nc):
                cmax = jnp.maximum(cmax, cols[c])
            m_prev = m_ref[hd, rows]
            m_new = jnp.maximum(m_prev, jnp.max(cmax, axis=-1, keepdims=True))
            m_ref[hd, rows] = m_new
            alpha = jnp.exp2(m_prev - m_new)
            p = jnp.concatenate([jnp.exp2(cols[c] - m_new).astype(BF16) for c in range(nc)], axis=-1)
            return p, alpha

        def value_part(unit, p, alpha):
            hd, part = unit
            rows = slice(part * sub, (part + 1) * sub)
            acc_ref[hd, rows] = acc_ref[hd, rows] * alpha + _dot(p, v_ref[:, hd * LANES:(hd + 1) * LANES])

        s_q = [scores(u) for u in units[:MLA_DEPTH]]
        pend = None
        for n, u in enumerate(units):
            if n + MLA_DEPTH < len(units):
                s_q.append(scores(units[n + MLA_DEPTH]))
            cur = softmax_part(u, s_q[n])
            if pend is not None:
                value_part(units[n - 1], *pend)
            pend = cur
        value_part(units[-1], *pend)

    @pl.when(ki < qi * ratio)
    def _():
        update(False)

    @pl.when(ki >= qi * ratio)
    def _():
        update(True)

    @pl.when(ki == qi * ratio + ratio - 1)
    def _():
        for j in range(MLA_HEADS // 2):
            ae = acc_ref[2 * j]
            ao = acc_ref[2 * j + 1]
            out = jnp.where(lo, ae * (1.0 / ae[:, 64:65]), ao * (1.0 / ao[:, 0:1]))
            o_ref[:, j * LANES:(j + 1) * LANES] = out.astype(BF16)


def _mla_layer(qm, km, vm):
    nb = SEQ // T_MQ
    ratio = T_MQ // T_MK
    qi = np.concatenate([np.full(ratio * (n + 1), n, np.int32) for n in range(nb)])
    ki = np.concatenate([np.arange(ratio * (n + 1), dtype=np.int32) for n in range(nb)])
    grid_spec = pltpu.PrefetchScalarGridSpec(
        num_scalar_prefetch=2,
        grid=(int(qi.shape[0]),),
        in_specs=[pl.BlockSpec((T_MQ, 1024), lambda s, qi, ki: (qi[s], 0)),
                  pl.BlockSpec((T_MK, 1024), lambda s, qi, ki: (ki[s], 0)),
                  pl.BlockSpec((T_MK, 1024), lambda s, qi, ki: (ki[s], 0))],
        out_specs=pl.BlockSpec((T_MQ, 512), lambda s, qi, ki: (qi[s], 0)),
        scratch_shapes=[pltpu.VMEM((MLA_HEADS, T_MQ, LANES), F32),
                        pltpu.VMEM((MLA_HEADS, T_MQ, LANES), F32)],
    )
    return pl.pallas_call(
        _mla_kernel,
        out_shape=jax.ShapeDtypeStruct((SEQ, 512), BF16),
        grid_spec=grid_spec,
        compiler_params=_cparams(1),
        name="mla_attn",
    )(jnp.asarray(qi), jnp.asarray(ki), qm, km, vm)


def _post_kernel(x_ref, os_ref, om_ref, mod_ref, n2_ref, wout_ref, wrh_ref, wrl_ref, br_ref,
                 xo_ref, h2_ref, meta_ref, cnt_ref, carry_ref):
    _init_carry(pl.program_id(0), carry_ref)
    for sub in range(N_SUB):
        rows_at = slice(sub * T_SUB, (sub + 1) * T_SUB)
        y = _dot(os_ref[rows_at, :], wout_ref[0:512, :]) + _dot(om_ref[rows_at, :], wout_ref[512:1024, :])
        _tail(x_ref[rows_at, :], y, mod_ref[...], n2_ref[...], wrh_ref[...], wrl_ref[...], br_ref[...],
              rows_at, xo_ref, h2_ref, meta_ref, cnt_ref, carry_ref)


def _post_layer(x, o_s, o_m, mod, n2, w_out, wr_hi, wr_lo, br):
    shapes, specs, scratch = _tail_out(T_TOK)
    return pl.pallas_call(
        _post_kernel,
        out_shape=shapes,
        grid=(SEQ // T_TOK,),
        in_specs=[pl.BlockSpec((T_TOK, D), lambda i: (i, 0)),
                  pl.BlockSpec((T_TOK, 512), lambda i: (i, 0)),
                  pl.BlockSpec((T_TOK, 512), lambda i: (i, 0)),
                  _full((8, D)), _full((1, D)), _full((D, D)),
                  _full((D, LANES)), _full((D, LANES)), _full((1, LANES))],
        out_specs=specs,
        scratch_shapes=scratch,
        compiler_params=_cparams(1),
        name="odd_post",
    )(x, o_s, o_m, mod, n2, w_out, wr_hi, wr_lo, br)


def _dispatch_plan(meta, cnt):
    grp = meta[:, EPG].astype(jnp.int32)
    rank = meta[:, EPG + 1].astype(jnp.int32)
    counts = cnt[0, :N_GROUPS].astype(jnp.int32)
    padded = ((counts + T_MOE - 1) // T_MOE) * T_MOE
    ends = jnp.cumsum(padded)
    pos = (ends - padded)[grp] + rank
    n_used = ends[-1] // T_MOE
    tile_start = jnp.arange(N_TILES, dtype=jnp.int32) * T_MOE
    tile_group = jnp.minimum(jnp.sum(tile_start[:, None] >= ends[None, :], axis=1), N_GROUPS - 1)
    return pos.reshape(SEQ // T_DISP, 1, T_DISP), tile_group.astype(jnp.int32), n_used.reshape(1)


def _row_copies(n, src_at, dst_at, sem):
    for r in range(n):
        pltpu.make_async_copy(src_at(r), dst_at(r), sem).start(priority=r % 2)


def _disp_kernel(pos_ref, x_ref, init_ref, o_ref, sem):
    del init_ref
    _row_copies(T_DISP,
                lambda r: x_ref.at[pl.ds(r, 1), :],
                lambda r: o_ref.at[pl.ds(pos_ref[0, 0, r], 1), :], sem)
    pltpu.make_async_copy(x_ref, o_ref.at[pl.ds(0, T_DISP), :], sem).wait()


def _dispatch(pos, h2a, hs):
    return pl.pallas_call(
        _disp_kernel,
        out_shape=jax.ShapeDtypeStruct((N_SORT, ROW_W), F32),
        grid=(SEQ // T_DISP,),
        in_specs=[pl.BlockSpec((1, 1, T_DISP), lambda i: (i, 0, 0), memory_space=pltpu.SMEM),
                  pl.BlockSpec((T_DISP, ROW_W), lambda i: (i, 0)),
                  pl.BlockSpec(memory_space=pl.ANY)],
        out_specs=pl.BlockSpec(memory_space=pl.ANY),
        scratch_shapes=[pltpu.SemaphoreType.DMA(())],
        input_output_aliases={2: 0},
        compiler_params=_cparams(1),
        name="moe_dispatch",
    )(pos, h2a, hs)


def _moe_kernel(tg_ref, nu_ref, x_ref, wg_ref, wu_ref, wd_ref, o_ref, xb_ref, act_ref):
    del tg_ref
    i = pl.program_id(0)
    j = pl.program_id(1)
    last = EPG // E_STEP - 1

    @pl.when(i < nu_ref[0])
    def _():
        @pl.when(j == 0)
        def _():
            xb_ref[...] = x_ref[:, 0:D].astype(BF16)

        xb = xb_ref[...]
        meta = x_ref[:, D:ROW_W]
        lane = lax.broadcasted_iota(jnp.int32, meta.shape, 1)
        for k in range(E_STEP):
            a = _dot(xb, wg_ref[0, k].astype(BF16))
            u = _dot(xb, wu_ref[0, k].astype(BF16))
            gate = jnp.sum(jnp.where(lane == j * E_STEP + k, meta, 0.0), axis=-1, keepdims=True)
            act = (a * jax.nn.sigmoid(a) * u * gate).astype(BF16)
            for jj in range(EPG // E_STEP):
                @pl.when(j == jj)
                def _(jj=jj, k=k, act=act):
                    act_ref[:, (jj * E_STEP + k) * FF:(jj * E_STEP + k + 1) * FF] = act

        @pl.when(j == last)
        def _():
            o_ref[...] = _dot(act_ref[...], wd_ref[0, 0].astype(BF16))

    @pl.when((i >= nu_ref[0]) & (j == last))
    def _():
        o_ref[...] = jnp.zeros(o_ref.shape, F32)


def _moe_experts(tile_group, n_used, hs, wg, wu, wd, layer):
    def tile(i, e, tg, nu):
        return (jnp.minimum(i, nu[0] - 1), 0)

    nj = EPG // E_STEP

    def expert(i, j, tg, nu):
        return (layer, jnp.where(i < nu[0], tg[i] * nj + j, tg[nu[0] - 1] * nj + nj - 1), 0, 0)

    def group(i, j, tg, nu):
        return (layer, tg[jnp.minimum(i, nu[0] - 1)], 0, 0)

    grid_spec = pltpu.PrefetchScalarGridSpec(
        num_scalar_prefetch=2,
        grid=(N_TILES, nj),
        in_specs=[pl.BlockSpec((T_MOE, ROW_W), tile),
                  pl.BlockSpec((1, E_STEP, D, FF), expert),
                  pl.BlockSpec((1, E_STEP, D, FF), expert),
                  pl.BlockSpec((1, 1, EPG * FF, D), group, pipeline_mode=pl.Buffered(1))],
        out_specs=pl.BlockSpec((T_MOE, D), lambda i, e, tg, nu: (i, 0)),
        scratch_shapes=[pltpu.VMEM((T_MOE, D), BF16), pltpu.VMEM((T_MOE, EPG * FF), BF16)],
    )
    return pl.pallas_call(
        _moe_kernel,
        out_shape=jax.ShapeDtypeStruct((N_SORT, D), F32),
        grid_spec=grid_spec,
        compiler_params=pltpu.CompilerParams(dimension_semantics=("arbitrary", "arbitrary"),
                                             vmem_limit_bytes=MOE_VMEM_LIMIT),
        name="moe_experts",
    )(tile_group, n_used, hs, wg, wu, wd.reshape(DEPTH, N_GROUPS, EPG * FF, D))


def _comb_kernel(pos_ref, x_ref, mod_ref, y_ref, o_ref, buf_ref, sem):
    _row_copies(T_DISP,
                lambda r: y_ref.at[pl.ds(pos_ref[0, 0, r], 1), :],
                lambda r: buf_ref.at[pl.ds(r, 1), :], sem)
    pltpu.make_async_copy(y_ref.at[pl.ds(0, T_DISP), :], buf_ref, sem).wait()
    o_ref[...] = x_ref[...] + mod_ref[5:6, :] * buf_ref[...]


def _combine(pos, x, mod, ys):
    return pl.pallas_call(
        _comb_kernel,
        out_shape=jax.ShapeDtypeStruct((SEQ, D), F32),
        grid=(SEQ // T_DISP,),
        in_specs=[pl.BlockSpec((1, 1, T_DISP), lambda i: (i, 0, 0), memory_space=pltpu.SMEM),
                  pl.BlockSpec((T_DISP, D), lambda i: (i, 0)),
                  pl.BlockSpec((8, D), lambda i: (0, 0)),
                  pl.BlockSpec(memory_space=pl.ANY)],
        out_specs=pl.BlockSpec((T_DISP, D), lambda i: (i, 0)),
        scratch_shapes=[pltpu.VMEM((T_DISP, D), F32), pltpu.SemaphoreType.DMA(())],
        compiler_params=_cparams(1),
        name="moe_combine",
    )(pos, x, mod, ys)


def _moe_layer(x, h2a, meta, cnt, mod, wg, wu, wd, layer, hs):
    pos, tile_group, n_used = _dispatch_plan(meta, cnt)
    hs = _dispatch(pos, h2a, hs)
    ys = _moe_experts(tile_group, n_used, hs, wg, wu, wd, layer)
    return _combine(pos, x, mod, ys), hs


def _pad_heads(w, heads, dim):
    k = w.shape[0]
    w = w.reshape(k, heads, dim)
    return jnp.pad(w, ((0, 0), (0, 0), (0, LANES - dim))).reshape(k, heads * LANES)


def _pad_gain(g):
    return jnp.pad(g, (0, LANES - g.shape[0])).reshape(1, LANES)


def _odd_weights(w_in):
    q_s = _pad_heads(w_in[:, 0:512], SWA_HEADS, HEAD_DIM)
    k_s = _pad_heads(w_in[:, 512:640], SWA_KV, HEAD_DIM)
    v = w_in[:, 640:768]
    v_s = jnp.concatenate([v[:, 0:64], v[:, 0:64], v[:, 64:128], v[:, 64:128]], axis=1)
    c_q = w_in[:, 768:1152]
    c_kv = w_in[:, 1152:1408]
    k_r = jnp.pad(w_in[:, 1408:1440], ((0, 0), (MLA_NOPE, LANES - MLA_QK)))
    k_r_sw = jnp.pad(_swap_halves(w_in[:, 1408:1440]), ((0, 0), (MLA_NOPE, LANES - MLA_QK)))
    return jnp.concatenate([q_s, k_s, v_s, c_q, c_kv, k_r, k_r_sw], axis=1).astype(BF16)


def _swap_halves(t):
    half = t.shape[-1] // 2
    return jnp.concatenate([t[..., half:], t[..., :half]], axis=-1)


def _uq_weights(w_uq):
    w = w_uq.reshape(Q_LORA, MLA_HEADS, MLA_QK)
    plain = jnp.pad(w, ((0, 0), (0, 0), (0, LANES - MLA_QK)))
    swapped = jnp.pad(_swap_halves(w[:, :, MLA_NOPE:]), ((0, 0), (0, 0), (MLA_NOPE, LANES - MLA_QK)))
    return jnp.concatenate([plain.reshape(Q_LORA, -1), swapped.reshape(Q_LORA, -1)], axis=1)


def _rope_gains(g):
    plain = jnp.pad(g, (0, LANES - MLA_QK)).reshape(1, LANES)
    partner = jnp.pad(_swap_halves(g[MLA_NOPE:]), (MLA_NOPE, LANES - MLA_QK)).reshape(1, LANES)
    return plain, partner


def _router_weights(w_group, b_group, w_expert, b_expert):
    w = jnp.pad(jnp.concatenate([w_expert, w_group], axis=1), ((0, 0), (0, LANES - N_EXPERTS - N_GROUPS)))
    hi = w.astype(BF16)
    lo = (w - hi.astype(F32)).astype(BF16)
    b = jnp.pad(jnp.concatenate([b_expert, b_group]), (0, LANES - N_EXPERTS - N_GROUPS)).reshape(1, LANES)
    return hi, lo, b


def kernel(x, c, positions, ada_w, ada_b, norm1_g, norm2_g, cp_w_in, conv_w, pool_w, pool_scale,
           cp_w_out, at_w_in, swa_q_g, swa_k_g, swa_sinks, mla_q_norm_g, mla_kv_norm_g, mla_w_uq,
           mla_w_ukv, mla_q_g, mla_k_g, at_w_out, moe_w_group, moe_b_group, moe_w_expert,
           moe_b_expert, moe_w_gate, moe_w_up, moe_w_down):
    xs = x.reshape(SEQ, D)
    mods = _ada_mod(c, ada_w, ada_b)
    tc, ts1, ts2 = _rope_tables(positions)
    hs = jnp.zeros((N_SORT, ROW_W), F32)
    for l in range(DEPTH):
        i = l // 2
        mod = mods[l]
        n1 = norm1_g[l].reshape(1, D)
        n2 = norm2_g[l].reshape(1, D)
        wr_hi, wr_lo, br = _router_weights(moe_w_group[l], moe_b_group[l], moe_w_expert[l], moe_b_expert[l])
        if l % 2 == 0:
            xs, h2a, meta, cnt = _even_layer(
                xs, mod, n1, n2, cp_w_in[i].astype(BF16), conv_w[i], pool_w[i].astype(BF16),
                pool_scale[i].reshape(1, 4 * POOL_G), cp_w_out[i].astype(BF16), wr_hi, wr_lo, br)
        else:
            ukv = mla_w_ukv[i].reshape(KV_LORA, MLA_HEADS, MLA_NOPE + MLA_V)
            wuk = _pad_heads(ukv[:, :, :MLA_NOPE].reshape(KV_LORA, MLA_HEADS * MLA_NOPE), MLA_HEADS, MLA_NOPE)
            wuv = ukv[:, :, MLA_NOPE:].reshape(KV_LORA, MLA_HEADS * MLA_V)
            wuq = _uq_weights(mla_w_uq[i])
            gmq, gmqs = _rope_gains(mla_q_g[i])
            gmk, gmks = _rope_gains(mla_k_g[i])
            qs, ks, vs, qm, km, vm = _proj_layer(
                xs, mod, n1, _odd_weights(at_w_in[i]), _pad_gain(swa_q_g[i]), _pad_gain(swa_k_g[i]),
                mla_q_norm_g[i].reshape(1, Q_LORA), mla_kv_norm_g[i].reshape(1, KV_LORA),
                wuq.astype(BF16), wuk.astype(BF16), wuv.astype(BF16),
                gmq, gmqs, gmk, gmks, tc, ts1, ts2)
            o_s = _swa_layer(swa_sinks[i], qs, ks, vs)
            o_m = _mla_layer(qm, km, vm)
            xs, h2a, meta, cnt = _post_layer(xs, o_s, o_m, mod, n2, at_w_out[i].astype(BF16), wr_hi, wr_lo, br)
        xs, hs = _moe_layer(xs, h2a, meta, cnt, mod, moe_w_gate, moe_w_up, moe_w_down, l, hs)
    return xs.reshape(1, SEQ, D)
```

```python
import functools

import numpy as np
import jax
import jax.numpy as jnp
from jax import lax
from jax.experimental import pallas as pl
from jax.experimental.pallas import tpu as pltpu

F32 = jnp.float32
BF16 = jnp.bfloat16

D = 1024
SEQ = 16384
DEPTH = 4
EPS = 1e-6
LANES = 128
CONV_CH = 512
POOL_WINDOWS = (2, 4, 8, 16)
POOL_G = 128
HALO = 16
SWA_HEADS = 8
SWA_KV = 2
HEAD_DIM = 64
WINDOW = 128
MLA_HEADS = 8
MLA_NOPE = 64
MLA_ROPE = 32
MLA_QK = MLA_NOPE + MLA_ROPE
MLA_V = 64
Q_LORA = 384
KV_LORA = 256
ROPE_THETA = 10000.0
N_GROUPS = 4
EPG = 8
N_EXPERTS = N_GROUPS * EPG
FF = 256
NEG = -1e30
LOG2E = 1.4426950408889634

T_TOK = 512
N_SUB = 2
T_SUB = T_TOK // N_SUB
T_ATT = 512
T_MOE = 512
T_DISP = 1024
ROW_W = D + LANES
N_SORT = SEQ + N_GROUPS * T_MOE
N_TILES = N_SORT // T_MOE
E_STEP = 8
T_MQ = 1024
T_MK = 512
MLA_SPLIT = 1
MLA_DEPTH = 3
VMEM_LIMIT = 48 * 1024 * 1024
MOE_VMEM_LIMIT = 56 * 1024 * 1024

HEADS_W = MLA_HEADS * LANES
KV_W = SWA_KV * LANES
ATT_W = MLA_HEADS * MLA_V
O_QS = 0
O_KS = O_QS + HEADS_W
O_VS = O_KS + KV_W
O_CQ = O_VS + KV_W
O_CKV = O_CQ + Q_LORA
O_KR = O_CKV + KV_LORA
ODD_W = O_KR + 2 * LANES


def _cparams(n_axes=1):
    return pltpu.CompilerParams(dimension_semantics=("arbitrary",) * n_axes,
                                vmem_limit_bytes=VMEM_LIMIT)


def _rms(x):
    return x * lax.rsqrt(jnp.mean(x * x, axis=-1, keepdims=True) + EPS)


def _dot(a, b):
    return jnp.dot(a, b, preferred_element_type=F32)


def _dot_nt(a, b):
    return lax.dot_general(a, b, (((1,), (1,)), ((), ())), preferred_element_type=F32)


def _ada_kernel(c_ref, w_ref, b_ref, o_ref):
    c = c_ref[...]
    ca = c * jax.nn.sigmoid(c)
    o_ref[0] = jnp.sum(w_ref[0] * ca, axis=0, keepdims=True) + b_ref[0]


def _ada_mod(c, ada_w, ada_b):
    c_col = c.reshape(D, 1)
    b = ada_b.reshape(DEPTH * 6, 1, D)
    out = pl.pallas_call(
        _ada_kernel,
        out_shape=jax.ShapeDtypeStruct((DEPTH * 6, 1, D), F32),
        grid=(DEPTH, 6),
        in_specs=[pl.BlockSpec((D, 1), lambda l, j: (0, 0)),
                  pl.BlockSpec((1, D, D), lambda l, j: (l, 0, j)),
                  pl.BlockSpec((1, 1, D), lambda l, j: (l * 6 + j, 0, 0))],
        out_specs=pl.BlockSpec((1, 1, D), lambda l, j: (l * 6 + j, 0, 0)),
        compiler_params=_cparams(2),
        name="ada_mod",
    )(c_col, ada_w, b)
    mod = out.reshape(DEPTH, 6, D)
    return jnp.pad(mod, ((0, 0), (0, 2), (0, 0)))


def _rope_kernel(pos_ref, inv_ref, c_ref, s1_ref, s2_ref):
    pos = pos_ref[...].astype(F32)
    ang = pos * inv_ref[...]
    lane = lax.broadcasted_iota(jnp.int32, ang.shape, 1)
    cs = jnp.cos(ang)
    sn = jnp.sin(ang)
    mid = MLA_NOPE + MLA_ROPE // 2
    c_ref[...] = jnp.where(lane < MLA_NOPE, 1.0, jnp.where(lane < MLA_QK, cs, 0.0))
    s1_ref[...] = jnp.where((lane >= MLA_NOPE) & (lane < mid), -sn, 0.0)
    s2_ref[...] = jnp.where((lane >= mid) & (lane < MLA_QK), sn, 0.0)


def _rope_tables(positions):
    half = MLA_ROPE // 2
    inv = jnp.power(ROPE_THETA, -jnp.arange(half, dtype=F32) / half)
    inv_lane = jnp.concatenate([jnp.zeros((MLA_NOPE,), F32), inv, inv,
                                jnp.zeros((LANES - MLA_QK,), F32)]).reshape(1, LANES)
    pos = positions.reshape(SEQ, 1)
    shp = jax.ShapeDtypeStruct((SEQ, LANES), F32)
    spec = pl.BlockSpec((T_TOK, LANES), lambda i: (i, 0))
    return pl.pallas_call(
        _rope_kernel,
        out_shape=(shp, shp, shp),
        grid=(SEQ // T_TOK,),
        in_specs=[pl.BlockSpec((T_TOK, 1), lambda i: (i, 0)),
                  pl.BlockSpec((1, LANES), lambda i: (0, 0))],
        out_specs=(spec, spec, spec),
        compiler_params=_cparams(1),
        name="rope_tables",
    )(pos, inv_lane)


def _route(lg):
    lane = lax.broadcasted_iota(jnp.int32, lg.shape, 1)
    lane_f = lane.astype(F32)
    is_g = (lane >= N_EXPERTS) & (lane < N_EXPERTS + N_GROUPS)
    gl = jnp.where(is_g, lg, NEG)
    gmax = jnp.max(gl, axis=-1, keepdims=True)
    gidx = jnp.min(jnp.where(is_g & (gl == gmax), lane_f - N_EXPERTS, 1e3), axis=-1, keepdims=True)
    gsum = jnp.sum(jnp.where(is_g, jnp.exp(gl - gmax), 0.0), axis=-1, keepdims=True)
    gw = 1.0 / gsum
    grp_of_lane = (lane >> 3).astype(F32)
    in_grp = (lane < N_EXPERTS) & (grp_of_lane == gidx)
    el = jnp.where(in_grp, lg, NEG)
    m1 = jnp.max(el, axis=-1, keepdims=True)
    i1 = jnp.min(jnp.where(in_grp & (el == m1), lane_f, 1e3), axis=-1, keepdims=True)
    rest = in_grp & (lane_f != i1)
    el2 = jnp.where(rest, lg, NEG)
    m2 = jnp.max(el2, axis=-1, keepdims=True)
    i2 = jnp.min(jnp.where(rest & (el2 == m2), lane_f, 1e3), axis=-1, keepdims=True)
    r = jnp.exp(m2 - m1)
    w1 = gw / (1.0 + r)
    w2 = w1 * r
    return jnp.where(lane_f == i1, w1, jnp.where(lane_f == i2, w2, 0.0)), gidx


def _tail(x, y, mod, n2, wr_hi, wr_lo, br, rows_at, xo_ref, h2_ref, meta_ref, cnt_ref, carry_ref):
    gate1 = mod[2:3]
    shift2, scale2 = mod[3:4], mod[4:5]
    xn = x + gate1 * y
    xo_ref[rows_at, :] = xn
    h2 = _rms(xn) * n2 * (1.0 + scale2) + shift2
    hi = h2.astype(BF16)
    lo = (h2 - hi.astype(F32)).astype(BF16)
    both = _dot(hi, jnp.concatenate([wr_hi, wr_lo], axis=1))
    lg = both[:, 0:LANES] + both[:, LANES:2 * LANES] + _dot(lo, wr_hi) + br
    gates, gidx = _route(lg)
    rows = lg.shape[0]
    lane = lax.broadcasted_iota(jnp.int32, (rows, LANES), 1)
    lane_f = lane.astype(F32)
    g8 = gates
    for g in range(1, N_GROUPS):
        g8 = g8 + pltpu.roll(gates, LANES - EPG * g, 1)
    onehot = (lane_f == gidx).astype(F32)
    r = lax.broadcasted_iota(jnp.int32, (rows, rows), 0)
    c = lax.broadcasted_iota(jnp.int32, (rows, rows), 1)
    before = jnp.where(c < r, 1.0, 0.0).astype(BF16)
    cum = _dot(before, onehot.astype(BF16)) + carry_ref[0:1, :]
    rank = jnp.sum(jnp.where(lane_f == gidx, cum, 0.0), axis=-1, keepdims=True)
    meta = jnp.where(lane < EPG, g8, jnp.where(lane == EPG, gidx, jnp.where(lane == EPG + 1, rank, 0.0)))
    h2_ref[rows_at, 0:D] = h2
    h2_ref[rows_at, D:D + LANES] = meta
    meta_ref[rows_at, :] = meta
    total = carry_ref[0:1, :] + jnp.sum(onehot, axis=0, keepdims=True)
    carry_ref[...] = jnp.broadcast_to(total, carry_ref.shape)
    cnt_ref[...] = jnp.broadcast_to(total, cnt_ref.shape)


def _even_kernel(x_ref, xh_ref, mod_ref, n1_ref, n2_ref, win_ref, cw_ref, pw_ref, ps_ref,
                 wout_ref, wrh_ref, wrl_ref, br_ref, xo_ref, h2_ref, meta_ref, cnt_ref, carry_ref):
    i = pl.program_id(0)
    _init_carry(i, carry_ref)
    for sub in range(N_SUB):
        rows_at = slice(sub * T_SUB, (sub + 1) * T_SUB)
        halo = xh_ref[...] if sub == 0 else x_ref[sub * T_SUB - HALO: sub * T_SUB, :]
        _even_rows(i * T_TOK + sub * T_SUB, x_ref[rows_at, :], halo, rows_at, mod_ref, n1_ref, n2_ref, win_ref,
                   cw_ref, pw_ref, ps_ref, wout_ref, wrh_ref, wrl_ref, br_ref, xo_ref, h2_ref, meta_ref,
                   cnt_ref, carry_ref)


def _init_carry(i, carry_ref):
    @pl.when(i == 0)
    def _():
        carry_ref[...] = jnp.zeros(carry_ref.shape, F32)


def _even_rows(tok0, x, halo, rows_at, mod_ref, n1_ref, n2_ref, win_ref, cw_ref, pw_ref, ps_ref,
               wout_ref, wrh_ref, wrl_ref, br_ref, xo_ref, h2_ref, meta_ref, cnt_ref, carry_ref):
    mod = mod_ref[...]
    shift1, scale1 = mod[0:1], mod[1:2]
    xa = jnp.concatenate([halo, x], axis=0)
    h = _rms(xa) * n1_ref[...] * (1.0 + scale1) + shift1
    z = _dot(h.astype(BF16), win_ref[...])
    rows = T_SUB + HALO
    row = lax.broadcasted_iota(jnp.int32, (rows, 1), 0)
    tpos = tok0 + row - HALO
    live = (tpos >= 0).astype(F32)
    bg = z[:, 0:CONV_CH]
    v = z[:, CONV_CH:2 * CONV_CH] * z[:, 2 * CONV_CH:3 * CONV_CH] * live
    cw = cw_ref[...]
    conv = v * cw[0:1] + pltpu.roll(v, 1, 0) * cw[1:2] + pltpu.roll(v, 2, 0) * cw[2:3]
    parts = [(bg * conv)[HALO:]]
    ps = ps_ref[...]
    tcount = (tpos + 1).astype(F32)
    for gi, w in enumerate(POOL_WINDOWS):
        ug = z[:, 3 * CONV_CH + gi * POOL_G: 3 * CONV_CH + (gi + 1) * POOL_G] * live
        s = ug
        k = 1
        while k < w:
            s = s + pltpu.roll(s, k, 0)
            k *= 2
        inv = 1.0 / jnp.minimum(tcount, float(w))
        d = (s * inv - ug)[HALO:].astype(BF16)
        parts.append(_dot(d, pw_ref[gi]) * ps[:, gi * POOL_G:(gi + 1) * POOL_G])
    cat = jnp.concatenate(parts, axis=-1).astype(BF16)
    y = _dot(cat, wout_ref[...])
    _tail(x, y, mod, n2_ref[...], wrh_ref[...], wrl_ref[...], br_ref[...], rows_at, xo_ref, h2_ref, meta_ref,
          cnt_ref, carry_ref)


def _tail_out(n):
    shapes = (jax.ShapeDtypeStruct((SEQ, D), F32), jax.ShapeDtypeStruct((SEQ, ROW_W), F32),
              jax.ShapeDtypeStruct((SEQ, LANES), F32), jax.ShapeDtypeStruct((8, LANES), F32))
    specs = (pl.BlockSpec((n, D), lambda i: (i, 0)), pl.BlockSpec((n, ROW_W), lambda i: (i, 0)),
             pl.BlockSpec((n, LANES), lambda i: (i, 0)), pl.BlockSpec((8, LANES), lambda i: (0, 0)))
    scratch = [pltpu.VMEM((8, LANES), F32)]
    return shapes, specs, scratch


def _full(shape):
    nd = len(shape)
    return pl.BlockSpec(shape, lambda i: (0,) * nd)


def _even_layer(x, mod, n1, n2, w_in, conv_w, pool_w, pool_scale, w_out, wr_hi, wr_lo, br):
    shapes, specs, scratch = _tail_out(T_TOK)
    hb = T_TOK // HALO
    return pl.pallas_call(
        _even_kernel,
        out_shape=shapes,
        grid=(SEQ // T_TOK,),
        in_specs=[pl.BlockSpec((T_TOK, D), lambda i: (i, 0)),
                  pl.BlockSpec((HALO, D), lambda i: (jnp.maximum(i * hb - 1, 0), 0)),
                  _full((8, D)), _full((1, D)), _full((1, D)),
                  _full((D, 4 * CONV_CH)), _full((3, CONV_CH)), _full((4, POOL_G, POOL_G)),
                  _full((1, 4 * POOL_G)), _full((D, D)),
                  _full((D, LANES)), _full((D, LANES)), _full((1, LANES))],
        out_specs=specs,
        scratch_shapes=scratch,
        compiler_params=_cparams(1),
        name="even_mixer",
    )(x, x, mod, n1, n2, w_in, conv_w, pool_w, pool_scale, w_out, wr_hi, wr_lo, br)


def _proj_kernel(x_ref, mod_ref, n1_ref, win_ref, gsq_ref, gsk_ref, gqn_ref, gkvn_ref,
                 wuq_ref, wuk_ref, wuv_ref, gmq_ref, gmqs_ref, gmk_ref, gmks_ref, c_ref, s1_ref, s2_ref,
                 qs_ref, ks_ref, vs_ref, qm_ref, km_ref, vm_ref):
    refs = (x_ref, mod_ref, n1_ref, win_ref, gsq_ref, gsk_ref, gqn_ref, gkvn_ref, wuq_ref, wuk_ref, wuv_ref,
            gmq_ref, gmqs_ref, gmk_ref, gmks_ref, c_ref, s1_ref, s2_ref, qs_ref, ks_ref, vs_ref, qm_ref, km_ref, vm_ref)
    _proj_rows(slice(0, T_TOK), *refs)


def _proj_rows(ra, x_ref, mod_ref, n1_ref, win_ref, gsq_ref, gsk_ref, gqn_ref, gkvn_ref,
               wuq_ref, wuk_ref, wuv_ref, gmq_ref, gmqs_ref, gmk_ref, gmks_ref, c_ref, s1_ref, s2_ref,
               qs_ref, ks_ref, vs_ref, qm_ref, km_ref, vm_ref):
    x = x_ref[ra, :]
    mod = mod_ref[...]
    shift1, scale1 = mod[0:1], mod[1:2]
    h = _rms(x) * n1_ref[...] * (1.0 + scale1) + shift1
    z = _dot(h.astype(BF16), win_ref[...])

    def head_norm(t, g, dim):
        ms = jnp.sum(t * t, axis=-1, keepdims=True) * (1.0 / dim)
        return t * lax.rsqrt(ms + EPS) * g

    gsq, gsk = gsq_ref[...] * (HEAD_DIM ** -0.5), gsk_ref[...]
    for hd in range(SWA_HEADS):
        qh = head_norm(z[:, O_QS + hd * LANES: O_QS + (hd + 1) * LANES], gsq, HEAD_DIM)
        qs_ref[ra, hd * LANES:(hd + 1) * LANES] = qh.astype(BF16)
    for kv in range(SWA_KV):
        kh = head_norm(z[:, O_KS + kv * LANES: O_KS + (kv + 1) * LANES], gsk, HEAD_DIM)
        ks_ref[ra, kv * LANES:(kv + 1) * LANES] = kh.astype(BF16)
    vs_ref[ra, :] = z[:, O_VS:O_CQ].astype(BF16)

    cq = (_rms(z[:, O_CQ:O_CKV]) * gqn_ref[...]).astype(BF16)
    ckv = (_rms(z[:, O_CKV:O_KR]) * gkvn_ref[...]).astype(BF16)
    qm = _dot(cq, wuq_ref[...])
    kn = _dot(ckv, wuk_ref[...])
    vm = _dot(ckv, wuv_ref[...])
    lane = lax.broadcasted_iota(jnp.int32, (ra.stop - ra.start, LANES), 1)
    for j in range(MLA_HEADS // 2):
        vv = vm[:, j * LANES:(j + 1) * LANES]
        even = jnp.where(lane < MLA_V, vv, jnp.where(lane == MLA_V, 1.0, 0.0))
        odd = jnp.where(lane < MLA_V, jnp.where(lane == 0, 1.0, 0.0), vv)
        vm_ref[ra, (2 * j) * LANES:(2 * j + 1) * LANES] = even.astype(BF16)
        vm_ref[ra, (2 * j + 1) * LANES:(2 * j + 2) * LANES] = odd.astype(BF16)
    kr = z[:, O_KR:O_KR + LANES]
    krs = z[:, O_KR + LANES:ODD_W]
    cs = c_ref[ra, :]
    sp = s1_ref[ra, :] + s2_ref[ra, :]
    qscale = MLA_QK ** -0.5 * LOG2E
    gcq = cs * (gmq_ref[...] * qscale)
    gsq_r = sp * (gmqs_ref[...] * qscale)
    gck = cs * gmk_ref[...]
    krot = krs * (sp * gmks_ref[...])

    def inv_rms(t):
        return lax.rsqrt(jnp.sum(t * t, axis=-1, keepdims=True) * (1.0 / MLA_QK) + EPS)

    for hd in range(MLA_HEADS):
        sl = slice(hd * LANES, (hd + 1) * LANES)
        sw = slice((MLA_HEADS + hd) * LANES, (MLA_HEADS + hd + 1) * LANES)
        qh = qm[:, sl]
        qm_ref[ra, sl] = ((qh * gcq + qm[:, sw] * gsq_r) * inv_rms(qh)).astype(BF16)
        kh = kn[:, sl] + kr
        km_ref[ra, sl] = ((kh * gck + krot) * inv_rms(kh)).astype(BF16)


def _proj_layer(x, mod, n1, w_in, gsq, gsk, gqn, gkvn, wuq, wuk, wuv, gmq, gmqs, gmk, gmks, tc, ts1, ts2):
    def tok(wd):
        return pl.BlockSpec((T_TOK, wd), lambda i: (i, 0))
    widths = (HEADS_W, KV_W, KV_W, HEADS_W, HEADS_W, HEADS_W)
    return pl.pallas_call(
        _proj_kernel,
        out_shape=tuple(jax.ShapeDtypeStruct((SEQ, wd), BF16) for wd in widths),
        grid=(SEQ // T_TOK,),
        in_specs=[tok(D), _full((8, D)), _full((1, D)), _full((D, ODD_W)),
                  _full((1, LANES)), _full((1, LANES)), _full((1, Q_LORA)), _full((1, KV_LORA)),
                  _full((Q_LORA, 2 * HEADS_W)), _full((KV_LORA, HEADS_W)), _full((KV_LORA, ATT_W)),
                  _full((1, LANES)), _full((1, LANES)), _full((1, LANES)), _full((1, LANES)),
                  tok(LANES), tok(LANES), tok(LANES)],
        out_specs=tuple(tok(wd) for wd in widths),
        compiler_params=_cparams(1),
        name="odd_proj",
    )(x, mod, n1, w_in, gsq, gsk, gqn, gkvn, wuq, wuk, wuv, gmq, gmqs, gmk, gmks, tc, ts1, ts2)


def _swa_kernel(sink_ref, q_ref, k_ref, kh_ref, v_ref, vh_ref, o_ref):
    i = pl.program_id(0)
    kcat = jnp.concatenate([kh_ref[...], k_ref[...]], axis=0)
    vcat = jnp.concatenate([vh_ref[...], v_ref[...]], axis=0)
    grp = SWA_HEADS // SWA_KV
    r = lax.broadcasted_iota(jnp.int32, (grp * WINDOW, 2 * WINDOW), 0) & (WINDOW - 1)
    c = lax.broadcasted_iota(jnp.int32, (grp * WINDOW, 2 * WINDOW), 1)
    rel = WINDOW + r - c
    lane = lax.broadcasted_iota(jnp.int32, (WINDOW, LANES), 1)
    for sb in range(T_ATT // WINDOW):
        rows = slice(sb * WINDOW, (sb + 1) * WINDOW)
        kb = kcat[sb * WINDOW: sb * WINDOW + 2 * WINDOW]
        vb = vcat[sb * WINDOW: sb * WINDOW + 2 * WINDOW]
        kpos = i * T_ATT + (sb - 1) * WINDOW + c
        ok = (rel >= 0) & (rel < WINDOW) & (kpos >= 0)
        outs = []
        for kv in range(SWA_KV):
            q = jnp.concatenate([q_ref[rows, (kv * grp + g) * LANES:(kv * grp + g + 1) * LANES]
                                 for g in range(grp)], axis=0)
            sink = jnp.concatenate([jnp.full((WINDOW, 1), sink_ref[kv * grp + g], F32)
                                    for g in range(grp)], axis=0)
            s = jnp.where(ok, _dot_nt(q, kb[:, kv * LANES:(kv + 1) * LANES]), NEG)
            m = jnp.maximum(jnp.max(s, axis=-1, keepdims=True), sink)
            e = jnp.exp(s - m)
            den = jnp.sum(e, axis=-1, keepdims=True) + jnp.exp(sink - m)
            p = (e * (1.0 / den)).astype(BF16)
            o = _dot(p, vb[:, kv * LANES:(kv + 1) * LANES])
            outs += [o[g * WINDOW:(g + 1) * WINDOW] for g in range(grp)]
        for j in range(SWA_HEADS // 2):
            pair = jnp.where(lane < HEAD_DIM, outs[2 * j], outs[2 * j + 1])
            o_ref[rows, j * LANES:(j + 1) * LANES] = pair.astype(BF16)


def _swa_layer(sinks, qs, ks, vs):
    hb = T_ATT // WINDOW
    return pl.pallas_call(
        _swa_kernel,
        out_shape=jax.ShapeDtypeStruct((SEQ, ATT_W), BF16),
        grid=(SEQ // T_ATT,),
        in_specs=[pl.BlockSpec(memory_space=pltpu.SMEM),
                  pl.BlockSpec((T_ATT, HEADS_W), lambda i: (i, 0)),
                  pl.BlockSpec((T_ATT, KV_W), lambda i: (i, 0)),
                  pl.BlockSpec((WINDOW, KV_W), lambda i: (jnp.maximum(i * hb - 1, 0), 0)),
                  pl.BlockSpec((T_ATT, KV_W), lambda i: (i, 0)),
                  pl.BlockSpec((WINDOW, KV_W), lambda i: (jnp.maximum(i * hb - 1, 0), 0))],
        out_specs=pl.BlockSpec((T_ATT, ATT_W), lambda i: (i, 0)),
        compiler_params=_cparams(1),
        name="swa_attn",
    )(sinks, qs, ks, ks, vs, vs)


def _mla_kernel(qi_ref, ki_ref, q_ref, k_ref, v_ref, o_ref, m_ref, acc_ref):
    step = pl.program_id(0)
    qi = qi_ref[step]
    ki = ki_ref[step]

    @pl.when(ki == 0)
    def _():
        m_ref[...] = jnp.full(m_ref.shape, NEG, F32)
        acc_ref[...] = jnp.zeros(acc_ref.shape, F32)

    lane = lax.broadcasted_iota(jnp.int32, (T_MQ, LANES), 1)
    lo = lane < MLA_V
    nc = T_MK // LANES
    sub = T_MQ // MLA_SPLIT
    ratio = T_MQ // T_MK
    units = [(hd, part) for hd in range(MLA_HEADS) for part in range(MLA_SPLIT)]

    def scores(unit):
        hd, part = unit
        sl = slice(hd * LANES, (hd + 1) * LANES)
        return _dot_nt(q_ref[part * sub:(part + 1) * sub, sl], k_ref[:, sl])

    def update(masked):
        if masked:
            r = lax.broadcasted_iota(jnp.int32, (sub, LANES), 0)
            lane_s = lax.broadcasted_iota(jnp.int32, (sub, LANES), 1)

        def softmax_part(unit, s):
            hd, part = unit
            rows = slice(part * sub, (part + 1) * sub)
            cols = [s[:, c * LANES:(c + 1) * LANES] for c in range(nc)]
            if masked:
                off = qi * T_MQ - ki * T_MK + part * sub
                cols = [jnp.where(r + off >= lane_s + c * LANES, cols[c], NEG) for c in range(nc)]
            cmax = cols[0]
            for c in range(1, nc):
                cmax = jnp.maximum(cmax, cols[c])
            m_prev = m_ref[hd, rows]
            m_new = jnp.maximum(m_prev, jnp.max(cmax, axis=-1, keepdims=True))
            m_ref[hd, rows] = m_new
            alpha = jnp.exp2(m_prev - m_new)
            p = jnp.concatenate([jnp.exp2(cols[c] - m_new).astype(BF16) for c in range(nc)], axis=-1)
            return p, alpha

        def value_part(unit, p, alpha):
            hd, part = unit
            rows = slice(part * sub, (part + 1) * sub)
            acc_ref[hd, rows] = acc_ref[hd, rows] * alpha + _dot(p, v_ref[:, hd * LANES:(hd + 1) * LANES])

        s_q = [scores(u) for u in units[:MLA_DEPTH]]
        pend = None
        for n, u in enumerate(units):
            if n + MLA_DEPTH < len(units):
                s_q.append(scores(units[n + MLA_DEPTH]))
            cur = softmax_part(u, s_q[n])
            if pend is not None:
                value_part(units[n - 1], *pend)
            pend = cur
        value_part(units[-1], *pend)

    @pl.when(ki < qi * ratio)
    def _():
        update(False)

    @pl.when(ki >= qi * ratio)
    def _():
        update(True)

    @pl.when(ki == qi * ratio + ratio - 1)
    def _():
        for j in range(MLA_HEADS // 2):
            ae = acc_ref[2 * j]
            ao = acc_ref[2 * j + 1]
            out = jnp.where(lo, ae * (1.0 / ae[:, MLA_V:MLA_V + 1]), ao * (1.0 / ao[:, 0:1]))
            o_ref[:, j * LANES:(j + 1) * LANES] = out.astype(BF16)


def _mla_layer(qm, km, vm):
    nb = SEQ // T_MQ
    ratio = T_MQ // T_MK
    qi = np.concatenate([np.full(ratio * (n + 1), n, np.int32) for n in range(nb)])
    ki = np.concatenate([np.arange(ratio * (n + 1), dtype=np.int32) for n in range(nb)])
    grid_spec = pltpu.PrefetchScalarGridSpec(
        num_scalar_prefetch=2,
        grid=(int(qi.shape[0]),),
        in_specs=[pl.BlockSpec((T_MQ, HEADS_W), lambda s, qi, ki: (qi[s], 0)),
                  pl.BlockSpec((T_MK, HEADS_W), lambda s, qi, ki: (ki[s], 0)),
                  pl.BlockSpec((T_MK, HEADS_W), lambda s, qi, ki: (ki[s], 0))],
        out_specs=pl.BlockSpec((T_MQ, ATT_W), lambda s, qi, ki: (qi[s], 0)),
        scratch_shapes=[pltpu.VMEM((MLA_HEADS, T_MQ, LANES), F32),
                        pltpu.VMEM((MLA_HEADS, T_MQ, LANES), F32)],
    )
    return pl.pallas_call(
        _mla_kernel,
        out_shape=jax.ShapeDtypeStruct((SEQ, ATT_W), BF16),
        grid_spec=grid_spec,
        compiler_params=_cparams(1),
        name="mla_attn",
    )(jnp.asarray(qi), jnp.asarray(ki), qm, km, vm)


def _post_kernel(x_ref, os_ref, om_ref, mod_ref, n2_ref, wout_ref, wrh_ref, wrl_ref, br_ref,
                 xo_ref, h2_ref, meta_ref, cnt_ref, carry_ref):
    _init_carry(pl.program_id(0), carry_ref)
    for sub in range(N_SUB):
        rows_at = slice(sub * T_SUB, (sub + 1) * T_SUB)
        y = _dot(os_ref[rows_at, :], wout_ref[0:ATT_W, :]) + _dot(om_ref[rows_at, :], wout_ref[ATT_W:2 * ATT_W, :])
        _tail(x_ref[rows_at, :], y, mod_ref[...], n2_ref[...], wrh_ref[...], wrl_ref[...], br_ref[...],
              rows_at, xo_ref, h2_ref, meta_ref, cnt_ref, carry_ref)


def _post_layer(x, o_s, o_m, mod, n2, w_out, wr_hi, wr_lo, br):
    shapes, specs, scratch = _tail_out(T_TOK)
    return pl.pallas_call(
        _post_kernel,
        out_shape=shapes,
        grid=(SEQ // T_TOK,),
        in_specs=[pl.BlockSpec((T_TOK, D), lambda i: (i, 0)),
                  pl.BlockSpec((T_TOK, ATT_W), lambda i: (i, 0)),
                  pl.BlockSpec((T_TOK, ATT_W), lambda i: (i, 0)),
                  _full((8, D)), _full((1, D)), _full((D, D)),
                  _full((D, LANES)), _full((D, LANES)), _full((1, LANES))],
        out_specs=specs,
        scratch_shapes=scratch,
        compiler_params=_cparams(1),
        name="odd_post",
    )(x, o_s, o_m, mod, n2, w_out, wr_hi, wr_lo, br)


def _dispatch_plan(meta, cnt):
    grp = meta[:, EPG].astype(jnp.int32)
    rank = meta[:, EPG + 1].astype(jnp.int32)
    counts = cnt[0, :N_GROUPS].astype(jnp.int32)
    padded = ((counts + T_MOE - 1) // T_MOE) * T_MOE
    ends = jnp.cumsum(padded)
    pos = (ends - padded)[grp] + rank
    n_used = ends[-1] // T_MOE
    tile_start = jnp.arange(N_TILES, dtype=jnp.int32) * T_MOE
    tile_group = jnp.minimum(jnp.sum(tile_start[:, None] >= ends[None, :], axis=1), N_GROUPS - 1)
    return pos.reshape(SEQ // T_DISP, 1, T_DISP), tile_group.astype(jnp.int32), n_used.reshape(1)


def _row_copies(n, src_at, dst_at, sem):
    for r in range(n):
        pltpu.make_async_copy(src_at(r), dst_at(r), sem).start(priority=r % 2)


def _disp_kernel(pos_ref, x_ref, init_ref, o_ref, sem):
    del init_ref
    _row_copies(T_DISP,
                lambda r: x_ref.at[pl.ds(r, 1), :],
                lambda r: o_ref.at[pl.ds(pos_ref[0, 0, r], 1), :], sem)
    pltpu.make_async_copy(x_ref, o_ref.at[pl.ds(0, T_DISP), :], sem).wait()


def _dispatch(pos, h2a, hs):
    return pl.pallas_call(
        _disp_kernel,
        out_shape=jax.ShapeDtypeStruct((N_SORT, ROW_W), F32),
        grid=(SEQ // T_DISP,),
        in_specs=[pl.BlockSpec((1, 1, T_DISP), lambda i: (i, 0, 0), memory_space=pltpu.SMEM),
                  pl.BlockSpec((T_DISP, ROW_W), lambda i: (i, 0)),
                  pl.BlockSpec(memory_space=pl.ANY)],
        out_specs=pl.BlockSpec(memory_space=pl.ANY),
        scratch_shapes=[pltpu.SemaphoreType.DMA(())],
        input_output_aliases={2: 0},
        compiler_params=_cparams(1),
        name="moe_dispatch",
    )(pos, h2a, hs)


def _moe_kernel(tg_ref, nu_ref, x_ref, wg_ref, wu_ref, wd_ref, o_ref, xb_ref, act_ref):
    del tg_ref
    i = pl.program_id(0)
    j = pl.program_id(1)
    last = EPG // E_STEP - 1

    @pl.when(i < nu_ref[0])
    def _():
        @pl.when(j == 0)
        def _():
            xb_ref[...] = x_ref[:, 0:D].astype(BF16)

        xb = xb_ref[...]
        meta = x_ref[:, D:ROW_W]
        lane = lax.broadcasted_iota(jnp.int32, meta.shape, 1)
        for k in range(E_STEP):
            a = _dot(xb, wg_ref[0, k].astype(BF16))
            u = _dot(xb, wu_ref[0, k].astype(BF16))
            gate = jnp.sum(jnp.where(lane == j * E_STEP + k, meta, 0.0), axis=-1, keepdims=True)
            act = (a * jax.nn.sigmoid(a) * u * gate).astype(BF16)
            for jj in range(EPG // E_STEP):
                @pl.when(j == jj)
                def _(jj=jj, k=k, act=act):
                    act_ref[:, (jj * E_STEP + k) * FF:(jj * E_STEP + k + 1) * FF] = act

        @pl.when(j == last)
        def _():
            o_ref[...] = _dot(act_ref[...], wd_ref[0, 0].astype(BF16))

    @pl.when((i >= nu_ref[0]) & (j == last))
    def _():
        o_ref[...] = jnp.zeros(o_ref.shape, F32)


def _moe_experts(tile_group, n_used, hs, wg, wu, wd, layer):
    def tile(i, e, tg, nu):
        return (jnp.minimum(i, nu[0] - 1), 0)

    nj = EPG // E_STEP

    def expert(i, j, tg, nu):
        return (layer, jnp.where(i < nu[0], tg[i] * nj + j, tg[nu[0] - 1] * nj + nj - 1), 0, 0)

    def group(i, j, tg, nu):
        return (layer, tg[jnp.minimum(i, nu[0] - 1)], 0, 0)

    grid_spec = pltpu.PrefetchScalarGridSpec(
        num_scalar_prefetch=2,
        grid=(N_TILES, nj),
        in_specs=[pl.BlockSpec((T_MOE, ROW_W), tile),
                  pl.BlockSpec((1, E_STEP, D, FF), expert),
                  pl.BlockSpec((1, E_STEP, D, FF), expert),
                  pl.BlockSpec((1, 1, EPG * FF, D), group, pipeline_mode=pl.Buffered(1))],
        out_specs=pl.BlockSpec((T_MOE, D), lambda i, e, tg, nu: (i, 0)),
        scratch_shapes=[pltpu.VMEM((T_MOE, D), BF16), pltpu.VMEM((T_MOE, EPG * FF), BF16)],
    )
    return pl.pallas_call(
        _moe_kernel,
        out_shape=jax.ShapeDtypeStruct((N_SORT, D), F32),
        grid_spec=grid_spec,
        compiler_params=pltpu.CompilerParams(dimension_semantics=("arbitrary", "arbitrary"),
                                             vmem_limit_bytes=MOE_VMEM_LIMIT),
        name="moe_experts",
    )(tile_group, n_used, hs, wg, wu, wd.reshape(DEPTH, N_GROUPS, EPG * FF, D))


def _comb_kernel(pos_ref, x_ref, mod_ref, y_ref, o_ref, buf_ref, sem):
    _row_copies(T_DISP,
                lambda r: y_ref.at[pl.ds(pos_ref[0, 0, r], 1), :],
                lambda r: buf_ref.at[pl.ds(r, 1), :], sem)
    pltpu.make_async_copy(y_ref.at[pl.ds(0, T_DISP), :], buf_ref, sem).wait()
    o_ref[...] = x_ref[...] + mod_ref[5:6, :] * buf_ref[...]


def _combine(pos, x, mod, ys):
    return pl.pallas_call(
        _comb_kernel,
        out_shape=jax.ShapeDtypeStruct((SEQ, D), F32),
        grid=(SEQ // T_DISP,),
        in_specs=[pl.BlockSpec((1, 1, T_DISP), lambda i: (i, 0, 0), memory_space=pltpu.SMEM),
                  pl.BlockSpec((T_DISP, D), lambda i: (i, 0)),
                  pl.BlockSpec((8, D), lambda i: (0, 0)),
                  pl.BlockSpec(memory_space=pl.ANY)],
        out_specs=pl.BlockSpec((T_DISP, D), lambda i: (i, 0)),
        scratch_shapes=[pltpu.VMEM((T_DISP, D), F32), pltpu.SemaphoreType.DMA(())],
        compiler_params=_cparams(1),
        name="moe_combine",
    )(pos, x, mod, ys)


def _moe_layer(x, h2a, meta, cnt, mod, wg, wu, wd, layer, hs):
    pos, tile_group, n_used = _dispatch_plan(meta, cnt)
    hs = _dispatch(pos, h2a, hs)
    ys = _moe_experts(tile_group, n_used, hs, wg, wu, wd, layer)
    return _combine(pos, x, mod, ys), hs


def _pad_heads(w, heads, dim):
    k = w.shape[0]
    w = w.reshape(k, heads, dim)
    return jnp.pad(w, ((0, 0), (0, 0), (0, LANES - dim))).reshape(k, heads * LANES)


def _pad_gain(g):
    return jnp.pad(g, (0, LANES - g.shape[0])).reshape(1, LANES)


def _odd_weights(w_in):
    bounds = np.cumsum([0, SWA_HEADS * HEAD_DIM, SWA_KV * HEAD_DIM, SWA_KV * HEAD_DIM, Q_LORA, KV_LORA, MLA_ROPE])
    q_s, k_s, v, c_q, c_kv, k_rope = [w_in[:, int(a):int(b)] for a, b in zip(bounds[:-1], bounds[1:])]
    q_s = _pad_heads(q_s, SWA_HEADS, HEAD_DIM)
    k_s = _pad_heads(k_s, SWA_KV, HEAD_DIM)
    v0, v1 = v[:, 0:HEAD_DIM], v[:, HEAD_DIM:2 * HEAD_DIM]
    v_s = jnp.concatenate([v0, v0, v1, v1], axis=1)
    k_r = jnp.pad(k_rope, ((0, 0), (MLA_NOPE, LANES - MLA_QK)))
    k_r_sw = jnp.pad(_swap_halves(k_rope), ((0, 0), (MLA_NOPE, LANES - MLA_QK)))
    return jnp.concatenate([q_s, k_s, v_s, c_q, c_kv, k_r, k_r_sw], axis=1).astype(BF16)


def _swap_halves(t):
    half = t.shape[-1] // 2
    return jnp.concatenate([t[..., half:], t[..., :half]], axis=-1)


def _uq_weights(w_uq):
    w = w_uq.reshape(Q_LORA, MLA_HEADS, MLA_QK)
    plain = jnp.pad(w, ((0, 0), (0, 0), (0, LANES - MLA_QK)))
    swapped = jnp.pad(_swap_halves(w[:, :, MLA_NOPE:]), ((0, 0), (0, 0), (MLA_NOPE, LANES - MLA_QK)))
    return jnp.concatenate([plain.reshape(Q_LORA, -1), swapped.reshape(Q_LORA, -1)], axis=1)


def _rope_gains(g):
    plain = jnp.pad(g, (0, LANES - MLA_QK)).reshape(1, LANES)
    partner = jnp.pad(_swap_halves(g[MLA_NOPE:]), (MLA_NOPE, LANES - MLA_QK)).reshape(1, LANES)
    return plain, partner


def _router_weights(w_group, b_group, w_expert, b_expert):
    w = jnp.pad(jnp.concatenate([w_expert, w_group], axis=1), ((0, 0), (0, LANES - N_EXPERTS - N_GROUPS)))
    hi = w.astype(BF16)
    lo = (w - hi.astype(F32)).astype(BF16)
    b = jnp.pad(jnp.concatenate([b_expert, b_group]), (0, LANES - N_EXPERTS - N_GROUPS)).reshape(1, LANES)
    return hi, lo, b


def kernel(x, c, positions, ada_w, ada_b, norm1_g, norm2_g, cp_w_in, conv_w, pool_w, pool_scale,
           cp_w_out, at_w_in, swa_q_g, swa_k_g, swa_sinks, mla_q_norm_g, mla_kv_norm_g, mla_w_uq,
           mla_w_ukv, mla_q_g, mla_k_g, at_w_out, moe_w_group, moe_b_group, moe_w_expert,
           moe_b_expert, moe_w_gate, moe_w_up, moe_w_down):
    xs = x.reshape(SEQ, D)
    mods = _ada_mod(c, ada_w, ada_b)
    tc, ts1, ts2 = _rope_tables(positions)
    hs = jnp.zeros((N_SORT, ROW_W), F32)
    for l in range(DEPTH):
        i = l // 2
        mod = mods[l]
        n1 = norm1_g[l].reshape(1, D)
        n2 = norm2_g[l].reshape(1, D)
        wr_hi, wr_lo, br = _router_weights(moe_w_group[l], moe_b_group[l], moe_w_expert[l], moe_b_expert[l])
        if l % 2 == 0:
            xs, h2a, meta, cnt = _even_layer(
                xs, mod, n1, n2, cp_w_in[i].astype(BF16), conv_w[i], pool_w[i].astype(BF16),
                pool_scale[i].reshape(1, 4 * POOL_G), cp_w_out[i].astype(BF16), wr_hi, wr_lo, br)
        else:
            ukv = mla_w_ukv[i].reshape(KV_LORA, MLA_HEADS, MLA_NOPE + MLA_V)
            wuk = _pad_heads(ukv[:, :, :MLA_NOPE].reshape(KV_LORA, MLA_HEADS * MLA_NOPE), MLA_HEADS, MLA_NOPE)
            wuv = ukv[:, :, MLA_NOPE:].reshape(KV_LORA, MLA_HEADS * MLA_V)
            wuq = _uq_weights(mla_w_uq[i])
            gmq, gmqs = _rope_gains(mla_q_g[i])
            gmk, gmks = _rope_gains(mla_k_g[i])
            qs, ks, vs, qm, km, vm = _proj_layer(
                xs, mod, n1, _odd_weights(at_w_in[i]), _pad_gain(swa_q_g[i]), _pad_gain(swa_k_g[i]),
                mla_q_norm_g[i].reshape(1, Q_LORA), mla_kv_norm_g[i].reshape(1, KV_LORA),
                wuq.astype(BF16), wuk.astype(BF16), wuv.astype(BF16),
                gmq, gmqs, gmk, gmks, tc, ts1, ts2)
            o_s = _swa_layer(swa_sinks[i], qs, ks, vs)
            o_m = _mla_layer(qm, km, vm)
            xs, h2a, meta, cnt = _post_layer(xs, o_s, o_m, mod, n2, at_w_out[i].astype(BF16), wr_hi, wr_lo, br)
        xs, hs = _moe_layer(xs, h2a, meta, cnt, mod, moe_w_gate, moe_w_up, moe_w_down, l, hs)
    return xs.reshape(1, SEQ, D)
```

```python
import functools

import numpy as np
import jax
import jax.numpy as jnp
from jax import lax
from jax.experimental import pallas as pl
from jax.experimental.pallas import tpu as pltpu

F32 = jnp.float32
BF16 = jnp.bfloat16

D = 1024
SEQ = 16384
DEPTH = 4
EPS = 1e-6
LANES = 128
CONV_CH = 512
POOL_WINDOWS = (2, 4, 8, 16)
POOL_G = 128
HALO = 16
SWA_HEADS = 8
SWA_KV = 2
HEAD_DIM = 64
WINDOW = 128
MLA_HEADS = 8
MLA_NOPE = 64
MLA_ROPE = 32
MLA_QK = MLA_NOPE + MLA_ROPE
MLA_V = 64
Q_LORA = 384
KV_LORA = 256
ROPE_THETA = 10000.0
N_GROUPS = 4
EPG = 8
N_EXPERTS = N_GROUPS * EPG
FF = 256
NEG = -1e30
LOG2E = 1.4426950408889634

T_TOK = 512
N_SUB = 2
T_SUB = T_TOK // N_SUB
T_ATT = 512
T_MOE = 512
T_DISP = 1024
ROW_W = D + LANES
N_SORT = SEQ + N_GROUPS * T_MOE
N_TILES = N_SORT // T_MOE
E_STEP = 8
T_MQ = 1024
T_MK = 512
MLA_SPLIT = 1
MLA_DEPTH = 4
VMEM_LIMIT = 48 * 1024 * 1024
MOE_VMEM_LIMIT = 56 * 1024 * 1024

HEADS_W = MLA_HEADS * LANES
KV_W = SWA_KV * LANES
ATT_W = MLA_HEADS * MLA_V
O_QS = 0
O_KS = O_QS + HEADS_W
O_VS = O_KS + KV_W
O_CQ = O_VS + KV_W
O_CKV = O_CQ + Q_LORA
O_KR = O_CKV + KV_LORA
ODD_W = O_KR + 2 * LANES


def _cparams(n_axes=1):
    return pltpu.CompilerParams(dimension_semantics=("arbitrary",) * n_axes,
                                vmem_limit_bytes=VMEM_LIMIT)


def _rms(x):
    return x * lax.rsqrt(jnp.mean(x * x, axis=-1, keepdims=True) + EPS)


def _dot(a, b):
    return jnp.dot(a, b, preferred_element_type=F32)


def _dot_nt(a, b):
    return lax.dot_general(a, b, (((1,), (1,)), ((), ())), preferred_element_type=F32)


def _ada_kernel(c_ref, w_ref, b_ref, o_ref):
    c = c_ref[...]
    ca = c * jax.nn.sigmoid(c)
    o_ref[0] = jnp.sum(w_ref[0] * ca, axis=0, keepdims=True) + b_ref[0]


def _ada_mod(c, ada_w, ada_b):
    c_col = c.reshape(D, 1)
    b = ada_b.reshape(DEPTH * 6, 1, D)
    out = pl.pallas_call(
        _ada_kernel,
        out_shape=jax.ShapeDtypeStruct((DEPTH * 6, 1, D), F32),
        grid=(DEPTH, 6),
        in_specs=[pl.BlockSpec((D, 1), lambda l, j: (0, 0)),
                  pl.BlockSpec((1, D, D), lambda l, j: (l, 0, j)),
                  pl.BlockSpec((1, 1, D), lambda l, j: (l * 6 + j, 0, 0))],
        out_specs=pl.BlockSpec((1, 1, D), lambda l, j: (l * 6 + j, 0, 0)),
        compiler_params=_cparams(2),
        name="ada_mod",
    )(c_col, ada_w, b)
    mod = out.reshape(DEPTH, 6, D)
    return jnp.pad(mod, ((0, 0), (0, 2), (0, 0)))


def _rope_kernel(pos_ref, inv_ref, c_ref, s1_ref, s2_ref):
    pos = pos_ref[...].astype(F32)
    ang = pos * inv_ref[...]
    lane = lax.broadcasted_iota(jnp.int32, ang.shape, 1)
    cs = jnp.cos(ang)
    sn = jnp.sin(ang)
    mid = MLA_NOPE + MLA_ROPE // 2
    c_ref[...] = jnp.where(lane < MLA_NOPE, 1.0, jnp.where(lane < MLA_QK, cs, 0.0))
    s1_ref[...] = jnp.where((lane >= MLA_NOPE) & (lane < mid), -sn, 0.0)
    s2_ref[...] = jnp.where((lane >= mid) & (lane < MLA_QK), sn, 0.0)


def _rope_tables(positions):
    half = MLA_ROPE // 2
    inv = jnp.power(ROPE_THETA, -jnp.arange(half, dtype=F32) / half)
    inv_lane = jnp.concatenate([jnp.zeros((MLA_NOPE,), F32), inv, inv,
                                jnp.zeros((LANES - MLA_QK,), F32)]).reshape(1, LANES)
    pos = positions.reshape(SEQ, 1)
    shp = jax.ShapeDtypeStruct((SEQ, LANES), F32)
    spec = pl.BlockSpec((T_TOK, LANES), lambda i: (i, 0))
    return pl.pallas_call(
        _rope_kernel,
        out_shape=(shp, shp, shp),
        grid=(SEQ // T_TOK,),
        in_specs=[pl.BlockSpec((T_TOK, 1), lambda i: (i, 0)),
                  pl.BlockSpec((1, LANES), lambda i: (0, 0))],
        out_specs=(spec, spec, spec),
        compiler_params=_cparams(1),
        name="rope_tables",
    )(pos, inv_lane)


def _route(lg):
    lane = lax.broadcasted_iota(jnp.int32, lg.shape, 1)
    lane_f = lane.astype(F32)
    is_g = (lane >= N_EXPERTS) & (lane < N_EXPERTS + N_GROUPS)
    gl = jnp.where(is_g, lg, NEG)
    gmax = jnp.max(gl, axis=-1, keepdims=True)
    gidx = jnp.min(jnp.where(is_g & (gl == gmax), lane_f - N_EXPERTS, 1e3), axis=-1, keepdims=True)
    gsum = jnp.sum(jnp.where(is_g, jnp.exp(gl - gmax), 0.0), axis=-1, keepdims=True)
    gw = 1.0 / gsum
    grp_of_lane = (lane >> 3).astype(F32)
    in_grp = (lane < N_EXPERTS) & (grp_of_lane == gidx)
    el = jnp.where(in_grp, lg, NEG)
    m1 = jnp.max(el, axis=-1, keepdims=True)
    i1 = jnp.min(jnp.where(in_grp & (el == m1), lane_f, 1e3), axis=-1, keepdims=True)
    rest = in_grp & (lane_f != i1)
    el2 = jnp.where(rest, lg, NEG)
    m2 = jnp.max(el2, axis=-1, keepdims=True)
    i2 = jnp.min(jnp.where(rest & (el2 == m2), lane_f, 1e3), axis=-1, keepdims=True)
    r = jnp.exp(m2 - m1)
    w1 = gw / (1.0 + r)
    w2 = w1 * r
    return jnp.where(lane_f == i1, w1, jnp.where(lane_f == i2, w2, 0.0)), gidx


def _tail(x, y, mod, n2, wr_hi, wr_lo, br, rows_at, xo_ref, h2_ref, meta_ref, cnt_ref, carry_ref):
    gate1 = mod[2:3]
    shift2, scale2 = mod[3:4], mod[4:5]
    xn = x + gate1 * y
    xo_ref[rows_at, :] = xn
    h2 = _rms(xn) * n2 * (1.0 + scale2) + shift2
    hi = h2.astype(BF16)
    lo = (h2 - hi.astype(F32)).astype(BF16)
    both = _dot(hi, jnp.concatenate([wr_hi, wr_lo], axis=1))
    lg = both[:, 0:LANES] + both[:, LANES:2 * LANES] + _dot(lo, wr_hi) + br
    gates, gidx = _route(lg)
    rows = lg.shape[0]
    lane = lax.broadcasted_iota(jnp.int32, (rows, LANES), 1)
    lane_f = lane.astype(F32)
    g8 = gates
    for g in range(1, N_GROUPS):
        g8 = g8 + pltpu.roll(gates, LANES - EPG * g, 1)
    onehot = (lane_f == gidx).astype(F32)
    r = lax.broadcasted_iota(jnp.int32, (rows, rows), 0)
    c = lax.broadcasted_iota(jnp.int32, (rows, rows), 1)
    before = jnp.where(c < r, 1.0, 0.0).astype(BF16)
    cum = _dot(before, onehot.astype(BF16)) + carry_ref[0:1, :]
    rank = jnp.sum(jnp.where(lane_f == gidx, cum, 0.0), axis=-1, keepdims=True)
    meta = jnp.where(lane < EPG, g8, jnp.where(lane == EPG, gidx, jnp.where(lane == EPG + 1, rank, 0.0)))
    h2_ref[rows_at, 0:D] = h2
    h2_ref[rows_at, D:D + LANES] = meta
    meta_ref[rows_at, :] = meta
    total = carry_ref[0:1, :] + jnp.sum(onehot, axis=0, keepdims=True)
    carry_ref[...] = jnp.broadcast_to(total, carry_ref.shape)
    cnt_ref[...] = jnp.broadcast_to(total, cnt_ref.shape)


def _even_kernel(x_ref, xh_ref, mod_ref, n1_ref, n2_ref, win_ref, cw_ref, pw_ref, ps_ref,
                 wout_ref, wrh_ref, wrl_ref, br_ref, xo_ref, h2_ref, meta_ref, cnt_ref, carry_ref):
    i = pl.program_id(0)
    _init_carry(i, carry_ref)
    for sub in range(N_SUB):
        rows_at = slice(sub * T_SUB, (sub + 1) * T_SUB)
        halo = xh_ref[...] if sub == 0 else x_ref[sub * T_SUB - HALO: sub * T_SUB, :]
        _even_rows(i * T_TOK + sub * T_SUB, x_ref[rows_at, :], halo, rows_at, mod_ref, n1_ref, n2_ref, win_ref,
                   cw_ref, pw_ref, ps_ref, wout_ref, wrh_ref, wrl_ref, br_ref, xo_ref, h2_ref, meta_ref,
                   cnt_ref, carry_ref)


def _init_carry(i, carry_ref):
    @pl.when(i == 0)
    def _():
        carry_ref[...] = jnp.zeros(carry_ref.shape, F32)


def _even_rows(tok0, x, halo, rows_at, mod_ref, n1_ref, n2_ref, win_ref, cw_ref, pw_ref, ps_ref,
               wout_ref, wrh_ref, wrl_ref, br_ref, xo_ref, h2_ref, meta_ref, cnt_ref, carry_ref):
    mod = mod_ref[...]
    shift1, scale1 = mod[0:1], mod[1:2]
    xa = jnp.concatenate([halo, x], axis=0)
    h = _rms(xa) * n1_ref[...] * (1.0 + scale1) + shift1
    z = _dot(h.astype(BF16), win_ref[...])
    rows = T_SUB + HALO
    row = lax.broadcasted_iota(jnp.int32, (rows, 1), 0)
    tpos = tok0 + row - HALO
    live = (tpos >= 0).astype(F32)
    bg = z[:, 0:CONV_CH]
    v = z[:, CONV_CH:2 * CONV_CH] * z[:, 2 * CONV_CH:3 * CONV_CH] * live
    cw = cw_ref[...]
    conv = v * cw[0:1] + pltpu.roll(v, 1, 0) * cw[1:2] + pltpu.roll(v, 2, 0) * cw[2:3]
    parts = [(bg * conv)[HALO:]]
    ps = ps_ref[...]
    tcount = (tpos + 1).astype(F32)
    for gi, w in enumerate(POOL_WINDOWS):
        ug = z[:, 3 * CONV_CH + gi * POOL_G: 3 * CONV_CH + (gi + 1) * POOL_G] * live
        s = ug
        k = 1
        while k < w:
            s = s + pltpu.roll(s, k, 0)
            k *= 2
        inv = 1.0 / jnp.minimum(tcount, float(w))
        d = (s * inv - ug)[HALO:].astype(BF16)
        parts.append(_dot(d, pw_ref[gi]) * ps[:, gi * POOL_G:(gi + 1) * POOL_G])
    cat = jnp.concatenate(parts, axis=-1).astype(BF16)
    y = _dot(cat, wout_ref[...])
    _tail(x, y, mod, n2_ref[...], wrh_ref[...], wrl_ref[...], br_ref[...], rows_at, xo_ref, h2_ref, meta_ref,
          cnt_ref, carry_ref)


def _tail_out(n):
    shapes = (jax.ShapeDtypeStruct((SEQ, D), F32), jax.ShapeDtypeStruct((SEQ, ROW_W), F32),
              jax.ShapeDtypeStruct((SEQ, LANES), F32), jax.ShapeDtypeStruct((8, LANES), F32))
    specs = (pl.BlockSpec((n, D), lambda i: (i, 0)), pl.BlockSpec((n, ROW_W), lambda i: (i, 0)),
             pl.BlockSpec((n, LANES), lambda i: (i, 0)), pl.BlockSpec((8, LANES), lambda i: (0, 0)))
    scratch = [pltpu.VMEM((8, LANES), F32)]
    return shapes, specs, scratch


def _full(shape):
    nd = len(shape)
    return pl.BlockSpec(shape, lambda i: (0,) * nd)


def _even_layer(x, mod, n1, n2, w_in, conv_w, pool_w, pool_scale, w_out, wr_hi, wr_lo, br):
    shapes, specs, scratch = _tail_out(T_TOK)
    hb = T_TOK // HALO
    return pl.pallas_call(
        _even_kernel,
        out_shape=shapes,
        grid=(SEQ // T_TOK,),
        in_specs=[pl.BlockSpec((T_TOK, D), lambda i: (i, 0)),
                  pl.BlockSpec((HALO, D), lambda i: (jnp.maximum(i * hb - 1, 0), 0)),
                  _full((8, D)), _full((1, D)), _full((1, D)),
                  _full((D, 4 * CONV_CH)), _full((3, CONV_CH)), _full((4, POOL_G, POOL_G)),
                  _full((1, 4 * POOL_G)), _full((D, D)),
                  _full((D, LANES)), _full((D, LANES)), _full((1, LANES))],
        out_specs=specs,
        scratch_shapes=scratch,
        compiler_params=_cparams(1),
        name="even_mixer",
    )(x, x, mod, n1, n2, w_in, conv_w, pool_w, pool_scale, w_out, wr_hi, wr_lo, br)


def _proj_kernel(x_ref, mod_ref, n1_ref, win_ref, gsq_ref, gsk_ref, gqn_ref, gkvn_ref,
                 wuq_ref, wuk_ref, wuv_ref, gmq_ref, gmqs_ref, gmk_ref, gmks_ref, c_ref, s1_ref, s2_ref,
                 qs_ref, ks_ref, vs_ref, qm_ref, km_ref, vm_ref):
    refs = (x_ref, mod_ref, n1_ref, win_ref, gsq_ref, gsk_ref, gqn_ref, gkvn_ref, wuq_ref, wuk_ref, wuv_ref,
            gmq_ref, gmqs_ref, gmk_ref, gmks_ref, c_ref, s1_ref, s2_ref, qs_ref, ks_ref, vs_ref, qm_ref, km_ref, vm_ref)
    _proj_rows(slice(0, T_TOK), *refs)


def _proj_rows(ra, x_ref, mod_ref, n1_ref, win_ref, gsq_ref, gsk_ref, gqn_ref, gkvn_ref,
               wuq_ref, wuk_ref, wuv_ref, gmq_ref, gmqs_ref, gmk_ref, gmks_ref, c_ref, s1_ref, s2_ref,
               qs_ref, ks_ref, vs_ref, qm_ref, km_ref, vm_ref):
    x = x_ref[ra, :]
    mod = mod_ref[...]
    shift1, scale1 = mod[0:1], mod[1:2]
    h = _rms(x) * n1_ref[...] * (1.0 + scale1) + shift1
    z = _dot(h.astype(BF16), win_ref[...])

    def head_norm(t, g, dim):
        ms = jnp.sum(t * t, axis=-1, keepdims=True) * (1.0 / dim)
        return t * lax.rsqrt(ms + EPS) * g

    gsq, gsk = gsq_ref[...] * (HEAD_DIM ** -0.5), gsk_ref[...]
    for hd in range(SWA_HEADS):
        qh = head_norm(z[:, O_QS + hd * LANES: O_QS + (hd + 1) * LANES], gsq, HEAD_DIM)
        qs_ref[ra, hd * LANES:(hd + 1) * LANES] = qh.astype(BF16)
    for kv in range(SWA_KV):
        kh = head_norm(z[:, O_KS + kv * LANES: O_KS + (kv + 1) * LANES], gsk, HEAD_DIM)
        ks_ref[ra, kv * LANES:(kv + 1) * LANES] = kh.astype(BF16)
    vs_ref[ra, :] = z[:, O_VS:O_CQ].astype(BF16)

    cq = (_rms(z[:, O_CQ:O_CKV]) * gqn_ref[...]).astype(BF16)
    ckv = (_rms(z[:, O_CKV:O_KR]) * gkvn_ref[...]).astype(BF16)
    qm = _dot(cq, wuq_ref[...])
    kn = _dot(ckv, wuk_ref[...])
    vm = _dot(ckv, wuv_ref[...])
    lane = lax.broadcasted_iota(jnp.int32, (ra.stop - ra.start, LANES), 1)
    for j in range(MLA_HEADS // 2):
        vv = vm[:, j * LANES:(j + 1) * LANES]
        even = jnp.where(lane < MLA_V, vv, jnp.where(lane == MLA_V, 1.0, 0.0))
        odd = jnp.where(lane < MLA_V, jnp.where(lane == 0, 1.0, 0.0), vv)
        vm_ref[ra, (2 * j) * LANES:(2 * j + 1) * LANES] = even.astype(BF16)
        vm_ref[ra, (2 * j + 1) * LANES:(2 * j + 2) * LANES] = odd.astype(BF16)
    kr = z[:, O_KR:O_KR + LANES]
    krs = z[:, O_KR + LANES:ODD_W]
    cs = c_ref[ra, :]
    sp = s1_ref[ra, :] + s2_ref[ra, :]
    qscale = MLA_QK ** -0.5 * LOG2E
    gcq = cs * (gmq_ref[...] * qscale)
    gsq_r = sp * (gmqs_ref[...] * qscale)
    gck = cs * gmk_ref[...]
    krot = krs * (sp * gmks_ref[...])

    def inv_rms(t):
        return lax.rsqrt(jnp.sum(t * t, axis=-1, keepdims=True) * (1.0 / MLA_QK) + EPS)

    for hd in range(MLA_HEADS):
        sl = slice(hd * LANES, (hd + 1) * LANES)
        sw = slice((MLA_HEADS + hd) * LANES, (MLA_HEADS + hd + 1) * LANES)
        qh = qm[:, sl]
        qm_ref[ra, sl] = ((qh * gcq + qm[:, sw] * gsq_r) * inv_rms(qh)).astype(BF16)
        kh = kn[:, sl] + kr
        km_ref[ra, sl] = ((kh * gck + krot) * inv_rms(kh)).astype(BF16)


def _proj_layer(x, mod, n1, w_in, gsq, gsk, gqn, gkvn, wuq, wuk, wuv, gmq, gmqs, gmk, gmks, tc, ts1, ts2):
    def tok(wd):
        return pl.BlockSpec((T_TOK, wd), lambda i: (i, 0))
    widths = (HEADS_W, KV_W, KV_W, HEADS_W, HEADS_W, HEADS_W)
    return pl.pallas_call(
        _proj_kernel,
        out_shape=tuple(jax.ShapeDtypeStruct((SEQ, wd), BF16) for wd in widths),
        grid=(SEQ // T_TOK,),
        in_specs=[tok(D), _full((8, D)), _full((1, D)), _full((D, ODD_W)),
                  _full((1, LANES)), _full((1, LANES)), _full((1, Q_LORA)), _full((1, KV_LORA)),
                  _full((Q_LORA, 2 * HEADS_W)), _full((KV_LORA, HEADS_W)), _full((KV_LORA, ATT_W)),
                  _full((1, LANES)), _full((1, LANES)), _full((1, LANES)), _full((1, LANES)),
                  tok(LANES), tok(LANES), tok(LANES)],
        out_specs=tuple(tok(wd) for wd in widths),
        compiler_params=_cparams(1),
        name="odd_proj",
    )(x, mod, n1, w_in, gsq, gsk, gqn, gkvn, wuq, wuk, wuv, gmq, gmqs, gmk, gmks, tc, ts1, ts2)


def _swa_kernel(sink_ref, q_ref, k_ref, kh_ref, v_ref, vh_ref, o_ref):
    i = pl.program_id(0)
    kcat = jnp.concatenate([kh_ref[...], k_ref[...]], axis=0)
    vcat = jnp.concatenate([vh_ref[...], v_ref[...]], axis=0)
    grp = SWA_HEADS // SWA_KV
    r = lax.broadcasted_iota(jnp.int32, (grp * WINDOW, 2 * WINDOW), 0) & (WINDOW - 1)
    c = lax.broadcasted_iota(jnp.int32, (grp * WINDOW, 2 * WINDOW), 1)
    rel = WINDOW + r - c
    lane = lax.broadcasted_iota(jnp.int32, (WINDOW, LANES), 1)
    for sb in range(T_ATT // WINDOW):
        rows = slice(sb * WINDOW, (sb + 1) * WINDOW)
        kb = kcat[sb * WINDOW: sb * WINDOW + 2 * WINDOW]
        vb = vcat[sb * WINDOW: sb * WINDOW + 2 * WINDOW]
        kpos = i * T_ATT + (sb - 1) * WINDOW + c
        ok = (rel >= 0) & (rel < WINDOW) & (kpos >= 0)
        outs = []
        for kv in range(SWA_KV):
            q = jnp.concatenate([q_ref[rows, (kv * grp + g) * LANES:(kv * grp + g + 1) * LANES]
                                 for g in range(grp)], axis=0)
            sink = jnp.concatenate([jnp.full((WINDOW, 1), sink_ref[kv * grp + g], F32)
                                    for g in range(grp)], axis=0)
            s = jnp.where(ok, _dot_nt(q, kb[:, kv * LANES:(kv + 1) * LANES]), NEG)
            m = jnp.maximum(jnp.max(s, axis=-1, keepdims=True), sink)
            e = jnp.exp(s - m)
            den = jnp.sum(e, axis=-1, keepdims=True) + jnp.exp(sink - m)
            p = (e * (1.0 / den)).astype(BF16)
            o = _dot(p, vb[:, kv * LANES:(kv + 1) * LANES])
            outs += [o[g * WINDOW:(g + 1) * WINDOW] for g in range(grp)]
        for j in range(SWA_HEADS // 2):
            pair = jnp.where(lane < HEAD_DIM, outs[2 * j], outs[2 * j + 1])
            o_ref[rows, j * LANES:(j + 1) * LANES] = pair.astype(BF16)


def _swa_layer(sinks, qs, ks, vs):
    hb = T_ATT // WINDOW
    return pl.pallas_call(
        _swa_kernel,
        out_shape=jax.ShapeDtypeStruct((SEQ, ATT_W), BF16),
        grid=(SEQ // T_ATT,),
        in_specs=[pl.BlockSpec(memory_space=pltpu.SMEM),
                  pl.BlockSpec((T_ATT, HEADS_W), lambda i: (i, 0)),
                  pl.BlockSpec((T_ATT, KV_W), lambda i: (i, 0)),
                  pl.BlockSpec((WINDOW, KV_W), lambda i: (jnp.maximum(i * hb - 1, 0), 0)),
                  pl.BlockSpec((T_ATT, KV_W), lambda i: (i, 0)),
                  pl.BlockSpec((WINDOW, KV_W), lambda i: (jnp.maximum(i * hb - 1, 0), 0))],
        out_specs=pl.BlockSpec((T_ATT, ATT_W), lambda i: (i, 0)),
        compiler_params=_cparams(1),
        name="swa_attn",
    )(sinks, qs, ks, ks, vs, vs)


def _mla_kernel(qi_ref, ki_ref, q_ref, k_ref, v_ref, o_ref, m_ref, acc_ref):
    step = pl.program_id(0)
    qi = qi_ref[step]
    ki = ki_ref[step]

    @pl.when(ki == 0)
    def _():
        m_ref[...] = jnp.full(m_ref.shape, NEG, F32)
        acc_ref[...] = jnp.zeros(acc_ref.shape, F32)

    lane = lax.broadcasted_iota(jnp.int32, (T_MQ, LANES), 1)
    lo = lane < MLA_V
    nc = T_MK // LANES
    sub = T_MQ // MLA_SPLIT
    ratio = T_MQ // T_MK
    units = [(hd, part) for hd in range(MLA_HEADS) for part in range(MLA_SPLIT)]

    def scores(unit):
        hd, part = unit
        sl = slice(hd * LANES, (hd + 1) * LANES)
        return _dot_nt(q_ref[part * sub:(part + 1) * sub, sl], k_ref[:, sl])

    def update(masked):
        if masked:
            r = lax.broadcasted_iota(jnp.int32, (sub, LANES), 0)
            lane_s = lax.broadcasted_iota(jnp.int32, (sub, LANES), 1)

        def softmax_part(unit, s):
            hd, part = unit
            rows = slice(part * sub, (part + 1) * sub)
            cols = [s[:, c * LANES:(c + 1) * LANES] for c in range(nc)]
            if masked:
                off = qi * T_MQ - ki * T_MK + part * sub
                cols = [jnp.where(r + off >= lane_s + c * LANES, cols[c], NEG) for c in range(nc)]
            cmax = cols[0]
            for c in range(1, nc):
                cmax = jnp.maximum(cmax, cols[c])
            m_prev = m_ref[hd, rows]
            m_new = jnp.maximum(m_prev, jnp.max(cmax, axis=-1, keepdims=True))
            m_ref[hd, rows] = m_new
            alpha = jnp.exp2(m_prev - m_new)
            p = jnp.concatenate([jnp.exp2(cols[c] - m_new).astype(BF16) for c in range(nc)], axis=-1)
            return p, alpha

        def value_part(unit, p, alpha):
            hd, part = unit
            rows = slice(part * sub, (part + 1) * sub)
            acc_ref[hd, rows] = acc_ref[hd, rows] * alpha + _dot(p, v_ref[:, hd * LANES:(hd + 1) * LANES])

        s_q = [scores(u) for u in units[:MLA_DEPTH]]
        pend = None
        for n, u in enumerate(units):
            if n + MLA_DEPTH < len(units):
                s_q.append(scores(units[n + MLA_DEPTH]))
            cur = softmax_part(u, s_q[n])
            if pend is not None:
                value_part(units[n - 1], *pend)
            pend = cur
        value_part(units[-1], *pend)

    @pl.when(ki < qi * ratio)
    def _():
        update(False)

    @pl.when(ki >= qi * ratio)
    def _():
        update(True)

    @pl.when(ki == qi * ratio + ratio - 1)
    def _():
        for j in range(MLA_HEADS // 2):
            ae = acc_ref[2 * j]
            ao = acc_ref[2 * j + 1]
            out = jnp.where(lo, ae * (1.0 / ae[:, MLA_V:MLA_V + 1]), ao * (1.0 / ao[:, 0:1]))
            o_ref[:, j * LANES:(j + 1) * LANES] = out.astype(BF16)


def _mla_layer(qm, km, vm):
    nb = SEQ // T_MQ
    ratio = T_MQ // T_MK
    qi = np.concatenate([np.full(ratio * (n + 1), n, np.int32) for n in range(nb)])
    ki = np.concatenate([np.arange(ratio * (n + 1), dtype=np.int32) for n in range(nb)])
    grid_spec = pltpu.PrefetchScalarGridSpec(
        num_scalar_prefetch=2,
        grid=(int(qi.shape[0]),),
        in_specs=[pl.BlockSpec((T_MQ, HEADS_W), lambda s, qi, ki: (qi[s], 0)),
                  pl.BlockSpec((T_MK, HEADS_W), lambda s, qi, ki: (ki[s], 0)),
                  pl.BlockSpec((T_MK, HEADS_W), lambda s, qi, ki: (ki[s], 0))],
        out_specs=pl.BlockSpec((T_MQ, ATT_W), lambda s, qi, ki: (qi[s], 0)),
        scratch_shapes=[pltpu.VMEM((MLA_HEADS, T_MQ, LANES), F32),
                        pltpu.VMEM((MLA_HEADS, T_MQ, LANES), F32)],
    )
    return pl.pallas_call(
        _mla_kernel,
        out_shape=jax.ShapeDtypeStruct((SEQ, ATT_W), BF16),
        grid_spec=grid_spec,
        compiler_params=_cparams(1),
        name="mla_attn",
    )(jnp.asarray(qi), jnp.asarray(ki), qm, km, vm)


def _post_kernel(x_ref, os_ref, om_ref, mod_ref, n2_ref, wout_ref, wrh_ref, wrl_ref, br_ref,
                 xo_ref, h2_ref, meta_ref, cnt_ref, carry_ref):
    _init_carry(pl.program_id(0), carry_ref)
    for sub in range(N_SUB):
        rows_at = slice(sub * T_SUB, (sub + 1) * T_SUB)
        y = _dot(os_ref[rows_at, :], wout_ref[0:ATT_W, :]) + _dot(om_ref[rows_at, :], wout_ref[ATT_W:2 * ATT_W, :])
        _tail(x_ref[rows_at, :], y, mod_ref[...], n2_ref[...], wrh_ref[...], wrl_ref[...], br_ref[...],
              rows_at, xo_ref, h2_ref, meta_ref, cnt_ref, carry_ref)


def _post_layer(x, o_s, o_m, mod, n2, w_out, wr_hi, wr_lo, br):
    shapes, specs, scratch = _tail_out(T_TOK)
    return pl.pallas_call(
        _post_kernel,
        out_shape=shapes,
        grid=(SEQ // T_TOK,),
        in_specs=[pl.BlockSpec((T_TOK, D), lambda i: (i, 0)),
                  pl.BlockSpec((T_TOK, ATT_W), lambda i: (i, 0)),
                  pl.BlockSpec((T_TOK, ATT_W), lambda i: (i, 0)),
                  _full((8, D)), _full((1, D)), _full((D, D)),
                  _full((D, LANES)), _full((D, LANES)), _full((1, LANES))],
        out_specs=specs,
        scratch_shapes=scratch,
        compiler_params=_cparams(1),
        name="odd_post",
    )(x, o_s, o_m, mod, n2, w_out, wr_hi, wr_lo, br)


def _dispatch_plan(meta, cnt):
    grp = meta[:, EPG].astype(jnp.int32)
    rank = meta[:, EPG + 1].astype(jnp.int32)
    counts = cnt[0, :N_GROUPS].astype(jnp.int32)
    padded = ((counts + T_MOE - 1) // T_MOE) * T_MOE
    ends = jnp.cumsum(padded)
    pos = (ends - padded)[grp] + rank
    n_used = ends[-1] // T_MOE
    tile_start = jnp.arange(N_TILES, dtype=jnp.int32) * T_MOE
    tile_group = jnp.minimum(jnp.sum(tile_start[:, None] >= ends[None, :], axis=1), N_GROUPS - 1)
    return pos.reshape(SEQ // T_DISP, 1, T_DISP), tile_group.astype(jnp.int32), n_used.reshape(1)


def _row_copies(n, src_at, dst_at, sem):
    for r in range(n):
        pltpu.make_async_copy(src_at(r), dst_at(r), sem).start(priority=r % 2)


def _disp_kernel(pos_ref, x_ref, init_ref, o_ref, sem):
    del init_ref
    _row_copies(T_DISP,
                lambda r: x_ref.at[pl.ds(r, 1), :],
                lambda r: o_ref.at[pl.ds(pos_ref[0, 0, r], 1), :], sem)
    pltpu.make_async_copy(x_ref, o_ref.at[pl.ds(0, T_DISP), :], sem).wait()


def _dispatch(pos, h2a, hs):
    return pl.pallas_call(
        _disp_kernel,
        out_shape=jax.ShapeDtypeStruct((N_SORT, ROW_W), F32),
        grid=(SEQ // T_DISP,),
        in_specs=[pl.BlockSpec((1, 1, T_DISP), lambda i: (i, 0, 0), memory_space=pltpu.SMEM),
                  pl.BlockSpec((T_DISP, ROW_W), lambda i: (i, 0)),
                  pl.BlockSpec(memory_space=pl.ANY)],
        out_specs=pl.BlockSpec(memory_space=pl.ANY),
        scratch_shapes=[pltpu.SemaphoreType.DMA(())],
        input_output_aliases={2: 0},
        compiler_params=_cparams(1),
        name="moe_dispatch",
    )(pos, h2a, hs)


def _moe_kernel(tg_ref, nu_ref, x_ref, wg_ref, wu_ref, wd_ref, o_ref, xb_ref, act_ref):
    del tg_ref
    i = pl.program_id(0)
    j = pl.program_id(1)
    last = EPG // E_STEP - 1

    @pl.when(i < nu_ref[0])
    def _():
        @pl.when(j == 0)
        def _():
            xb_ref[...] = x_ref[:, 0:D].astype(BF16)

        xb = xb_ref[...]
        meta = x_ref[:, D:ROW_W]
        lane = lax.broadcasted_iota(jnp.int32, meta.shape, 1)
        for k in range(E_STEP):
            a = _dot(xb, wg_ref[0, k].astype(BF16))
            u = _dot(xb, wu_ref[0, k].astype(BF16))
            gate = jnp.sum(jnp.where(lane == j * E_STEP + k, meta, 0.0), axis=-1, keepdims=True)
            act = (a * jax.nn.sigmoid(a) * u * gate).astype(BF16)
            for jj in range(EPG // E_STEP):
                @pl.when(j == jj)
                def _(jj=jj, k=k, act=act):
                    act_ref[:, (jj * E_STEP + k) * FF:(jj * E_STEP + k + 1) * FF] = act

        @pl.when(j == last)
        def _():
            o_ref[...] = _dot(act_ref[...], wd_ref[0, 0].astype(BF16))

    @pl.when((i >= nu_ref[0]) & (j == last))
    def _():
        o_ref[...] = jnp.zeros(o_ref.shape, F32)


def _moe_experts(tile_group, n_used, hs, wg, wu, wd, layer):
    def tile(i, e, tg, nu):
        return (jnp.minimum(i, nu[0] - 1), 0)

    nj = EPG // E_STEP

    def expert(i, j, tg, nu):
        return (layer, jnp.where(i < nu[0], tg[i] * nj + j, tg[nu[0] - 1] * nj + nj - 1), 0, 0)

    def group(i, j, tg, nu):
        return (layer, tg[jnp.minimum(i, nu[0] - 1)], 0, 0)

    grid_spec = pltpu.PrefetchScalarGridSpec(
        num_scalar_prefetch=2,
        grid=(N_TILES, nj),
        in_specs=[pl.BlockSpec((T_MOE, ROW_W), tile),
                  pl.BlockSpec((1, E_STEP, D, FF), expert),
                  pl.BlockSpec((1, E_STEP, D, FF), expert),
                  pl.BlockSpec((1, 1, EPG * FF, D), group, pipeline_mode=pl.Buffered(1))],
        out_specs=pl.BlockSpec((T_MOE, D), lambda i, e, tg, nu: (i, 0)),
        scratch_shapes=[pltpu.VMEM((T_MOE, D), BF16), pltpu.VMEM((T_MOE, EPG * FF), BF16)],
    )
    return pl.pallas_call(
        _moe_kernel,
        out_shape=jax.ShapeDtypeStruct((N_SORT, D), F32),
        grid_spec=grid_spec,
        compiler_params=pltpu.CompilerParams(dimension_semantics=("arbitrary", "arbitrary"),
                                             vmem_limit_bytes=MOE_VMEM_LIMIT),
        name="moe_experts",
    )(tile_group, n_used, hs, wg, wu, wd.reshape(DEPTH, N_GROUPS, EPG * FF, D))


def _comb_kernel(pos_ref, x_ref, mod_ref, y_ref, o_ref, buf_ref, sem):
    _row_copies(T_DISP,
                lambda r: y_ref.at[pl.ds(pos_ref[0, 0, r], 1), :],
                lambda r: buf_ref.at[pl.ds(r, 1), :], sem)
    pltpu.make_async_copy(y_ref.at[pl.ds(0, T_DISP), :], buf_ref, sem).wait()
    o_ref[...] = x_ref[...] + mod_ref[5:6, :] * buf_ref[...]


def _combine(pos, x, mod, ys):
    return pl.pallas_call(
        _comb_kernel,
        out_shape=jax.ShapeDtypeStruct((SEQ, D), F32),
        grid=(SEQ // T_DISP,),
        in_specs=[pl.BlockSpec((1, 1, T_DISP), lambda i: (i, 0, 0), memory_space=pltpu.SMEM),
                  pl.BlockSpec((T_DISP, D), lambda i: (i, 0)),
                  pl.BlockSpec((8, D), lambda i: (0, 0)),
                  pl.BlockSpec(memory_space=pl.ANY)],
        out_specs=pl.BlockSpec((T_DISP, D), lambda i: (i, 0)),
        scratch_shapes=[pltpu.VMEM((T_DISP, D), F32), pltpu.SemaphoreType.DMA(())],
        compiler_params=_cparams(1),
        name="moe_combine",
    )(pos, x, mod, ys)


def _moe_layer(x, h2a, meta, cnt, mod, wg, wu, wd, layer, hs):
    pos, tile_group, n_used = _dispatch_plan(meta, cnt)
    hs = _dispatch(pos, h2a, hs)
    ys = _moe_experts(tile_group, n_used, hs, wg, wu, wd, layer)
    return _combine(pos, x, mod, ys), hs


def _pad_heads(w, heads, dim):
    k = w.shape[0]
    w = w.reshape(k, heads, dim)
    return jnp.pad(w, ((0, 0), (0, 0), (0, LANES - dim))).reshape(k, heads * LANES)


def _pad_gain(g):
    return jnp.pad(g, (0, LANES - g.shape[0])).reshape(1, LANES)


def _odd_weights(w_in):
    bounds = np.cumsum([0, SWA_HEADS * HEAD_DIM, SWA_KV * HEAD_DIM, SWA_KV * HEAD_DIM, Q_LORA, KV_LORA, MLA_ROPE])
    q_s, k_s, v, c_q, c_kv, k_rope = [w_in[:, int(a):int(b)] for a, b in zip(bounds[:-1], bounds[1:])]
    q_s = _pad_heads(q_s, SWA_HEADS, HEAD_DIM)
    k_s = _pad_heads(k_s, SWA_KV, HEAD_DIM)
    v0, v1 = v[:, 0:HEAD_DIM], v[:, HEAD_DIM:2 * HEAD_DIM]
    v_s = jnp.concatenate([v0, v0, v1, v1], axis=1)
    k_r = jnp.pad(k_rope, ((0, 0), (MLA_NOPE, LANES - MLA_QK)))
    k_r_sw = jnp.pad(_swap_halves(k_rope), ((0, 0), (MLA_NOPE, LANES - MLA_QK)))
    return jnp.concatenate([q_s, k_s, v_s, c_q, c_kv, k_r, k_r_sw], axis=1).astype(BF16)


def _swap_halves(t):
    half = t.shape[-1] // 2
    return jnp.concatenate([t[..., half:], t[..., :half]], axis=-1)


def _uq_weights(w_uq):
    w = w_uq.reshape(Q_LORA, MLA_HEADS, MLA_QK)
    plain = jnp.pad(w, ((0, 0), (0, 0), (0, LANES - MLA_QK)))
    swapped = jnp.pad(_swap_halves(w[:, :, MLA_NOPE:]), ((0, 0), (0, 0), (MLA_NOPE, LANES - MLA_QK)))
    return jnp.concatenate([plain.reshape(Q_LORA, -1), swapped.reshape(Q_LORA, -1)], axis=1)


def _rope_gains(g):
    plain = jnp.pad(g, (0, LANES - MLA_QK)).reshape(1, LANES)
    partner = jnp.pad(_swap_halves(g[MLA_NOPE:]), (MLA_NOPE, LANES - MLA_QK)).reshape(1, LANES)
    return plain, partner


def _router_weights(w_group, b_group, w_expert, b_expert):
    w = jnp.pad(jnp.concatenate([w_expert, w_group], axis=1), ((0, 0), (0, LANES - N_EXPERTS - N_GROUPS)))
    hi = w.astype(BF16)
    lo = (w - hi.astype(F32)).astype(BF16)
    b = jnp.pad(jnp.concatenate([b_expert, b_group]), (0, LANES - N_EXPERTS - N_GROUPS)).reshape(1, LANES)
    return hi, lo, b


def kernel(x, c, positions, ada_w, ada_b, norm1_g, norm2_g, cp_w_in, conv_w, pool_w, pool_scale,
           cp_w_out, at_w_in, swa_q_g, swa_k_g, swa_sinks, mla_q_norm_g, mla_kv_norm_g, mla_w_uq,
           mla_w_ukv, mla_q_g, mla_k_g, at_w_out, moe_w_group, moe_b_group, moe_w_expert,
           moe_b_expert, moe_w_gate, moe_w_up, moe_w_down):
    xs = x.reshape(SEQ, D)
    mods = _ada_mod(c, ada_w, ada_b)
    tc, ts1, ts2 = _rope_tables(positions)
    hs = jnp.zeros((N_SORT, ROW_W), F32)
    for l in range(DEPTH):
        i = l // 2
        mod = mods[l]
        n1 = norm1_g[l].reshape(1, D)
        n2 = norm2_g[l].reshape(1, D)
        wr_hi, wr_lo, br = _router_weights(moe_w_group[l], moe_b_group[l], moe_w_expert[l], moe_b_expert[l])
        if l % 2 == 0:
            xs, h2a, meta, cnt = _even_layer(
                xs, mod, n1, n2, cp_w_in[i].astype(BF16), conv_w[i], pool_w[i].astype(BF16),
                pool_scale[i].reshape(1, 4 * POOL_G), cp_w_out[i].astype(BF16), wr_hi, wr_lo, br)
        else:
            ukv = mla_w_ukv[i].reshape(KV_LORA, MLA_HEADS, MLA_NOPE + MLA_V)
            wuk = _pad_heads(ukv[:, :, :MLA_NOPE].reshape(KV_LORA, MLA_HEADS * MLA_NOPE), MLA_HEADS, MLA_NOPE)
            wuv = ukv[:, :, MLA_NOPE:].reshape(KV_LORA, MLA_HEADS * MLA_V)
            wuq = _uq_weights(mla_w_uq[i])
            gmq, gmqs = _rope_gains(mla_q_g[i])
            gmk, gmks = _rope_gains(mla_k_g[i])
            qs, ks, vs, qm, km, vm = _proj_layer(
                xs, mod, n1, _odd_weights(at_w_in[i]), _pad_gain(swa_q_g[i]), _pad_gain(swa_k_g[i]),
                mla_q_norm_g[i].reshape(1, Q_LORA), mla_kv_norm_g[i].reshape(1, KV_LORA),
                wuq.astype(BF16), wuk.astype(BF16), wuv.astype(BF16),
                gmq, gmqs, gmk, gmks, tc, ts1, ts2)
            o_s = _swa_layer(swa_sinks[i], qs, ks, vs)
            o_m = _mla_layer(qm, km, vm)
            xs, h2a, meta, cnt = _post_layer(xs, o_s, o_m, mod, n2, at_w_out[i].astype(BF16), wr_hi, wr_lo, br)
        xs, hs = _moe_layer(xs, h2a, meta, cnt, mod, moe_w_gate, moe_w_up, moe_w_down, l, hs)
    return xs.reshape(1, SEQ, D)
```

```python
import functools

import numpy as np
import jax
import jax.numpy as jnp
from jax import lax
from jax.experimental import pallas as pl
from jax.experimental.pallas import tpu as pltpu

F32 = jnp.float32
BF16 = jnp.bfloat16

D = 1024
SEQ = 16384
DEPTH = 4
EPS = 1e-6
LANES = 128
CONV_CH = 512
POOL_WINDOWS = (2, 4, 8, 16)
POOL_G = 128
HALO = 16
SWA_HEADS = 8
SWA_KV = 2
HEAD_DIM = 64
WINDOW = 128
MLA_HEADS = 8
MLA_NOPE = 64
MLA_ROPE = 32
MLA_QK = MLA_NOPE + MLA_ROPE
MLA_V = 64
Q_LORA = 384
KV_LORA = 256
ROPE_THETA = 10000.0
N_GROUPS = 4
EPG = 8
N_EXPERTS = N_GROUPS * EPG
FF = 256
NEG = -1e30
LOG2E = 1.4426950408889634

T_TOK = 512
N_SUB = 2
T_SUB = T_TOK // N_SUB
T_ATT = 512
T_MOE = 512
T_DISP = 2048
ROW_W = D + LANES
N_SORT = SEQ + N_GROUPS * T_MOE
N_TILES = N_SORT // T_MOE
E_STEP = 8
T_MQ = 1024
T_MK = 512
MLA_SPLIT = 1
MLA_DEPTH = 4
VMEM_LIMIT = 48 * 1024 * 1024
MOE_VMEM_LIMIT = 56 * 1024 * 1024

HEADS_W = MLA_HEADS * LANES
KV_W = SWA_KV * LANES
ATT_W = MLA_HEADS * MLA_V
O_QS = 0
O_KS = O_QS + HEADS_W
O_VS = O_KS + KV_W
O_CQ = O_VS + KV_W
O_CKV = O_CQ + Q_LORA
O_KR = O_CKV + KV_LORA
ODD_W = O_KR + 2 * LANES


def _cparams(n_axes=1):
    return pltpu.CompilerParams(dimension_semantics=("arbitrary",) * n_axes,
                                vmem_limit_bytes=VMEM_LIMIT)


def _rms(x):
    return x * lax.rsqrt(jnp.mean(x * x, axis=-1, keepdims=True) + EPS)


def _dot(a, b):
    return jnp.dot(a, b, preferred_element_type=F32)


def _dot_nt(a, b):
    return lax.dot_general(a, b, (((1,), (1,)), ((), ())), preferred_element_type=F32)


def _ada_kernel(c_ref, w_ref, b_ref, o_ref):
    c = c_ref[...]
    ca = c * jax.nn.sigmoid(c)
    o_ref[0] = jnp.sum(w_ref[0] * ca, axis=0, keepdims=True) + b_ref[0]


def _ada_mod(c, ada_w, ada_b):
    c_col = c.reshape(D, 1)
    b = ada_b.reshape(DEPTH * 6, 1, D)
    out = pl.pallas_call(
        _ada_kernel,
        out_shape=jax.ShapeDtypeStruct((DEPTH * 6, 1, D), F32),
        grid=(DEPTH, 6),
        in_specs=[pl.BlockSpec((D, 1), lambda l, j: (0, 0)),
                  pl.BlockSpec((1, D, D), lambda l, j: (l, 0, j)),
                  pl.BlockSpec((1, 1, D), lambda l, j: (l * 6 + j, 0, 0))],
        out_specs=pl.BlockSpec((1, 1, D), lambda l, j: (l * 6 + j, 0, 0)),
        compiler_params=_cparams(2),
        name="ada_mod",
    )(c_col, ada_w, b)
    mod = out.reshape(DEPTH, 6, D)
    return jnp.pad(mod, ((0, 0), (0, 2), (0, 0)))


def _rope_kernel(pos_ref, inv_ref, c_ref, s1_ref, s2_ref):
    pos = pos_ref[...].astype(F32)
    ang = pos * inv_ref[...]
    lane = lax.broadcasted_iota(jnp.int32, ang.shape, 1)
    cs = jnp.cos(ang)
    sn = jnp.sin(ang)
    mid = MLA_NOPE + MLA_ROPE // 2
    c_ref[...] = jnp.where(lane < MLA_NOPE, 1.0, jnp.where(lane < MLA_QK, cs, 0.0))
    s1_ref[...] = jnp.where((lane >= MLA_NOPE) & (lane < mid), -sn, 0.0)
    s2_ref[...] = jnp.where((lane >= mid) & (lane < MLA_QK), sn, 0.0)


def _rope_tables(positions):
    half = MLA_ROPE // 2
    inv = jnp.power(ROPE_THETA, -jnp.arange(half, dtype=F32) / half)
    inv_lane = jnp.concatenate([jnp.zeros((MLA_NOPE,), F32), inv, inv,
                                jnp.zeros((LANES - MLA_QK,), F32)]).reshape(1, LANES)
    pos = positions.reshape(SEQ, 1)
    shp = jax.ShapeDtypeStruct((SEQ, LANES), F32)
    spec = pl.BlockSpec((T_TOK, LANES), lambda i: (i, 0))
    return pl.pallas_call(
        _rope_kernel,
        out_shape=(shp, shp, shp),
        grid=(SEQ // T_TOK,),
        in_specs=[pl.BlockSpec((T_TOK, 1), lambda i: (i, 0)),
                  pl.BlockSpec((1, LANES), lambda i: (0, 0))],
        out_specs=(spec, spec, spec),
        compiler_params=_cparams(1),
        name="rope_tables",
    )(pos, inv_lane)


def _route(lg):
    lane = lax.broadcasted_iota(jnp.int32, lg.shape, 1)
    lane_f = lane.astype(F32)
    is_g = (lane >= N_EXPERTS) & (lane < N_EXPERTS + N_GROUPS)
    gl = jnp.where(is_g, lg, NEG)
    gmax = jnp.max(gl, axis=-1, keepdims=True)
    gidx = jnp.min(jnp.where(is_g & (gl == gmax), lane_f - N_EXPERTS, 1e3), axis=-1, keepdims=True)
    gsum = jnp.sum(jnp.where(is_g, jnp.exp(gl - gmax), 0.0), axis=-1, keepdims=True)
    gw = 1.0 / gsum
    grp_of_lane = (lane >> 3).astype(F32)
    in_grp = (lane < N_EXPERTS) & (grp_of_lane == gidx)
    el = jnp.where(in_grp, lg, NEG)
    m1 = jnp.max(el, axis=-1, keepdims=True)
    i1 = jnp.min(jnp.where(in_grp & (el == m1), lane_f, 1e3), axis=-1, keepdims=True)
    rest = in_grp & (lane_f != i1)
    el2 = jnp.where(rest, lg, NEG)
    m2 = jnp.max(el2, axis=-1, keepdims=True)
    i2 = jnp.min(jnp.where(rest & (el2 == m2), lane_f, 1e3), axis=-1, keepdims=True)
    r = jnp.exp(m2 - m1)
    w1 = gw / (1.0 + r)
    w2 = w1 * r
    return jnp.where(lane_f == i1, w1, jnp.where(lane_f == i2, w2, 0.0)), gidx


def _tail(x, y, mod, n2, wr_hi, wr_lo, br, rows_at, xo_ref, h2_ref, meta_ref, cnt_ref, carry_ref):
    gate1 = mod[2:3]
    shift2, scale2 = mod[3:4], mod[4:5]
    xn = x + gate1 * y
    xo_ref[rows_at, :] = xn
    h2 = _rms(xn) * n2 * (1.0 + scale2) + shift2
    hi = h2.astype(BF16)
    lo = (h2 - hi.astype(F32)).astype(BF16)
    both = _dot(hi, jnp.concatenate([wr_hi, wr_lo], axis=1))
    lg = both[:, 0:LANES] + both[:, LANES:2 * LANES] + _dot(lo, wr_hi) + br
    gates, gidx = _route(lg)
    rows = lg.shape[0]
    lane = lax.broadcasted_iota(jnp.int32, (rows, LANES), 1)
    lane_f = lane.astype(F32)
    g8 = gates
    for g in range(1, N_GROUPS):
        g8 = g8 + pltpu.roll(gates, LANES - EPG * g, 1)
    onehot = (lane_f == gidx).astype(F32)
    r = lax.broadcasted_iota(jnp.int32, (rows, rows), 0)
    c = lax.broadcasted_iota(jnp.int32, (rows, rows), 1)
    before = jnp.where(c < r, 1.0, 0.0).astype(BF16)
    cum = _dot(before, onehot.astype(BF16)) + carry_ref[0:1, :]
    rank = jnp.sum(jnp.where(lane_f == gidx, cum, 0.0), axis=-1, keepdims=True)
    meta = jnp.where(lane < EPG, g8, jnp.where(lane == EPG, gidx, jnp.where(lane == EPG + 1, rank, 0.0)))
    h2_ref[rows_at, 0:D] = h2
    h2_ref[rows_at, D:D + LANES] = meta
    meta_ref[rows_at, :] = meta
    total = carry_ref[0:1, :] + jnp.sum(onehot, axis=0, keepdims=True)
    carry_ref[...] = jnp.broadcast_to(total, carry_ref.shape)
    cnt_ref[...] = jnp.broadcast_to(total, cnt_ref.shape)


def _even_kernel(x_ref, xh_ref, mod_ref, n1_ref, n2_ref, win_ref, cw_ref, pw_ref, ps_ref,
                 wout_ref, wrh_ref, wrl_ref, br_ref, xo_ref, h2_ref, meta_ref, cnt_ref, carry_ref):
    i = pl.program_id(0)
    _init_carry(i, carry_ref)
    for sub in range(N_SUB):
        rows_at = slice(sub * T_SUB, (sub + 1) * T_SUB)
        halo = xh_ref[...] if sub == 0 else x_ref[sub * T_SUB - HALO: sub * T_SUB, :]
        _even_rows(i * T_TOK + sub * T_SUB, x_ref[rows_at, :], halo, rows_at, mod_ref, n1_ref, n2_ref, win_ref,
                   cw_ref, pw_ref, ps_ref, wout_ref, wrh_ref, wrl_ref, br_ref, xo_ref, h2_ref, meta_ref,
                   cnt_ref, carry_ref)


def _init_carry(i, carry_ref):
    @pl.when(i == 0)
    def _():
        carry_ref[...] = jnp.zeros(carry_ref.shape, F32)


def _even_rows(tok0, x, halo, rows_at, mod_ref, n1_ref, n2_ref, win_ref, cw_ref, pw_ref, ps_ref,
               wout_ref, wrh_ref, wrl_ref, br_ref, xo_ref, h2_ref, meta_ref, cnt_ref, carry_ref):
    mod = mod_ref[...]
    shift1, scale1 = mod[0:1], mod[1:2]
    xa = jnp.concatenate([halo, x], axis=0)
    h = _rms(xa) * n1_ref[...] * (1.0 + scale1) + shift1
    z = _dot(h.astype(BF16), win_ref[...])
    rows = T_SUB + HALO
    row = lax.broadcasted_iota(jnp.int32, (rows, 1), 0)
    tpos = tok0 + row - HALO
    live = (tpos >= 0).astype(F32)
    bg = z[:, 0:CONV_CH]
    v = z[:, CONV_CH:2 * CONV_CH] * z[:, 2 * CONV_CH:3 * CONV_CH] * live
    cw = cw_ref[...]
    conv = v * cw[0:1] + pltpu.roll(v, 1, 0) * cw[1:2] + pltpu.roll(v, 2, 0) * cw[2:3]
    parts = [(bg * conv)[HALO:]]
    ps = ps_ref[...]
    tcount = (tpos + 1).astype(F32)
    for gi, w in enumerate(POOL_WINDOWS):
        ug = z[:, 3 * CONV_CH + gi * POOL_G: 3 * CONV_CH + (gi + 1) * POOL_G] * live
        s = ug
        k = 1
        while k < w:
            s = s + pltpu.roll(s, k, 0)
            k *= 2
        inv = 1.0 / jnp.minimum(tcount, float(w))
        d = (s * inv - ug)[HALO:].astype(BF16)
        parts.append(_dot(d, pw_ref[gi]) * ps[:, gi * POOL_G:(gi + 1) * POOL_G])
    cat = jnp.concatenate(parts, axis=-1).astype(BF16)
    y = _dot(cat, wout_ref[...])
    _tail(x, y, mod, n2_ref[...], wrh_ref[...], wrl_ref[...], br_ref[...], rows_at, xo_ref, h2_ref, meta_ref,
          cnt_ref, carry_ref)


def _tail_out(n):
    shapes = (jax.ShapeDtypeStruct((SEQ, D), F32), jax.ShapeDtypeStruct((SEQ, ROW_W), F32),
              jax.ShapeDtypeStruct((SEQ, LANES), F32), jax.ShapeDtypeStruct((8, LANES), F32))
    specs = (pl.BlockSpec((n, D), lambda i: (i, 0)), pl.BlockSpec((n, ROW_W), lambda i: (i, 0)),
             pl.BlockSpec((n, LANES), lambda i: (i, 0)), pl.BlockSpec((8, LANES), lambda i: (0, 0)))
    scratch = [pltpu.VMEM((8, LANES), F32)]
    return shapes, specs, scratch


def _full(shape):
    nd = len(shape)
    return pl.BlockSpec(shape, lambda i: (0,) * nd)


def _even_layer(x, mod, n1, n2, w_in, conv_w, pool_w, pool_scale, w_out, wr_hi, wr_lo, br):
    shapes, specs, scratch = _tail_out(T_TOK)
    hb = T_TOK // HALO
    return pl.pallas_call(
        _even_kernel,
        out_shape=shapes,
        grid=(SEQ // T_TOK,),
        in_specs=[pl.BlockSpec((T_TOK, D), lambda i: (i, 0)),
                  pl.BlockSpec((HALO, D), lambda i: (jnp.maximum(i * hb - 1, 0), 0)),
                  _full((8, D)), _full((1, D)), _full((1, D)),
                  _full((D, 4 * CONV_CH)), _full((3, CONV_CH)), _full((4, POOL_G, POOL_G)),
                  _full((1, 4 * POOL_G)), _full((D, D)),
                  _full((D, LANES)), _full((D, LANES)), _full((1, LANES))],
        out_specs=specs,
        scratch_shapes=scratch,
        compiler_params=_cparams(1),
        name="even_mixer",
    )(x, x, mod, n1, n2, w_in, conv_w, pool_w, pool_scale, w_out, wr_hi, wr_lo, br)


def _proj_kernel(x_ref, mod_ref, n1_ref, win_ref, gsq_ref, gsk_ref, gqn_ref, gkvn_ref,
                 wuq_ref, wuk_ref, wuv_ref, gmq_ref, gmqs_ref, gmk_ref, gmks_ref, c_ref, s1_ref, s2_ref,
                 qs_ref, ks_ref, vs_ref, qm_ref, km_ref, vm_ref):
    refs = (x_ref, mod_ref, n1_ref, win_ref, gsq_ref, gsk_ref, gqn_ref, gkvn_ref, wuq_ref, wuk_ref, wuv_ref,
            gmq_ref, gmqs_ref, gmk_ref, gmks_ref, c_ref, s1_ref, s2_ref, qs_ref, ks_ref, vs_ref, qm_ref, km_ref, vm_ref)
    _proj_rows(slice(0, T_TOK), *refs)


def _proj_rows(ra, x_ref, mod_ref, n1_ref, win_ref, gsq_ref, gsk_ref, gqn_ref, gkvn_ref,
               wuq_ref, wuk_ref, wuv_ref, gmq_ref, gmqs_ref, gmk_ref, gmks_ref, c_ref, s1_ref, s2_ref,
               qs_ref, ks_ref, vs_ref, qm_ref, km_ref, vm_ref):
    x = x_ref[ra, :]
    mod = mod_ref[...]
    shift1, scale1 = mod[0:1], mod[1:2]
    h = _rms(x) * n1_ref[...] * (1.0 + scale1) + shift1
    z = _dot(h.astype(BF16), win_ref[...])

    def head_norm(t, g, dim):
        ms = jnp.sum(t * t, axis=-1, keepdims=True) * (1.0 / dim)
        return t * lax.rsqrt(ms + EPS) * g

    gsq, gsk = gsq_ref[...] * (HEAD_DIM ** -0.5), gsk_ref[...]
    for hd in range(SWA_HEADS):
        qh = head_norm(z[:, O_QS + hd * LANES: O_QS + (hd + 1) * LANES], gsq, HEAD_DIM)
        qs_ref[ra, hd * LANES:(hd + 1) * LANES] = qh.astype(BF16)
    for kv in range(SWA_KV):
        kh = head_norm(z[:, O_KS + kv * LANES: O_KS + (kv + 1) * LANES], gsk, HEAD_DIM)
        ks_ref[ra, kv * LANES:(kv + 1) * LANES] = kh.astype(BF16)
    vs_ref[ra, :] = z[:, O_VS:O_CQ].astype(BF16)

    cq = (_rms(z[:, O_CQ:O_CKV]) * gqn_ref[...]).astype(BF16)
    ckv = (_rms(z[:, O_CKV:O_KR]) * gkvn_ref[...]).astype(BF16)
    qm = _dot(cq, wuq_ref[...])
    kn = _dot(ckv, wuk_ref[...])
    vm = _dot(ckv, wuv_ref[...])
    lane = lax.broadcasted_iota(jnp.int32, (ra.stop - ra.start, LANES), 1)
    for j in range(MLA_HEADS // 2):
        vv = vm[:, j * LANES:(j + 1) * LANES]
        even = jnp.where(lane < MLA_V, vv, jnp.where(lane == MLA_V, 1.0, 0.0))
        odd = jnp.where(lane < MLA_V, jnp.where(lane == 0, 1.0, 0.0), vv)
        vm_ref[ra, (2 * j) * LANES:(2 * j + 1) * LANES] = even.astype(BF16)
        vm_ref[ra, (2 * j + 1) * LANES:(2 * j + 2) * LANES] = odd.astype(BF16)
    kr = z[:, O_KR:O_KR + LANES]
    krs = z[:, O_KR + LANES:ODD_W]
    cs = c_ref[ra, :]
    sp = s1_ref[ra, :] + s2_ref[ra, :]
    qscale = MLA_QK ** -0.5 * LOG2E
    gcq = cs * (gmq_ref[...] * qscale)
    gsq_r = sp * (gmqs_ref[...] * qscale)
    gck = cs * gmk_ref[...]
    krot = krs * (sp * gmks_ref[...])

    def inv_rms(t):
        return lax.rsqrt(jnp.sum(t * t, axis=-1, keepdims=True) * (1.0 / MLA_QK) + EPS)

    for hd in range(MLA_HEADS):
        sl = slice(hd * LANES, (hd + 1) * LANES)
        sw = slice((MLA_HEADS + hd) * LANES, (MLA_HEADS + hd + 1) * LANES)
        qh = qm[:, sl]
        qm_ref[ra, sl] = ((qh * gcq + qm[:, sw] * gsq_r) * inv_rms(qh)).astype(BF16)
        kh = kn[:, sl] + kr
        km_ref[ra, sl] = ((kh * gck + krot) * inv_rms(kh)).astype(BF16)


def _proj_layer(x, mod, n1, w_in, gsq, gsk, gqn, gkvn, wuq, wuk, wuv, gmq, gmqs, gmk, gmks, tc, ts1, ts2):
    def tok(wd):
        return pl.BlockSpec((T_TOK, wd), lambda i: (i, 0))
    widths = (HEADS_W, KV_W, KV_W, HEADS_W, HEADS_W, HEADS_W)
    return pl.pallas_call(
        _proj_kernel,
        out_shape=tuple(jax.ShapeDtypeStruct((SEQ, wd), BF16) for wd in widths),
        grid=(SEQ // T_TOK,),
        in_specs=[tok(D), _full((8, D)), _full((1, D)), _full((D, ODD_W)),
                  _full((1, LANES)), _full((1, LANES)), _full((1, Q_LORA)), _full((1, KV_LORA)),
                  _full((Q_LORA, 2 * HEADS_W)), _full((KV_LORA, HEADS_W)), _full((KV_LORA, ATT_W)),
                  _full((1, LANES)), _full((1, LANES)), _full((1, LANES)), _full((1, LANES)),
                  tok(LANES), tok(LANES), tok(LANES)],
        out_specs=tuple(tok(wd) for wd in widths),
        compiler_params=_cparams(1),
        name="odd_proj",
    )(x, mod, n1, w_in, gsq, gsk, gqn, gkvn, wuq, wuk, wuv, gmq, gmqs, gmk, gmks, tc, ts1, ts2)


def _swa_kernel(sink_ref, q_ref, k_ref, kh_ref, v_ref, vh_ref, o_ref):
    i = pl.program_id(0)
    kcat = jnp.concatenate([kh_ref[...], k_ref[...]], axis=0)
    vcat = jnp.concatenate([vh_ref[...], v_ref[...]], axis=0)
    grp = SWA_HEADS // SWA_KV
    r = lax.broadcasted_iota(jnp.int32, (grp * WINDOW, 2 * WINDOW), 0) & (WINDOW - 1)
    c = lax.broadcasted_iota(jnp.int32, (grp * WINDOW, 2 * WINDOW), 1)
    rel = WINDOW + r - c
    lane = lax.broadcasted_iota(jnp.int32, (WINDOW, LANES), 1)
    for sb in range(T_ATT // WINDOW):
        rows = slice(sb * WINDOW, (sb + 1) * WINDOW)
        kb = kcat[sb * WINDOW: sb * WINDOW + 2 * WINDOW]
        vb = vcat[sb * WINDOW: sb * WINDOW + 2 * WINDOW]
        kpos = i * T_ATT + (sb - 1) * WINDOW + c
        ok = (rel >= 0) & (rel < WINDOW) & (kpos >= 0)
        outs = []
        for kv in range(SWA_KV):
            q = jnp.concatenate([q_ref[rows, (kv * grp + g) * LANES:(kv * grp + g + 1) * LANES]
                                 for g in range(grp)], axis=0)
            sink = jnp.concatenate([jnp.full((WINDOW, 1), sink_ref[kv * grp + g], F32)
                                    for g in range(grp)], axis=0)
            s = jnp.where(ok, _dot_nt(q, kb[:, kv * LANES:(kv + 1) * LANES]), NEG)
            m = jnp.maximum(jnp.max(s, axis=-1, keepdims=True), sink)
            e = jnp.exp(s - m)
            den = jnp.sum(e, axis=-1, keepdims=True) + jnp.exp(sink - m)
            p = (e * (1.0 / den)).astype(BF16)
            o = _dot(p, vb[:, kv * LANES:(kv + 1) * LANES])
            outs += [o[g * WINDOW:(g + 1) * WINDOW] for g in range(grp)]
        for j in range(SWA_HEADS // 2):
            pair = jnp.where(lane < HEAD_DIM, outs[2 * j], outs[2 * j + 1])
            o_ref[rows, j * LANES:(j + 1) * LANES] = pair.astype(BF16)


def _swa_layer(sinks, qs, ks, vs):
    hb = T_ATT // WINDOW
    return pl.pallas_call(
        _swa_kernel,
        out_shape=jax.ShapeDtypeStruct((SEQ, ATT_W), BF16),
        grid=(SEQ // T_ATT,),
        in_specs=[pl.BlockSpec(memory_space=pltpu.SMEM),
                  pl.BlockSpec((T_ATT, HEADS_W), lambda i: (i, 0)),
                  pl.BlockSpec((T_ATT, KV_W), lambda i: (i, 0)),
                  pl.BlockSpec((WINDOW, KV_W), lambda i: (jnp.maximum(i * hb - 1, 0), 0)),
                  pl.BlockSpec((T_ATT, KV_W), lambda i: (i, 0)),
                  pl.BlockSpec((WINDOW, KV_W), lambda i: (jnp.maximum(i * hb - 1, 0), 0))],
        out_specs=pl.BlockSpec((T_ATT, ATT_W), lambda i: (i, 0)),
        compiler_params=_cparams(1),
        name="swa_attn",
    )(sinks, qs, ks, ks, vs, vs)


def _mla_kernel(qi_ref, ki_ref, q_ref, k_ref, v_ref, o_ref, m_ref, acc_ref):
    step = pl.program_id(0)
    qi = qi_ref[step]
    ki = ki_ref[step]

    @pl.when(ki == 0)
    def _():
        m_ref[...] = jnp.full(m_ref.shape, NEG, F32)
        acc_ref[...] = jnp.zeros(acc_ref.shape, F32)

    lane = lax.broadcasted_iota(jnp.int32, (T_MQ, LANES), 1)
    lo = lane < MLA_V
    nc = T_MK // LANES
    sub = T_MQ // MLA_SPLIT
    ratio = T_MQ // T_MK
    units = [(hd, part) for hd in range(MLA_HEADS) for part in range(MLA_SPLIT)]

    def scores(unit):
        hd, part = unit
        sl = slice(hd * LANES, (hd + 1) * LANES)
        return _dot_nt(q_ref[part * sub:(part + 1) * sub, sl], k_ref[:, sl])

    def update(masked):
        if masked:
            r = lax.broadcasted_iota(jnp.int32, (sub, LANES), 0)
            lane_s = lax.broadcasted_iota(jnp.int32, (sub, LANES), 1)

        def softmax_part(unit, s):
            hd, part = unit
            rows = slice(part * sub, (part + 1) * sub)
            cols = [s[:, c * LANES:(c + 1) * LANES] for c in range(nc)]
            if masked:
                off = qi * T_MQ - ki * T_MK + part * sub
                cols = [jnp.where(r + off >= lane_s + c * LANES, cols[c], NEG) for c in range(nc)]
            cmax = cols[0]
            for c in range(1, nc):
                cmax = jnp.maximum(cmax, cols[c])
            m_prev = m_ref[hd, rows]
            m_new = jnp.maximum(m_prev, jnp.max(cmax, axis=-1, keepdims=True))
            m_ref[hd, rows] = m_new
            alpha = jnp.exp2(m_prev - m_new)
            p = jnp.concatenate([jnp.exp2(cols[c] - m_new).astype(BF16) for c in range(nc)], axis=-1)
            return p, alpha

        def value_part(unit, p, alpha):
            hd, part = unit
            rows = slice(part * sub, (part + 1) * sub)
            acc_ref[hd, rows] = acc_ref[hd, rows] * alpha + _dot(p, v_ref[:, hd * LANES:(hd + 1) * LANES])

        s_q = [scores(u) for u in units[:MLA_DEPTH]]
        pend = None
        for n, u in enumerate(units):
            if n + MLA_DEPTH < len(units):
                s_q.append(scores(units[n + MLA_DEPTH]))
            cur = softmax_part(u, s_q[n])
            if pend is not None:
                value_part(units[n - 1], *pend)
            pend = cur
        value_part(units[-1], *pend)

    @pl.when(ki < qi * ratio)
    def _():
        update(False)

    @pl.when(ki >= qi * ratio)
    def _():
        update(True)

    @pl.when(ki == qi * ratio + ratio - 1)
    def _():
        for j in range(MLA_HEADS // 2):
            ae = acc_ref[2 * j]
            ao = acc_ref[2 * j + 1]
            out = jnp.where(lo, ae * (1.0 / ae[:, MLA_V:MLA_V + 1]), ao * (1.0 / ao[:, 0:1]))
            o_ref[:, j * LANES:(j + 1) * LANES] = out.astype(BF16)


def _mla_layer(qm, km, vm):
    nb = SEQ // T_MQ
    ratio = T_MQ // T_MK
    qi = np.concatenate([np.full(ratio * (n + 1), n, np.int32) for n in range(nb)])
    ki = np.concatenate([np.arange(ratio * (n + 1), dtype=np.int32) for n in range(nb)])
    grid_spec = pltpu.PrefetchScalarGridSpec(
        num_scalar_prefetch=2,
        grid=(int(qi.shape[0]),),
        in_specs=[pl.BlockSpec((T_MQ, HEADS_W), lambda s, qi, ki: (qi[s], 0)),
                  pl.BlockSpec((T_MK, HEADS_W), lambda s, qi, ki: (ki[s], 0)),
                  pl.BlockSpec((T_MK, HEADS_W), lambda s, qi, ki: (ki[s], 0))],
        out_specs=pl.BlockSpec((T_MQ, ATT_W), lambda s, qi, ki: (qi[s], 0)),
        scratch_shapes=[pltpu.VMEM((MLA_HEADS, T_MQ, LANES), F32),
                        pltpu.VMEM((MLA_HEADS, T_MQ, LANES), F32)],
    )
    return pl.pallas_call(
        _mla_kernel,
        out_shape=jax.ShapeDtypeStruct((SEQ, ATT_W), BF16),
        grid_spec=grid_spec,
        compiler_params=_cparams(1),
        name="mla_attn",
    )(jnp.asarray(qi), jnp.asarray(ki), qm, km, vm)


def _post_kernel(x_ref, os_ref, om_ref, mod_ref, n2_ref, wout_ref, wrh_ref, wrl_ref, br_ref,
                 xo_ref, h2_ref, meta_ref, cnt_ref, carry_ref):
    _init_carry(pl.program_id(0), carry_ref)
    for sub in range(N_SUB):
        rows_at = slice(sub * T_SUB, (sub + 1) * T_SUB)
        y = _dot(os_ref[rows_at, :], wout_ref[0:ATT_W, :]) + _dot(om_ref[rows_at, :], wout_ref[ATT_W:2 * ATT_W, :])
        _tail(x_ref[rows_at, :], y, mod_ref[...], n2_ref[...], wrh_ref[...], wrl_ref[...], br_ref[...],
              rows_at, xo_ref, h2_ref, meta_ref, cnt_ref, carry_ref)


def _post_layer(x, o_s, o_m, mod, n2, w_out, wr_hi, wr_lo, br):
    shapes, specs, scratch = _tail_out(T_TOK)
    return pl.pallas_call(
        _post_kernel,
        out_shape=shapes,
        grid=(SEQ // T_TOK,),
        in_specs=[pl.BlockSpec((T_TOK, D), lambda i: (i, 0)),
                  pl.BlockSpec((T_TOK, ATT_W), lambda i: (i, 0)),
                  pl.BlockSpec((T_TOK, ATT_W), lambda i: (i, 0)),
                  _full((8, D)), _full((1, D)), _full((D, D)),
                  _full((D, LANES)), _full((D, LANES)), _full((1, LANES))],
        out_specs=specs,
        scratch_shapes=scratch,
        compiler_params=_cparams(1),
        name="odd_post",
    )(x, o_s, o_m, mod, n2, w_out, wr_hi, wr_lo, br)


def _dispatch_plan(meta, cnt):
    grp = meta[:, EPG].astype(jnp.int32)
    rank = meta[:, EPG + 1].astype(jnp.int32)
    counts = cnt[0, :N_GROUPS].astype(jnp.int32)
    padded = ((counts + T_MOE - 1) // T_MOE) * T_MOE
    ends = jnp.cumsum(padded)
    pos = (ends - padded)[grp] + rank
    n_used = ends[-1] // T_MOE
    tile_start = jnp.arange(N_TILES, dtype=jnp.int32) * T_MOE
    tile_group = jnp.minimum(jnp.sum(tile_start[:, None] >= ends[None, :], axis=1), N_GROUPS - 1)
    return pos.reshape(SEQ // T_DISP, 1, T_DISP), tile_group.astype(jnp.int32), n_used.reshape(1)


def _row_copies(n, src_at, dst_at, sem):
    for r in range(n):
        pltpu.make_async_copy(src_at(r), dst_at(r), sem).start(priority=r % 2)


def _disp_kernel(pos_ref, x_ref, init_ref, o_ref, sem):
    del init_ref
    _row_copies(T_DISP,
                lambda r: x_ref.at[pl.ds(r, 1), :],
                lambda r: o_ref.at[pl.ds(pos_ref[0, 0, r], 1), :], sem)
    pltpu.make_async_copy(x_ref, o_ref.at[pl.ds(0, T_DISP), :], sem).wait()


def _dispatch(pos, h2a, hs):
    return pl.pallas_call(
        _disp_kernel,
        out_shape=jax.ShapeDtypeStruct((N_SORT, ROW_W), F32),
        grid=(SEQ // T_DISP,),
        in_specs=[pl.BlockSpec((1, 1, T_DISP), lambda i: (i, 0, 0), memory_space=pltpu.SMEM),
                  pl.BlockSpec((T_DISP, ROW_W), lambda i: (i, 0)),
                  pl.BlockSpec(memory_space=pl.ANY)],
        out_specs=pl.BlockSpec(memory_space=pl.ANY),
        scratch_shapes=[pltpu.SemaphoreType.DMA(())],
        input_output_aliases={2: 0},
        compiler_params=_cparams(1),
        name="moe_dispatch",
    )(pos, h2a, hs)


def _moe_kernel(tg_ref, nu_ref, x_ref, wg_ref, wu_ref, wd_ref, o_ref, xb_ref, act_ref):
    del tg_ref
    i = pl.program_id(0)
    j = pl.program_id(1)
    last = EPG // E_STEP - 1

    @pl.when(i < nu_ref[0])
    def _():
        @pl.when(j == 0)
        def _():
            xb_ref[...] = x_ref[:, 0:D].astype(BF16)

        xb = xb_ref[...]
        meta = x_ref[:, D:ROW_W]
        lane = lax.broadcasted_iota(jnp.int32, meta.shape, 1)
        for k in range(E_STEP):
            a = _dot(xb, wg_ref[0, k].astype(BF16))
            u = _dot(xb, wu_ref[0, k].astype(BF16))
            gate = jnp.sum(jnp.where(lane == j * E_STEP + k, meta, 0.0), axis=-1, keepdims=True)
            act = (a * jax.nn.sigmoid(a) * u * gate).astype(BF16)
            for jj in range(EPG // E_STEP):
                @pl.when(j == jj)
                def _(jj=jj, k=k, act=act):
                    act_ref[:, (jj * E_STEP + k) * FF:(jj * E_STEP + k + 1) * FF] = act

        @pl.when(j == last)
        def _():
            o_ref[...] = _dot(act_ref[...], wd_ref[0, 0].astype(BF16))

    @pl.when((i >= nu_ref[0]) & (j == last))
    def _():
        o_ref[...] = jnp.zeros(o_ref.shape, F32)


def _moe_experts(tile_group, n_used, hs, wg, wu, wd, layer):
    def tile(i, e, tg, nu):
        return (jnp.minimum(i, nu[0] - 1), 0)

    nj = EPG // E_STEP

    def expert(i, j, tg, nu):
        return (layer, jnp.where(i < nu[0], tg[i] * nj + j, tg[nu[0] - 1] * nj + nj - 1), 0, 0)

    def group(i, j, tg, nu):
        return (layer, tg[jnp.minimum(i, nu[0] - 1)], 0, 0)

    grid_spec = pltpu.PrefetchScalarGridSpec(
        num_scalar_prefetch=2,
        grid=(N_TILES, nj),
        in_specs=[pl.BlockSpec((T_MOE, ROW_W), tile),
                  pl.BlockSpec((1, E_STEP, D, FF), expert),
                  pl.BlockSpec((1, E_STEP, D, FF), expert),
                  pl.BlockSpec((1, 1, EPG * FF, D), group, pipeline_mode=pl.Buffered(1))],
        out_specs=pl.BlockSpec((T_MOE, D), lambda i, e, tg, nu: (i, 0)),
        scratch_shapes=[pltpu.VMEM((T_MOE, D), BF16), pltpu.VMEM((T_MOE, EPG * FF), BF16)],
    )
    return pl.pallas_call(
        _moe_kernel,
        out_shape=jax.ShapeDtypeStruct((N_SORT, D), F32),
        grid_spec=grid_spec,
        compiler_params=pltpu.CompilerParams(dimension_semantics=("arbitrary", "arbitrary"),
                                             vmem_limit_bytes=MOE_VMEM_LIMIT),
        name="moe_experts",
    )(tile_group, n_used, hs, wg, wu, wd.reshape(DEPTH, N_GROUPS, EPG * FF, D))


def _comb_kernel(pos_ref, x_ref, mod_ref, y_ref, o_ref, buf_ref, sem):
    _row_copies(T_DISP,
                lambda r: y_ref.at[pl.ds(pos_ref[0, 0, r], 1), :],
                lambda r: buf_ref.at[pl.ds(r, 1), :], sem)
    pltpu.make_async_copy(y_ref.at[pl.ds(0, T_DISP), :], buf_ref, sem).wait()
    o_ref[...] = x_ref[...] + mod_ref[5:6, :] * buf_ref[...]


def _combine(pos, x, mod, ys):
    return pl.pallas_call(
        _comb_kernel,
        out_shape=jax.ShapeDtypeStruct((SEQ, D), F32),
        grid=(SEQ // T_DISP,),
        in_specs=[pl.BlockSpec((1, 1, T_DISP), lambda i: (i, 0, 0), memory_space=pltpu.SMEM),
                  pl.BlockSpec((T_DISP, D), lambda i: (i, 0)),
                  pl.BlockSpec((8, D), lambda i: (0, 0)),
                  pl.BlockSpec(memory_space=pl.ANY)],
        out_specs=pl.BlockSpec((T_DISP, D), lambda i: (i, 0)),
        scratch_shapes=[pltpu.VMEM((T_DISP, D), F32), pltpu.SemaphoreType.DMA(())],
        compiler_params=_cparams(1),
        name="moe_combine",
    )(pos, x, mod, ys)


def _moe_layer(x, h2a, meta, cnt, mod, wg, wu, wd, layer, hs):
    pos, tile_group, n_used = _dispatch_plan(meta, cnt)
    hs = _dispatch(pos, h2a, hs)
    ys = _moe_experts(tile_group, n_used, hs, wg, wu, wd, layer)
    return _combine(pos, x, mod, ys), hs


def _pad_heads(w, heads, dim):
    k = w.shape[0]
    w = w.reshape(k, heads, dim)
    return jnp.pad(w, ((0, 0), (0, 0), (0, LANES - dim))).reshape(k, heads * LANES)


def _pad_gain(g):
    return jnp.pad(g, (0, LANES - g.shape[0])).reshape(1, LANES)


def _odd_weights(w_in):
    bounds = np.cumsum([0, SWA_HEADS * HEAD_DIM, SWA_KV * HEAD_DIM, SWA_KV * HEAD_DIM, Q_LORA, KV_LORA, MLA_ROPE])
    q_s, k_s, v, c_q, c_kv, k_rope = [w_in[:, int(a):int(b)] for a, b in zip(bounds[:-1], bounds[1:])]
    q_s = _pad_heads(q_s, SWA_HEADS, HEAD_DIM)
    k_s = _pad_heads(k_s, SWA_KV, HEAD_DIM)
    v0, v1 = v[:, 0:HEAD_DIM], v[:, HEAD_DIM:2 * HEAD_DIM]
    v_s = jnp.concatenate([v0, v0, v1, v1], axis=1)
    k_r = jnp.pad(k_rope, ((0, 0), (MLA_NOPE, LANES - MLA_QK)))
    k_r_sw = jnp.pad(_swap_halves(k_rope), ((0, 0), (MLA_NOPE, LANES - MLA_QK)))
    return jnp.concatenate([q_s, k_s, v_s, c_q, c_kv, k_r, k_r_sw], axis=1).astype(BF16)


def _swap_halves(t):
    half = t.shape[-1] // 2
    return jnp.concatenate([t[..., half:], t[..., :half]], axis=-1)


def _uq_weights(w_uq):
    w = w_uq.reshape(Q_LORA, MLA_HEADS, MLA_QK)
    plain = jnp.pad(w, ((0, 0), (0, 0), (0, LANES - MLA_QK)))
    swapped = jnp.pad(_swap_halves(w[:, :, MLA_NOPE:]), ((0, 0), (0, 0), (MLA_NOPE, LANES - MLA_QK)))
    return jnp.concatenate([plain.reshape(Q_LORA, -1), swapped.reshape(Q_LORA, -1)], axis=1)


def _rope_gains(g):
    plain = jnp.pad(g, (0, LANES - MLA_QK)).reshape(1, LANES)
    partner = jnp.pad(_swap_halves(g[MLA_NOPE:]), (MLA_NOPE, LANES - MLA_QK)).reshape(1, LANES)
    return plain, partner


def _router_weights(w_group, b_group, w_expert, b_expert):
    w = jnp.pad(jnp.concatenate([w_expert, w_group], axis=1), ((0, 0), (0, LANES - N_EXPERTS - N_GROUPS)))
    hi = w.astype(BF16)
    lo = (w - hi.astype(F32)).astype(BF16)
    b = jnp.pad(jnp.concatenate([b_expert, b_group]), (0, LANES - N_EXPERTS - N_GROUPS)).reshape(1, LANES)
    return hi, lo, b


def kernel(x, c, positions, ada_w, ada_b, norm1_g, norm2_g, cp_w_in, conv_w, pool_w, pool_scale,
           cp_w_out, at_w_in, swa_q_g, swa_k_g, swa_sinks, mla_q_norm_g, mla_kv_norm_g, mla_w_uq,
           mla_w_ukv, mla_q_g, mla_k_g, at_w_out, moe_w_group, moe_b_group, moe_w_expert,
           moe_b_expert, moe_w_gate, moe_w_up, moe_w_down):
    xs = x.reshape(SEQ, D)
    mods = _ada_mod(c, ada_w, ada_b)
    tc, ts1, ts2 = _rope_tables(positions)
    hs = jnp.zeros((N_SORT, ROW_W), F32)
    for l in range(DEPTH):
        i = l // 2
        mod = mods[l]
        n1 = norm1_g[l].reshape(1, D)
        n2 = norm2_g[l].reshape(1, D)
        wr_hi, wr_lo, br = _router_weights(moe_w_group[l], moe_b_group[l], moe_w_expert[l], moe_b_expert[l])
        if l % 2 == 0:
            xs, h2a, meta, cnt = _even_layer(
                xs, mod, n1, n2, cp_w_in[i].astype(BF16), conv_w[i], pool_w[i].astype(BF16),
                pool_scale[i].reshape(1, 4 * POOL_G), cp_w_out[i].astype(BF16), wr_hi, wr_lo, br)
        else:
            ukv = mla_w_ukv[i].reshape(KV_LORA, MLA_HEADS, MLA_NOPE + MLA_V)
            wuk = _pad_heads(ukv[:, :, :MLA_NOPE].reshape(KV_LORA, MLA_HEADS * MLA_NOPE), MLA_HEADS, MLA_NOPE)
            wuv = ukv[:, :, MLA_NOPE:].reshape(KV_LORA, MLA_HEADS * MLA_V)
            wuq = _uq_weights(mla_w_uq[i])
            gmq, gmqs = _rope_gains(mla_q_g[i])
            gmk, gmks = _rope_gains(mla_k_g[i])
            qs, ks, vs, qm, km, vm = _proj_layer(
                xs, mod, n1, _odd_weights(at_w_in[i]), _pad_gain(swa_q_g[i]), _pad_gain(swa_k_g[i]),
                mla_q_norm_g[i].reshape(1, Q_LORA), mla_kv_norm_g[i].reshape(1, KV_LORA),
                wuq.astype(BF16), wuk.astype(BF16), wuv.astype(BF16),
                gmq, gmqs, gmk, gmks, tc, ts1, ts2)
            o_s = _swa_layer(swa_sinks[i], qs, ks, vs)
            o_m = _mla_layer(qm, km, vm)
            xs, h2a, meta, cnt = _post_layer(xs, o_s, o_m, mod, n2, at_w_out[i].astype(BF16), wr_hi, wr_lo, br)
        xs, hs = _moe_layer(xs, h2a, meta, cnt, mod, moe_w_gate, moe_w_up, moe_w_down, l, hs)
    return xs.reshape(1, SEQ, D)
```
